```python
import math
import jax, jax.numpy as jnp
from jax import lax
import numpy as np

D_MODEL = 1024
BATCH = 8
SEQ = 8192
DEPTH = 2
DEC_BATCH = 8
DEC_SEQ = 64
PAST_LEN = 2048

CHUNK = 64
Q_BLOCK = 128
N_EVEN = (DEPTH + 1) // 2
N_ODD = DEPTH // 2
D_FF = 2816
EPS = 1e-6

M_HEADS = 4
M_DH = 128
M_W = M_HEADS * M_DH
A_HEADS = 4
A_NOPE = 128
A_ROPE = 64
A_VDIM = 128
Q_LORA = 384
KV_LORA = 256
ROPE_THETA = 10000.0
MLA_SCALE = (A_NOPE + A_ROPE) ** -0.5
S5_W = 512
S5_GSIZE = 16
S5_GROUPS = S5_W // S5_GSIZE
S5_P = 64
DT_MIN = 0.001
DT_MAX = 0.1
SB_HEADS = 8
SB_DH = 64
SB_W = SB_HEADS * SB_DH

EVEN_IN = 4 * M_W + 2 * M_HEADS + Q_LORA + KV_LORA + A_ROPE
EVEN_OUT = M_W + A_HEADS * A_VDIM
ODD_IN = S5_W + 3 * SB_W
ODD_OUT = S5_W + SB_W

kernel_name = 'hybrid_stream_encoder_step'


def rmsnorm(x, g):
    xf = x.astype(jnp.float32)
    y = xf * lax.rsqrt(jnp.mean(xf * xf, axis=-1, keepdims=True) + EPS)
    return (y * g.astype(jnp.float32)).astype(x.dtype)


def swiglu(x, w_gate, w_up, w_down):
    return (jax.nn.silu(x @ w_gate) * (x @ w_up)) @ w_down


def split_cols(z, sizes):
    parts, start = [], 0
    for s in sizes:
        parts.append(z[..., start:start + s])
        start += s
    return parts


def rope(x, pos):
    half = A_ROPE // 2
    inv_freq = ROPE_THETA ** (-jnp.arange(half, dtype=jnp.float32) / half)
    ang = pos.astype(jnp.float32)[:, None] * inv_freq[None, :]
    shape = (pos.shape[0],) + (1,) * (x.ndim - 3) + (half,)
    cos, sin = jnp.cos(ang).reshape(shape), jnp.sin(ang).reshape(shape)
    xf = x.astype(jnp.float32)
    x1, x2 = xf[..., :half], xf[..., half:]
    return jnp.concatenate([x1 * cos - x2 * sin, x1 * sin + x2 * cos], axis=-1).astype(x.dtype)


def blockwise(attend_fn, q_parts, q_pos, kv_parts, k_pos):
    L = q_pos.shape[0]
    nb = L // Q_BLOCK

    def to_blocks(a):
        return jnp.moveaxis(a.reshape((a.shape[0], nb, Q_BLOCK) + a.shape[2:]), 1, 0)

    def one(args):
        qp = args[-1]
        return attend_fn(*args[:-1], qp, *kv_parts, k_pos)

    out = lax.map(one, tuple(to_blocks(a) for a in q_parts) + (q_pos.reshape(nb, Q_BLOCK),))
    out = jnp.moveaxis(out, 0, 1)
    return out.reshape((out.shape[0], L) + out.shape[3:])


def mlstm_chunk(carry, inp):
    C, n, m = carry
    q, k, v, ig, lf = inp
    L = q.shape[2]
    b = jnp.cumsum(lf, axis=-1)
    causal = jnp.tril(jnp.ones((L, L), dtype=bool))
    log_d = jnp.where(causal, b[..., :, None] - b[..., None, :] + ig[..., None, :], -jnp.inf)
    m_inter = b + m[..., None]
    m_t = jnp.maximum(m_inter, jnp.max(log_d, axis=-1))
    w_intra = jnp.exp(log_d - m_t[..., None])
    w_inter = jnp.exp(m_inter - m_t)
    s = jnp.einsum('bhtd,bhsd->bhts', q, k) * w_intra
    num = w_inter[..., None] * jnp.einsum('bhvk,bhtk->bhtv', C, q) + jnp.einsum('bhts,bhsv->bhtv', s, v)
    den = w_inter * jnp.einsum('bhk,bhtk->bht', n, q) + jnp.sum(s, axis=-1)
    h = num / jnp.maximum(jnp.abs(den), jnp.exp(-m_t))[..., None]
    m_new = m_t[..., -1]
    w_s = jnp.exp(b[..., -1:] - b + ig - m_new[..., None])
    decay = jnp.exp(b[..., -1] + m - m_new)
    C_new = decay[..., None, None] * C + jnp.einsum('bhs,bhsv,bhsk->bhvk', w_s, v, k)
    n_new = decay[..., None] * n + jnp.einsum('bhs,bhsk->bhk', w_s, k)
    return (C_new, n_new, m_new), h


def mlstm_mix(q, k, v, ig, lf, state):
    f32 = jnp.float32
    Bn, L = q.shape[:2]
    q = q.astype(f32).transpose(0, 2, 1, 3)
    k = (k.astype(f32) * M_DH ** -0.5).transpose(0, 2, 1, 3)
    v = v.astype(f32).transpose(0, 2, 1, 3)
    ig = ig.astype(f32).transpose(0, 2, 1)
    lf = lf.astype(f32).transpose(0, 2, 1)
    if state is None:
        nc = L // CHUNK

        def split(a):
            return jnp.moveaxis(a.reshape(a.shape[:2] + (nc, CHUNK) + a.shape[3:]), 2, 0)

        init = (jnp.zeros((Bn, M_HEADS, M_DH, M_DH), f32), jnp.zeros((Bn, M_HEADS, M_DH), f32),
                jnp.zeros((Bn, M_HEADS), f32))
        final, h = lax.scan(mlstm_chunk, init, tuple(split(a) for a in (q, k, v, ig, lf)))
        h = jnp.moveaxis(h, 0, 2).reshape(Bn, M_HEADS, L, M_DH)
    else:
        final, h = mlstm_chunk(tuple(s.astype(f32) for s in state), (q, k, v, ig, lf))
    return h.transpose(0, 2, 1, 3), final


def expand_latent(latent, w_ukv):
    Bn, L, _ = latent.shape
    kv = (latent @ w_ukv).reshape(Bn, L, A_HEADS, A_NOPE + A_VDIM)
    return kv[..., :A_NOPE], kv[..., A_NOPE:]


def mla_attend(q_nope, q_rope, q_pos, k_nope, k_rope, v, k_pos):
    s = (jnp.einsum('bqhd,bkhd->bhqk', q_nope, k_nope)
         + jnp.einsum('bqhd,bkd->bhqk', q_rope, k_rope)).astype(jnp.float32) * MLA_SCALE
    visible = (k_pos // CHUNK)[None, :] <= (q_pos // CHUNK)[:, None]
    p = jax.nn.softmax(jnp.where(visible, s, -jnp.inf), axis=-1)
    return jnp.einsum('bhqk,bkhd->bqhd', p, v.astype(jnp.float32))


def s5_discretize(A_re, A_im, B_re, B_im, log_dt):
    f32 = jnp.float32
    dt = jnp.exp(log_dt.astype(f32))[:, None]
    ar, ai = A_re.astype(f32), A_im.astype(f32)
    mag = jnp.exp(dt * ar)
    ab_r, ab_i = mag * jnp.cos(dt * ai), mag * jnp.sin(dt * ai)
    den = ar * ar + ai * ai
    c_r = ((ab_r - 1.0) * ar + ab_i * ai) / den
    c_i = (ab_i * ar - (ab_r - 1.0) * ai) / den
    br, bi = B_re.astype(f32), B_im.astype(f32)
    bb_r = c_r[..., None] * br - c_i[..., None] * bi
    bb_i = c_r[..., None] * bi + c_i[..., None] * br
    return ab_r, ab_i, bb_r, bb_i


def ssm_combine(e1, e2):
    a1r, a1i, b1r, b1i = e1
    a2r, a2i, b2r, b2i = e2
    return (a2r * a1r - a2i * a1i, a2r * a1i + a2i * a1r,
            a2r * b1r - a2i * b1i + b2r, a2r * b1i + a2i * b1r + b2i)


def s5_mix(u, A_re, A_im, B_re, B_im, C_re, C_im, D, log_dt, w_glu, state):
    f32 = jnp.float32
    Bn, L, _ = u.shape
    uf = u.astype(f32)
    ab_r, ab_i, bb_r, bb_i = s5_discretize(A_re, A_im, B_re, B_im, log_dt)
    ug = uf.reshape(Bn, L, S5_GROUPS, S5_GSIZE)
    bu_r = jnp.einsum('blgc,gpc->blgp', ug, bb_r)
    bu_i = jnp.einsum('blgc,gpc->blgp', ug, bb_i)
    a_r = jnp.broadcast_to(ab_r, (1, L, S5_GROUPS, S5_P))
    a_i = jnp.broadcast_to(ab_i, (1, L, S5_GROUPS, S5_P))
    pa_r, pa_i, x_r, x_i = lax.associative_scan(ssm_combine, (a_r, a_i, bu_r, bu_i), axis=1)
    if state is not None:
        x0r, x0i = state[0].astype(f32)[:, None], state[1].astype(f32)[:, None]
        x_r, x_i = x_r + pa_r * x0r - pa_i * x0i, x_i + pa_r * x0i + pa_i * x0r
    y = (jnp.einsum('blgp,gcp->blgc', x_r, C_re.astype(f32))
         - jnp.einsum('blgp,gcp->blgc', x_i, C_im.astype(f32)))
    y = y.reshape(Bn, L, S5_W) + D.astype(f32) * uf
    g = jax.nn.gelu(y)
    out = g * jax.nn.sigmoid(g @ w_glu.astype(f32))
    return out, (x_r[:, -1], x_i[:, -1])


def sb_attend(q, q_pos, k, v, k_pos):
    z = jnp.einsum('bqhd,bkhd->bhqk', q, k).astype(jnp.float32) * SB_DH ** -0.5
    causal = k_pos[None, :] < q_pos[:, None]
    log_1mb = jnp.where(causal, jax.nn.log_sigmoid(-z), 0.0)
    log_after = lax.cumsum(log_1mb, axis=3, reverse=True) - log_1mb
    log_w = jnp.where(causal, jax.nn.log_sigmoid(z) + log_after, -jnp.inf)
    return jnp.einsum('bhqk,bkhd->bqhd', jnp.exp(log_w), v.astype(jnp.float32))


def even_mixer(h, pos, w_in, w_out, b_igate, b_fgate, out_norm, q_norm, kv_norm, w_uq, w_ukv,
               mem_state, kv_cache):
    f32 = jnp.float32
    Bn, L, _ = h.shape
    q_m, k_m, v_m, o_m, i_pre, f_pre, c_q, c_kv, k_r = split_cols(
        h @ w_in, [M_W, M_W, M_W, M_W, M_HEADS, M_HEADS, Q_LORA, KV_LORA, A_ROPE])
    heads = (Bn, L, M_HEADS, M_DH)
    h_m, new_mem = mlstm_mix(q_m.reshape(heads), k_m.reshape(heads), v_m.reshape(heads),
                             i_pre.astype(f32) + b_igate.astype(f32),
                             jax.nn.log_sigmoid(f_pre.astype(f32) + b_fgate.astype(f32)), mem_state)
    h_m = rmsnorm(h_m, out_norm.reshape(M_HEADS, M_DH)).reshape(Bn, L, M_W)
    h_m = jax.nn.sigmoid(o_m.astype(f32)) * h_m
    q = (rmsnorm(c_q, q_norm) @ w_uq).reshape(Bn, L, A_HEADS, A_NOPE + A_ROPE)
    q_nope, q_rope = q[..., :A_NOPE], rope(q[..., A_NOPE:], pos)
    latent, k_rope = rmsnorm(c_kv, kv_norm), rope(k_r, pos)
    if kv_cache is None:
        k_nope, v = expand_latent(latent, w_ukv)
        h_a = blockwise(mla_attend, (q_nope, q_rope), pos, (k_nope, k_rope, v), pos)
    else:
        lat_all = jnp.concatenate([kv_cache[0].astype(latent.dtype), latent], axis=1)
        kr_all = jnp.concatenate([kv_cache[1].astype(k_rope.dtype), k_rope], axis=1)
        k_nope, v = expand_latent(lat_all, w_ukv)
        h_a = mla_attend(q_nope, q_rope, pos, k_nope, kr_all, v, jnp.arange(lat_all.shape[1]))
    mixed = jnp.concatenate([h_m, h_a.reshape(Bn, L, A_HEADS * A_VDIM)], axis=-1).astype(h.dtype)
    return mixed @ w_out, new_mem, (latent, k_rope)


def odd_mixer(h, pos, w_in, w_out, A_re, A_im, B_re, B_im, C_re, C_im, D, log_dt, w_glu,
              ssm_state, kv_cache):
    Bn, L, _ = h.shape
    u, q, k, v = split_cols(h @ w_in, [S5_W, SB_W, SB_W, SB_W])
    h_s, new_ssm = s5_mix(u, A_re, A_im, B_re, B_im, C_re, C_im, D, log_dt, w_glu, ssm_state)
    heads = (Bn, L, SB_HEADS, SB_DH)
    q, k, v = q.reshape(heads), k.reshape(heads), v.reshape(heads)
    if kv_cache is None:
        h_b = blockwise(sb_attend, (q,), pos, (k, v), pos)
    else:
        k_all = jnp.concatenate([kv_cache[0].astype(k.dtype), k], axis=1)
        v_all = jnp.concatenate([kv_cache[1].astype(v.dtype), v], axis=1)
        h_b = sb_attend(q, pos, k_all, v_all, jnp.arange(k_all.shape[1]))
    mixed = jnp.concatenate([h_s, h_b.reshape(Bn, L, SB_W)], axis=-1).astype(h.dtype)
    return mixed @ w_out, new_ssm, (k, v)


def setup_inputs(seed: int = 0) -> dict:
    key = jax.random.key(seed)
    ks = jax.random.split(key, 48)
    keys = iter([ks[i] for i in range(48)])
    f32 = jnp.float32

    def nrm(shape, scale):
        return jax.random.normal(next(keys), shape, f32) * scale

    def gain(shape):
        return 1.0 + 0.01 * jax.random.normal(next(keys), shape, f32)

    s5_log_dt = jax.random.uniform(next(keys), (N_ODD, S5_GROUPS), f32,
                                   math.log(DT_MIN), math.log(DT_MAX))
    return dict(
        x_prompt=nrm((BATCH, SEQ, D_MODEL), 1.0),
        x_sample=nrm((DEC_BATCH, DEC_SEQ, D_MODEL), 1.0),
        cache_mla_latent=nrm((N_EVEN, DEC_BATCH, PAST_LEN, KV_LORA), 1.0),
        cache_mla_krope=nrm((N_EVEN, DEC_BATCH, PAST_LEN, A_ROPE), 1.0),
        state_mlstm_C=nrm((N_EVEN, DEC_BATCH, M_HEADS, M_DH, M_DH), 0.3),
        state_mlstm_n=nrm((N_EVEN, DEC_BATCH, M_HEADS, M_DH), 0.3),
        state_mlstm_m=nrm((N_EVEN, DEC_BATCH, M_HEADS), 1.0),
        state_s5_re=nrm((N_ODD, DEC_BATCH, S5_GROUPS, S5_P), 0.3),
        state_s5_im=nrm((N_ODD, DEC_BATCH, S5_GROUPS, S5_P), 0.3),
        cache_sb_k=nrm((N_ODD, DEC_BATCH, PAST_LEN, SB_HEADS, SB_DH), 1.0),
        cache_sb_v=nrm((N_ODD, DEC_BATCH, PAST_LEN, SB_HEADS, SB_DH), 1.0),
        norm_ffn=gain((DEPTH, 2, D_MODEL)),
        norm_mix=gain((DEPTH, D_MODEL)),
        norm_final=gain((D_MODEL,)),
        ffn_w_gate=nrm((DEPTH, 2, D_MODEL, D_FF), D_MODEL ** -0.5),
        ffn_w_up=nrm((DEPTH, 2, D_MODEL, D_FF), D_MODEL ** -0.5),
        ffn_w_down=nrm((DEPTH, 2, D_FF, D_MODEL), D_FF ** -0.5),
        even_w_in=nrm((N_EVEN, D_MODEL, EVEN_IN), D_MODEL ** -0.5),
        even_w_out=nrm((N_EVEN, EVEN_OUT, D_MODEL), EVEN_OUT ** -0.5),
        mlstm_b_igate=nrm((N_EVEN, M_HEADS), 0.1),
        mlstm_b_fgate=jnp.linspace(3.0, 6.0, M_HEADS, dtype=f32)[None, :] + nrm((N_EVEN, M_HEADS), 0.1),
        mlstm_out_norm=gain((N_EVEN, M_W)),
        mla_q_norm=gain((N_EVEN, Q_LORA)),
        mla_kv_norm=gain((N_EVEN, KV_LORA)),
        mla_w_uq=nrm((N_EVEN, Q_LORA, A_HEADS * (A_NOPE + A_ROPE)), Q_LORA ** -0.5),
        mla_w_ukv=nrm((N_EVEN, KV_LORA, A_HEADS * (A_NOPE + A_VDIM)), KV_LORA ** -0.5),
        odd_w_in=nrm((N_ODD, D_MODEL, ODD_IN), D_MODEL ** -0.5),
        odd_w_out=nrm((N_ODD, ODD_OUT, D_MODEL), ODD_OUT ** -0.5),
        s5_A_re=-0.5 + nrm((N_ODD, S5_GROUPS, S5_P), 0.01),
        s5_A_im=math.pi * jnp.arange(S5_P, dtype=f32) + nrm((N_ODD, S5_GROUPS, S5_P), 0.01),
        s5_B_re=nrm((N_ODD, S5_GROUPS, S5_P, S5_GSIZE), (2 * S5_GSIZE) ** -0.5),
        s5_B_im=nrm((N_ODD, S5_GROUPS, S5_P, S5_GSIZE), (2 * S5_GSIZE) ** -0.5),
        s5_C_re=nrm((N_ODD, S5_GROUPS, S5_GSIZE, S5_P), S5_P ** -0.5),
        s5_C_im=nrm((N_ODD, S5_GROUPS, S5_GSIZE, S5_P), S5_P ** -0.5),
        s5_D=nrm((N_ODD, S5_W), 1.0),
        s5_log_dt=s5_log_dt,
        s5_w_glu=nrm((N_ODD, S5_W, S5_W), S5_W ** -0.5),
    )


def reference(x_prompt, x_sample, cache_mla_latent, cache_mla_krope, state_mlstm_C, state_mlstm_n,
              state_mlstm_m, state_s5_re, state_s5_im, cache_sb_k, cache_sb_v, norm_ffn, norm_mix,
              norm_final, ffn_w_gate, ffn_w_up, ffn_w_down, even_w_in, even_w_out, mlstm_b_igate,
              mlstm_b_fgate, mlstm_out_norm, mla_q_norm, mla_kv_norm, mla_w_uq, mla_w_ukv, odd_w_in,
              odd_w_out, s5_A_re, s5_A_im, s5_B_re, s5_B_im, s5_C_re, s5_C_im, s5_D, s5_log_dt,
              s5_w_glu):
    past = cache_sb_k.shape[2]
    pos_p = jnp.arange(x_prompt.shape[1])
    pos_s = past + jnp.arange(x_sample.shape[1])

    def trunk(x, pos, use_cache):
        lat, krp, mem_c, mem_n, mem_m, ssm_re, ssm_im, sb_k, sb_v = ([] for _ in range(9))
        for l in range(DEPTH):
            j = l // 2
            h = rmsnorm(x, norm_ffn[l, 0])
            x = x + (0.5 * swiglu(h, ffn_w_gate[l, 0], ffn_w_up[l, 0], ffn_w_down[l, 0])).astype(x.dtype)
            h = rmsnorm(x, norm_mix[l])
            if l % 2 == 0:
                mem = (state_mlstm_C[j], state_mlstm_n[j], state_mlstm_m[j]) if use_cache else None
                kv = (cache_mla_latent[j], cache_mla_krope[j]) if use_cache else None
                out, (c_new, n_new, m_new), (lat_new, kr_new) = even_mixer(
                    h, pos, even_w_in[j], even_w_out[j], mlstm_b_igate[j], mlstm_b_fgate[j],
                    mlstm_out_norm[j], mla_q_norm[j], mla_kv_norm[j], mla_w_uq[j], mla_w_ukv[j], mem, kv)
                mem_c.append(c_new)
                mem_n.append(n_new)
                mem_m.append(m_new)
                lat.append(lat_new)
                krp.append(kr_new)
            else:
                ssm = (state_s5_re[j], state_s5_im[j]) if use_cache else None
                kv = (cache_sb_k[j], cache_sb_v[j]) if use_cache else None
                out, (re_new, im_new), (k_new, v_new) = odd_mixer(
                    h, pos, odd_w_in[j], odd_w_out[j], s5_A_re[j], s5_A_im[j], s5_B_re[j], s5_B_im[j],
                    s5_C_re[j], s5_C_im[j], s5_D[j], s5_log_dt[j], s5_w_glu[j], ssm, kv)
                ssm_re.append(re_new)
                ssm_im.append(im_new)
                sb_k.append(k_new)
                sb_v.append(v_new)
            x = x + out.astype(x.dtype)
            h = rmsnorm(x, norm_ffn[l, 1])
            x = x + (0.5 * swiglu(h, ffn_w_gate[l, 1], ffn_w_up[l, 1], ffn_w_down[l, 1])).astype(x.dtype)
        return (rmsnorm(x, norm_final), jnp.stack(lat), jnp.stack(krp), jnp.stack(mem_c),
                jnp.stack(mem_n), jnp.stack(mem_m), jnp.stack(ssm_re), jnp.stack(ssm_im),
                jnp.stack(sb_k), jnp.stack(sb_v))

    (y_prompt, lat_p, kr_p, c_p, n_p, m_p, re_p, im_p, k_p, v_p) = trunk(x_prompt, pos_p, False)
    (y_sample, lat_s, kr_s, c_s, n_s, m_s, re_s, im_s, k_s, v_s) = trunk(x_sample, pos_s, True)
    return (y_prompt, y_sample, lat_p, kr_p, c_p, n_p, m_p, re_p, im_p, k_p, v_p,
            lat_s, kr_s, c_s, n_s, m_s, re_s, im_s, k_s, v_s)
```

```python
import functools
import math

import jax
import jax.numpy as jnp
import numpy as np
from jax import lax
from jax.experimental import pallas as pl
from jax.experimental.pallas import tpu as pltpu

F32 = jnp.float32
BF16 = jnp.bfloat16

D_MODEL = 1024
CHUNK = 64
D_FF = 2816
EPS = 1e-6
M_HEADS = 4
M_DH = 128
M_W = M_HEADS * M_DH
A_HEADS = 4
A_NOPE = 128
A_ROPE = 64
A_VDIM = 128
Q_LORA = 384
KV_LORA = 256
ROPE_THETA = 10000.0
MLA_SCALE = (A_NOPE + A_ROPE) ** -0.5
S5_W = 512
S5_GSIZE = 16
S5_GROUPS = S5_W // S5_GSIZE
S5_P = 64
S5_STATE = S5_GROUPS * S5_P
SB_HEADS = 8
SB_DH = 64
SB_W = SB_HEADS * SB_DH

LANES = 128
SUBLANES = 8
VMEM_LIMIT_BYTES = 56 * 1024 * 1024

TOKEN_TILE = 512
FF_CHUNK = 256
MLA_TILE = 512
SB_TQ = 256
SB_TK = 128
S5_TT = 64
SB_LOG_FLOOR = -90.0

HIGHEST = lax.Precision.HIGHEST


def _cparams(*sem):
    return pltpu.CompilerParams(dimension_semantics=sem, vmem_limit_bytes=VMEM_LIMIT_BYTES)


def _bdot(a, b):
    return jnp.dot(a.astype(BF16), b.astype(BF16), preferred_element_type=F32)


def _bdot_nt(a, b):
    return lax.dot_general(a.astype(BF16), b.astype(BF16), (((1,), (1,)), ((), ())),
                           preferred_element_type=F32)


def _rms(x, g):
    return x * lax.rsqrt(jnp.mean(x * x, axis=-1, keepdims=True) + EPS) * g


def _sigmoid(x):
    return 1.0 / (1.0 + jnp.exp(-x))


def _log_sigmoid(x):
    return jnp.minimum(x, 0.0) - jnp.log(1.0 + jnp.exp(-jnp.abs(x)))


def _const_spec(shape):
    nd = len(shape)
    return pl.BlockSpec(shape, lambda *_: (0,) * nd)


def _ffn_kernel(*refs, with_mix, with_final):
    it = iter(refs)
    x_ref = next(it)
    if with_mix:
        a_ref, b_ref, wa_ref, wb_ref = next(it), next(it), next(it), next(it)
    g_ref, wg_ref, wu_ref, wd_ref = next(it), next(it), next(it), next(it)
    gf_ref = next(it) if with_final else None
    o_ref = next(it)
    acc_ref = next(it)

    x = x_ref[...]
    if with_mix:
        x = x + _bdot(a_ref[...], wa_ref[...]) + _bdot(b_ref[...], wb_ref[...])
    h = _rms(x, g_ref[...]).astype(BF16)
    acc_ref[...] = jnp.zeros_like(acc_ref)
    for c in range(D_FF // FF_CHUNK):
        sl = slice(c * FF_CHUNK, (c + 1) * FF_CHUNK)
        g = jnp.dot(h, wg_ref[:, sl], preferred_element_type=F32)
        u = jnp.dot(h, wu_ref[:, sl], preferred_element_type=F32)
        a = (g * _sigmoid(g) * u).astype(BF16)
        acc_ref[...] += jnp.dot(a, wd_ref[sl, :], preferred_element_type=F32)
    y = x + 0.5 * acc_ref[...]
    if with_final:
        y = _rms(y, gf_ref[...])
    o_ref[...] = y


def _ffn(x, gain, wg, wu, wd, *, mix=None, final_gain=None, tm, nlt):
    n = x.shape[0]
    row = lambda i: (i, 0)
    args, specs = [x], [pl.BlockSpec((tm, D_MODEL), row)]
    if mix is not None:
        a, b, wa, wb, a_time_major = mix
        a_map = (lambda i: (i % nlt, i // nlt)) if a_time_major else row
        args += [a, b, wa, wb]
        specs += [pl.BlockSpec((tm, a.shape[-1] if not a_time_major else wa.shape[0]), a_map),
                  pl.BlockSpec((tm, b.shape[-1]), row),
                  _const_spec(wa.shape), _const_spec(wb.shape)]
    args += [gain.reshape(1, D_MODEL), wg, wu, wd]
    specs += [_const_spec((1, D_MODEL)), _const_spec(wg.shape), _const_spec(wu.shape),
              _const_spec(wd.shape)]
    if final_gain is not None:
        args.append(final_gain.reshape(1, D_MODEL))
        specs.append(_const_spec((1, D_MODEL)))
    return pl.pallas_call(
        functools.partial(_ffn_kernel, with_mix=mix is not None, with_final=final_gain is not None),
        grid=(n // tm,),
        in_specs=specs,
        out_specs=pl.BlockSpec((tm, D_MODEL), row),
        out_shape=jax.ShapeDtypeStruct((n, D_MODEL), F32),
        scratch_shapes=[pltpu.VMEM((tm, D_MODEL), F32)],
        compiler_params=_cparams("parallel"),
        name="ffn",
    )(*args)


def _even_in_kernel(x_ref, gain_ref, wm_ref, wg_ref, bg_ref, wcq_ref, wckv_ref, wkra_ref, wkrb_ref,
                    qn_ref, kvn_ref, wq_ref, ws_ref, wuk_ref, wuv_ref, cos_ref, sin_ref,
                    qm_o, km_o, vm_o, om_o, gates_o, q_o, kn_o, v_o, kr128_o, lat_o, kr_o):
    h = _rms(x_ref[...], gain_ref[...]).astype(BF16)
    m = jnp.dot(h, wm_ref[...], preferred_element_type=F32)
    qm_o[...] = m[:, :M_W].astype(BF16)
    km_o[...] = (m[:, M_W:2 * M_W] * M_DH ** -0.5).astype(BF16)
    vm_o[...] = m[:, 2 * M_W:3 * M_W]
    om_o[...] = m[:, 3 * M_W:]

    g = jnp.dot(h, wg_ref[...], preferred_element_type=F32) + bg_ref[...]
    lane = lax.broadcasted_iota(jnp.int32, g.shape, 1)
    gates_o[...] = jnp.where(lane < M_HEADS, g, _log_sigmoid(g))

    cos, sin = cos_ref[...], sin_ref[...]
    cq = jnp.dot(h, wcq_ref[...], preferred_element_type=F32)
    cqn = _rms(cq, qn_ref[...]).astype(BF16)
    qf = jnp.dot(cqn, wq_ref[...], preferred_element_type=F32)
    qs = jnp.dot(cqn, ws_ref[...], preferred_element_type=F32)
    for hh in range(A_HEADS):
        base = hh * 2 * LANES
        q_o[:, base:base + LANES] = (qf[:, base:base + LANES] * MLA_SCALE).astype(BF16)
        rot = qf[:, base + LANES:base + 2 * LANES] * cos + qs[:, hh * LANES:(hh + 1) * LANES] * sin
        q_o[:, base + LANES:base + 2 * LANES] = (rot * MLA_SCALE).astype(BF16)

    ckv = jnp.dot(h, wckv_ref[...], preferred_element_type=F32)
    lat = _rms(ckv, kvn_ref[...])
    lat_o[...] = lat
    latb = lat.astype(BF16)
    kn_o[...] = jnp.dot(latb, wuk_ref[...], preferred_element_type=F32).astype(BF16)
    v_o[...] = jnp.dot(latb, wuv_ref[...], preferred_element_type=F32).astype(BF16)

    kr = (jnp.dot(h, wkra_ref[...], preferred_element_type=F32) * cos
          + jnp.dot(h, wkrb_ref[...], preferred_element_type=F32) * sin)
    kr128_o[...] = kr.astype(BF16)
    kr_o[...] = kr[:, :A_ROPE]


def _even_in(x, gain, w, cos, sin, *, tm, nlt):
    n = x.shape[0]
    row = lambda i: (i, 0)
    pos = lambda i: (i % nlt, 0)
    weights = [w["wm"], w["wg"], w["bg"], w["wcq"], w["wckv"], w["wkra"], w["wkrb"], w["qn"], w["kvn"],
               w["wq"], w["ws"], w["wuk"], w["wuv"]]
    in_specs = ([pl.BlockSpec((tm, D_MODEL), row), _const_spec((1, D_MODEL))]
                + [_const_spec(a.shape) for a in weights]
                + [pl.BlockSpec((tm, LANES), pos), pl.BlockSpec((tm, LANES), pos)])
    outs = [(M_W, BF16), (M_W, BF16), (M_W, F32), (M_W, F32), (LANES, F32),
            (A_HEADS * 2 * LANES, BF16), (A_HEADS * A_NOPE, BF16), (A_HEADS * A_VDIM, BF16),
            (LANES, BF16), (KV_LORA, F32), (A_ROPE, F32)]
    return pl.pallas_call(
        _even_in_kernel,
        grid=(n // tm,),
        in_specs=in_specs,
        out_specs=[pl.BlockSpec((tm, c), row) for c, _ in outs],
        out_shape=[jax.ShapeDtypeStruct((n, c), dt) for c, dt in outs],
        compiler_params=_cparams("parallel"),
        name="even_in",
    )(x, gain.reshape(1, D_MODEL), *weights, cos, sin)


def _kv_expand_kernel(lat_ref, wuk_ref, wuv_ref, kn_o, v_o):
    latb = lat_ref[...].astype(BF16)
    kn_o[...] = jnp.dot(latb, wuk_ref[...], preferred_element_type=F32).astype(BF16)
    v_o[...] = jnp.dot(latb, wuv_ref[...], preferred_element_type=F32).astype(BF16)


def _kv_expand(lat, wuk, wuv):
    n = lat.shape[0]
    tm = min(TOKEN_TILE, n)
    row = lambda i: (i, 0)
    return pl.pallas_call(
        _kv_expand_kernel,
        grid=(n // tm,),
        in_specs=[pl.BlockSpec((tm, KV_LORA), row), _const_spec(wuk.shape), _const_spec(wuv.shape)],
        out_specs=[pl.BlockSpec((tm, wuk.shape[1]), row), pl.BlockSpec((tm, wuv.shape[1]), row)],
        out_shape=[jax.ShapeDtypeStruct((n, wuk.shape[1]), BF16),
                   jax.ShapeDtypeStruct((n, wuv.shape[1]), BF16)],
        compiler_params=_cparams("parallel"),
        name="kv_expand",
    )(lat, wuk, wuv)


def _mlstm_kernel(qm_ref, km_ref, vm_ref, om_ref, g_ref, onorm_ref, c0_ref, n0_ref, m0_ref,
                  h_ref, cout_ref, nout_ref, mout_ref, c_sc, n_sc, m_sc, *, n_chunks):
    lt = pl.program_id(1)

    @pl.when(lt == 0)
    def _():
        c_sc[...] = c0_ref[0]
        n_sc[...] = n0_ref[0]
        m_sc[...] = m0_ref[0]

    row = lax.broadcasted_iota(jnp.int32, (CHUNK, CHUNK), 0)
    col = lax.broadcasted_iota(jnp.int32, (CHUNK, CHUNK), 1)
    causal = col <= row
    ltri = causal.astype(F32)
    utri = (row <= col).astype(F32)

    def chunk(c, carry):
        r0 = pl.multiple_of(c * CHUNK, CHUNK)
        rows = pl.ds(r0, CHUNK)
        gates = g_ref[0, rows, :]
        gates_t = gates.T
        cum_col = jnp.dot(ltri, gates, precision=HIGHEST, preferred_element_type=F32)
        cum_row = jnp.dot(gates_t, utri, precision=HIGHEST, preferred_element_type=F32)
        for hh in range(M_HEADS):
            hs = slice(hh * M_DH, (hh + 1) * M_DH)
            ig_row = gates_t[hh:hh + 1, :]
            ig_col = gates[:, hh:hh + 1]
            b_row = cum_row[M_HEADS + hh:M_HEADS + hh + 1, :]
            b_col = cum_col[:, M_HEADS + hh:M_HEADS + hh + 1]
            m_prev = m_sc[hh][:, :1]
            q = qm_ref[0, rows, hs]
            k = km_ref[0, rows, hs]
            v = vm_ref[0, rows, hs]
            c_h = c_sc[hh]
            n_h = n_sc[hh]

            log_d = jnp.where(causal, b_col - b_row + ig_row, -jnp.inf)
            m_inter = b_col + m_prev
            m_t = jnp.maximum(m_inter, jnp.max(log_d, axis=-1, keepdims=True))
            w_intra = jnp.exp(log_d - m_t)
            w_inter = jnp.exp(m_inter - m_t)
            s = _bdot_nt(q, k) * w_intra
            num = w_inter * _bdot_nt(q, c_h) + _bdot(s, v)
            den = (w_inter * jnp.sum(q.astype(F32) * n_h, axis=-1, keepdims=True)
                   + jnp.sum(s, axis=-1, keepdims=True))
            hv = num / jnp.maximum(jnp.abs(den), jnp.exp(-m_t))

            m_new = m_t[CHUNK - 1:CHUNK, :]
            b_last = b_col[CHUNK - 1:CHUNK, :]
            w_s = jnp.exp(b_last - b_col + ig_col - m_new)
            decay = jnp.exp(b_last + m_prev - m_new)
            wv_t = (w_s * v).T
            c_sc[hh] = decay * c_h + _bdot(wv_t, k)
            n_sc[hh] = decay * n_h + jnp.sum(w_s * k.astype(F32), axis=0, keepdims=True)
            m_sc[hh] = jnp.broadcast_to(m_new, (1, LANES))

            hn = _rms(hv, onorm_ref[:, hs])
            h_ref[0, rows, hs] = (_sigmoid(om_ref[0, rows, hs]) * hn).astype(BF16)
        return carry

    lax.fori_loop(0, n_chunks, chunk, 0)

    @pl.when(lt == pl.num_programs(1) - 1)
    def _():
        cout_ref[0] = c_sc[...]
        nout_ref[0] = n_sc[...]
        mout_ref[0] = m_sc[...]


def _mlstm(qm, km, vm, om, gates, onorm, c0, n0, m0, *, tl):
    b, l, _ = qm.shape
    tile = lambda i, j: (i, j, 0)
    st4 = lambda i, j: (i, 0, 0, 0)
    return pl.pallas_call(
        functools.partial(_mlstm_kernel, n_chunks=tl // CHUNK),
        grid=(b, l // tl),
        in_specs=[pl.BlockSpec((1, tl, M_W), tile)] * 4
        + [pl.BlockSpec((1, tl, LANES), tile), _const_spec((1, M_W)),
           pl.BlockSpec((1, M_HEADS, M_DH, M_DH), st4),
           pl.BlockSpec((1, M_HEADS, 1, M_DH), st4),
           pl.BlockSpec((1, M_HEADS, 1, LANES), st4)],
        out_specs=[pl.BlockSpec((1, tl, M_W), tile),
                   pl.BlockSpec((1, M_HEADS, M_DH, M_DH), st4),
                   pl.BlockSpec((1, M_HEADS, 1, M_DH), st4),
                   pl.BlockSpec((1, M_HEADS, 1, LANES), st4)],
        out_shape=[jax.ShapeDtypeStruct((b, l, M_W), BF16),
                   jax.ShapeDtypeStruct((b, M_HEADS, M_DH, M_DH), F32),
                   jax.ShapeDtypeStruct((b, M_HEADS, 1, M_DH), F32),
                   jax.ShapeDtypeStruct((b, M_HEADS, 1, LANES), F32)],
        scratch_shapes=[pltpu.VMEM((M_HEADS, M_DH, M_DH), F32),
                        pltpu.VMEM((M_HEADS, 1, M_DH), F32),
                        pltpu.VMEM((M_HEADS, 1, LANES), F32)],
        compiler_params=_cparams("parallel", "arbitrary"),
        name="mlstm",
    )(qm, km, vm, om, gates, onorm.reshape(1, M_W), c0, n0, m0)


def _mla_kernel(qi_ref, kj_ref, last_ref, q_ref, kn_ref, kr_ref, v_ref, o_ref, m_sc, l_sc, acc_sc,
                *, tq, tk, offset):
    p = pl.program_id(1)
    i = qi_ref[p]
    j = kj_ref[p]

    @pl.when(j == 0)
    def _():
        m_sc[...] = jnp.full_like(m_sc, -jnp.inf)
        l_sc[...] = jnp.zeros_like(l_sc)
        acc_sc[...] = jnp.zeros_like(acc_sc)

    q_chunk = (offset + i * tq + lax.broadcasted_iota(jnp.int32, (tq, tk), 0)) // CHUNK
    k_chunk = (j * tk + lax.broadcasted_iota(jnp.int32, (tq, tk), 1)) // CHUNK
    visible = k_chunk <= q_chunk
    kr = kr_ref[0]
    for hh in range(A_HEADS):
        q = q_ref[0, :, hh * 2 * LANES:(hh + 1) * 2 * LANES]
        k = jnp.concatenate([kn_ref[0, :, hh * A_NOPE:(hh + 1) * A_NOPE], kr], axis=-1)
        s = jnp.where(visible, _bdot_nt(q, k), -jnp.inf)
        m_prev = m_sc[hh]
        m_new = jnp.maximum(m_prev, jnp.max(s, axis=-1, keepdims=True))
        alpha = jnp.exp(m_prev - m_new)
        pr = jnp.exp(s - m_new)
        l_sc[hh] = alpha * l_sc[hh] + jnp.sum(pr, axis=-1, keepdims=True)
        acc_sc[hh] = alpha * acc_sc[hh] + _bdot(pr, v_ref[0, :, hh * A_VDIM:(hh + 1) * A_VDIM])
        m_sc[hh] = m_new

    @pl.when(last_ref[p] == 1)
    def _():
        for hh in range(A_HEADS):
            o_ref[0, :, hh * A_VDIM:(hh + 1) * A_VDIM] = (acc_sc[hh] / l_sc[hh]).astype(BF16)


def _mla_pairs(lq, lk, tq, tk, offset):
    qi, kj, last = [], [], []
    for i in range(lq // tq):
        k_end = ((offset + (i + 1) * tq - 1) // CHUNK + 1) * CHUNK
        nj = min(-(-k_end // tk), lk // tk)
        for j in range(nj):
            qi.append(i)
            kj.append(j)
            last.append(int(j == nj - 1))
    return (np.asarray(qi, np.int32), np.asarray(kj, np.int32), np.asarray(last, np.int32))


def _mla_attention(q, kn, kr, v, *, tq, tk, offset):
    b, lq, _ = q.shape
    lk = kn.shape[1]
    qi, kj, last = _mla_pairs(lq, lk, tq, tk, offset)
    qmap = lambda bb, p, qi_r, kj_r, last_r: (bb, qi_r[p], 0)
    kmap = lambda bb, p, qi_r, kj_r, last_r: (bb, kj_r[p], 0)
    grid_spec = pltpu.PrefetchScalarGridSpec(
        num_scalar_prefetch=3,
        grid=(b, len(qi)),
        in_specs=[pl.BlockSpec((1, tq, q.shape[-1]), qmap),
                  pl.BlockSpec((1, tk, kn.shape[-1]), kmap),
                  pl.BlockSpec((1, tk, kr.shape[-1]), kmap),
                  pl.BlockSpec((1, tk, v.shape[-1]), kmap)],
        out_specs=pl.BlockSpec((1, tq, A_HEADS * A_VDIM), qmap),
        scratch_shapes=[pltpu.VMEM((A_HEADS, tq, 1), F32),
                        pltpu.VMEM((A_HEADS, tq, 1), F32),
                        pltpu.VMEM((A_HEADS, tq, A_VDIM), F32)],
    )
    return pl.pallas_call(
        functools.partial(_mla_kernel, tq=tq, tk=tk, offset=offset),
        grid_spec=grid_spec,
        out_shape=jax.ShapeDtypeStruct((b, lq, A_HEADS * A_VDIM), BF16),
        compiler_params=_cparams("parallel", "arbitrary"),
        name="mla_attention",
    )(jnp.asarray(qi), jnp.asarray(kj), jnp.asarray(last), q, kn, kr, v)


def _odd_in_kernel(x_ref, gain_ref, wu_ref, wq_ref, wk_ref, wv_ref, u_o, q_o, k_o, v_o, kb_o, vb_o):
    h = _rms(x_ref[...], gain_ref[...]).astype(BF16)
    u_o[...] = jnp.dot(h, wu_ref[...], preferred_element_type=F32)
    q_o[...] = (jnp.dot(h, wq_ref[...], preferred_element_type=F32) * SB_DH ** -0.5).astype(BF16)
    k = jnp.dot(h, wk_ref[...], preferred_element_type=F32)
    v = jnp.dot(h, wv_ref[...], preferred_element_type=F32)
    k_o[...] = k
    v_o[...] = v
    kb_o[...] = k.astype(BF16)
    vb_o[...] = v.astype(BF16)


def _odd_in(x, gain, w, *, tm, nlt, nb):
    n = x.shape[0]
    row = lambda i: (i, 0)
    weights = [w["wu"], w["wq"], w["wk"], w["wv"]]
    return pl.pallas_call(
        _odd_in_kernel,
        grid=(n // tm,),
        in_specs=[pl.BlockSpec((tm, D_MODEL), row), _const_spec((1, D_MODEL))]
        + [_const_spec(a.shape) for a in weights],
        out_specs=[pl.BlockSpec((tm, S5_W), lambda i: (i % nlt, i // nlt))]
        + [pl.BlockSpec((tm, SB_W), row)] * 5,
        out_shape=[jax.ShapeDtypeStruct((nlt * tm, nb * S5_W), F32),
                   jax.ShapeDtypeStruct((n, SB_W), BF16),
                   jax.ShapeDtypeStruct((n, SB_W), F32),
                   jax.ShapeDtypeStruct((n, SB_W), F32),
                   jax.ShapeDtypeStruct((n, SB_W), BF16),
                   jax.ShapeDtypeStruct((n, SB_W), BF16)],
        compiler_params=_cparams("parallel"),
        name="odd_in",
    )(x, gain.reshape(1, D_MODEL), *weights)


S5_COLS = 512


def _s5_kernel(u_ref, bbr_ref, bbi_ref, ar_ref, ai_ref, ccr_ref, cci_ref, d_ref, wglu_ref,
               x0r_ref, x0i_ref, hs_o, xr_o, xi_o, xr_buf, xi_buf, xr_st, xi_st, *, tt, nb):
    t = pl.program_id(0)

    @pl.when(t == 0)
    def _():
        xr_st[...] = x0r_ref[...]
        xi_st[...] = x0i_ref[...]

    u = u_ref[...]
    ub = u.astype(BF16)
    xr_buf[...] = jnp.dot(ub, bbr_ref[...], preferred_element_type=F32)
    xi_buf[...] = jnp.dot(ub, bbi_ref[...], preferred_element_type=F32)

    for cg in range(S5_STATE // S5_COLS):
        cs = slice(cg * S5_COLS, (cg + 1) * S5_COLS)
        ar = jnp.broadcast_to(ar_ref[:, cs], (nb, S5_COLS))
        ai = jnp.broadcast_to(ai_ref[:, cs], (nb, S5_COLS))

        def step(s, carry, cs=cs, ar=ar, ai=ai):
            xr, xi = carry
            rows = pl.ds(pl.multiple_of(s * nb, nb), nb)
            nxr = ar * xr - ai * xi + xr_buf[rows, cs]
            nxi = ar * xi + ai * xr + xi_buf[rows, cs]
            xr_buf[rows, cs] = nxr
            xi_buf[rows, cs] = nxi
            return nxr, nxi

        xr, xi = lax.fori_loop(0, tt, step, (xr_st[:, cs], xi_st[:, cs]), unroll=8)
        xr_st[:, cs] = xr
        xi_st[:, cs] = xi

    y = (_bdot(xr_buf[...], ccr_ref[...]) + _bdot(xi_buf[...], cci_ref[...]) + d_ref[...] * u)
    g = jax.nn.gelu(y, approximate=True)
    hs_o[...] = (g * _sigmoid(_bdot(g, wglu_ref[...]))).astype(BF16)

    @pl.when(t == pl.num_programs(0) - 1)
    def _():
        xr_o[...] = xr_st[...]
        xi_o[...] = xi_st[...]


def _s5(u_tm, w, x0r, x0i, *, nb, tt):
    rows = u_tm.shape[0]
    steps = rows // (tt * nb)
    blk = lambda t: (t, 0)
    consts = [w["bbr"], w["bbi"], w["ar"], w["ai"], w["ccr"], w["cci"], w["d"], w["wglu"], x0r, x0i]
    return pl.pallas_call(
        functools.partial(_s5_kernel, tt=tt, nb=nb),
        grid=(steps,),
        in_specs=[pl.BlockSpec((tt * nb, S5_W), blk)] + [_const_spec(a.shape) for a in consts],
        out_specs=[pl.BlockSpec((tt * nb, S5_W), blk), _const_spec((nb, S5_STATE)),
                   _const_spec((nb, S5_STATE))],
        out_shape=[jax.ShapeDtypeStruct((rows, S5_W), BF16),
                   jax.ShapeDtypeStruct((nb, S5_STATE), F32),
                   jax.ShapeDtypeStruct((nb, S5_STATE), F32)],
        scratch_shapes=[pltpu.VMEM((tt * nb, S5_STATE), F32), pltpu.VMEM((tt * nb, S5_STATE), F32),
                        pltpu.VMEM((nb, S5_STATE), F32), pltpu.VMEM((nb, S5_STATE), F32)],
        compiler_params=_cparams("arbitrary"),
        name="s5",
    )(u_tm, *consts)


def _sb_kernel(q_ref, k_ref, v_ref, o_ref, r_sc, acc_sc, qh_sc, *, tq, tk, offset):
    i = pl.program_id(1)
    half = lax.broadcasted_iota(jnp.int32, (1, LANES), 1) // SB_DH
    for hh in range(SB_HEADS):
        pair, sub = divmod(hh, 2)
        qq = q_ref[0, :, pair * LANES:(pair + 1) * LANES]
        qh_sc[hh] = jnp.where(half == sub, qq, jnp.zeros_like(qq))
    r_sc[...] = jnp.zeros_like(r_sc)
    acc_sc[...] = jnp.zeros_like(acc_sc)

    q_pos = offset + i * tq + lax.broadcasted_iota(jnp.int32, (tq, tk), 0)
    col = lax.broadcasted_iota(jnp.int32, (tq, tk), 1)
    jrow = lax.broadcasted_iota(jnp.int32, (tk, tk), 0)
    jcol = lax.broadcasted_iota(jnp.int32, (tk, tk), 1)
    later = (jrow > jcol).astype(BF16)
    j_top = (offset + (i + 1) * tq - 2) // tk

    def cond(carry):
        j, r_max = carry
        return jnp.logical_and(j >= 0, r_max > SB_LOG_FLOOR)

    def body(carry):
        j, _ = carry
        k0 = pl.multiple_of(j * tk, tk)
        causal = (k0 + col) < q_pos
        r_max = jnp.full((1, 1), -jnp.inf, F32)
        for pair in range(SB_HEADS // 2):
            ps = slice(pair * LANES, (pair + 1) * LANES)
            kk = k_ref[0, pl.ds(k0, tk), ps]
            vv = v_ref[0, pl.ds(k0, tk), ps]
            pv = jnp.zeros((tq, LANES), F32)
            for sub in range(2):
                hh = 2 * pair + sub
                z = _bdot_nt(qh_sc[hh], kk)
                sp = jnp.maximum(z, 0.0) + jnp.log(1.0 + jnp.exp(-jnp.abs(z)))
                l1 = jnp.where(causal, -sp, 0.0)
                hi = l1.astype(BF16)
                r1 = l1 - hi.astype(F32)
                mid = r1.astype(BF16)
                lo = (r1 - mid.astype(F32)).astype(BF16)
                cum = (jnp.dot(hi, later, preferred_element_type=F32)
                       + jnp.dot(mid, later, preferred_element_type=F32)
                       + jnp.dot(lo, later, preferred_element_type=F32))
                r_old = r_sc[hh]
                w = jnp.exp(jnp.where(causal, z - sp + cum + r_old, -jnp.inf))
                vh = jnp.where(half.reshape(1, LANES) == sub, vv, jnp.zeros_like(vv))
                pv = pv + _bdot(w, vh)
                r_new = r_old + jnp.sum(l1, axis=-1, keepdims=True)
                r_sc[hh] = r_new
                r_max = jnp.maximum(r_max, jnp.max(r_new, axis=0, keepdims=True))
            acc_sc[:, ps] += pv
        return j - 1, r_max[0, 0]

    lax.while_loop(cond, body, (j_top, jnp.float32(0.0)))
    o_ref[0] = acc_sc[...].astype(BF16)


def _sb_attention(q, k, v, *, tq, tk, offset):
    b, lq, _ = q.shape
    lk = k.shape[1]
    return pl.pallas_call(
        functools.partial(_sb_kernel, tq=tq, tk=tk, offset=offset),
        grid=(b, lq // tq),
        in_specs=[pl.BlockSpec((1, tq, SB_W), lambda bb, i: (bb, i, 0)),
                  pl.BlockSpec((1, lk, SB_W), lambda bb, i: (bb, 0, 0)),
                  pl.BlockSpec((1, lk, SB_W), lambda bb, i: (bb, 0, 0))],
        out_specs=pl.BlockSpec((1, tq, SB_W), lambda bb, i: (bb, i, 0)),
        out_shape=jax.ShapeDtypeStruct((b, lq, SB_W), BF16),
        scratch_shapes=[pltpu.VMEM((SB_HEADS, tq, 1), F32),
                        pltpu.VMEM((tq, SB_W), F32),
                        pltpu.VMEM((SB_HEADS, tq, LANES), BF16)],
        compiler_params=_cparams("parallel", "arbitrary"),
        name="sb_attention",
    )(q, k, v)


def _pad_cols(w, width):
    return jnp.pad(w, ((0, 0), (0, width - w.shape[1])))


def _even_weights(w_in, b_ig, b_fg, q_norm, kv_norm, w_uq, w_ukv):
    o = 4 * M_W
    w_gate = w_in[:, o:o + 2 * M_HEADS]
    o += 2 * M_HEADS
    w_cq = w_in[:, o:o + Q_LORA]
    o += Q_LORA
    w_ckv = w_in[:, o:o + KV_LORA]
    o += KV_LORA
    w_kr = w_in[:, o:o + A_ROPE]
    hr = A_ROPE // 2
    wkra = _pad_cols(w_kr, LANES)
    wkrb = _pad_cols(jnp.concatenate([w_kr[:, hr:], w_kr[:, :hr]], axis=1), LANES)
    uq = w_uq.reshape(Q_LORA, A_HEADS, A_NOPE + A_ROPE)
    zeros = jnp.zeros((Q_LORA, A_HEADS, LANES - A_ROPE), w_uq.dtype)
    wq = jnp.concatenate([uq, zeros], axis=-1).reshape(Q_LORA, A_HEADS * 2 * LANES)
    ws = jnp.concatenate([uq[..., A_NOPE + hr:], uq[..., A_NOPE:A_NOPE + hr], zeros],
                         axis=-1).reshape(Q_LORA, A_HEADS * LANES)
    ukv = w_ukv.reshape(KV_LORA, A_HEADS, A_NOPE + A_VDIM)
    return dict(
        wm=w_in[:, :4 * M_W].astype(BF16),
        wg=_pad_cols(w_gate, LANES).astype(BF16),
        bg=_pad_cols(jnp.concatenate([b_ig, b_fg]).reshape(1, 2 * M_HEADS).astype(F32), LANES),
        wcq=w_cq.astype(BF16), wckv=w_ckv.astype(BF16),
        wkra=wkra.astype(BF16), wkrb=wkrb.astype(BF16),
        qn=q_norm.reshape(1, Q_LORA).astype(F32), kvn=kv_norm.reshape(1, KV_LORA).astype(F32),
        wq=wq.astype(BF16), ws=ws.astype(BF16),
        wuk=ukv[..., :A_NOPE].reshape(KV_LORA, A_HEADS * A_NOPE).astype(BF16),
        wuv=ukv[..., A_NOPE:].reshape(KV_LORA, A_HEADS * A_VDIM).astype(BF16),
    )


def _rope_tables(pos):
    half = A_ROPE // 2
    inv_freq = ROPE_THETA ** (-jnp.arange(half, dtype=F32) / half)
    ang = pos.astype(F32)[:, None] * inv_freq[None, :]
    cos, sin = jnp.cos(ang), jnp.sin(ang)
    pad = jnp.zeros((pos.shape[0], LANES - A_ROPE), F32)
    return (jnp.concatenate([cos, cos, pad], axis=1), jnp.concatenate([-sin, sin, pad], axis=1))


def _s5_weights(a_re, a_im, b_re, b_im, c_re, c_im, d, log_dt, w_glu):
    dt = jnp.exp(log_dt.astype(F32))[:, None]
    ar, ai = a_re.astype(F32), a_im.astype(F32)
    mag = jnp.exp(dt * ar)
    ab_r, ab_i = mag * jnp.cos(dt * ai), mag * jnp.sin(dt * ai)
    den = ar * ar + ai * ai
    c_r = ((ab_r - 1.0) * ar + ab_i * ai) / den
    c_i = (ab_i * ar - (ab_r - 1.0) * ai) / den
    br, bi = b_re.astype(F32), b_im.astype(F32)
    bb_r = c_r[..., None] * br - c_i[..., None] * bi
    bb_i = c_r[..., None] * bi + c_i[..., None] * br
    eye = jnp.eye(S5_GROUPS, dtype=F32)
    bd_in = lambda m: jnp.einsum("gpc,gh->gchp", m, eye).reshape(S5_W, S5_STATE)
    bd_out = lambda m: jnp.einsum("gcp,gh->gphc", m, eye).reshape(S5_STATE, S5_W)
    return dict(
        bbr=bd_in(bb_r).astype(BF16), bbi=bd_in(bb_i).astype(BF16),
        ar=ab_r.reshape(1, S5_STATE), ai=ab_i.reshape(1, S5_STATE),
        ccr=bd_out(c_re.astype(F32)).astype(BF16), cci=bd_out(-c_im.astype(F32)).astype(BF16),
        d=d.reshape(1, S5_W).astype(F32), wglu=w_glu.astype(BF16),
    )


def _pad_keys(a, mult):
    pad = (-a.shape[1]) % mult
    return jnp.pad(a, ((0, 0), (0, pad), (0, 0))) if pad else a


def _trunk(x, pos0, cache, p):
    nb, l, _ = x.shape
    assert nb == SUBLANES, "the S5 layout puts the streams of one time step on the sublanes"
    assert l % CHUNK == 0
    n = nb * l
    tm = min(TOKEN_TILE, l)
    assert l % tm == 0
    nlt = l // tm
    past = 0 if cache is None else cache["sb_k"].shape[1]
    assert pos0 == past
    xf = x.reshape(n, D_MODEL)
    bf = lambda a: a.astype(BF16)

    xf = _ffn(xf, p["norm_ffn"][0, 0], bf(p["ffn_w_gate"][0, 0]), bf(p["ffn_w_up"][0, 0]),
              bf(p["ffn_w_down"][0, 0]), tm=tm, nlt=nlt)
    ew = _even_weights(p["even_w_in"][0], p["mlstm_b_igate"][0], p["mlstm_b_fgate"][0],
                       p["mla_q_norm"][0], p["mla_kv_norm"][0], p["mla_w_uq"][0], p["mla_w_ukv"][0])
    cos, sin = _rope_tables(pos0 + jnp.arange(l))
    qm, km, vm, om, gates, q, kn, v, kr128, lat, kr = _even_in(
        xf, p["norm_mix"][0], ew, cos, sin, tm=tm, nlt=nlt)

    if cache is None:
        c0 = jnp.zeros((nb, M_HEADS, M_DH, M_DH), F32)
        n0 = jnp.zeros((nb, M_HEADS, 1, M_DH), F32)
        m0 = jnp.zeros((nb, M_HEADS, 1, LANES), F32)
    else:
        c0 = cache["mlstm_c"].astype(F32)
        n0 = cache["mlstm_n"].astype(F32).reshape(nb, M_HEADS, 1, M_DH)
        m0 = jnp.broadcast_to(cache["mlstm_m"].astype(F32)[..., None, None], (nb, M_HEADS, 1, LANES))
    r3 = lambda a: a.reshape(nb, l, a.shape[-1])
    h_m, c_new, n_new, m_new = _mlstm(r3(qm), r3(km), r3(vm), r3(om), r3(gates),
                                      p["mlstm_out_norm"][0], c0, n0, m0, tl=tm)

    kn3, v3, kr3 = r3(kn), r3(v), r3(kr128)
    if cache is not None:
        ckn, cv = _kv_expand(cache["mla_latent"].astype(F32).reshape(nb * past, KV_LORA),
                             ew["wuk"], ew["wuv"])
        ckr = jnp.pad(cache["mla_krope"], ((0, 0), (0, 0), (0, LANES - A_ROPE))).astype(BF16)
        kn3 = jnp.concatenate([ckn.reshape(nb, past, -1), kn3], axis=1)
        v3 = jnp.concatenate([cv.reshape(nb, past, -1), v3], axis=1)
        kr3 = jnp.concatenate([ckr, kr3], axis=1)
    mla_tq = min(MLA_TILE, l)
    mla_tk = mla_tq if cache is None else -(-(past + l) // LANES) * LANES
    h_a = _mla_attention(r3(q), _pad_keys(kn3, mla_tk), _pad_keys(kr3, mla_tk), _pad_keys(v3, mla_tk),
                         tq=mla_tq, tk=mla_tk, offset=past)

    w_out = bf(p["even_w_out"][0])
    xf = _ffn(xf, p["norm_ffn"][0, 1], bf(p["ffn_w_gate"][0, 1]), bf(p["ffn_w_up"][0, 1]),
              bf(p["ffn_w_down"][0, 1]),
              mix=(h_m.reshape(n, M_W), h_a.reshape(n, -1), w_out[:M_W], w_out[M_W:], False),
              tm=tm, nlt=nlt)

    xf = _ffn(xf, p["norm_ffn"][1, 0], bf(p["ffn_w_gate"][1, 0]), bf(p["ffn_w_up"][1, 0]),
              bf(p["ffn_w_down"][1, 0]), tm=tm, nlt=nlt)
    w_in = p["odd_w_in"][0]
    ow = dict(wu=bf(w_in[:, :S5_W]), wq=bf(w_in[:, S5_W:S5_W + SB_W]),
              wk=bf(w_in[:, S5_W + SB_W:S5_W + 2 * SB_W]), wv=bf(w_in[:, S5_W + 2 * SB_W:]))
    u_tm, qs, ks, vs, ksb, vsb = _odd_in(xf, p["norm_mix"][1], ow, tm=tm, nlt=nlt, nb=nb)

    sw = _s5_weights(p["s5_A_re"][0], p["s5_A_im"][0], p["s5_B_re"][0], p["s5_B_im"][0],
                     p["s5_C_re"][0], p["s5_C_im"][0], p["s5_D"][0], p["s5_log_dt"][0], p["s5_w_glu"][0])
    if cache is None:
        x0r = jnp.zeros((nb, S5_STATE), F32)
        x0i = jnp.zeros((nb, S5_STATE), F32)
    else:
        x0r = cache["s5_re"].astype(F32).reshape(nb, S5_STATE)
        x0i = cache["s5_im"].astype(F32).reshape(nb, S5_STATE)
    h_s, xr_new, xi_new = _s5(u_tm.reshape(l * nb, S5_W), sw, x0r, x0i, nb=nb, tt=min(S5_TT, l))

    k3, v3 = r3(ksb), r3(vsb)
    if cache is not None:
        k3 = jnp.concatenate([bf(cache["sb_k"].reshape(nb, past, SB_W)), k3], axis=1)
        v3 = jnp.concatenate([bf(cache["sb_v"].reshape(nb, past, SB_W)), v3], axis=1)
    sb_tq = min(SB_TQ, l)
    h_b = _sb_attention(r3(qs), _pad_keys(k3, SB_TK), _pad_keys(v3, SB_TK), tq=sb_tq, tk=SB_TK,
                        offset=past)

    w_out = bf(p["odd_w_out"][0])
    y = _ffn(xf, p["norm_ffn"][1, 1], bf(p["ffn_w_gate"][1, 1]), bf(p["ffn_w_up"][1, 1]),
             bf(p["ffn_w_down"][1, 1]),
             mix=(h_s.reshape(l, nb * S5_W), h_b.reshape(n, SB_W), w_out[:S5_W], w_out[S5_W:], True),
             final_gain=p["norm_final"], tm=tm, nlt=nlt)

    return (y.reshape(nb, l, D_MODEL),
            lat.reshape(1, nb, l, KV_LORA), kr.reshape(1, nb, l, A_ROPE),
            c_new[None], n_new.reshape(1, nb, M_HEADS, M_DH), m_new[:, :, 0, 0][None],
            xr_new.reshape(1, nb, S5_GROUPS, S5_P), xi_new.reshape(1, nb, S5_GROUPS, S5_P),
            ks.reshape(1, nb, l, SB_HEADS, SB_DH), vs.reshape(1, nb, l, SB_HEADS, SB_DH))


def kernel(x_prompt, x_sample, cache_mla_latent, cache_mla_krope, state_mlstm_C, state_mlstm_n, state_mlstm_m, state_s5_re, state_s5_im, cache_sb_k, cache_sb_v, norm_ffn, norm_mix, norm_final, ffn_w_gate, ffn_w_up, ffn_w_down, even_w_in, even_w_out, mlstm_b_igate, mlstm_b_fgate, mlstm_out_norm, mla_q_norm, mla_kv_norm, mla_w_uq, mla_w_ukv, odd_w_in, odd_w_out, s5_A_re, s5_A_im, s5_B_re, s5_B_im, s5_C_re, s5_C_im, s5_D, s5_log_dt, s5_w_glu):
    p = dict(norm_ffn=norm_ffn, norm_mix=norm_mix, norm_final=norm_final, ffn_w_gate=ffn_w_gate,
             ffn_w_up=ffn_w_up, ffn_w_down=ffn_w_down, even_w_in=even_w_in, even_w_out=even_w_out,
             mlstm_b_igate=mlstm_b_igate, mlstm_b_fgate=mlstm_b_fgate, mlstm_out_norm=mlstm_out_norm,
             mla_q_norm=mla_q_norm, mla_kv_norm=mla_kv_norm, mla_w_uq=mla_w_uq, mla_w_ukv=mla_w_ukv,
             odd_w_in=odd_w_in, odd_w_out=odd_w_out, s5_A_re=s5_A_re, s5_A_im=s5_A_im, s5_B_re=s5_B_re,
             s5_B_im=s5_B_im, s5_C_re=s5_C_re, s5_C_im=s5_C_im, s5_D=s5_D, s5_log_dt=s5_log_dt,
             s5_w_glu=s5_w_glu)
    past = cache_sb_k.shape[2]
    cache = dict(mla_latent=cache_mla_latent[0], mla_krope=cache_mla_krope[0],
                 mlstm_c=state_mlstm_C[0], mlstm_n=state_mlstm_n[0], mlstm_m=state_mlstm_m[0],
                 s5_re=state_s5_re[0], s5_im=state_s5_im[0],
                 sb_k=cache_sb_k[0], sb_v=cache_sb_v[0])
    out_p = _trunk(x_prompt, 0, None, p)
    out_s = _trunk(x_sample, past, cache, p)
    return (out_p[0], out_s[0]) + tuple(out_p[1:]) + tuple(out_s[1:])
```

```python
import functools
import math

import jax
import jax.numpy as jnp
import numpy as np
from jax import lax
from jax.experimental import pallas as pl
from jax.experimental.pallas import tpu as pltpu

F32 = jnp.float32
BF16 = jnp.bfloat16

D_MODEL = 1024
CHUNK = 64
D_FF = 2816
EPS = 1e-6
M_HEADS = 4
M_DH = 128
M_W = M_HEADS * M_DH
A_HEADS = 4
A_NOPE = 128
A_ROPE = 64
A_VDIM = 128
Q_LORA = 384
KV_LORA = 256
ROPE_THETA = 10000.0
MLA_SCALE = (A_NOPE + A_ROPE) ** -0.5
S5_W = 512
S5_GSIZE = 16
S5_GROUPS = S5_W // S5_GSIZE
S5_P = 64
S5_STATE = S5_GROUPS * S5_P
SB_HEADS = 8
SB_DH = 64
SB_W = SB_HEADS * SB_DH

LANES = 128
SUBLANES = 8
VMEM_LIMIT_BYTES = 56 * 1024 * 1024

TOKEN_TILE = 512
FF_CHUNK = 256
MLA_TILE = 512
SB_TILE = 128
S5_TT = 64
MLSTM_STREAMS = 2
SB_LOG_FLOOR = -24.0
LOG2E = math.log2(math.e)

HIGHEST = lax.Precision.HIGHEST


def _cparams(*sem):
    return pltpu.CompilerParams(dimension_semantics=sem, vmem_limit_bytes=VMEM_LIMIT_BYTES)


def _bdot(a, b):
    return jnp.dot(a.astype(BF16), b.astype(BF16), preferred_element_type=F32)


def _bdot_nt(a, b):
    return lax.dot_general(a.astype(BF16), b.astype(BF16), (((1,), (1,)), ((), ())),
                           preferred_element_type=F32)


def _rms(x, g):
    return x * lax.rsqrt(jnp.mean(x * x, axis=-1, keepdims=True) + EPS) * g


def _sigmoid(x):
    return 1.0 / (1.0 + jnp.exp(-x))


def _log_sigmoid(x):
    return jnp.minimum(x, 0.0) - jnp.log(1.0 + jnp.exp(-jnp.abs(x)))


def _const_spec(shape):
    nd = len(shape)
    return pl.BlockSpec(shape, lambda *_: (0,) * nd)


def _ffn_kernel(*refs, with_mix, with_final):
    it = iter(refs)
    x_ref = next(it)
    if with_mix:
        a_ref, b_ref, wa_ref, wb_ref = next(it), next(it), next(it), next(it)
    g_ref, wg_ref, wu_ref, wd_ref = next(it), next(it), next(it), next(it)
    gf_ref = next(it) if with_final else None
    o_ref = next(it)
    acc_ref = next(it)

    x = x_ref[...]
    if with_mix:
        x = x + _bdot(a_ref[...], wa_ref[...]) + _bdot(b_ref[...], wb_ref[...])
    h = _rms(x, g_ref[...]).astype(BF16)
    acc_ref[...] = jnp.zeros_like(acc_ref)
    for c in range(D_FF // FF_CHUNK):
        sl = slice(c * FF_CHUNK, (c + 1) * FF_CHUNK)
        g = jnp.dot(h, wg_ref[:, sl], preferred_element_type=F32)
        u = jnp.dot(h, wu_ref[:, sl], preferred_element_type=F32)
        a = (g * _sigmoid(g) * u).astype(BF16)
        acc_ref[...] += jnp.dot(a, wd_ref[sl, :], preferred_element_type=F32)
    y = x + 0.5 * acc_ref[...]
    if with_final:
        y = _rms(y, gf_ref[...])
    o_ref[...] = y


def _ffn(x, gain, wg, wu, wd, *, mix=None, final_gain=None, tm):
    n = x.shape[0]
    row = lambda i: (i, 0)
    args, specs = [x], [pl.BlockSpec((tm, D_MODEL), row)]
    if mix is not None:
        a, b, wa, wb = mix
        args += [a, b, wa, wb]
        specs += [pl.BlockSpec((tm, a.shape[-1]), row), pl.BlockSpec((tm, b.shape[-1]), row),
                  _const_spec(wa.shape), _const_spec(wb.shape)]
    args += [gain.reshape(1, D_MODEL), wg, wu, wd]
    specs += [_const_spec((1, D_MODEL)), _const_spec(wg.shape), _const_spec(wu.shape),
              _const_spec(wd.shape)]
    if final_gain is not None:
        args.append(final_gain.reshape(1, D_MODEL))
        specs.append(_const_spec((1, D_MODEL)))
    return pl.pallas_call(
        functools.partial(_ffn_kernel, with_mix=mix is not None, with_final=final_gain is not None),
        grid=(n // tm,),
        in_specs=specs,
        out_specs=pl.BlockSpec((tm, D_MODEL), row),
        out_shape=jax.ShapeDtypeStruct((n, D_MODEL), F32),
        scratch_shapes=[pltpu.VMEM((tm, D_MODEL), F32)],
        compiler_params=_cparams("parallel"),
        name="ffn",
    )(*args)


def _even_in_kernel(x_ref, gain_ref, wm_ref, wg_ref, bg_ref, wcq_ref, wckv_ref, wkra_ref, wkrb_ref,
                    qn_ref, kvn_ref, wq_ref, ws_ref, wuk_ref, wuv_ref, cos_ref, sin_ref,
                    qm_o, km_o, vm_o, om_o, gates_o, q_o, kn_o, v_o, kr128_o, lat_o, kr_o):
    h = _rms(x_ref[...], gain_ref[...]).astype(BF16)
    m = jnp.dot(h, wm_ref[...], preferred_element_type=F32)
    qm_o[...] = m[:, :M_W].astype(BF16)
    km_o[...] = (m[:, M_W:2 * M_W] * M_DH ** -0.5).astype(BF16)
    vm_o[...] = m[:, 2 * M_W:3 * M_W]
    om_o[...] = m[:, 3 * M_W:]

    g = jnp.dot(h, wg_ref[...], preferred_element_type=F32) + bg_ref[...]
    lane = lax.broadcasted_iota(jnp.int32, g.shape, 1)
    gates_o[...] = jnp.where(lane < M_HEADS, g, _log_sigmoid(g))

    cos, sin = cos_ref[...], sin_ref[...]
    cq = jnp.dot(h, wcq_ref[...], preferred_element_type=F32)
    cqn = _rms(cq, qn_ref[...]).astype(BF16)
    qf = jnp.dot(cqn, wq_ref[...], preferred_element_type=F32)
    qs = jnp.dot(cqn, ws_ref[...], preferred_element_type=F32)
    q_scale = MLA_SCALE * LOG2E
    for hh in range(A_HEADS):
        base = hh * 2 * LANES
        q_o[:, base:base + LANES] = (qf[:, base:base + LANES] * q_scale).astype(BF16)
        rot = qf[:, base + LANES:base + 2 * LANES] * cos + qs[:, hh * LANES:(hh + 1) * LANES] * sin
        q_o[:, base + LANES:base + 2 * LANES] = (rot * q_scale).astype(BF16)

    ckv = jnp.dot(h, wckv_ref[...], preferred_element_type=F32)
    lat = _rms(ckv, kvn_ref[...])
    lat_o[...] = lat
    latb = lat.astype(BF16)
    kn_o[...] = jnp.dot(latb, wuk_ref[...], preferred_element_type=F32).astype(BF16)
    v_o[...] = jnp.dot(latb, wuv_ref[...], preferred_element_type=F32).astype(BF16)

    kr = (jnp.dot(h, wkra_ref[...], preferred_element_type=F32) * cos
          + jnp.dot(h, wkrb_ref[...], preferred_element_type=F32) * sin)
    kr128_o[...] = kr.astype(BF16)
    kr_o[...] = kr[:, :A_ROPE]


def _even_in(x, gain, w, cos, sin, *, tm):
    n = x.shape[0]
    row = lambda i: (i, 0)
    n_pos_tiles = cos.shape[0] // tm
    pos = lambda i: (i % n_pos_tiles, 0)
    weights = [w["wm"], w["wg"], w["bg"], w["wcq"], w["wckv"], w["wkra"], w["wkrb"], w["qn"], w["kvn"],
               w["wq"], w["ws"], w["wuk"], w["wuv"]]
    in_specs = ([pl.BlockSpec((tm, D_MODEL), row), _const_spec((1, D_MODEL))]
                + [_const_spec(a.shape) for a in weights]
                + [pl.BlockSpec((tm, LANES), pos), pl.BlockSpec((tm, LANES), pos)])
    outs = [(M_W, BF16), (M_W, BF16), (M_W, F32), (M_W, F32), (LANES, F32),
            (A_HEADS * 2 * LANES, BF16), (A_HEADS * A_NOPE, BF16), (A_HEADS * A_VDIM, BF16),
            (LANES, BF16), (KV_LORA, F32), (A_ROPE, F32)]
    return pl.pallas_call(
        _even_in_kernel,
        grid=(n // tm,),
        in_specs=in_specs,
        out_specs=[pl.BlockSpec((tm, c), row) for c, _ in outs],
        out_shape=[jax.ShapeDtypeStruct((n, c), dt) for c, dt in outs],
        compiler_params=_cparams("parallel"),
        name="even_in",
    )(x, gain.reshape(1, D_MODEL), *weights, cos, sin)


def _kv_expand_kernel(lat_ref, wuk_ref, wuv_ref, kn_o, v_o):
    latb = lat_ref[...].astype(BF16)
    kn_o[...] = jnp.dot(latb, wuk_ref[...], preferred_element_type=F32).astype(BF16)
    v_o[...] = jnp.dot(latb, wuv_ref[...], preferred_element_type=F32).astype(BF16)


def _kv_expand(lat, wuk, wuv):
    n = lat.shape[0]
    tm = min(TOKEN_TILE, n)
    row = lambda i: (i, 0)
    return pl.pallas_call(
        _kv_expand_kernel,
        grid=(n // tm,),
        in_specs=[pl.BlockSpec((tm, KV_LORA), row), _const_spec(wuk.shape), _const_spec(wuv.shape)],
        out_specs=[pl.BlockSpec((tm, wuk.shape[1]), row), pl.BlockSpec((tm, wuv.shape[1]), row)],
        out_shape=[jax.ShapeDtypeStruct((n, wuk.shape[1]), BF16),
                   jax.ShapeDtypeStruct((n, wuv.shape[1]), BF16)],
        compiler_params=_cparams("parallel"),
        name="kv_expand",
    )(lat, wuk, wuv)


def _mlstm_kernel(qm_ref, km_ref, vm_ref, om_ref, g_ref, onorm_ref, c0_ref, n0_ref, m0_ref,
                  h_ref, cout_ref, nout_ref, mout_ref, c_sc, n_sc, m_sc, *, n_chunks, n_streams):
    lt = pl.program_id(1)

    @pl.when(lt == 0)
    def _():
        c_sc[...] = c0_ref[...]
        n_sc[...] = n0_ref[...]
        m_sc[...] = m0_ref[...]

    row = lax.broadcasted_iota(jnp.int32, (CHUNK, CHUNK), 0)
    col = lax.broadcasted_iota(jnp.int32, (CHUNK, CHUNK), 1)
    causal = col <= row
    ltri = causal.astype(F32)
    utri = (row <= col).astype(F32)

    def chunk(c, carry):
        r0 = pl.multiple_of(c * CHUNK, CHUNK)
        rows = pl.ds(r0, CHUNK)
        chains = [(bb, hh) for bb in range(n_streams) for hh in range(M_HEADS)]
        hsl = lambda hh: slice(hh * M_DH, (hh + 1) * M_DH)
        gate = {}
        for bb in range(n_streams):
            gates = g_ref[bb, rows, :]
            gates_t = gates.T
            cum_col = jnp.dot(ltri, gates, precision=HIGHEST, preferred_element_type=F32)
            cum_row = jnp.dot(gates_t, utri, precision=HIGHEST, preferred_element_type=F32)
            gate[bb] = (gates, gates_t, cum_col, cum_row)
        qk, qc = {}, {}
        for ch in chains:
            bb, hh = ch
            q = qm_ref[bb, rows, hsl(hh)]
            qk[ch] = _bdot_nt(q, km_ref[bb, rows, hsl(hh)])
            qc[ch] = _bdot_nt(q, c_sc[bb, hh])
        st = {}
        for ch in chains:
            bb, hh = ch
            gates, gates_t, cum_col, cum_row = gate[bb]
            ig_row = gates_t[hh:hh + 1, :]
            ig_col = gates[:, hh:hh + 1]
            b_row = cum_row[M_HEADS + hh:M_HEADS + hh + 1, :]
            b_col = cum_col[:, M_HEADS + hh:M_HEADS + hh + 1]
            m_prev = m_sc[bb, hh][:, :1]
            log_d = jnp.where(causal, b_col - b_row + ig_row, -jnp.inf)
            m_inter = b_col + m_prev
            m_t = jnp.maximum(m_inter, jnp.max(log_d, axis=-1, keepdims=True))
            s = qk[ch] * jnp.exp(log_d - m_t)
            w_inter = jnp.exp(m_inter - m_t)
            m_new = m_t[CHUNK - 1:CHUNK, :]
            b_last = b_col[CHUNK - 1:CHUNK, :]
            w_s = jnp.exp(b_last - b_col + ig_col - m_new)
            decay = jnp.exp(b_last + m_prev - m_new)
            sv = _bdot(s, vm_ref[bb, rows, hsl(hh)])
            st[ch] = (s, sv, w_inter, m_t, m_new, w_s, decay)
        for ch in chains:
            bb, hh = ch
            s, sv, w_inter, m_t, m_new, w_s, decay = st[ch]
            q = qm_ref[bb, rows, hsl(hh)]
            k = km_ref[bb, rows, hsl(hh)]
            v = vm_ref[bb, rows, hsl(hh)]
            c_h = c_sc[bb, hh]
            n_h = n_sc[bb, hh]
            num = w_inter * qc[ch] + sv
            den = (w_inter * jnp.sum(q.astype(F32) * n_h, axis=-1, keepdims=True)
                   + jnp.sum(s, axis=-1, keepdims=True))
            hv = num / jnp.maximum(jnp.abs(den), jnp.exp(-m_t))
            wv_t = (w_s * v).T
            c_sc[bb, hh] = decay * c_h + _bdot(wv_t, k)
            n_sc[bb, hh] = decay * n_h + jnp.sum(w_s * k.astype(F32), axis=0, keepdims=True)
            m_sc[bb, hh] = jnp.broadcast_to(m_new, (1, LANES))
            hn = _rms(hv, onorm_ref[:, hsl(hh)])
            h_ref[bb, rows, hsl(hh)] = (_sigmoid(om_ref[bb, rows, hsl(hh)]) * hn).astype(BF16)
        return carry

    lax.fori_loop(0, n_chunks, chunk, 0)

    @pl.when(lt == pl.num_programs(1) - 1)
    def _():
        cout_ref[...] = c_sc[...]
        nout_ref[...] = n_sc[...]
        mout_ref[...] = m_sc[...]


def _mlstm(qm, km, vm, om, gates, onorm, c0, n0, m0, *, tl):
    b, l, _ = qm.shape
    ns = MLSTM_STREAMS
    tile = lambda i, j: (i, j, 0)
    st4 = lambda i, j: (i, 0, 0, 0)
    state = [((ns, M_HEADS, M_DH, M_DH), (b, M_HEADS, M_DH, M_DH)),
             ((ns, M_HEADS, 1, M_DH), (b, M_HEADS, 1, M_DH)),
             ((ns, M_HEADS, 1, LANES), (b, M_HEADS, 1, LANES))]
    return pl.pallas_call(
        functools.partial(_mlstm_kernel, n_chunks=tl // CHUNK, n_streams=ns),
        grid=(b // ns, l // tl),
        in_specs=[pl.BlockSpec((ns, tl, M_W), tile)] * 4
        + [pl.BlockSpec((ns, tl, LANES), tile), _const_spec((1, M_W))]
        + [pl.BlockSpec(blk, st4) for blk, _ in state],
        out_specs=[pl.BlockSpec((ns, tl, M_W), tile)] + [pl.BlockSpec(blk, st4) for blk, _ in state],
        out_shape=[jax.ShapeDtypeStruct((b, l, M_W), BF16)]
        + [jax.ShapeDtypeStruct(full, F32) for _, full in state],
        scratch_shapes=[pltpu.VMEM(blk, F32) for blk, _ in state],
        compiler_params=_cparams("parallel", "arbitrary"),
        name="mlstm",
    )(qm, km, vm, om, gates, onorm.reshape(1, M_W), c0, n0, m0)


def _mla_kernel(qi_ref, kj_ref, flag_ref, q_ref, kn_ref, kr_ref, vt_ref, o_ref, m_sc, l_sc, acc_sc,
                *, tq, tk, offset):
    p = pl.program_id(1)
    i = qi_ref[p]
    j = kj_ref[p]
    flags = flag_ref[p]

    @pl.when(j == 0)
    def _():
        m_sc[...] = jnp.full_like(m_sc, -jnp.inf)
        l_sc[...] = jnp.zeros_like(l_sc)
        acc_sc[...] = jnp.zeros_like(acc_sc)

    def sweep(masked):
        kr = kr_ref[0]
        scores = []
        for hh in range(A_HEADS):
            q = q_ref[0, :, hh * 2 * LANES:(hh + 1) * 2 * LANES]
            k = jnp.concatenate([kn_ref[0, :, hh * A_NOPE:(hh + 1) * A_NOPE], kr], axis=-1)
            scores.append(_bdot_nt(k, q))
        if masked:
            k_chunk = (j * tk + lax.broadcasted_iota(jnp.int32, (tk, tq), 0)) // CHUNK
            q_chunk = (offset + i * tq + lax.broadcasted_iota(jnp.int32, (tk, tq), 1)) // CHUNK
            visible = k_chunk <= q_chunk
            scores = [jnp.where(visible, s, -jnp.inf) for s in scores]
        for hh in range(A_HEADS):
            s = scores[hh]
            m_prev = m_sc[hh]
            m_new = jnp.maximum(m_prev, jnp.max(s, axis=0, keepdims=True))
            alpha = jnp.exp2(m_prev - m_new)
            pr = jnp.exp2(s - m_new)
            l_sc[hh] = alpha * l_sc[hh] + jnp.sum(pr, axis=0, keepdims=True)
            pv = jnp.dot(vt_ref[0, hh * A_VDIM:(hh + 1) * A_VDIM, :], pr.astype(BF16),
                         preferred_element_type=F32)
            acc_sc[hh] = alpha * acc_sc[hh] + pv
            m_sc[hh] = m_new

    @pl.when((flags & 2) != 0)
    def _():
        sweep(True)

    @pl.when((flags & 2) == 0)
    def _():
        sweep(False)

    @pl.when((flags & 1) != 0)
    def _():
        for hh in range(A_HEADS):
            o_ref[0, :, hh * A_VDIM:(hh + 1) * A_VDIM] = (acc_sc[hh] / l_sc[hh]).T.astype(BF16)


def _transpose_kernel(x_ref, o_ref):
    o_ref[0] = x_ref[0].astype(F32).T.astype(o_ref.dtype)


def _transpose_tiles(x, t):
    b, l, c = x.shape
    return pl.pallas_call(
        _transpose_kernel,
        grid=(b, l // t),
        in_specs=[pl.BlockSpec((1, t, c), lambda bb, i: (bb, i, 0))],
        out_specs=pl.BlockSpec((1, c, t), lambda bb, i: (bb, 0, i)),
        out_shape=jax.ShapeDtypeStruct((b, c, l), x.dtype),
        compiler_params=_cparams("parallel", "parallel"),
        name="transpose_tiles",
    )(x)


def _mla_pairs(lq, lk, tq, tk, offset):
    qi, kj, flags = [], [], []
    for i in range(lq // tq):
        first_row, last_row = offset + i * tq, offset + (i + 1) * tq - 1
        k_end = (last_row // CHUNK + 1) * CHUNK
        k_all = (first_row // CHUNK + 1) * CHUNK
        nj = min(-(-k_end // tk), lk // tk)
        for j in range(nj):
            qi.append(i)
            kj.append(j)
            flags.append(int(j == nj - 1) + 2 * int((j + 1) * tk > k_all))
    return (np.asarray(qi, np.int32), np.asarray(kj, np.int32), np.asarray(flags, np.int32))


def _mla_attention(q, kn, kr, v, *, tq, tk, offset):
    b, lq, _ = q.shape
    lk = kn.shape[1]
    vt = _transpose_tiles(v, tk)
    qi, kj, flags = _mla_pairs(lq, lk, tq, tk, offset)
    qmap = lambda bb, p, qi_r, kj_r, flag_r: (bb, qi_r[p], 0)
    kmap = lambda bb, p, qi_r, kj_r, flag_r: (bb, kj_r[p], 0)
    grid_spec = pltpu.PrefetchScalarGridSpec(
        num_scalar_prefetch=3,
        grid=(b, len(qi)),
        in_specs=[pl.BlockSpec((1, tq, q.shape[-1]), qmap),
                  pl.BlockSpec((1, tk, kn.shape[-1]), kmap),
                  pl.BlockSpec((1, tk, kr.shape[-1]), kmap),
                  pl.BlockSpec((1, vt.shape[1], tk), lambda bb, p, qi_r, kj_r, flag_r: (bb, 0, kj_r[p]))],
        out_specs=pl.BlockSpec((1, tq, A_HEADS * A_VDIM), qmap),
        scratch_shapes=[pltpu.VMEM((A_HEADS, 1, tq), F32),
                        pltpu.VMEM((A_HEADS, 1, tq), F32),
                        pltpu.VMEM((A_HEADS, A_VDIM, tq), F32)],
    )
    return pl.pallas_call(
        functools.partial(_mla_kernel, tq=tq, tk=tk, offset=offset),
        grid_spec=grid_spec,
        out_shape=jax.ShapeDtypeStruct((b, lq, A_HEADS * A_VDIM), BF16),
        compiler_params=_cparams("parallel", "arbitrary"),
        name="mla_attention",
    )(jnp.asarray(qi), jnp.asarray(kj), jnp.asarray(flags), q, kn, kr, vt)


def _odd_in_kernel(x_ref, gain_ref, wu_ref, wq_ref, wk_ref, wv_ref, u_o, q_o, k_o, v_o, kb_o, vb_o):
    h = _rms(x_ref[...], gain_ref[...]).astype(BF16)
    u_o[...] = jnp.dot(h, wu_ref[...], preferred_element_type=F32)
    q_o[...] = (jnp.dot(h, wq_ref[...], preferred_element_type=F32) * SB_DH ** -0.5).astype(BF16)
    k = jnp.dot(h, wk_ref[...], preferred_element_type=F32)
    v = jnp.dot(h, wv_ref[...], preferred_element_type=F32)
    k_o[...] = k
    v_o[...] = v
    kb_o[...] = k.astype(BF16)
    vb_o[...] = v.astype(BF16)


def _odd_in(x, gain, w, *, tm):
    n = x.shape[0]
    row = lambda i: (i, 0)
    weights = [w["wu"], w["wq"], w["wk"], w["wv"]]
    return pl.pallas_call(
        _odd_in_kernel,
        grid=(n // tm,),
        in_specs=[pl.BlockSpec((tm, D_MODEL), row), _const_spec((1, D_MODEL))]
        + [_const_spec(a.shape) for a in weights],
        out_specs=[pl.BlockSpec((tm, S5_W), row)] + [pl.BlockSpec((tm, SB_W), row)] * 5,
        out_shape=[jax.ShapeDtypeStruct((n, S5_W), F32),
                   jax.ShapeDtypeStruct((n, SB_W), BF16),
                   jax.ShapeDtypeStruct((n, SB_W), F32),
                   jax.ShapeDtypeStruct((n, SB_W), F32),
                   jax.ShapeDtypeStruct((n, SB_W), BF16),
                   jax.ShapeDtypeStruct((n, SB_W), BF16)],
        compiler_params=_cparams("parallel"),
        name="odd_in",
    )(x, gain.reshape(1, D_MODEL), *weights)


S5_BLOCKS = 4
S5_BLK_IN = S5_W // S5_BLOCKS
S5_BLK_ST = S5_STATE // S5_BLOCKS


def _s5_kernel(u_ref, bbr_ref, bbi_ref, ar_ref, ai_ref, ccr_ref, cci_ref, d_ref, wglu_ref,
               x0r_ref, x0i_ref, hs_o, xr_o, xi_o, u_tm, hs_tm, xr_buf, xi_buf, xr_st, xi_st,
               *, tt, nb):
    t = pl.program_id(0)

    @pl.when(t == 0)
    def _():
        xr_st[...] = x0r_ref[...]
        xi_st[...] = x0i_ref[...]

    for bb in range(nb):
        for k in range(S5_BLOCKS):
            u_tm[k, pl.ds(bb, tt, stride=nb), :] = u_ref[bb, :, k * S5_BLK_IN:(k + 1) * S5_BLK_IN]

    for k in range(S5_BLOCKS):
        cs = slice(k * S5_BLK_ST, (k + 1) * S5_BLK_ST)
        ub = u_tm[k].astype(BF16)
        xr_buf[:, cs] = jnp.dot(ub, bbr_ref[k], preferred_element_type=F32)
        xi_buf[:, cs] = jnp.dot(ub, bbi_ref[k], preferred_element_type=F32)

    for k in range(S5_BLOCKS):
        cs = slice(k * S5_BLK_ST, (k + 1) * S5_BLK_ST)
        ar = jnp.broadcast_to(ar_ref[:, cs], (nb, S5_BLK_ST))
        ai = jnp.broadcast_to(ai_ref[:, cs], (nb, S5_BLK_ST))

        def step(s, carry, cs=cs, ar=ar, ai=ai):
            xr, xi = carry
            rows = pl.ds(pl.multiple_of(s * nb, nb), nb)
            nxr = ar * xr - ai * xi + xr_buf[rows, cs]
            nxi = ar * xi + ai * xr + xi_buf[rows, cs]
            xr_buf[rows, cs] = nxr
            xi_buf[rows, cs] = nxi
            return nxr, nxi

        xr, xi = lax.fori_loop(0, tt, step, (xr_st[:, cs], xi_st[:, cs]), unroll=8)
        xr_st[:, cs] = xr
        xi_st[:, cs] = xi

    gs = []
    for k in range(S5_BLOCKS):
        cs = slice(k * S5_BLK_ST, (k + 1) * S5_BLK_ST)
        y = (_bdot(xr_buf[:, cs], ccr_ref[k]) + _bdot(xi_buf[:, cs], cci_ref[k])
             + d_ref[:, k * S5_BLK_IN:(k + 1) * S5_BLK_IN] * u_tm[k])
        gs.append(jax.nn.gelu(y, approximate=True))
    g = jnp.concatenate(gs, axis=-1)
    hs = g * _sigmoid(_bdot(g, wglu_ref[...]))
    for k in range(S5_BLOCKS):
        hs_tm[k] = hs[:, k * S5_BLK_IN:(k + 1) * S5_BLK_IN]
    for bb in range(nb):
        for k in range(S5_BLOCKS):
            hs_o[bb, :, k * S5_BLK_IN:(k + 1) * S5_BLK_IN] = (
                hs_tm[k, pl.ds(bb, tt, stride=nb), :].astype(BF16))

    @pl.when(t == pl.num_programs(0) - 1)
    def _():
        xr_o[...] = xr_st[...]
        xi_o[...] = xi_st[...]


def _s5(u, w, x0r, x0i, *, tt):
    nb, l, _ = u.shape
    blk = lambda t: (0, t, 0)
    consts = [w["bbr"], w["bbi"], w["ar"], w["ai"], w["ccr"], w["cci"], w["d"], w["wglu"], x0r, x0i]
    return pl.pallas_call(
        functools.partial(_s5_kernel, tt=tt, nb=nb),
        grid=(l // tt,),
        in_specs=[pl.BlockSpec((nb, tt, S5_W), blk)] + [_const_spec(a.shape) for a in consts],
        out_specs=[pl.BlockSpec((nb, tt, S5_W), blk), _const_spec((nb, S5_STATE)),
                   _const_spec((nb, S5_STATE))],
        out_shape=[jax.ShapeDtypeStruct((nb, l, S5_W), BF16),
                   jax.ShapeDtypeStruct((nb, S5_STATE), F32),
                   jax.ShapeDtypeStruct((nb, S5_STATE), F32)],
        scratch_shapes=[pltpu.VMEM((S5_BLOCKS, tt * nb, S5_BLK_IN), F32),
                        pltpu.VMEM((S5_BLOCKS, tt * nb, S5_BLK_IN), F32),
                        pltpu.VMEM((tt * nb, S5_STATE), F32), pltpu.VMEM((tt * nb, S5_STATE), F32),
                        pltpu.VMEM((nb, S5_STATE), F32), pltpu.VMEM((nb, S5_STATE), F32)],
        compiler_params=_cparams("arbitrary"),
        name="s5",
    )(u, *consts)


def _sb_kernel(q_ref, k_ref, v_ref, o_ref, r_sc, acc_sc, qh_sc, *, tq, tk, offset):
    i = pl.program_id(1)
    half = lax.broadcasted_iota(jnp.int32, (1, LANES), 1) // SB_DH
    for hh in range(SB_HEADS):
        pair, sub = divmod(hh, 2)
        qq = q_ref[0, :, pair * LANES:(pair + 1) * LANES]
        qh_sc[hh] = jnp.where(half == sub, qq, jnp.zeros_like(qq))
    r_sc[...] = jnp.zeros_like(r_sc)
    acc_sc[...] = jnp.zeros_like(acc_sc)

    jrow = lax.broadcasted_iota(jnp.int32, (tk, tk), 0)
    jcol = lax.broadcasted_iota(jnp.int32, (tk, tk), 1)
    later = (jrow > jcol).astype(BF16)
    j_diag = (offset + i * tq) // tk

    def tile(j, masked):
        k0 = pl.multiple_of(j * tk, tk)
        if masked:
            q_pos = offset + i * tq + lax.broadcasted_iota(jnp.int32, (tq, tk), 0)
            causal = (k0 + lax.broadcasted_iota(jnp.int32, (tq, tk), 1)) < q_pos
        kk = [k_ref[0, pl.ds(k0, tk), pr * LANES:(pr + 1) * LANES] for pr in range(SB_HEADS // 2)]
        vv = [v_ref[0, pl.ds(k0, tk), pr * LANES:(pr + 1) * LANES] for pr in range(SB_HEADS // 2)]
        zs = [_bdot_nt(qh_sc[hh], kk[hh // 2]) for hh in range(SB_HEADS)]
        l1s, lbs, cums = [], [], []
        for hh in range(SB_HEADS):
            z = zs[hh]
            sp = jnp.maximum(z, 0.0) + jnp.log(1.0 + jnp.exp(-jnp.abs(z)))
            l1 = -sp
            lb = z - sp
            if masked:
                l1 = jnp.where(causal, l1, 0.0)
            hi = l1.astype(BF16)
            lo = (l1 - hi.astype(F32)).astype(BF16)
            cums.append(jnp.dot(hi, later, preferred_element_type=F32)
                        + jnp.dot(lo, later, preferred_element_type=F32))
            l1s.append(l1)
            lbs.append(lb)
        r_max = jnp.full((1, 1), -jnp.inf, F32)
        for pr in range(SB_HEADS // 2):
            pv = jnp.zeros((tq, LANES), F32)
            for sub in range(2):
                hh = 2 * pr + sub
                r_old = r_sc[hh]
                log_w = lbs[hh] + cums[hh] + r_old
                if masked:
                    log_w = jnp.where(causal, log_w, -jnp.inf)
                vh = jnp.where(half == sub, vv[pr], jnp.zeros_like(vv[pr]))
                pv = pv + _bdot(jnp.exp(log_w), vh)
                r_new = r_old + jnp.sum(l1s[hh], axis=-1, keepdims=True)
                r_sc[hh] = r_new
                r_max = jnp.maximum(r_max, jnp.max(r_new, axis=0, keepdims=True))
            acc_sc[:, pr * LANES:(pr + 1) * LANES] += pv
        return r_max[0, 0]

    r_diag = tile(j_diag, True)

    def cond(carry):
        j, r_max = carry
        return jnp.logical_and(j >= 0, r_max > SB_LOG_FLOOR)

    def body(carry):
        j, _ = carry
        return j - 1, tile(j, False)

    lax.while_loop(cond, body, (j_diag - 1, r_diag))
    o_ref[0] = acc_sc[...].astype(BF16)


def _sb_attention(q, k, v, *, tq, tk, offset):
    b, lq, _ = q.shape
    lk = k.shape[1]
    assert tk % tq == 0 and offset % tk == 0 and lk % tk == 0
    return pl.pallas_call(
        functools.partial(_sb_kernel, tq=tq, tk=tk, offset=offset),
        grid=(b, lq // tq),
        in_specs=[pl.BlockSpec((1, tq, SB_W), lambda bb, i: (bb, i, 0)),
                  pl.BlockSpec((1, lk, SB_W), lambda bb, i: (bb, 0, 0)),
                  pl.BlockSpec((1, lk, SB_W), lambda bb, i: (bb, 0, 0))],
        out_specs=pl.BlockSpec((1, tq, SB_W), lambda bb, i: (bb, i, 0)),
        out_shape=jax.ShapeDtypeStruct((b, lq, SB_W), BF16),
        scratch_shapes=[pltpu.VMEM((SB_HEADS, tq, 1), F32),
                        pltpu.VMEM((tq, SB_W), F32),
                        pltpu.VMEM((SB_HEADS, tq, LANES), BF16)],
        compiler_params=_cparams("parallel", "arbitrary"),
        name="sb_attention",
    )(q, k, v)


def _pad_cols(w, width):
    return jnp.pad(w, ((0, 0), (0, width - w.shape[1])))


def _even_weights(w_in, b_ig, b_fg, q_norm, kv_norm, w_uq, w_ukv):
    o = 4 * M_W
    w_gate = w_in[:, o:o + 2 * M_HEADS]
    o += 2 * M_HEADS
    w_cq = w_in[:, o:o + Q_LORA]
    o += Q_LORA
    w_ckv = w_in[:, o:o + KV_LORA]
    o += KV_LORA
    w_kr = w_in[:, o:o + A_ROPE]
    hr = A_ROPE // 2
    wkra = _pad_cols(w_kr, LANES)
    wkrb = _pad_cols(jnp.concatenate([w_kr[:, hr:], w_kr[:, :hr]], axis=1), LANES)
    uq = w_uq.reshape(Q_LORA, A_HEADS, A_NOPE + A_ROPE)
    zeros = jnp.zeros((Q_LORA, A_HEADS, LANES - A_ROPE), w_uq.dtype)
    wq = jnp.concatenate([uq, zeros], axis=-1).reshape(Q_LORA, A_HEADS * 2 * LANES)
    ws = jnp.concatenate([uq[..., A_NOPE + hr:], uq[..., A_NOPE:A_NOPE + hr], zeros],
                         axis=-1).reshape(Q_LORA, A_HEADS * LANES)
    ukv = w_ukv.reshape(KV_LORA, A_HEADS, A_NOPE + A_VDIM)
    return dict(
        wm=w_in[:, :4 * M_W].astype(BF16),
        wg=_pad_cols(w_gate, LANES).astype(BF16),
        bg=_pad_cols(jnp.concatenate([b_ig, b_fg]).reshape(1, 2 * M_HEADS).astype(F32), LANES),
        wcq=w_cq.astype(BF16), wckv=w_ckv.astype(BF16),
        wkra=wkra.astype(BF16), wkrb=wkrb.astype(BF16),
        qn=q_norm.reshape(1, Q_LORA).astype(F32), kvn=kv_norm.reshape(1, KV_LORA).astype(F32),
        wq=wq.astype(BF16), ws=ws.astype(BF16),
        wuk=ukv[..., :A_NOPE].reshape(KV_LORA, A_HEADS * A_NOPE).astype(BF16),
        wuv=ukv[..., A_NOPE:].reshape(KV_LORA, A_HEADS * A_VDIM).astype(BF16),
    )


def _rope_tables(pos):
    half = A_ROPE // 2
    inv_freq = ROPE_THETA ** (-jnp.arange(half, dtype=F32) / half)
    ang = pos.astype(F32)[:, None] * inv_freq[None, :]
    cos, sin = jnp.cos(ang), jnp.sin(ang)
    pad = jnp.zeros((pos.shape[0], LANES - A_ROPE), F32)
    return (jnp.concatenate([cos, cos, pad], axis=1), jnp.concatenate([-sin, sin, pad], axis=1))


def _s5_weights(a_re, a_im, b_re, b_im, c_re, c_im, d, log_dt, w_glu):
    dt = jnp.exp(log_dt.astype(F32))[:, None]
    ar, ai = a_re.astype(F32), a_im.astype(F32)
    mag = jnp.exp(dt * ar)
    ab_r, ab_i = mag * jnp.cos(dt * ai), mag * jnp.sin(dt * ai)
    den = ar * ar + ai * ai
    c_r = ((ab_r - 1.0) * ar + ab_i * ai) / den
    c_i = (ab_i * ar - (ab_r - 1.0) * ai) / den
    br, bi = b_re.astype(F32), b_im.astype(F32)
    bb_r = c_r[..., None] * br - c_i[..., None] * bi
    bb_i = c_r[..., None] * bi + c_i[..., None] * br
    gb = S5_GROUPS // S5_BLOCKS
    eye = jnp.eye(gb, dtype=F32)
    bd_in = lambda m: jnp.einsum("kgpc,gh->kgchp", m.reshape(S5_BLOCKS, gb, S5_P, S5_GSIZE),
                                 eye).reshape(S5_BLOCKS, S5_BLK_IN, S5_BLK_ST)
    bd_out = lambda m: jnp.einsum("kgcp,gh->kgphc", m.reshape(S5_BLOCKS, gb, S5_GSIZE, S5_P),
                                  eye).reshape(S5_BLOCKS, S5_BLK_ST, S5_BLK_IN)
    return dict(
        bbr=bd_in(bb_r).astype(BF16), bbi=bd_in(bb_i).astype(BF16),
        ar=ab_r.reshape(1, S5_STATE), ai=ab_i.reshape(1, S5_STATE),
        ccr=bd_out(c_re.astype(F32)).astype(BF16), cci=bd_out(-c_im.astype(F32)).astype(BF16),
        d=d.reshape(1, S5_W).astype(F32), wglu=w_glu.astype(BF16),
    )


def _pad_keys(a, mult):
    pad = (-a.shape[1]) % mult
    return jnp.pad(a, ((0, 0), (0, pad), (0, 0))) if pad else a


def _trunk(x, pos0, cache, p):
    nb, l, _ = x.shape
    assert nb == SUBLANES, "the S5 layout puts the streams of one time step on the sublanes"
    assert l % CHUNK == 0
    n = nb * l
    tm = min(TOKEN_TILE, n)
    tl = min(tm, l)
    assert l % tl == 0 and tm % tl == 0
    past = 0 if cache is None else cache["sb_k"].shape[1]
    assert pos0 == past
    xf = x.reshape(n, D_MODEL)
    bf = lambda a: a.astype(BF16)

    xf = _ffn(xf, p["norm_ffn"][0, 0], bf(p["ffn_w_gate"][0, 0]), bf(p["ffn_w_up"][0, 0]),
              bf(p["ffn_w_down"][0, 0]), tm=tm)
    ew = _even_weights(p["even_w_in"][0], p["mlstm_b_igate"][0], p["mlstm_b_fgate"][0],
                       p["mla_q_norm"][0], p["mla_kv_norm"][0], p["mla_w_uq"][0], p["mla_w_ukv"][0])
    cos, sin = _rope_tables(pos0 + jnp.arange(l))
    if tm > l:
        cos, sin = jnp.tile(cos, (tm // l, 1)), jnp.tile(sin, (tm // l, 1))
    qm, km, vm, om, gates, q, kn, v, kr128, lat, kr = _even_in(
        xf, p["norm_mix"][0], ew, cos, sin, tm=tm)

    if cache is None:
        c0 = jnp.zeros((nb, M_HEADS, M_DH, M_DH), F32)
        n0 = jnp.zeros((nb, M_HEADS, 1, M_DH), F32)
        m0 = jnp.zeros((nb, M_HEADS, 1, LANES), F32)
    else:
        c0 = cache["mlstm_c"].astype(F32)
        n0 = cache["mlstm_n"].astype(F32).reshape(nb, M_HEADS, 1, M_DH)
        m0 = jnp.broadcast_to(cache["mlstm_m"].astype(F32)[..., None, None], (nb, M_HEADS, 1, LANES))
    r3 = lambda a: a.reshape(nb, l, a.shape[-1])
    h_m, c_new, n_new, m_new = _mlstm(r3(qm), r3(km), r3(vm), r3(om), r3(gates),
                                      p["mlstm_out_norm"][0], c0, n0, m0, tl=tl)

    kn3, v3, kr3 = r3(kn), r3(v), r3(kr128)
    if cache is not None:
        ckn, cv = _kv_expand(cache["mla_latent"].astype(F32).reshape(nb * past, KV_LORA),
                             ew["wuk"], ew["wuv"])
        ckr = jnp.pad(cache["mla_krope"], ((0, 0), (0, 0), (0, LANES - A_ROPE))).astype(BF16)
        kn3 = jnp.concatenate([ckn.reshape(nb, past, -1), kn3], axis=1)
        v3 = jnp.concatenate([cv.reshape(nb, past, -1), v3], axis=1)
        kr3 = jnp.concatenate([ckr, kr3], axis=1)
    mla_tq = min(MLA_TILE, -(-l // LANES) * LANES)
    mla_tk = mla_tq if cache is None else -(-(past + max(l, mla_tq)) // LANES) * LANES
    h_a = _mla_attention(_pad_keys(r3(q), mla_tq), _pad_keys(kn3, mla_tk), _pad_keys(kr3, mla_tk),
                         _pad_keys(v3, mla_tk), tq=mla_tq, tk=mla_tk, offset=past)[:, :l]

    w_out = bf(p["even_w_out"][0])
    xf = _ffn(xf, p["norm_ffn"][0, 1], bf(p["ffn_w_gate"][0, 1]), bf(p["ffn_w_up"][0, 1]),
              bf(p["ffn_w_down"][0, 1]),
              mix=(h_m.reshape(n, M_W), h_a.reshape(n, -1), w_out[:M_W], w_out[M_W:]), tm=tm)

    xf = _ffn(xf, p["norm_ffn"][1, 0], bf(p["ffn_w_gate"][1, 0]), bf(p["ffn_w_up"][1, 0]),
              bf(p["ffn_w_down"][1, 0]), tm=tm)
    w_in = p["odd_w_in"][0]
    ow = dict(wu=bf(w_in[:, :S5_W]), wq=bf(w_in[:, S5_W:S5_W + SB_W]),
              wk=bf(w_in[:, S5_W + SB_W:S5_W + 2 * SB_W]), wv=bf(w_in[:, S5_W + 2 * SB_W:]))
    u, qs, ks, vs, ksb, vsb = _odd_in(xf, p["norm_mix"][1], ow, tm=tm)

    sw = _s5_weights(p["s5_A_re"][0], p["s5_A_im"][0], p["s5_B_re"][0], p["s5_B_im"][0],
                     p["s5_C_re"][0], p["s5_C_im"][0], p["s5_D"][0], p["s5_log_dt"][0], p["s5_w_glu"][0])
    if cache is None:
        x0r = jnp.zeros((nb, S5_STATE), F32)
        x0i = jnp.zeros((nb, S5_STATE), F32)
    else:
        x0r = cache["s5_re"].astype(F32).reshape(nb, S5_STATE)
        x0i = cache["s5_im"].astype(F32).reshape(nb, S5_STATE)
    h_s, xr_new, xi_new = _s5(r3(u), sw, x0r, x0i, tt=min(S5_TT, l))

    k3, v3 = r3(ksb), r3(vsb)
    if cache is not None:
        k3 = jnp.concatenate([bf(cache["sb_k"].reshape(nb, past, SB_W)), k3], axis=1)
        v3 = jnp.concatenate([bf(cache["sb_v"].reshape(nb, past, SB_W)), v3], axis=1)
    h_b = _sb_attention(r3(qs), _pad_keys(k3, SB_TILE), _pad_keys(v3, SB_TILE), tq=min(SB_TILE, l),
                        tk=SB_TILE, offset=past)

    w_out = bf(p["odd_w_out"][0])
    y = _ffn(xf, p["norm_ffn"][1, 1], bf(p["ffn_w_gate"][1, 1]), bf(p["ffn_w_up"][1, 1]),
             bf(p["ffn_w_down"][1, 1]),
             mix=(h_s.reshape(n, S5_W), h_b.reshape(n, SB_W), w_out[:S5_W], w_out[S5_W:]),
             final_gain=p["norm_final"], tm=tm)

    return (y.reshape(nb, l, D_MODEL),
            lat.reshape(1, nb, l, KV_LORA), kr.reshape(1, nb, l, A_ROPE),
            c_new[None], n_new.reshape(1, nb, M_HEADS, M_DH), m_new[:, :, 0, 0][None],
            xr_new.reshape(1, nb, S5_GROUPS, S5_P), xi_new.reshape(1, nb, S5_GROUPS, S5_P),
            ks.reshape(1, nb, l, SB_HEADS, SB_DH), vs.reshape(1, nb, l, SB_HEADS, SB_DH))


def kernel(x_prompt, x_sample, cache_mla_latent, cache_mla_krope, state_mlstm_C, state_mlstm_n, state_mlstm_m, state_s5_re, state_s5_im, cache_sb_k, cache_sb_v, norm_ffn, norm_mix, norm_final, ffn_w_gate, ffn_w_up, ffn_w_down, even_w_in, even_w_out, mlstm_b_igate, mlstm_b_fgate, mlstm_out_norm, mla_q_norm, mla_kv_norm, mla_w_uq, mla_w_ukv, odd_w_in, odd_w_out, s5_A_re, s5_A_im, s5_B_re, s5_B_im, s5_C_re, s5_C_im, s5_D, s5_log_dt, s5_w_glu):
    p = dict(norm_ffn=norm_ffn, norm_mix=norm_mix, norm_final=norm_final, ffn_w_gate=ffn_w_gate,
             ffn_w_up=ffn_w_up, ffn_w_down=ffn_w_down, even_w_in=even_w_in, even_w_out=even_w_out,
             mlstm_b_igate=mlstm_b_igate, mlstm_b_fgate=mlstm_b_fgate, mlstm_out_norm=mlstm_out_norm,
             mla_q_norm=mla_q_norm, mla_kv_norm=mla_kv_norm, mla_w_uq=mla_w_uq, mla_w_ukv=mla_w_ukv,
             odd_w_in=odd_w_in, odd_w_out=odd_w_out, s5_A_re=s5_A_re, s5_A_im=s5_A_im, s5_B_re=s5_B_re,
             s5_B_im=s5_B_im, s5_C_re=s5_C_re, s5_C_im=s5_C_im, s5_D=s5_D, s5_log_dt=s5_log_dt,
             s5_w_glu=s5_w_glu)
    past = cache_sb_k.shape[2]
    cache = dict(mla_latent=cache_mla_latent[0], mla_krope=cache_mla_krope[0],
                 mlstm_c=state_mlstm_C[0], mlstm_n=state_mlstm_n[0], mlstm_m=state_mlstm_m[0],
                 s5_re=state_s5_re[0], s5_im=state_s5_im[0],
                 sb_k=cache_sb_k[0], sb_v=cache_sb_v[0])
    out_p = _trunk(x_prompt, 0, None, p)
    out_s = _trunk(x_sample, past, cache, p)
    return (out_p[0], out_s[0]) + tuple(out_p[1:]) + tuple(out_s[1:])
```

```python
import functools
import math

import jax
import jax.numpy as jnp
import numpy as np
from jax import lax
from jax.experimental import pallas as pl
from jax.experimental.pallas import tpu as pltpu

F32 = jnp.float32
BF16 = jnp.bfloat16

D_MODEL = 1024
CHUNK = 64
D_FF = 2816
EPS = 1e-6
M_HEADS = 4
M_DH = 128
M_W = M_HEADS * M_DH
A_HEADS = 4
A_NOPE = 128
A_ROPE = 64
A_VDIM = 128
Q_LORA = 384
KV_LORA = 256
ROPE_THETA = 10000.0
MLA_SCALE = (A_NOPE + A_ROPE) ** -0.5
S5_W = 512
S5_GSIZE = 16
S5_GROUPS = S5_W // S5_GSIZE
S5_P = 64
S5_STATE = S5_GROUPS * S5_P
SB_HEADS = 8
SB_DH = 64
SB_W = SB_HEADS * SB_DH

LANES = 128
SUBLANES = 8
VMEM_LIMIT_BYTES = 56 * 1024 * 1024

TOKEN_TILE = 512
FF_CHUNK = 256
MLA_TILE = 512
SB_TILE = 128
S5_TT = 64
MLSTM_STREAMS = 2
SB_LOG_FLOOR = -24.0
LOG2E = math.log2(math.e)

HIGHEST = lax.Precision.HIGHEST


def _cparams(*sem):
    return pltpu.CompilerParams(dimension_semantics=sem, vmem_limit_bytes=VMEM_LIMIT_BYTES)


def _bdot(a, b):
    return jnp.dot(a.astype(BF16), b.astype(BF16), preferred_element_type=F32)


def _bdot_nt(a, b):
    return lax.dot_general(a.astype(BF16), b.astype(BF16), (((1,), (1,)), ((), ())),
                           preferred_element_type=F32)


def _rms(x, g):
    return x * lax.rsqrt(jnp.mean(x * x, axis=-1, keepdims=True) + EPS) * g


def _sigmoid(x):
    return 1.0 / (1.0 + jnp.exp(-x))


def _log_sigmoid(x):
    return jnp.minimum(x, 0.0) - jnp.log(1.0 + jnp.exp(-jnp.abs(x)))


def _const_spec(shape):
    nd = len(shape)
    return pl.BlockSpec(shape, lambda *_: (0,) * nd)


def _ffn_kernel(*refs, with_mix, with_final):
    it = iter(refs)
    x_ref = next(it)
    if with_mix:
        a_ref, b_ref, wa_ref, wb_ref = next(it), next(it), next(it), next(it)
    g_ref, wg_ref, wu_ref, wd_ref = next(it), next(it), next(it), next(it)
    gf_ref = next(it) if with_final else None
    o_ref = next(it)
    acc_ref = next(it)

    x = x_ref[...]
    if with_mix:
        x = x + _bdot(a_ref[...], wa_ref[...]) + _bdot(b_ref[...], wb_ref[...])
    h = _rms(x, g_ref[...]).astype(BF16)
    acc_ref[...] = jnp.zeros_like(acc_ref)
    for c in range(D_FF // FF_CHUNK):
        sl = slice(c * FF_CHUNK, (c + 1) * FF_CHUNK)
        g = jnp.dot(h, wg_ref[:, sl], preferred_element_type=F32)
        u = jnp.dot(h, wu_ref[:, sl], preferred_element_type=F32)
        a = (g * _sigmoid(g) * u).astype(BF16)
        acc_ref[...] += jnp.dot(a, wd_ref[sl, :], preferred_element_type=F32)
    y = x + 0.5 * acc_ref[...]
    if with_final:
        y = _rms(y, gf_ref[...])
    o_ref[...] = y


def _ffn(x, gain, wg, wu, wd, *, mix=None, final_gain=None, tm):
    n = x.shape[0]
    row = lambda i: (i, 0)
    args, specs = [x], [pl.BlockSpec((tm, D_MODEL), row)]
    if mix is not None:
        a, b, wa, wb = mix
        args += [a, b, wa, wb]
        specs += [pl.BlockSpec((tm, a.shape[-1]), row), pl.BlockSpec((tm, b.shape[-1]), row),
                  _const_spec(wa.shape), _const_spec(wb.shape)]
    args += [gain.reshape(1, D_MODEL), wg, wu, wd]
    specs += [_const_spec((1, D_MODEL)), _const_spec(wg.shape), _const_spec(wu.shape),
              _const_spec(wd.shape)]
    if final_gain is not None:
        args.append(final_gain.reshape(1, D_MODEL))
        specs.append(_const_spec((1, D_MODEL)))
    return pl.pallas_call(
        functools.partial(_ffn_kernel, with_mix=mix is not None, with_final=final_gain is not None),
        grid=(n // tm,),
        in_specs=specs,
        out_specs=pl.BlockSpec((tm, D_MODEL), row),
        out_shape=jax.ShapeDtypeStruct((n, D_MODEL), F32),
        scratch_shapes=[pltpu.VMEM((tm, D_MODEL), F32)],
        compiler_params=_cparams("parallel"),
        name="ffn",
    )(*args)


def _even_in_kernel(x_ref, gain_ref, wm_ref, wg_ref, bg_ref, wcq_ref, wckv_ref, wkra_ref, wkrb_ref,
                    qn_ref, kvn_ref, wq_ref, ws_ref, wuk_ref, wuv_ref, cos_ref, sin_ref,
                    qm_o, km_o, vm_o, om_o, gates_o, q_o, kn_o, v_o, kr128_o, lat_o, kr_o):
    h = _rms(x_ref[...], gain_ref[...]).astype(BF16)
    m = jnp.dot(h, wm_ref[...], preferred_element_type=F32)
    qm_o[...] = m[:, :M_W].astype(BF16)
    km_o[...] = (m[:, M_W:2 * M_W] * M_DH ** -0.5).astype(BF16)
    vm_o[...] = m[:, 2 * M_W:3 * M_W].astype(BF16)
    om_o[...] = m[:, 3 * M_W:]

    g = jnp.dot(h, wg_ref[...], preferred_element_type=F32) + bg_ref[...]
    lane = lax.broadcasted_iota(jnp.int32, g.shape, 1)
    gates_o[...] = jnp.where(lane < M_HEADS, g, _log_sigmoid(g))

    cos, sin = cos_ref[...], sin_ref[...]
    cq = jnp.dot(h, wcq_ref[...], preferred_element_type=F32)
    cqn = _rms(cq, qn_ref[...]).astype(BF16)
    qf = jnp.dot(cqn, wq_ref[...], preferred_element_type=F32)
    qs = jnp.dot(cqn, ws_ref[...], preferred_element_type=F32)
    q_scale = MLA_SCALE * LOG2E
    for hh in range(A_HEADS):
        base = hh * 2 * LANES
        q_o[:, base:base + LANES] = (qf[:, base:base + LANES] * q_scale).astype(BF16)
        rot = qf[:, base + LANES:base + 2 * LANES] * cos + qs[:, hh * LANES:(hh + 1) * LANES] * sin
        q_o[:, base + LANES:base + 2 * LANES] = (rot * q_scale).astype(BF16)

    ckv = jnp.dot(h, wckv_ref[...], preferred_element_type=F32)
    lat = _rms(ckv, kvn_ref[...])
    lat_o[...] = lat
    latb = lat.astype(BF16)
    kn_o[...] = jnp.dot(latb, wuk_ref[...], preferred_element_type=F32).astype(BF16)
    v_o[...] = jnp.dot(latb, wuv_ref[...], preferred_element_type=F32).astype(BF16)

    kr = (jnp.dot(h, wkra_ref[...], preferred_element_type=F32) * cos
          + jnp.dot(h, wkrb_ref[...], preferred_element_type=F32) * sin)
    kr128_o[...] = kr.astype(BF16)
    kr_o[...] = kr[:, :A_ROPE]


def _even_in(x, gain, w, cos, sin, *, tm):
    n = x.shape[0]
    row = lambda i: (i, 0)
    n_pos_tiles = cos.shape[0] // tm
    pos = lambda i: (i % n_pos_tiles, 0)
    weights = [w["wm"], w["wg"], w["bg"], w["wcq"], w["wckv"], w["wkra"], w["wkrb"], w["qn"], w["kvn"],
               w["wq"], w["ws"], w["wuk"], w["wuv"]]
    in_specs = ([pl.BlockSpec((tm, D_MODEL), row), _const_spec((1, D_MODEL))]
                + [_const_spec(a.shape) for a in weights]
                + [pl.BlockSpec((tm, LANES), pos), pl.BlockSpec((tm, LANES), pos)])
    outs = [(M_W, BF16), (M_W, BF16), (M_W, BF16), (M_W, F32), (LANES, F32),
            (A_HEADS * 2 * LANES, BF16), (A_HEADS * A_NOPE, BF16), (A_HEADS * A_VDIM, BF16),
            (LANES, BF16), (KV_LORA, F32), (A_ROPE, F32)]
    return pl.pallas_call(
        _even_in_kernel,
        grid=(n // tm,),
        in_specs=in_specs,
        out_specs=[pl.BlockSpec((tm, c), row) for c, _ in outs],
        out_shape=[jax.ShapeDtypeStruct((n, c), dt) for c, dt in outs],
        compiler_params=_cparams("parallel"),
        name="even_in",
    )(x, gain.reshape(1, D_MODEL), *weights, cos, sin)


def _kv_expand_kernel(lat_ref, wuk_ref, wuv_ref, kn_o, v_o):
    latb = lat_ref[...].astype(BF16)
    kn_o[...] = jnp.dot(latb, wuk_ref[...], preferred_element_type=F32).astype(BF16)
    v_o[...] = jnp.dot(latb, wuv_ref[...], preferred_element_type=F32).astype(BF16)


def _kv_expand(lat, wuk, wuv):
    n = lat.shape[0]
    tm = min(4 * TOKEN_TILE, n)
    assert n % tm == 0
    row = lambda i: (i, 0)
    return pl.pallas_call(
        _kv_expand_kernel,
        grid=(n // tm,),
        in_specs=[pl.BlockSpec((tm, KV_LORA), row), _const_spec(wuk.shape), _const_spec(wuv.shape)],
        out_specs=[pl.BlockSpec((tm, wuk.shape[1]), row), pl.BlockSpec((tm, wuv.shape[1]), row)],
        out_shape=[jax.ShapeDtypeStruct((n, wuk.shape[1]), BF16),
                   jax.ShapeDtypeStruct((n, wuv.shape[1]), BF16)],
        compiler_params=_cparams("parallel"),
        name="kv_expand",
    )(lat, wuk, wuv)


def _mlstm_kernel(qm_ref, km_ref, vm_ref, om_ref, g_ref, onorm_ref, c0_ref, n0_ref, m0_ref,
                  h_ref, cout_ref, nout_ref, mout_ref, c_sc, n_sc, m_sc, *, n_chunks, n_streams):
    lt = pl.program_id(1)

    @pl.when(lt == 0)
    def _():
        c_sc[...] = c0_ref[...]
        n_sc[...] = n0_ref[...]
        m_sc[...] = m0_ref[...]

    row = lax.broadcasted_iota(jnp.int32, (CHUNK, CHUNK), 0)
    col = lax.broadcasted_iota(jnp.int32, (CHUNK, CHUNK), 1)
    causal = col <= row
    ltri = causal.astype(BF16)
    utri = (row <= col).astype(BF16)
    ones_kl = jnp.ones((CHUNK, LANES), BF16)

    def split3(a):
        hi = a.astype(BF16)
        r1 = a - hi.astype(F32)
        mid = r1.astype(BF16)
        return hi, mid, (r1 - mid.astype(F32)).astype(BF16)

    def chunk(c, carry):
        r0 = pl.multiple_of(c * CHUNK, CHUNK)
        rows = pl.ds(r0, CHUNK)
        chains = [(bb, hh) for bb in range(n_streams) for hh in range(M_HEADS)]
        hsl = lambda hh: slice(hh * M_DH, (hh + 1) * M_DH)
        gate = {}
        for bb in range(n_streams):
            gates = g_ref[bb, rows, :]
            gates_t = gates.T
            cc = jnp.dot(ltri, jnp.concatenate(split3(gates), axis=1), preferred_element_type=F32)
            cum_col = cc[:, :LANES] + cc[:, LANES:2 * LANES] + cc[:, 2 * LANES:]
            cr = jnp.dot(jnp.concatenate(split3(gates_t), axis=0), utri, preferred_element_type=F32)
            cum_row = cr[:LANES] + cr[LANES:2 * LANES] + cr[2 * LANES:]
            gate[bb] = (gates, gates_t, cum_col, cum_row)
        qk, qc, qn = {}, {}, {}
        for ch in chains:
            bb, hh = ch
            rhs = jnp.concatenate([c_sc[bb, hh].astype(BF16),
                                   jnp.broadcast_to(n_sc[bb, hh], (LANES, M_DH)).astype(BF16),
                                   km_ref[bb, rows, hsl(hh)]], axis=0)
            r = _bdot_nt(qm_ref[bb, rows, hsl(hh)], rhs)
            qc[ch], qn[ch], qk[ch] = r[:, :M_DH], r[:, M_DH:M_DH + LANES], r[:, M_DH + LANES:]
        st = {}
        for ch in chains:
            bb, hh = ch
            gates, gates_t, cum_col, cum_row = gate[bb]
            ig_row = gates_t[hh:hh + 1, :]
            ig_col = gates[:, hh:hh + 1]
            b_row = cum_row[M_HEADS + hh:M_HEADS + hh + 1, :]
            b_col = cum_col[:, M_HEADS + hh:M_HEADS + hh + 1]
            m_prev = m_sc[bb, hh][:, :1]
            b_lanes = jnp.broadcast_to(b_col, (CHUNK, LANES))
            log_d = jnp.where(causal, b_lanes[:, :CHUNK] - b_row + ig_row, -jnp.inf)
            m_t = jnp.maximum(b_col + m_prev, jnp.max(log_d, axis=-1, keepdims=True))
            m_lanes = jnp.broadcast_to(m_t, (CHUNK, LANES))
            s = (qk[ch] * jnp.exp(log_d - m_lanes[:, :CHUNK])).astype(BF16)
            w_inter = jnp.exp(b_lanes + m_prev - m_lanes)
            m_new = m_t[CHUNK - 1:CHUNK, :]
            b_last = b_col[CHUNK - 1:CHUNK, :]
            w_s = jnp.exp(b_last - b_col + ig_col - m_new)
            decay = jnp.exp(b_last + m_prev - m_new)
            r = jnp.dot(s, jnp.concatenate([vm_ref[bb, rows, hsl(hh)], ones_kl], axis=1),
                        preferred_element_type=F32)
            st[ch] = (r[:, :M_DH], r[:, M_DH:], w_inter, m_lanes, m_new, w_s, decay)
        for ch in chains:
            bb, hh = ch
            sv, s_sum, w_inter, m_lanes, m_new, w_s, decay = st[ch]
            k = km_ref[bb, rows, hsl(hh)]
            v = vm_ref[bb, rows, hsl(hh)]
            c_h = c_sc[bb, hh]
            n_h = n_sc[bb, hh]
            num = w_inter * qc[ch] + sv
            den = w_inter * qn[ch] + s_sum
            hv = num / jnp.maximum(jnp.abs(den), jnp.exp(-m_lanes))
            wv_t = (jnp.broadcast_to(w_s, (CHUNK, M_DH)) * v).T
            c_sc[bb, hh] = decay * c_h + _bdot(wv_t, k)
            n_sc[bb, hh] = decay * n_h + jnp.sum(w_s * k.astype(F32), axis=0, keepdims=True)
            m_sc[bb, hh] = jnp.broadcast_to(m_new, (1, LANES))
            hn = _rms(hv, onorm_ref[:, hsl(hh)])
            h_ref[bb, rows, hsl(hh)] = (_sigmoid(om_ref[bb, rows, hsl(hh)]) * hn).astype(BF16)
        return carry

    lax.fori_loop(0, n_chunks, chunk, 0)

    @pl.when(lt == pl.num_programs(1) - 1)
    def _():
        cout_ref[...] = c_sc[...]
        nout_ref[...] = n_sc[...]
        mout_ref[...] = m_sc[...]


def _mlstm(qm, km, vm, om, gates, onorm, c0, n0, m0, *, tl):
    b, l, _ = qm.shape
    ns = MLSTM_STREAMS
    tile = lambda i, j: (i, j, 0)
    st4 = lambda i, j: (i, 0, 0, 0)
    state = [((ns, M_HEADS, M_DH, M_DH), (b, M_HEADS, M_DH, M_DH)),
             ((ns, M_HEADS, 1, M_DH), (b, M_HEADS, 1, M_DH)),
             ((ns, M_HEADS, 1, LANES), (b, M_HEADS, 1, LANES))]
    return pl.pallas_call(
        functools.partial(_mlstm_kernel, n_chunks=tl // CHUNK, n_streams=ns),
        grid=(b // ns, l // tl),
        in_specs=[pl.BlockSpec((ns, tl, M_W), tile)] * 4
        + [pl.BlockSpec((ns, tl, LANES), tile), _const_spec((1, M_W))]
        + [pl.BlockSpec(blk, st4) for blk, _ in state],
        out_specs=[pl.BlockSpec((ns, tl, M_W), tile)] + [pl.BlockSpec(blk, st4) for blk, _ in state],
        out_shape=[jax.ShapeDtypeStruct((b, l, M_W), BF16)]
        + [jax.ShapeDtypeStruct(full, F32) for _, full in state],
        scratch_shapes=[pltpu.VMEM(blk, F32) for blk, _ in state],
        compiler_params=_cparams("parallel", "arbitrary"),
        name="mlstm",
    )(qm, km, vm, om, gates, onorm.reshape(1, M_W), c0, n0, m0)


MLA_SUM_ROWS = 16


def _mla_kernel(qi_ref, kj_ref, flag_ref, q_ref, kn_ref, kr_ref, vt_ref, o_ref, m_sc, acc_sc,
                *, tq, tk, offset):
    p = pl.program_id(1)
    i = qi_ref[p]
    j = kj_ref[p]
    flags = flag_ref[p]

    @pl.when(j == 0)
    def _():
        m_sc[...] = jnp.full_like(m_sc, -jnp.inf)
        acc_sc[...] = jnp.zeros_like(acc_sc)

    def sweep(masked):
        kr = kr_ref[0]
        ones = jnp.ones((MLA_SUM_ROWS, tk), BF16)
        scores = []
        for hh in range(A_HEADS):
            q = q_ref[0, :, hh * 2 * LANES:(hh + 1) * 2 * LANES]
            k = jnp.concatenate([kn_ref[0, :, hh * A_NOPE:(hh + 1) * A_NOPE], kr], axis=-1)
            scores.append(_bdot_nt(k, q))
        if masked:
            k_chunk = (j * tk + lax.broadcasted_iota(jnp.int32, (tk, tq), 0)) // CHUNK
            q_chunk = (offset + i * tq + lax.broadcasted_iota(jnp.int32, (tk, tq), 1)) // CHUNK
            visible = k_chunk <= q_chunk
            scores = [jnp.where(visible, s, -jnp.inf) for s in scores]
        for hh in range(A_HEADS):
            s = scores[hh]
            m_prev = m_sc[hh]
            m_new = jnp.maximum(m_prev, jnp.max(s, axis=0, keepdims=True))
            alpha = jnp.exp2(m_prev - m_new)
            pr = jnp.exp2(s - m_new).astype(BF16)
            vt = jnp.concatenate([vt_ref[0, hh * A_VDIM:(hh + 1) * A_VDIM, :], ones], axis=0)
            acc_sc[hh] = alpha * acc_sc[hh] + jnp.dot(vt, pr, preferred_element_type=F32)
            m_sc[hh] = m_new

    @pl.when((flags & 2) != 0)
    def _():
        sweep(True)

    @pl.when((flags & 2) == 0)
    def _():
        sweep(False)

    @pl.when((flags & 1) != 0)
    def _():
        for hh in range(A_HEADS):
            acc = acc_sc[hh]
            o_ref[0, :, hh * A_VDIM:(hh + 1) * A_VDIM] = (
                acc[:A_VDIM] / acc[A_VDIM:A_VDIM + 1]).T.astype(BF16)


def _transpose_kernel(x_ref, o_ref):
    o_ref[0] = x_ref[0].astype(F32).T.astype(o_ref.dtype)


def _transpose_tiles(x, t):
    b, l, c = x.shape
    return pl.pallas_call(
        _transpose_kernel,
        grid=(b, l // t),
        in_specs=[pl.BlockSpec((1, t, c), lambda bb, i: (bb, i, 0))],
        out_specs=pl.BlockSpec((1, c, t), lambda bb, i: (bb, 0, i)),
        out_shape=jax.ShapeDtypeStruct((b, c, l), x.dtype),
        compiler_params=_cparams("parallel", "parallel"),
        name="transpose_tiles",
    )(x)


def _mla_pairs(lq, lk, tq, tk, offset):
    qi, kj, flags = [], [], []
    for i in range(lq // tq):
        first_row, last_row = offset + i * tq, offset + (i + 1) * tq - 1
        k_end = (last_row // CHUNK + 1) * CHUNK
        k_all = (first_row // CHUNK + 1) * CHUNK
        nj = min(-(-k_end // tk), lk // tk)
        for j in range(nj):
            qi.append(i)
            kj.append(j)
            flags.append(int(j == nj - 1) + 2 * int((j + 1) * tk > k_all))
    return (np.asarray(qi, np.int32), np.asarray(kj, np.int32), np.asarray(flags, np.int32))


def _mla_attention(q, kn, kr, v, *, tq, tk, offset):
    b, lq, _ = q.shape
    lk = kn.shape[1]
    vt = _transpose_tiles(v, tk)
    qi, kj, flags = _mla_pairs(lq, lk, tq, tk, offset)
    qmap = lambda bb, p, qi_r, kj_r, flag_r: (bb, qi_r[p], 0)
    kmap = lambda bb, p, qi_r, kj_r, flag_r: (bb, kj_r[p], 0)
    grid_spec = pltpu.PrefetchScalarGridSpec(
        num_scalar_prefetch=3,
        grid=(b, len(qi)),
        in_specs=[pl.BlockSpec((1, tq, q.shape[-1]), qmap),
                  pl.BlockSpec((1, tk, kn.shape[-1]), kmap),
                  pl.BlockSpec((1, tk, kr.shape[-1]), kmap),
                  pl.BlockSpec((1, vt.shape[1], tk), lambda bb, p, qi_r, kj_r, flag_r: (bb, 0, kj_r[p]))],
        out_specs=pl.BlockSpec((1, tq, A_HEADS * A_VDIM), qmap),
        scratch_shapes=[pltpu.VMEM((A_HEADS, 1, tq), F32),
                        pltpu.VMEM((A_HEADS, A_VDIM + MLA_SUM_ROWS, tq), F32)],
    )
    return pl.pallas_call(
        functools.partial(_mla_kernel, tq=tq, tk=tk, offset=offset),
        grid_spec=grid_spec,
        out_shape=jax.ShapeDtypeStruct((b, lq, A_HEADS * A_VDIM), BF16),
        compiler_params=_cparams("parallel", "arbitrary"),
        name="mla_attention",
    )(jnp.asarray(qi), jnp.asarray(kj), jnp.asarray(flags), q, kn, kr, vt)


def _odd_in_kernel(x_ref, gain_ref, wu_ref, wq_ref, wk_ref, wv_ref, u_o, q_o, k_o, v_o, kb_o, vb_o):
    h = _rms(x_ref[...], gain_ref[...]).astype(BF16)
    u_o[...] = jnp.dot(h, wu_ref[...], preferred_element_type=F32)
    q_o[...] = (jnp.dot(h, wq_ref[...], preferred_element_type=F32) * SB_DH ** -0.5).astype(BF16)
    k = jnp.dot(h, wk_ref[...], preferred_element_type=F32)
    v = jnp.dot(h, wv_ref[...], preferred_element_type=F32)
    k_o[...] = k
    v_o[...] = v
    kb_o[...] = k.astype(BF16)
    vb_o[...] = v.astype(BF16)


def _odd_in(x, gain, w, *, tm):
    n = x.shape[0]
    row = lambda i: (i, 0)
    weights = [w["wu"], w["wq"], w["wk"], w["wv"]]
    return pl.pallas_call(
        _odd_in_kernel,
        grid=(n // tm,),
        in_specs=[pl.BlockSpec((tm, D_MODEL), row), _const_spec((1, D_MODEL))]
        + [_const_spec(a.shape) for a in weights],
        out_specs=[pl.BlockSpec((tm, S5_W), row)] + [pl.BlockSpec((tm, SB_W), row)] * 5,
        out_shape=[jax.ShapeDtypeStruct((n, S5_W), F32),
                   jax.ShapeDtypeStruct((n, SB_W), BF16),
                   jax.ShapeDtypeStruct((n, SB_W), F32),
                   jax.ShapeDtypeStruct((n, SB_W), F32),
                   jax.ShapeDtypeStruct((n, SB_W), BF16),
                   jax.ShapeDtypeStruct((n, SB_W), BF16)],
        compiler_params=_cparams("parallel"),
        name="odd_in",
    )(x, gain.reshape(1, D_MODEL), *weights)


S5_BLOCKS = 4
S5_BLK_IN = S5_W // S5_BLOCKS
S5_BLK_ST = S5_STATE // S5_BLOCKS


def _s5_kernel(u_ref, bbr_ref, bbi_ref, ar_ref, ai_ref, ccr_ref, cci_ref, d_ref, wglu_ref,
               x0r_ref, x0i_ref, hs_o, xr_o, xi_o, u_tm, hs_tm, xr_buf, xi_buf, xr_st, xi_st,
               *, tt, nb):
    t = pl.program_id(0)

    @pl.when(t == 0)
    def _():
        xr_st[...] = x0r_ref[...]
        xi_st[...] = x0i_ref[...]

    for bb in range(nb):
        for k in range(S5_BLOCKS):
            u_tm[k, pl.ds(bb, tt, stride=nb), :] = u_ref[bb, :, k * S5_BLK_IN:(k + 1) * S5_BLK_IN]

    for k in range(S5_BLOCKS):
        cs = slice(k * S5_BLK_ST, (k + 1) * S5_BLK_ST)
        ub = u_tm[k].astype(BF16)
        xr_buf[:, cs] = jnp.dot(ub, bbr_ref[k], preferred_element_type=F32)
        xi_buf[:, cs] = jnp.dot(ub, bbi_ref[k], preferred_element_type=F32)

    for k in range(S5_BLOCKS):
        cs = slice(k * S5_BLK_ST, (k + 1) * S5_BLK_ST)
        ar = jnp.broadcast_to(ar_ref[:, cs], (nb, S5_BLK_ST))
        ai = jnp.broadcast_to(ai_ref[:, cs], (nb, S5_BLK_ST))

        def step(s, carry, cs=cs, ar=ar, ai=ai):
            xr, xi = carry
            rows = pl.ds(pl.multiple_of(s * nb, nb), nb)
            nxr = ar * xr - ai * xi + xr_buf[rows, cs]
            nxi = ar * xi + ai * xr + xi_buf[rows, cs]
            xr_buf[rows, cs] = nxr
            xi_buf[rows, cs] = nxi
            return nxr, nxi

        xr, xi = lax.fori_loop(0, tt, step, (xr_st[:, cs], xi_st[:, cs]), unroll=8)
        xr_st[:, cs] = xr
        xi_st[:, cs] = xi

    gs = []
    for k in range(S5_BLOCKS):
        cs = slice(k * S5_BLK_ST, (k + 1) * S5_BLK_ST)
        y = (_bdot(xr_buf[:, cs], ccr_ref[k]) + _bdot(xi_buf[:, cs], cci_ref[k])
             + d_ref[:, k * S5_BLK_IN:(k + 1) * S5_BLK_IN] * u_tm[k])
        gs.append(jax.nn.gelu(y, approximate=True))
    g = jnp.concatenate(gs, axis=-1)
    hs = g * _sigmoid(_bdot(g, wglu_ref[...]))
    for k in range(S5_BLOCKS):
        hs_tm[k] = hs[:, k * S5_BLK_IN:(k + 1) * S5_BLK_IN]
    for bb in range(nb):
        for k in range(S5_BLOCKS):
            hs_o[bb, :, k * S5_BLK_IN:(k + 1) * S5_BLK_IN] = (
                hs_tm[k, pl.ds(bb, tt, stride=nb), :].astype(BF16))

    @pl.when(t == pl.num_programs(0) - 1)
    def _():
        xr_o[...] = xr_st[...]
        xi_o[...] = xi_st[...]


def _s5(u, w, x0r, x0i, *, tt):
    nb, l, _ = u.shape
    blk = lambda t: (0, t, 0)
    consts = [w["bbr"], w["bbi"], w["ar"], w["ai"], w["ccr"], w["cci"], w["d"], w["wglu"], x0r, x0i]
    return pl.pallas_call(
        functools.partial(_s5_kernel, tt=tt, nb=nb),
        grid=(l // tt,),
        in_specs=[pl.BlockSpec((nb, tt, S5_W), blk)] + [_const_spec(a.shape) for a in consts],
        out_specs=[pl.BlockSpec((nb, tt, S5_W), blk), _const_spec((nb, S5_STATE)),
                   _const_spec((nb, S5_STATE))],
        out_shape=[jax.ShapeDtypeStruct((nb, l, S5_W), BF16),
                   jax.ShapeDtypeStruct((nb, S5_STATE), F32),
                   jax.ShapeDtypeStruct((nb, S5_STATE), F32)],
        scratch_shapes=[pltpu.VMEM((S5_BLOCKS, tt * nb, S5_BLK_IN), F32),
                        pltpu.VMEM((S5_BLOCKS, tt * nb, S5_BLK_IN), F32),
                        pltpu.VMEM((tt * nb, S5_STATE), F32), pltpu.VMEM((tt * nb, S5_STATE), F32),
                        pltpu.VMEM((nb, S5_STATE), F32), pltpu.VMEM((nb, S5_STATE), F32)],
        compiler_params=_cparams("arbitrary"),
        name="s5",
    )(u, *consts)


def _sb_kernel(q_ref, k_ref, v_ref, o_ref, r_sc, acc_sc, qh_sc, *, tq, tk, offset):
    i = pl.program_id(1)
    n_pairs = SB_HEADS // 2
    half = lax.broadcasted_iota(jnp.int32, (1, LANES), 1) // SB_DH
    for pr in range(n_pairs):
        qq = q_ref[0, :, pr * LANES:(pr + 1) * LANES]
        qh_sc[pr, :tq] = jnp.where(half == 0, qq, jnp.zeros_like(qq))
        qh_sc[pr, tq:] = jnp.where(half == 1, qq, jnp.zeros_like(qq))
    r_sc[...] = jnp.zeros_like(r_sc)
    acc_sc[...] = jnp.zeros_like(acc_sc)

    jrow = lax.broadcasted_iota(jnp.int32, (tk, tk), 0)
    jcol = lax.broadcasted_iota(jnp.int32, (tk, tk), 1)
    later = (jrow > jcol).astype(BF16)
    j_diag = (offset + i * tq) // tk

    def sweep(tiles):
        zs, vv, causal = {}, {}, {}
        for t, (j, masked, live) in enumerate(tiles):
            k0 = pl.multiple_of(j * tk, tk)
            if masked:
                q_pos = offset + i * tq + lax.broadcasted_iota(jnp.int32, (tq, tk), 0)
                causal[t] = (k0 + lax.broadcasted_iota(jnp.int32, (tq, tk), 1)) < q_pos
            for pr in range(n_pairs):
                ps = slice(pr * LANES, (pr + 1) * LANES)
                zs[t, pr] = _bdot_nt(qh_sc[pr], k_ref[0, pl.ds(k0, tk), ps])
                vv[t, pr] = v_ref[0, pl.ds(k0, tk), ps]
        l1s, lbs, cums = {}, {}, {}
        for t, (j, masked, live) in enumerate(tiles):
            for pr in range(n_pairs):
                parts = []
                for sub in range(2):
                    hh = 2 * pr + sub
                    z = zs[t, pr][sub * tq:(sub + 1) * tq]
                    sp = jnp.maximum(z, 0.0) + jnp.log(1.0 + jnp.exp(-jnp.abs(z)))
                    l1 = -sp
                    if masked:
                        l1 = jnp.where(causal[t], l1, 0.0)
                    if live is not None:
                        l1 = jnp.where(live, l1, 0.0)
                    hi = l1.astype(BF16)
                    parts += [hi, (l1 - hi.astype(F32)).astype(BF16)]
                    l1s[t, hh] = l1
                    lbs[t, hh] = z - sp
                c = jnp.dot(jnp.concatenate(parts, axis=0), later, preferred_element_type=F32)
                cums[t, 2 * pr] = c[:tq] + c[tq:2 * tq]
                cums[t, 2 * pr + 1] = c[2 * tq:3 * tq] + c[3 * tq:]
        for t, (j, masked, live) in enumerate(tiles):
            r_max = jnp.full((1, 1), -jnp.inf, F32)
            for pr in range(n_pairs):
                ws = []
                for sub in range(2):
                    hh = 2 * pr + sub
                    r_old = r_sc[hh]
                    log_w = lbs[t, hh] + cums[t, hh] + r_old
                    if masked:
                        log_w = jnp.where(causal[t], log_w, -jnp.inf)
                    if live is not None:
                        log_w = jnp.where(live, log_w, -jnp.inf)
                    ws.append(jnp.exp(log_w).astype(BF16))
                    r_new = r_old + jnp.sum(l1s[t, hh], axis=-1, keepdims=True)
                    r_sc[hh] = r_new
                    r_max = jnp.maximum(r_max, jnp.max(r_new, axis=0, keepdims=True))
                v2 = jnp.concatenate([jnp.where(half == 0, vv[t, pr], jnp.zeros_like(vv[t, pr])),
                                      jnp.where(half == 1, vv[t, pr], jnp.zeros_like(vv[t, pr]))],
                                     axis=0)
                acc_sc[:, pr * LANES:(pr + 1) * LANES] += jnp.dot(
                    jnp.concatenate(ws, axis=1), v2, preferred_element_type=F32)
        return r_max[0, 0]

    has_prev = j_diag >= 1
    r_first = sweep([(j_diag, True, None), (jnp.maximum(j_diag - 1, 0), False, has_prev)])

    def cond(carry):
        j, r_max = carry
        return jnp.logical_and(j >= 0, r_max > SB_LOG_FLOOR)

    def body(carry):
        j, _ = carry
        return j - 1, sweep([(j, False, None)])

    lax.while_loop(cond, body, (j_diag - 2, r_first))
    o_ref[0] = acc_sc[...].astype(BF16)


def _sb_attention(q, k, v, *, tq, tk, offset):
    b, lq, _ = q.shape
    lk = k.shape[1]
    assert tk % tq == 0 and offset % tk == 0 and lk % tk == 0
    return pl.pallas_call(
        functools.partial(_sb_kernel, tq=tq, tk=tk, offset=offset),
        grid=(b, lq // tq),
        in_specs=[pl.BlockSpec((1, tq, SB_W), lambda bb, i: (bb, i, 0)),
                  pl.BlockSpec((1, lk, SB_W), lambda bb, i: (bb, 0, 0)),
                  pl.BlockSpec((1, lk, SB_W), lambda bb, i: (bb, 0, 0))],
        out_specs=pl.BlockSpec((1, tq, SB_W), lambda bb, i: (bb, i, 0)),
        out_shape=jax.ShapeDtypeStruct((b, lq, SB_W), BF16),
        scratch_shapes=[pltpu.VMEM((SB_HEADS, tq, 1), F32),
                        pltpu.VMEM((tq, SB_W), F32),
                        pltpu.VMEM((SB_HEADS // 2, 2 * tq, LANES), BF16)],
        compiler_params=_cparams("parallel", "arbitrary"),
        name="sb_attention",
    )(q, k, v)


def _pad_cols(w, width):
    return jnp.pad(w, ((0, 0), (0, width - w.shape[1])))


def _even_weights(w_in, b_ig, b_fg, q_norm, kv_norm, w_uq, w_ukv):
    o = 4 * M_W
    w_gate = w_in[:, o:o + 2 * M_HEADS]
    o += 2 * M_HEADS
    w_cq = w_in[:, o:o + Q_LORA]
    o += Q_LORA
    w_ckv = w_in[:, o:o + KV_LORA]
    o += KV_LORA
    w_kr = w_in[:, o:o + A_ROPE]
    hr = A_ROPE // 2
    wkra = _pad_cols(w_kr, LANES)
    wkrb = _pad_cols(jnp.concatenate([w_kr[:, hr:], w_kr[:, :hr]], axis=1), LANES)
    uq = w_uq.reshape(Q_LORA, A_HEADS, A_NOPE + A_ROPE)
    zeros = jnp.zeros((Q_LORA, A_HEADS, LANES - A_ROPE), w_uq.dtype)
    wq = jnp.concatenate([uq, zeros], axis=-1).reshape(Q_LORA, A_HEADS * 2 * LANES)
    ws = jnp.concatenate([uq[..., A_NOPE + hr:], uq[..., A_NOPE:A_NOPE + hr], zeros],
                         axis=-1).reshape(Q_LORA, A_HEADS * LANES)
    ukv = w_ukv.reshape(KV_LORA, A_HEADS, A_NOPE + A_VDIM)
    return dict(
        wm=w_in[:, :4 * M_W].astype(BF16),
        wg=_pad_cols(w_gate, LANES).astype(BF16),
        bg=_pad_cols(jnp.concatenate([b_ig, b_fg]).reshape(1, 2 * M_HEADS).astype(F32), LANES),
        wcq=w_cq.astype(BF16), wckv=w_ckv.astype(BF16),
        wkra=wkra.astype(BF16), wkrb=wkrb.astype(BF16),
        qn=q_norm.reshape(1, Q_LORA).astype(F32), kvn=kv_norm.reshape(1, KV_LORA).astype(F32),
        wq=wq.astype(BF16), ws=ws.astype(BF16),
        wuk=ukv[..., :A_NOPE].reshape(KV_LORA, A_HEADS * A_NOPE).astype(BF16),
        wuv=ukv[..., A_NOPE:].reshape(KV_LORA, A_HEADS * A_VDIM).astype(BF16),
    )


def _rope_tables(pos):
    half = A_ROPE // 2
    inv_freq = ROPE_THETA ** (-jnp.arange(half, dtype=F32) / half)
    ang = pos.astype(F32)[:, None] * inv_freq[None, :]
    cos, sin = jnp.cos(ang), jnp.sin(ang)
    pad = jnp.zeros((pos.shape[0], LANES - A_ROPE), F32)
    return (jnp.concatenate([cos, cos, pad], axis=1), jnp.concatenate([-sin, sin, pad], axis=1))


def _s5_weights(a_re, a_im, b_re, b_im, c_re, c_im, d, log_dt, w_glu):
    dt = jnp.exp(log_dt.astype(F32))[:, None]
    ar, ai = a_re.astype(F32), a_im.astype(F32)
    mag = jnp.exp(dt * ar)
    ab_r, ab_i = mag * jnp.cos(dt * ai), mag * jnp.sin(dt * ai)
    den = ar * ar + ai * ai
    c_r = ((ab_r - 1.0) * ar + ab_i * ai) / den
    c_i = (ab_i * ar - (ab_r - 1.0) * ai) / den
    br, bi = b_re.astype(F32), b_im.astype(F32)
    bb_r = c_r[..., None] * br - c_i[..., None] * bi
    bb_i = c_r[..., None] * bi + c_i[..., None] * br
    gb = S5_GROUPS // S5_BLOCKS
    eye = jnp.eye(gb, dtype=F32)
    bd_in = lambda m: jnp.einsum("kgpc,gh->kgchp", m.reshape(S5_BLOCKS, gb, S5_P, S5_GSIZE),
                                 eye).reshape(S5_BLOCKS, S5_BLK_IN, S5_BLK_ST)
    bd_out = lambda m: jnp.einsum("kgcp,gh->kgphc", m.reshape(S5_BLOCKS, gb, S5_GSIZE, S5_P),
                                  eye).reshape(S5_BLOCKS, S5_BLK_ST, S5_BLK_IN)
    return dict(
        bbr=bd_in(bb_r).astype(BF16), bbi=bd_in(bb_i).astype(BF16),
        ar=ab_r.reshape(1, S5_STATE), ai=ab_i.reshape(1, S5_STATE),
        ccr=bd_out(c_re.astype(F32)).astype(BF16), cci=bd_out(-c_im.astype(F32)).astype(BF16),
        d=d.reshape(1, S5_W).astype(F32), wglu=w_glu.astype(BF16),
    )


def _pad_keys(a, mult):
    pad = (-a.shape[1]) % mult
    return jnp.pad(a, ((0, 0), (0, pad), (0, 0))) if pad else a


def _trunk(x, pos0, cache, p):
    nb, l, _ = x.shape
    assert nb == SUBLANES, "the S5 layout puts the streams of one time step on the sublanes"
    assert l % CHUNK == 0
    n = nb * l
    tm = min(TOKEN_TILE, n)
    tl = min(tm, l)
    assert l % tl == 0 and tm % tl == 0
    past = 0 if cache is None else cache["sb_k"].shape[1]
    assert pos0 == past
    xf = x.reshape(n, D_MODEL)
    bf = lambda a: a.astype(BF16)

    xf = _ffn(xf, p["norm_ffn"][0, 0], bf(p["ffn_w_gate"][0, 0]), bf(p["ffn_w_up"][0, 0]),
              bf(p["ffn_w_down"][0, 0]), tm=tm)
    ew = _even_weights(p["even_w_in"][0], p["mlstm_b_igate"][0], p["mlstm_b_fgate"][0],
                       p["mla_q_norm"][0], p["mla_kv_norm"][0], p["mla_w_uq"][0], p["mla_w_ukv"][0])
    cos, sin = _rope_tables(pos0 + jnp.arange(l))
    if tm > l:
        cos, sin = jnp.tile(cos, (tm // l, 1)), jnp.tile(sin, (tm // l, 1))
    qm, km, vm, om, gates, q, kn, v, kr128, lat, kr = _even_in(
        xf, p["norm_mix"][0], ew, cos, sin, tm=tm)

    if cache is None:
        c0 = jnp.zeros((nb, M_HEADS, M_DH, M_DH), F32)
        n0 = jnp.zeros((nb, M_HEADS, 1, M_DH), F32)
        m0 = jnp.zeros((nb, M_HEADS, 1, LANES), F32)
    else:
        c0 = cache["mlstm_c"].astype(F32)
        n0 = cache["mlstm_n"].astype(F32).reshape(nb, M_HEADS, 1, M_DH)
        m0 = jnp.broadcast_to(cache["mlstm_m"].astype(F32)[..., None, None], (nb, M_HEADS, 1, LANES))
    r3 = lambda a: a.reshape(nb, l, a.shape[-1])
    h_m, c_new, n_new, m_new = _mlstm(r3(qm), r3(km), r3(vm), r3(om), r3(gates),
                                      p["mlstm_out_norm"][0], c0, n0, m0, tl=tl)

    kn3, v3, kr3 = r3(kn), r3(v), r3(kr128)
    if cache is not None:
        ckn, cv = _kv_expand(cache["mla_latent"].astype(F32).reshape(nb * past, KV_LORA),
                             ew["wuk"], ew["wuv"])
        ckr = jnp.pad(cache["mla_krope"], ((0, 0), (0, 0), (0, LANES - A_ROPE))).astype(BF16)
        kn3 = jnp.concatenate([ckn.reshape(nb, past, -1), kn3], axis=1)
        v3 = jnp.concatenate([cv.reshape(nb, past, -1), v3], axis=1)
        kr3 = jnp.concatenate([ckr, kr3], axis=1)
    mla_tq = min(MLA_TILE, -(-l // LANES) * LANES)
    mla_tk = mla_tq if cache is None else -(-(past + max(l, mla_tq)) // LANES) * LANES
    h_a = _mla_attention(_pad_keys(r3(q), mla_tq), _pad_keys(kn3, mla_tk), _pad_keys(kr3, mla_tk),
                         _pad_keys(v3, mla_tk), tq=mla_tq, tk=mla_tk, offset=past)[:, :l]

    w_out = bf(p["even_w_out"][0])
    xf = _ffn(xf, p["norm_ffn"][0, 1], bf(p["ffn_w_gate"][0, 1]), bf(p["ffn_w_up"][0, 1]),
              bf(p["ffn_w_down"][0, 1]),
              mix=(h_m.reshape(n, M_W), h_a.reshape(n, -1), w_out[:M_W], w_out[M_W:]), tm=tm)

    xf = _ffn(xf, p["norm_ffn"][1, 0], bf(p["ffn_w_gate"][1, 0]), bf(p["ffn_w_up"][1, 0]),
              bf(p["ffn_w_down"][1, 0]), tm=tm)
    w_in = p["odd_w_in"][0]
    ow = dict(wu=bf(w_in[:, :S5_W]), wq=bf(w_in[:, S5_W:S5_W + SB_W]),
              wk=bf(w_in[:, S5_W + SB_W:S5_W + 2 * SB_W]), wv=bf(w_in[:, S5_W + 2 * SB_W:]))
    u, qs, ks, vs, ksb, vsb = _odd_in(xf, p["norm_mix"][1], ow, tm=tm)

    sw = _s5_weights(p["s5_A_re"][0], p["s5_A_im"][0], p["s5_B_re"][0], p["s5_B_im"][0],
                     p["s5_C_re"][0], p["s5_C_im"][0], p["s5_D"][0], p["s5_log_dt"][0], p["s5_w_glu"][0])
    if cache is None:
        x0r = jnp.zeros((nb, S5_STATE), F32)
        x0i = jnp.zeros((nb, S5_STATE), F32)
    else:
        x0r = cache["s5_re"].astype(F32).reshape(nb, S5_STATE)
        x0i = cache["s5_im"].astype(F32).reshape(nb, S5_STATE)
    h_s, xr_new, xi_new = _s5(r3(u), sw, x0r, x0i, tt=min(S5_TT, l))

    k3, v3 = r3(ksb), r3(vsb)
    if cache is not None:
        k3 = jnp.concatenate([bf(cache["sb_k"].reshape(nb, past, SB_W)), k3], axis=1)
        v3 = jnp.concatenate([bf(cache["sb_v"].reshape(nb, past, SB_W)), v3], axis=1)
    h_b = _sb_attention(r3(qs), _pad_keys(k3, SB_TILE), _pad_keys(v3, SB_TILE), tq=min(SB_TILE, l),
                        tk=SB_TILE, offset=past)

    w_out = bf(p["odd_w_out"][0])
    y = _ffn(xf, p["norm_ffn"][1, 1], bf(p["ffn_w_gate"][1, 1]), bf(p["ffn_w_up"][1, 1]),
             bf(p["ffn_w_down"][1, 1]),
             mix=(h_s.reshape(n, S5_W), h_b.reshape(n, SB_W), w_out[:S5_W], w_out[S5_W:]),
             final_gain=p["norm_final"], tm=tm)

    return (y.reshape(nb, l, D_MODEL),
            lat.reshape(1, nb, l, KV_LORA), kr.reshape(1, nb, l, A_ROPE),
            c_new[None], n_new.reshape(1, nb, M_HEADS, M_DH), m_new[:, :, 0, 0][None],
            xr_new.reshape(1, nb, S5_GROUPS, S5_P), xi_new.reshape(1, nb, S5_GROUPS, S5_P),
            ks.reshape(1, nb, l, SB_HEADS, SB_DH), vs.reshape(1, nb, l, SB_HEADS, SB_DH))


def kernel(x_prompt, x_sample, cache_mla_latent, cache_mla_krope, state_mlstm_C, state_mlstm_n, state_mlstm_m, state_s5_re, state_s5_im, cache_sb_k, cache_sb_v, norm_ffn, norm_mix, norm_final, ffn_w_gate, ffn_w_up, ffn_w_down, even_w_in, even_w_out, mlstm_b_igate, mlstm_b_fgate, mlstm_out_norm, mla_q_norm, mla_kv_norm, mla_w_uq, mla_w_ukv, odd_w_in, odd_w_out, s5_A_re, s5_A_im, s5_B_re, s5_B_im, s5_C_re, s5_C_im, s5_D, s5_log_dt, s5_w_glu):
    p = dict(norm_ffn=norm_ffn, norm_mix=norm_mix, norm_final=norm_final, ffn_w_gate=ffn_w_gate,
             ffn_w_up=ffn_w_up, ffn_w_down=ffn_w_down, even_w_in=even_w_in, even_w_out=even_w_out,
             mlstm_b_igate=mlstm_b_igate, mlstm_b_fgate=mlstm_b_fgate, mlstm_out_norm=mlstm_out_norm,
             mla_q_norm=mla_q_norm, mla_kv_norm=mla_kv_norm, mla_w_uq=mla_w_uq, mla_w_ukv=mla_w_ukv,
             odd_w_in=odd_w_in, odd_w_out=odd_w_out, s5_A_re=s5_A_re, s5_A_im=s5_A_im, s5_B_re=s5_B_re,
             s5_B_im=s5_B_im, s5_C_re=s5_C_re, s5_C_im=s5_C_im, s5_D=s5_D, s5_log_dt=s5_log_dt,
             s5_w_glu=s5_w_glu)
    past = cache_sb_k.shape[2]
    cache = dict(mla_latent=cache_mla_latent[0], mla_krope=cache_mla_krope[0],
                 mlstm_c=state_mlstm_C[0], mlstm_n=state_mlstm_n[0], mlstm_m=state_mlstm_m[0],
                 s5_re=state_s5_re[0], s5_im=state_s5_im[0],
                 sb_k=cache_sb_k[0], sb_v=cache_sb_v[0])
    out_p = _trunk(x_prompt, 0, None, p)
    out_s = _trunk(x_sample, past, cache, p)
    return (out_p[0], out_s[0]) + tuple(out_p[1:]) + tuple(out_s[1:])
```

```python
import functools
import math

import jax
import jax.numpy as jnp
import numpy as np
from jax import lax
from jax.experimental import pallas as pl
from jax.experimental.pallas import tpu as pltpu

F32 = jnp.float32
BF16 = jnp.bfloat16

D_MODEL = 1024
CHUNK = 64
D_FF = 2816
EPS = 1e-6
M_HEADS = 4
M_DH = 128
M_W = M_HEADS * M_DH
A_HEADS = 4
A_NOPE = 128
A_ROPE = 64
A_VDIM = 128
Q_LORA = 384
KV_LORA = 256
ROPE_THETA = 10000.0
MLA_SCALE = (A_NOPE + A_ROPE) ** -0.5
S5_W = 512
S5_GSIZE = 16
S5_GROUPS = S5_W // S5_GSIZE
S5_P = 64
S5_STATE = S5_GROUPS * S5_P
SB_HEADS = 8
SB_DH = 64
SB_W = SB_HEADS * SB_DH

LANES = 128
SUBLANES = 8
VMEM_LIMIT_BYTES = 56 * 1024 * 1024

TOKEN_TILE = 512
FF_CHUNK = 256
MLA_TILE = 512
SB_TILE = 128
SB_TILES_PER_STEP = 4
S5_TT = 128
MLSTM_STREAMS = 2
SB_LOG_FLOOR = -24.0
LOG2E = math.log2(math.e)

HIGHEST = lax.Precision.HIGHEST


def _cparams(*sem):
    return pltpu.CompilerParams(dimension_semantics=sem, vmem_limit_bytes=VMEM_LIMIT_BYTES)


def _bdot(a, b):
    return jnp.dot(a.astype(BF16), b.astype(BF16), preferred_element_type=F32)


def _bdot_nt(a, b):
    return lax.dot_general(a.astype(BF16), b.astype(BF16), (((1,), (1,)), ((), ())),
                           preferred_element_type=F32)


def _rms(x, g):
    return x * lax.rsqrt(jnp.mean(x * x, axis=-1, keepdims=True) + EPS) * g


def _sigmoid(x):
    return 1.0 / (1.0 + jnp.exp(-x))


def _log_sigmoid(x):
    return jnp.minimum(x, 0.0) - jnp.log(1.0 + jnp.exp(-jnp.abs(x)))


def _const_spec(shape):
    nd = len(shape)
    return pl.BlockSpec(shape, lambda *_: (0,) * nd)


def _ffn_kernel(*refs, with_mix, with_final):
    it = iter(refs)
    x_ref = next(it)
    if with_mix:
        a_ref, b_ref, wa_ref, wb_ref = next(it), next(it), next(it), next(it)
    g_ref, wg_ref, wu_ref, wd_ref = next(it), next(it), next(it), next(it)
    gf_ref = next(it) if with_final else None
    o_ref = next(it)
    acc_ref = next(it)

    x = x_ref[...]
    if with_mix:
        x = x + _bdot(a_ref[...], wa_ref[...]) + _bdot(b_ref[...], wb_ref[...])
    h = _rms(x, g_ref[...]).astype(BF16)
    acc_ref[...] = jnp.zeros_like(acc_ref)
    for c in range(D_FF // FF_CHUNK):
        sl = slice(c * FF_CHUNK, (c + 1) * FF_CHUNK)
        g = jnp.dot(h, wg_ref[:, sl], preferred_element_type=F32)
        u = jnp.dot(h, wu_ref[:, sl], preferred_element_type=F32)
        a = (g * _sigmoid(g) * u).astype(BF16)
        acc_ref[...] += jnp.dot(a, wd_ref[sl, :], preferred_element_type=F32)
    y = x + 0.5 * acc_ref[...]
    if with_final:
        y = _rms(y, gf_ref[...])
    o_ref[...] = y


def _ffn(x, gain, wg, wu, wd, *, mix=None, final_gain=None, tm):
    n = x.shape[0]
    row = lambda i: (i, 0)
    args, specs = [x], [pl.BlockSpec((tm, D_MODEL), row)]
    if mix is not None:
        a, b, wa, wb = mix
        args += [a, b, wa, wb]
        specs += [pl.BlockSpec((tm, a.shape[-1]), row), pl.BlockSpec((tm, b.shape[-1]), row),
                  _const_spec(wa.shape), _const_spec(wb.shape)]
    args += [gain.reshape(1, D_MODEL), wg, wu, wd]
    specs += [_const_spec((1, D_MODEL)), _const_spec(wg.shape), _const_spec(wu.shape),
              _const_spec(wd.shape)]
    if final_gain is not None:
        args.append(final_gain.reshape(1, D_MODEL))
        specs.append(_const_spec((1, D_MODEL)))
    return pl.pallas_call(
        functools.partial(_ffn_kernel, with_mix=mix is not None, with_final=final_gain is not None),
        grid=(n // tm,),
        in_specs=specs,
        out_specs=pl.BlockSpec((tm, D_MODEL), row),
        out_shape=jax.ShapeDtypeStruct((n, D_MODEL), F32),
        scratch_shapes=[pltpu.VMEM((tm, D_MODEL), F32)],
        compiler_params=_cparams("parallel"),
        name="ffn",
    )(*args)


def _even_in_kernel(x_ref, gain_ref, wm_ref, wg_ref, bg_ref, wcq_ref, wckv_ref, wkra_ref, wkrb_ref,
                    qn_ref, kvn_ref, wq_ref, ws_ref, wuk_ref, wuv_ref, cos_ref, sin_ref,
                    qm_o, km_o, vm_o, om_o, gates_o, q_o, kn_o, v_o, kr128_o, lat_o, kr_o,
                    *, v_transposed):
    h = _rms(x_ref[...], gain_ref[...]).astype(BF16)
    m = jnp.dot(h, wm_ref[...], preferred_element_type=F32)
    qm_o[...] = m[:, :M_W].astype(BF16)
    km_o[...] = (m[:, M_W:2 * M_W] * M_DH ** -0.5).astype(BF16)
    vm_o[...] = m[:, 2 * M_W:3 * M_W].astype(BF16)
    om_o[...] = m[:, 3 * M_W:]

    g = jnp.dot(h, wg_ref[...], preferred_element_type=F32) + bg_ref[...]
    lane = lax.broadcasted_iota(jnp.int32, g.shape, 1)
    gates_o[...] = jnp.where(lane < M_HEADS, g, _log_sigmoid(g))

    cos, sin = cos_ref[...], sin_ref[...]
    cq = jnp.dot(h, wcq_ref[...], preferred_element_type=F32)
    cqn = _rms(cq, qn_ref[...]).astype(BF16)
    qf = jnp.dot(cqn, wq_ref[...], preferred_element_type=F32)
    qs = jnp.dot(cqn, ws_ref[...], preferred_element_type=F32)
    q_scale = MLA_SCALE * LOG2E
    for hh in range(A_HEADS):
        base = hh * 2 * LANES
        q_o[:, base:base + LANES] = (qf[:, base:base + LANES] * q_scale).astype(BF16)
        rot = qf[:, base + LANES:base + 2 * LANES] * cos + qs[:, hh * LANES:(hh + 1) * LANES] * sin
        q_o[:, base + LANES:base + 2 * LANES] = (rot * q_scale).astype(BF16)

    ckv = jnp.dot(h, wckv_ref[...], preferred_element_type=F32)
    lat = _rms(ckv, kvn_ref[...])
    lat_o[...] = lat
    latb = lat.astype(BF16)
    kn_o[...] = jnp.dot(latb, wuk_ref[...], preferred_element_type=F32).astype(BF16)
    if v_transposed:
        v_o[0] = _bdot_nt(wuv_ref[...], latb).astype(BF16)
    else:
        v_o[...] = jnp.dot(latb, wuv_ref[...], preferred_element_type=F32).astype(BF16)

    kr = (jnp.dot(h, wkra_ref[...], preferred_element_type=F32) * cos
          + jnp.dot(h, wkrb_ref[...], preferred_element_type=F32) * sin)
    kr128_o[...] = kr.astype(BF16)
    kr_o[...] = kr[:, :A_ROPE]


def _even_in(x, gain, w, cos, sin, *, tm, seq_len):
    n = x.shape[0]
    row = lambda i: (i, 0)
    n_pos_tiles = cos.shape[0] // tm
    pos = lambda i: (i % n_pos_tiles, 0)
    v_transposed = seq_len % tm == 0
    n_time_tiles = max(seq_len // tm, 1)
    weights = [w["wm"], w["wg"], w["bg"], w["wcq"], w["wckv"], w["wkra"], w["wkrb"], w["qn"], w["kvn"],
               w["wq"], w["ws"], w["wuk"], w["wuv"].T if v_transposed else w["wuv"]]
    in_specs = ([pl.BlockSpec((tm, D_MODEL), row), _const_spec((1, D_MODEL))]
                + [_const_spec(a.shape) for a in weights]
                + [pl.BlockSpec((tm, LANES), pos), pl.BlockSpec((tm, LANES), pos)])
    outs = [(M_W, BF16), (M_W, BF16), (M_W, BF16), (M_W, F32), (LANES, F32),
            (A_HEADS * 2 * LANES, BF16), (A_HEADS * A_NOPE, BF16), (A_HEADS * A_VDIM, BF16),
            (LANES, BF16), (KV_LORA, F32), (A_ROPE, F32)]
    out_specs = [pl.BlockSpec((tm, c), row) for c, _ in outs]
    out_shape = [jax.ShapeDtypeStruct((n, c), dt) for c, dt in outs]
    if v_transposed:
        v_idx = 7
        out_specs[v_idx] = pl.BlockSpec((1, A_HEADS * A_VDIM, tm),
                                        lambda i: (i // n_time_tiles, 0, i % n_time_tiles))
        out_shape[v_idx] = jax.ShapeDtypeStruct((n // seq_len, A_HEADS * A_VDIM, seq_len), BF16)
    return pl.pallas_call(
        functools.partial(_even_in_kernel, v_transposed=v_transposed),
        grid=(n // tm,),
        in_specs=in_specs,
        out_specs=out_specs,
        out_shape=out_shape,
        compiler_params=_cparams("parallel"),
        name="even_in",
    )(x, gain.reshape(1, D_MODEL), *weights, cos, sin)


def _kv_expand_kernel(lat_ref, wuk_ref, wuv_ref, kn_o, v_o):
    latb = lat_ref[...].astype(BF16)
    kn_o[...] = jnp.dot(latb, wuk_ref[...], preferred_element_type=F32).astype(BF16)
    v_o[...] = jnp.dot(latb, wuv_ref[...], preferred_element_type=F32).astype(BF16)


def _kv_expand(lat, wuk, wuv):
    n = lat.shape[0]
    tm = min(4 * TOKEN_TILE, n)
    assert n % tm == 0
    row = lambda i: (i, 0)
    return pl.pallas_call(
        _kv_expand_kernel,
        grid=(n // tm,),
        in_specs=[pl.BlockSpec((tm, KV_LORA), row), _const_spec(wuk.shape), _const_spec(wuv.shape)],
        out_specs=[pl.BlockSpec((tm, wuk.shape[1]), row), pl.BlockSpec((tm, wuv.shape[1]), row)],
        out_shape=[jax.ShapeDtypeStruct((n, wuk.shape[1]), BF16),
                   jax.ShapeDtypeStruct((n, wuv.shape[1]), BF16)],
        compiler_params=_cparams("parallel"),
        name="kv_expand",
    )(lat, wuk, wuv)


def _mlstm_kernel(qm_ref, km_ref, vm_ref, om_ref, g_ref, onorm_ref, c0_ref, n0_ref, m0_ref,
                  h_ref, cout_ref, nout_ref, mout_ref, c_sc, n_sc, m_sc, *, n_chunks, n_streams):
    lt = pl.program_id(1)

    @pl.when(lt == 0)
    def _():
        c_sc[...] = c0_ref[...]
        n_sc[...] = n0_ref[...]
        m_sc[...] = m0_ref[...]

    row = lax.broadcasted_iota(jnp.int32, (CHUNK, CHUNK), 0)
    col = lax.broadcasted_iota(jnp.int32, (CHUNK, CHUNK), 1)
    causal = col <= row
    ltri = causal.astype(BF16)
    utri = (row <= col).astype(BF16)
    ones_kl = jnp.ones((CHUNK, LANES), BF16)

    def split3(a):
        hi = a.astype(BF16)
        r1 = a - hi.astype(F32)
        mid = r1.astype(BF16)
        return hi, mid, (r1 - mid.astype(F32)).astype(BF16)

    def chunk(c, carry):
        r0 = pl.multiple_of(c * CHUNK, CHUNK)
        rows = pl.ds(r0, CHUNK)
        chains = [(bb, hh) for bb in range(n_streams) for hh in range(M_HEADS)]
        hsl = lambda hh: slice(hh * M_DH, (hh + 1) * M_DH)
        gate = {}
        for bb in range(n_streams):
            gates = g_ref[bb, rows, :]
            gates_t = gates.T
            cc = jnp.dot(ltri, jnp.concatenate(split3(gates), axis=1), preferred_element_type=F32)
            cum_col = cc[:, :LANES] + cc[:, LANES:2 * LANES] + cc[:, 2 * LANES:]
            cr = jnp.dot(jnp.concatenate(split3(gates_t), axis=0), utri, preferred_element_type=F32)
            cum_row = cr[:LANES] + cr[LANES:2 * LANES] + cr[2 * LANES:]
            gate[bb] = (gates, gates_t, cum_col, cum_row)
        qk, qc, qn = {}, {}, {}
        for ch in chains:
            bb, hh = ch
            rhs = jnp.concatenate([c_sc[bb, hh].astype(BF16),
                                   jnp.broadcast_to(n_sc[bb, hh], (LANES, M_DH)).astype(BF16),
                                   km_ref[bb, rows, hsl(hh)]], axis=0)
            r = _bdot_nt(qm_ref[bb, rows, hsl(hh)], rhs)
            qc[ch], qn[ch], qk[ch] = r[:, :M_DH], r[:, M_DH:M_DH + LANES], r[:, M_DH + LANES:]
        st = {}
        for ch in chains:
            bb, hh = ch
            gates, gates_t, cum_col, cum_row = gate[bb]
            ig_row = gates_t[hh:hh + 1, :]
            ig_col = gates[:, hh:hh + 1]
            b_row = cum_row[M_HEADS + hh:M_HEADS + hh + 1, :]
            b_col = cum_col[:, M_HEADS + hh:M_HEADS + hh + 1]
            m_prev = m_sc[bb, hh][:, :1]
            b_lanes = jnp.broadcast_to(b_col, (CHUNK, LANES))
            log_d = jnp.where(causal, b_lanes[:, :CHUNK] - b_row + ig_row, -jnp.inf)
            m_t = jnp.maximum(b_col + m_prev, jnp.max(log_d, axis=-1, keepdims=True))
            m_lanes = jnp.broadcast_to(m_t, (CHUNK, LANES))
            s = (qk[ch] * jnp.exp(log_d - m_lanes[:, :CHUNK])).astype(BF16)
            w_inter = jnp.exp(b_lanes + m_prev - m_lanes)
            m_new = m_t[CHUNK - 1:CHUNK, :]
            b_last = b_col[CHUNK - 1:CHUNK, :]
            w_s = jnp.exp(b_last - b_col + ig_col - m_new)
            decay = jnp.exp(b_last + m_prev - m_new)
            r = jnp.dot(s, jnp.concatenate([vm_ref[bb, rows, hsl(hh)], ones_kl], axis=1),
                        preferred_element_type=F32)
            st[ch] = (r[:, :M_DH], r[:, M_DH:], w_inter, m_lanes, m_new, w_s, decay)
        for ch in chains:
            bb, hh = ch
            sv, s_sum, w_inter, m_lanes, m_new, w_s, decay = st[ch]
            k = km_ref[bb, rows, hsl(hh)]
            v = vm_ref[bb, rows, hsl(hh)]
            c_h = c_sc[bb, hh]
            n_h = n_sc[bb, hh]
            num = w_inter * qc[ch] + sv
            den = w_inter * qn[ch] + s_sum
            hv = num / jnp.maximum(jnp.abs(den), jnp.exp(-m_lanes))
            wv_t = (jnp.broadcast_to(w_s, (CHUNK, M_DH)) * v).T
            c_sc[bb, hh] = decay * c_h + _bdot(wv_t, k)
            n_sc[bb, hh] = decay * n_h + jnp.sum(w_s * k.astype(F32), axis=0, keepdims=True)
            m_sc[bb, hh] = jnp.broadcast_to(m_new, (1, LANES))
            hn = _rms(hv, onorm_ref[:, hsl(hh)])
            h_ref[bb, rows, hsl(hh)] = (_sigmoid(om_ref[bb, rows, hsl(hh)]) * hn).astype(BF16)
        return carry

    lax.fori_loop(0, n_chunks, chunk, 0)

    @pl.when(lt == pl.num_programs(1) - 1)
    def _():
        cout_ref[...] = c_sc[...]
        nout_ref[...] = n_sc[...]
        mout_ref[...] = m_sc[...]


def _mlstm(qm, km, vm, om, gates, onorm, c0, n0, m0, *, tl):
    b, l, _ = qm.shape
    ns = MLSTM_STREAMS
    tile = lambda i, j: (i, j, 0)
    st4 = lambda i, j: (i, 0, 0, 0)
    state = [((ns, M_HEADS, M_DH, M_DH), (b, M_HEADS, M_DH, M_DH)),
             ((ns, M_HEADS, 1, M_DH), (b, M_HEADS, 1, M_DH)),
             ((ns, M_HEADS, 1, LANES), (b, M_HEADS, 1, LANES))]
    return pl.pallas_call(
        functools.partial(_mlstm_kernel, n_chunks=tl // CHUNK, n_streams=ns),
        grid=(b // ns, l // tl),
        in_specs=[pl.BlockSpec((ns, tl, M_W), tile)] * 4
        + [pl.BlockSpec((ns, tl, LANES), tile), _const_spec((1, M_W))]
        + [pl.BlockSpec(blk, st4) for blk, _ in state],
        out_specs=[pl.BlockSpec((ns, tl, M_W), tile)] + [pl.BlockSpec(blk, st4) for blk, _ in state],
        out_shape=[jax.ShapeDtypeStruct((b, l, M_W), BF16)]
        + [jax.ShapeDtypeStruct(full, F32) for _, full in state],
        scratch_shapes=[pltpu.VMEM(blk, F32) for blk, _ in state],
        compiler_params=_cparams("parallel", "arbitrary"),
        name="mlstm",
    )(qm, km, vm, om, gates, onorm.reshape(1, M_W), c0, n0, m0)


MLA_SUM_ROWS = 16


def _mla_kernel(qi_ref, kj_ref, flag_ref, q_ref, kn_ref, kr_ref, vt_ref, o_ref, m_sc, acc_sc,
                *, tq, tkb, offset, variants):
    p = pl.program_id(1)
    i = qi_ref[p]
    j = kj_ref[p]
    flags = flag_ref[p]

    @pl.when(j == 0)
    def _():
        m_sc[...] = jnp.full_like(m_sc, -jnp.inf)
        acc_sc[...] = jnp.zeros_like(acc_sc)

    def sweep(nk, masked):
        kr = kr_ref[0, :nk, :]
        ones = jnp.ones((MLA_SUM_ROWS, nk), BF16)
        scores = []
        for hh in range(A_HEADS):
            q = q_ref[0, :, hh * 2 * LANES:(hh + 1) * 2 * LANES]
            k = jnp.concatenate([kn_ref[0, :nk, hh * A_NOPE:(hh + 1) * A_NOPE], kr], axis=-1)
            scores.append(_bdot_nt(k, q))
        if masked:
            k_chunk = (j * tkb + lax.broadcasted_iota(jnp.int32, (nk, tq), 0)) // CHUNK
            q_chunk = (offset + i * tq + lax.broadcasted_iota(jnp.int32, (nk, tq), 1)) // CHUNK
            visible = k_chunk <= q_chunk
            scores = [jnp.where(visible, s, -jnp.inf) for s in scores]
        for hh in range(A_HEADS):
            s = scores[hh]
            m_prev = m_sc[hh]
            m_new = jnp.maximum(m_prev, jnp.max(s, axis=0, keepdims=True))
            alpha = jnp.exp2(m_prev - m_new)
            pr = jnp.exp2(s - m_new).astype(BF16)
            vt = jnp.concatenate([vt_ref[0, hh * A_VDIM:(hh + 1) * A_VDIM, :nk], ones], axis=0)
            acc_sc[hh] = alpha * acc_sc[hh] + jnp.dot(vt, pr, preferred_element_type=F32)
            m_sc[hh] = m_new

    for vi, (nk, masked) in enumerate(variants):
        pl.when((flags >> 1) == vi)(functools.partial(sweep, nk, masked))

    @pl.when((flags & 1) != 0)
    def _():
        for hh in range(A_HEADS):
            acc = acc_sc[hh]
            o_ref[0, :, hh * A_VDIM:(hh + 1) * A_VDIM] = (
                acc[:A_VDIM] / acc[A_VDIM:A_VDIM + 1]).T.astype(BF16)


def _transpose_kernel(x_ref, o_ref):
    o_ref[0] = x_ref[0].astype(F32).T.astype(o_ref.dtype)


def _transpose_tiles(x, t):
    b, l, c = x.shape
    return pl.pallas_call(
        _transpose_kernel,
        grid=(b, l // t),
        in_specs=[pl.BlockSpec((1, t, c), lambda bb, i: (bb, i, 0))],
        out_specs=pl.BlockSpec((1, c, t), lambda bb, i: (bb, 0, i)),
        out_shape=jax.ShapeDtypeStruct((b, c, l), x.dtype),
        compiler_params=_cparams("parallel", "parallel"),
        name="transpose_tiles",
    )(x)


def _mla_pairs(lq, lk, tq, tkb, sub, offset):
    qi, kj, flags, variants = [], [], [], []
    for i in range(lq // tq):
        first_row, last_row = offset + i * tq, offset + (i + 1) * tq - 1
        k_end = min((last_row // CHUNK + 1) * CHUNK, lk)
        k_all = (first_row // CHUNK + 1) * CHUNK
        nj = -(-k_end // tkb)
        for j in range(nj):
            nk = -(-(min(k_end, (j + 1) * tkb) - j * tkb) // sub) * sub
            variant = (nk, j * tkb + nk > k_all)
            if variant not in variants:
                variants.append(variant)
            qi.append(i)
            kj.append(j)
            flags.append(int(j == nj - 1) + 2 * variants.index(variant))
    return (np.asarray(qi, np.int32), np.asarray(kj, np.int32), np.asarray(flags, np.int32),
            tuple(variants))


def _mla_attention(q, kn, kr, vt, *, tq, tkb, sub, offset):
    b, lq, _ = q.shape
    lk = kn.shape[1]
    assert lk % tkb == 0 and tkb % sub == 0 and vt.shape[2] == lk
    qi, kj, flags, variants = _mla_pairs(lq, lk, tq, tkb, sub, offset)
    qmap = lambda bb, p, qi_r, kj_r, flag_r: (bb, qi_r[p], 0)
    kmap = lambda bb, p, qi_r, kj_r, flag_r: (bb, kj_r[p], 0)
    grid_spec = pltpu.PrefetchScalarGridSpec(
        num_scalar_prefetch=3,
        grid=(b, len(qi)),
        in_specs=[pl.BlockSpec((1, tq, q.shape[-1]), qmap),
                  pl.BlockSpec((1, tkb, kn.shape[-1]), kmap),
                  pl.BlockSpec((1, tkb, kr.shape[-1]), kmap),
                  pl.BlockSpec((1, vt.shape[1], tkb), lambda bb, p, qi_r, kj_r, flag_r: (bb, 0, kj_r[p]))],
        out_specs=pl.BlockSpec((1, tq, A_HEADS * A_VDIM), qmap),
        scratch_shapes=[pltpu.VMEM((A_HEADS, 1, tq), F32),
                        pltpu.VMEM((A_HEADS, A_VDIM + MLA_SUM_ROWS, tq), F32)],
    )
    return pl.pallas_call(
        functools.partial(_mla_kernel, tq=tq, tkb=tkb, offset=offset, variants=variants),
        grid_spec=grid_spec,
        out_shape=jax.ShapeDtypeStruct((b, lq, A_HEADS * A_VDIM), BF16),
        compiler_params=_cparams("parallel", "arbitrary"),
        name="mla_attention",
    )(jnp.asarray(qi), jnp.asarray(kj), jnp.asarray(flags), q, kn, kr, vt)


def _odd_in_kernel(x_ref, gain_ref, wu_ref, wq_ref, wk_ref, wv_ref, u_o, q_o, k_o, v_o, kb_o, vb_o):
    h = _rms(x_ref[...], gain_ref[...]).astype(BF16)
    u_o[...] = jnp.dot(h, wu_ref[...], preferred_element_type=F32)
    q_o[...] = (jnp.dot(h, wq_ref[...], preferred_element_type=F32) * SB_DH ** -0.5).astype(BF16)
    k = jnp.dot(h, wk_ref[...], preferred_element_type=F32)
    v = jnp.dot(h, wv_ref[...], preferred_element_type=F32)
    k_o[...] = k
    v_o[...] = v
    kb_o[...] = k.astype(BF16)
    vb_o[...] = v.astype(BF16)


def _odd_in(x, gain, w, *, tm):
    n = x.shape[0]
    row = lambda i: (i, 0)
    weights = [w["wu"], w["wq"], w["wk"], w["wv"]]
    return pl.pallas_call(
        _odd_in_kernel,
        grid=(n // tm,),
        in_specs=[pl.BlockSpec((tm, D_MODEL), row), _const_spec((1, D_MODEL))]
        + [_const_spec(a.shape) for a in weights],
        out_specs=[pl.BlockSpec((tm, S5_W), row)] + [pl.BlockSpec((tm, SB_W), row)] * 5,
        out_shape=[jax.ShapeDtypeStruct((n, S5_W), F32),
                   jax.ShapeDtypeStruct((n, SB_W), BF16),
                   jax.ShapeDtypeStruct((n, SB_W), F32),
                   jax.ShapeDtypeStruct((n, SB_W), F32),
                   jax.ShapeDtypeStruct((n, SB_W), BF16),
                   jax.ShapeDtypeStruct((n, SB_W), BF16)],
        compiler_params=_cparams("parallel"),
        name="odd_in",
    )(x, gain.reshape(1, D_MODEL), *weights)


S5_BLOCKS = 4
S5_BLK_IN = S5_W // S5_BLOCKS
S5_BLK_ST = S5_STATE // S5_BLOCKS


def _s5_kernel(u_ref, bbr_ref, bbi_ref, ar_ref, ai_ref, ccr_ref, cci_ref, d_ref, wglu_ref,
               x0r_ref, x0i_ref, hs_o, xr_o, xi_o, u_tm, hs_tm, xr_buf, xi_buf, xr_st, xi_st,
               *, tt, nb):
    t = pl.program_id(0)

    @pl.when(t == 0)
    def _():
        xr_st[...] = x0r_ref[...]
        xi_st[...] = x0i_ref[...]

    for bb in range(nb):
        for k in range(S5_BLOCKS):
            u_tm[k, pl.ds(bb, tt, stride=nb), :] = u_ref[bb, :, k * S5_BLK_IN:(k + 1) * S5_BLK_IN]

    for k in range(S5_BLOCKS):
        cs = slice(k * S5_BLK_ST, (k + 1) * S5_BLK_ST)
        ub = u_tm[k].astype(BF16)
        xr_buf[:, cs] = jnp.dot(ub, bbr_ref[k], preferred_element_type=F32)
        xi_buf[:, cs] = jnp.dot(ub, bbi_ref[k], preferred_element_type=F32)

    for k in range(S5_BLOCKS):
        cs = slice(k * S5_BLK_ST, (k + 1) * S5_BLK_ST)
        ar = jnp.broadcast_to(ar_ref[:, cs], (nb, S5_BLK_ST))
        ai = jnp.broadcast_to(ai_ref[:, cs], (nb, S5_BLK_ST))

        def step(s, carry, cs=cs, ar=ar, ai=ai):
            xr, xi = carry
            rows = pl.ds(pl.multiple_of(s * nb, nb), nb)
            nxr = ar * xr - ai * xi + xr_buf[rows, cs]
            nxi = ar * xi + ai * xr + xi_buf[rows, cs]
            xr_buf[rows, cs] = nxr
            xi_buf[rows, cs] = nxi
            return nxr, nxi

        xr, xi = lax.fori_loop(0, tt, step, (xr_st[:, cs], xi_st[:, cs]), unroll=8)
        xr_st[:, cs] = xr
        xi_st[:, cs] = xi

    gs = []
    for k in range(S5_BLOCKS):
        cs = slice(k * S5_BLK_ST, (k + 1) * S5_BLK_ST)
        y = (_bdot(xr_buf[:, cs], ccr_ref[k]) + _bdot(xi_buf[:, cs], cci_ref[k])
             + d_ref[:, k * S5_BLK_IN:(k + 1) * S5_BLK_IN] * u_tm[k])
        gs.append(jax.nn.gelu(y, approximate=True))
    g = jnp.concatenate(gs, axis=-1)
    hs = g * _sigmoid(_bdot(g, wglu_ref[...]))
    for k in range(S5_BLOCKS):
        hs_tm[k] = hs[:, k * S5_BLK_IN:(k + 1) * S5_BLK_IN]
    for bb in range(nb):
        for k in range(S5_BLOCKS):
            hs_o[bb, :, k * S5_BLK_IN:(k + 1) * S5_BLK_IN] = (
                hs_tm[k, pl.ds(bb, tt, stride=nb), :].astype(BF16))

    @pl.when(t == pl.num_programs(0) - 1)
    def _():
        xr_o[...] = xr_st[...]
        xi_o[...] = xi_st[...]


def _s5(u, w, x0r, x0i, *, tt):
    nb, l, _ = u.shape
    blk = lambda t: (0, t, 0)
    consts = [w["bbr"], w["bbi"], w["ar"], w["ai"], w["ccr"], w["cci"], w["d"], w["wglu"], x0r, x0i]
    return pl.pallas_call(
        functools.partial(_s5_kernel, tt=tt, nb=nb),
        grid=(l // tt,),
        in_specs=[pl.BlockSpec((nb, tt, S5_W), blk)] + [_const_spec(a.shape) for a in consts],
        out_specs=[pl.BlockSpec((nb, tt, S5_W), blk), _const_spec((nb, S5_STATE)),
                   _const_spec((nb, S5_STATE))],
        out_shape=[jax.ShapeDtypeStruct((nb, l, S5_W), BF16),
                   jax.ShapeDtypeStruct((nb, S5_STATE), F32),
                   jax.ShapeDtypeStruct((nb, S5_STATE), F32)],
        scratch_shapes=[pltpu.VMEM((S5_BLOCKS, tt * nb, S5_BLK_IN), F32),
                        pltpu.VMEM((S5_BLOCKS, tt * nb, S5_BLK_IN), F32),
                        pltpu.VMEM((tt * nb, S5_STATE), F32), pltpu.VMEM((tt * nb, S5_STATE), F32),
                        pltpu.VMEM((nb, S5_STATE), F32), pltpu.VMEM((nb, S5_STATE), F32)],
        compiler_params=_cparams("arbitrary"),
        name="s5",
    )(u, *consts)


def _sb_kernel(q_ref, k_ref, v_ref, o_ref, r_sc, acc_sc, qh_sc, *, tq, tk, offset, n_sub):
    for su in range(n_sub):
        _sb_query_tile(pl.program_id(1) * n_sub + su, slice(su * tq, (su + 1) * tq),
                       q_ref, k_ref, v_ref, o_ref, r_sc, acc_sc, qh_sc, tq=tq, tk=tk, offset=offset)


def _sb_query_tile(i, rs, q_ref, k_ref, v_ref, o_ref, r_sc, acc_sc, qh_sc, *, tq, tk, offset):
    n_pairs = SB_HEADS // 2
    half = lax.broadcasted_iota(jnp.int32, (1, LANES), 1) // SB_DH
    for pr in range(n_pairs):
        qq = q_ref[0, rs, pr * LANES:(pr + 1) * LANES]
        qh_sc[pr, :tq] = jnp.where(half == 0, qq, jnp.zeros_like(qq))
        qh_sc[pr, tq:] = jnp.where(half == 1, qq, jnp.zeros_like(qq))
    r_sc[...] = jnp.zeros_like(r_sc)
    acc_sc[...] = jnp.zeros_like(acc_sc)

    jrow = lax.broadcasted_iota(jnp.int32, (tk, tk), 0)
    jcol = lax.broadcasted_iota(jnp.int32, (tk, tk), 1)
    later = (jrow > jcol).astype(BF16)
    j_diag = (offset + i * tq) // tk

    def sweep(tiles):
        zs, vv, causal = {}, {}, {}
        for t, (j, masked, live) in enumerate(tiles):
            k0 = pl.multiple_of(j * tk, tk)
            if masked:
                q_pos = offset + i * tq + lax.broadcasted_iota(jnp.int32, (tq, tk), 0)
                causal[t] = (k0 + lax.broadcasted_iota(jnp.int32, (tq, tk), 1)) < q_pos
            for pr in range(n_pairs):
                ps = slice(pr * LANES, (pr + 1) * LANES)
                zs[t, pr] = _bdot_nt(qh_sc[pr], k_ref[0, pl.ds(k0, tk), ps])
                vv[t, pr] = v_ref[0, pl.ds(k0, tk), ps]
        l1s, lbs, cums = {}, {}, {}
        for t, (j, masked, live) in enumerate(tiles):
            for pr in range(n_pairs):
                parts = []
                for sub in range(2):
                    hh = 2 * pr + sub
                    z = zs[t, pr][sub * tq:(sub + 1) * tq]
                    sp = jnp.maximum(z, 0.0) + jnp.log(1.0 + jnp.exp(-jnp.abs(z)))
                    l1 = -sp
                    if masked:
                        l1 = jnp.where(causal[t], l1, 0.0)
                    if live is not None:
                        l1 = jnp.where(live, l1, 0.0)
                    hi = l1.astype(BF16)
                    parts += [hi, (l1 - hi.astype(F32)).astype(BF16)]
                    l1s[t, hh] = l1
                    lbs[t, hh] = z - sp
                c = jnp.dot(jnp.concatenate(parts, axis=0), later, preferred_element_type=F32)
                cums[t, 2 * pr] = c[:tq] + c[tq:2 * tq]
                cums[t, 2 * pr + 1] = c[2 * tq:3 * tq] + c[3 * tq:]
        for t, (j, masked, live) in enumerate(tiles):
            r_max = jnp.full((1, 1), -jnp.inf, F32)
            for pr in range(n_pairs):
                ws = []
                for sub in range(2):
                    hh = 2 * pr + sub
                    r_old = r_sc[hh]
                    log_w = lbs[t, hh] + cums[t, hh] + r_old
                    if masked:
                        log_w = jnp.where(causal[t], log_w, -jnp.inf)
                    if live is not None:
                        log_w = jnp.where(live, log_w, -jnp.inf)
                    ws.append(jnp.exp(log_w).astype(BF16))
                    r_new = r_old + jnp.sum(l1s[t, hh], axis=-1, keepdims=True)
                    r_sc[hh] = r_new
                    r_max = jnp.maximum(r_max, jnp.max(r_new, axis=0, keepdims=True))
                v2 = jnp.concatenate([jnp.where(half == 0, vv[t, pr], jnp.zeros_like(vv[t, pr])),
                                      jnp.where(half == 1, vv[t, pr], jnp.zeros_like(vv[t, pr]))],
                                     axis=0)
                acc_sc[:, pr * LANES:(pr + 1) * LANES] += jnp.dot(
                    jnp.concatenate(ws, axis=1), v2, preferred_element_type=F32)
        return r_max[0, 0]

    has_prev = j_diag >= 1
    r_first = sweep([(j_diag, True, None), (jnp.maximum(j_diag - 1, 0), False, has_prev)])

    def cond(carry):
        j, r_max = carry
        return jnp.logical_and(j >= 0, r_max > SB_LOG_FLOOR)

    def body(carry):
        j, _ = carry
        return j - 1, sweep([(j, False, None)])

    lax.while_loop(cond, body, (j_diag - 2, r_first))
    o_ref[0, rs] = acc_sc[...].astype(BF16)


def _sb_attention(q, k, v, *, tq, tk, offset):
    b, lq, _ = q.shape
    lk = k.shape[1]
    assert tk % tq == 0 and offset % tk == 0 and lk % tk == 0
    n_sub = math.gcd(lq // tq, SB_TILES_PER_STEP)
    return pl.pallas_call(
        functools.partial(_sb_kernel, tq=tq, tk=tk, offset=offset, n_sub=n_sub),
        grid=(b, lq // (tq * n_sub)),
        in_specs=[pl.BlockSpec((1, tq * n_sub, SB_W), lambda bb, i: (bb, i, 0)),
                  pl.BlockSpec((1, lk, SB_W), lambda bb, i: (bb, 0, 0)),
                  pl.BlockSpec((1, lk, SB_W), lambda bb, i: (bb, 0, 0))],
        out_specs=pl.BlockSpec((1, tq * n_sub, SB_W), lambda bb, i: (bb, i, 0)),
        out_shape=jax.ShapeDtypeStruct((b, lq, SB_W), BF16),
        scratch_shapes=[pltpu.VMEM((SB_HEADS, tq, 1), F32),
                        pltpu.VMEM((tq, SB_W), F32),
                        pltpu.VMEM((SB_HEADS // 2, 2 * tq, LANES), BF16)],
        compiler_params=_cparams("parallel", "arbitrary"),
        name="sb_attention",
    )(q, k, v)


def _pad_cols(w, width):
    return jnp.pad(w, ((0, 0), (0, width - w.shape[1])))


def _even_weights(w_in, b_ig, b_fg, q_norm, kv_norm, w_uq, w_ukv):
    o = 4 * M_W
    w_gate = w_in[:, o:o + 2 * M_HEADS]
    o += 2 * M_HEADS
    w_cq = w_in[:, o:o + Q_LORA]
    o += Q_LORA
    w_ckv = w_in[:, o:o + KV_LORA]
    o += KV_LORA
    w_kr = w_in[:, o:o + A_ROPE]
    hr = A_ROPE // 2
    wkra = _pad_cols(w_kr, LANES)
    wkrb = _pad_cols(jnp.concatenate([w_kr[:, hr:], w_kr[:, :hr]], axis=1), LANES)
    uq = w_uq.reshape(Q_LORA, A_HEADS, A_NOPE + A_ROPE)
    zeros = jnp.zeros((Q_LORA, A_HEADS, LANES - A_ROPE), w_uq.dtype)
    wq = jnp.concatenate([uq, zeros], axis=-1).reshape(Q_LORA, A_HEADS * 2 * LANES)
    ws = jnp.concatenate([uq[..., A_NOPE + hr:], uq[..., A_NOPE:A_NOPE + hr], zeros],
                         axis=-1).reshape(Q_LORA, A_HEADS * LANES)
    ukv = w_ukv.reshape(KV_LORA, A_HEADS, A_NOPE + A_VDIM)
    return dict(
        wm=w_in[:, :4 * M_W].astype(BF16),
        wg=_pad_cols(w_gate, LANES).astype(BF16),
        bg=_pad_cols(jnp.concatenate([b_ig, b_fg]).reshape(1, 2 * M_HEADS).astype(F32), LANES),
        wcq=w_cq.astype(BF16), wckv=w_ckv.astype(BF16),
        wkra=wkra.astype(BF16), wkrb=wkrb.astype(BF16),
        qn=q_norm.reshape(1, Q_LORA).astype(F32), kvn=kv_norm.reshape(1, KV_LORA).astype(F32),
        wq=wq.astype(BF16), ws=ws.astype(BF16),
        wuk=ukv[..., :A_NOPE].reshape(KV_LORA, A_HEADS * A_NOPE).astype(BF16),
        wuv=ukv[..., A_NOPE:].reshape(KV_LORA, A_HEADS * A_VDIM).astype(BF16),
    )


def _rope_tables(pos):
    half = A_ROPE // 2
    inv_freq = ROPE_THETA ** (-jnp.arange(half, dtype=F32) / half)
    ang = pos.astype(F32)[:, None] * inv_freq[None, :]
    cos, sin = jnp.cos(ang), jnp.sin(ang)
    pad = jnp.zeros((pos.shape[0], LANES - A_ROPE), F32)
    return (jnp.concatenate([cos, cos, pad], axis=1), jnp.concatenate([-sin, sin, pad], axis=1))


def _s5_weights(a_re, a_im, b_re, b_im, c_re, c_im, d, log_dt, w_glu):
    dt = jnp.exp(log_dt.astype(F32))[:, None]
    ar, ai = a_re.astype(F32), a_im.astype(F32)
    mag = jnp.exp(dt * ar)
    ab_r, ab_i = mag * jnp.cos(dt * ai), mag * jnp.sin(dt * ai)
    den = ar * ar + ai * ai
    c_r = ((ab_r - 1.0) * ar + ab_i * ai) / den
    c_i = (ab_i * ar - (ab_r - 1.0) * ai) / den
    br, bi = b_re.astype(F32), b_im.astype(F32)
    bb_r = c_r[..., None] * br - c_i[..., None] * bi
    bb_i = c_r[..., None] * bi + c_i[..., None] * br
    gb = S5_GROUPS // S5_BLOCKS
    eye = jnp.eye(gb, dtype=F32)
    bd_in = lambda m: jnp.einsum("kgpc,gh->kgchp", m.reshape(S5_BLOCKS, gb, S5_P, S5_GSIZE),
                                 eye).reshape(S5_BLOCKS, S5_BLK_IN, S5_BLK_ST)
    bd_out = lambda m: jnp.einsum("kgcp,gh->kgphc", m.reshape(S5_BLOCKS, gb, S5_GSIZE, S5_P),
                                  eye).reshape(S5_BLOCKS, S5_BLK_ST, S5_BLK_IN)
    return dict(
        bbr=bd_in(bb_r).astype(BF16), bbi=bd_in(bb_i).astype(BF16),
        ar=ab_r.reshape(1, S5_STATE), ai=ab_i.reshape(1, S5_STATE),
        ccr=bd_out(c_re.astype(F32)).astype(BF16), cci=bd_out(-c_im.astype(F32)).astype(BF16),
        d=d.reshape(1, S5_W).astype(F32), wglu=w_glu.astype(BF16),
    )


def _pad_keys(a, mult):
    pad = (-a.shape[1]) % mult
    return jnp.pad(a, ((0, 0), (0, pad), (0, 0))) if pad else a


def _trunk(x, pos0, cache, p):
    nb, l, _ = x.shape
    assert nb == SUBLANES, "the S5 layout puts the streams of one time step on the sublanes"
    assert l % CHUNK == 0
    n = nb * l
    tm = min(TOKEN_TILE, n)
    tl = min(tm, l)
    assert l % tl == 0 and tm % tl == 0
    past = 0 if cache is None else cache["sb_k"].shape[1]
    assert pos0 == past
    xf = x.reshape(n, D_MODEL)
    bf = lambda a: a.astype(BF16)

    xf = _ffn(xf, p["norm_ffn"][0, 0], bf(p["ffn_w_gate"][0, 0]), bf(p["ffn_w_up"][0, 0]),
              bf(p["ffn_w_down"][0, 0]), tm=tm)
    ew = _even_weights(p["even_w_in"][0], p["mlstm_b_igate"][0], p["mlstm_b_fgate"][0],
                       p["mla_q_norm"][0], p["mla_kv_norm"][0], p["mla_w_uq"][0], p["mla_w_ukv"][0])
    cos, sin = _rope_tables(pos0 + jnp.arange(l))
    if tm > l:
        cos, sin = jnp.tile(cos, (tm // l, 1)), jnp.tile(sin, (tm // l, 1))
    qm, km, vm, om, gates, q, kn, v, kr128, lat, kr = _even_in(
        xf, p["norm_mix"][0], ew, cos, sin, tm=tm, seq_len=l)
    v_transposed = v.ndim == 3

    if cache is None:
        c0 = jnp.zeros((nb, M_HEADS, M_DH, M_DH), F32)
        n0 = jnp.zeros((nb, M_HEADS, 1, M_DH), F32)
        m0 = jnp.zeros((nb, M_HEADS, 1, LANES), F32)
    else:
        c0 = cache["mlstm_c"].astype(F32)
        n0 = cache["mlstm_n"].astype(F32).reshape(nb, M_HEADS, 1, M_DH)
        m0 = jnp.broadcast_to(cache["mlstm_m"].astype(F32)[..., None, None], (nb, M_HEADS, 1, LANES))
    r3 = lambda a: a.reshape(nb, l, a.shape[-1])
    h_m, c_new, n_new, m_new = _mlstm(r3(qm), r3(km), r3(vm), r3(om), r3(gates),
                                      p["mlstm_out_norm"][0], c0, n0, m0, tl=tl)

    mla_tq = min(MLA_TILE, -(-l // LANES) * LANES)
    if cache is None:
        mla_sub = mla_tq
        mla_tkb = min(2 * mla_tq, l)
    else:
        mla_sub = mla_tkb = -(-(past + max(l, mla_tq)) // LANES) * LANES
    kn3, kr3 = r3(kn), r3(kr128)
    v3, vt = (None, v) if v_transposed else (r3(v), None)
    if cache is not None:
        ckn, cv = _kv_expand(cache["mla_latent"].astype(F32).reshape(nb * past, KV_LORA),
                             ew["wuk"], ew["wuv"])
        ckr = jnp.pad(cache["mla_krope"], ((0, 0), (0, 0), (0, LANES - A_ROPE))).astype(BF16)
        kn3 = jnp.concatenate([ckn.reshape(nb, past, -1), kn3], axis=1)
        kr3 = jnp.concatenate([ckr, kr3], axis=1)
        cv3 = cv.reshape(nb, past, -1)
        if v_transposed:
            vt = jnp.concatenate([_transpose_tiles(cv3, math.gcd(past, TOKEN_TILE)), vt], axis=2)
        else:
            v3 = jnp.concatenate([cv3, v3], axis=1)
    if vt is None:
        vt = _transpose_tiles(_pad_keys(v3, mla_tkb), mla_sub)
    else:
        vt = jnp.pad(vt, ((0, 0), (0, 0), (0, (-vt.shape[2]) % mla_tkb)))
    h_a = _mla_attention(_pad_keys(r3(q), mla_tq), _pad_keys(kn3, mla_tkb), _pad_keys(kr3, mla_tkb),
                         vt, tq=mla_tq, tkb=mla_tkb, sub=mla_sub, offset=past)[:, :l]

    w_out = bf(p["even_w_out"][0])
    xf = _ffn(xf, p["norm_ffn"][0, 1], bf(p["ffn_w_gate"][0, 1]), bf(p["ffn_w_up"][0, 1]),
              bf(p["ffn_w_down"][0, 1]),
              mix=(h_m.reshape(n, M_W), h_a.reshape(n, -1), w_out[:M_W], w_out[M_W:]), tm=tm)

    xf = _ffn(xf, p["norm_ffn"][1, 0], bf(p["ffn_w_gate"][1, 0]), bf(p["ffn_w_up"][1, 0]),
              bf(p["ffn_w_down"][1, 0]), tm=tm)
    w_in = p["odd_w_in"][0]
    ow = dict(wu=bf(w_in[:, :S5_W]), wq=bf(w_in[:, S5_W:S5_W + SB_W]),
              wk=bf(w_in[:, S5_W + SB_W:S5_W + 2 * SB_W]), wv=bf(w_in[:, S5_W + 2 * SB_W:]))
    u, qs, ks, vs, ksb, vsb = _odd_in(xf, p["norm_mix"][1], ow, tm=tm)

    sw = _s5_weights(p["s5_A_re"][0], p["s5_A_im"][0], p["s5_B_re"][0], p["s5_B_im"][0],
                     p["s5_C_re"][0], p["s5_C_im"][0], p["s5_D"][0], p["s5_log_dt"][0], p["s5_w_glu"][0])
    if cache is None:
        x0r = jnp.zeros((nb, S5_STATE), F32)
        x0i = jnp.zeros((nb, S5_STATE), F32)
    else:
        x0r = cache["s5_re"].astype(F32).reshape(nb, S5_STATE)
        x0i = cache["s5_im"].astype(F32).reshape(nb, S5_STATE)
    h_s, xr_new, xi_new = _s5(r3(u), sw, x0r, x0i, tt=min(S5_TT, l))

    k3, v3 = r3(ksb), r3(vsb)
    if cache is not None:
        k3 = jnp.concatenate([bf(cache["sb_k"].reshape(nb, past, SB_W)), k3], axis=1)
        v3 = jnp.concatenate([bf(cache["sb_v"].reshape(nb, past, SB_W)), v3], axis=1)
    h_b = _sb_attention(r3(qs), _pad_keys(k3, SB_TILE), _pad_keys(v3, SB_TILE), tq=min(SB_TILE, l),
                        tk=SB_TILE, offset=past)

    w_out = bf(p["odd_w_out"][0])
    y = _ffn(xf, p["norm_ffn"][1, 1], bf(p["ffn_w_gate"][1, 1]), bf(p["ffn_w_up"][1, 1]),
             bf(p["ffn_w_down"][1, 1]),
             mix=(h_s.reshape(n, S5_W), h_b.reshape(n, SB_W), w_out[:S5_W], w_out[S5_W:]),
             final_gain=p["norm_final"], tm=tm)

    return (y.reshape(nb, l, D_MODEL),
            lat.reshape(1, nb, l, KV_LORA), kr.reshape(1, nb, l, A_ROPE),
            c_new[None], n_new.reshape(1, nb, M_HEADS, M_DH), m_new[:, :, 0, 0][None],
            xr_new.reshape(1, nb, S5_GROUPS, S5_P), xi_new.reshape(1, nb, S5_GROUPS, S5_P),
            ks.reshape(1, nb, l, SB_HEADS, SB_DH), vs.reshape(1, nb, l, SB_HEADS, SB_DH))


def kernel(x_prompt, x_sample, cache_mla_latent, cache_mla_krope, state_mlstm_C, state_mlstm_n, state_mlstm_m, state_s5_re, state_s5_im, cache_sb_k, cache_sb_v, norm_ffn, norm_mix, norm_final, ffn_w_gate, ffn_w_up, ffn_w_down, even_w_in, even_w_out, mlstm_b_igate, mlstm_b_fgate, mlstm_out_norm, mla_q_norm, mla_kv_norm, mla_w_uq, mla_w_ukv, odd_w_in, odd_w_out, s5_A_re, s5_A_im, s5_B_re, s5_B_im, s5_C_re, s5_C_im, s5_D, s5_log_dt, s5_w_glu):
    p = dict(norm_ffn=norm_ffn, norm_mix=norm_mix, norm_final=norm_final, ffn_w_gate=ffn_w_gate,
             ffn_w_up=ffn_w_up, ffn_w_down=ffn_w_down, even_w_in=even_w_in, even_w_out=even_w_out,
             mlstm_b_igate=mlstm_b_igate, mlstm_b_fgate=mlstm_b_fgate, mlstm_out_norm=mlstm_out_norm,
             mla_q_norm=mla_q_norm, mla_kv_norm=mla_kv_norm, mla_w_uq=mla_w_uq, mla_w_ukv=mla_w_ukv,
             odd_w_in=odd_w_in, odd_w_out=odd_w_out, s5_A_re=s5_A_re, s5_A_im=s5_A_im, s5_B_re=s5_B_re,
             s5_B_im=s5_B_im, s5_C_re=s5_C_re, s5_C_im=s5_C_im, s5_D=s5_D, s5_log_dt=s5_log_dt,
             s5_w_glu=s5_w_glu)
    past = cache_sb_k.shape[2]
    cache = dict(mla_latent=cache_mla_latent[0], mla_krope=cache_mla_krope[0],
                 mlstm_c=state_mlstm_C[0], mlstm_n=state_mlstm_n[0], mlstm_m=state_mlstm_m[0],
                 s5_re=state_s5_re[0], s5_im=state_s5_im[0],
                 sb_k=cache_sb_k[0], sb_v=cache_sb_v[0])
    out_p = _trunk(x_prompt, 0, None, p)
    out_s = _trunk(x_sample, past, cache, p)
    return (out_p[0], out_s[0]) + tuple(out_p[1:]) + tuple(out_s[1:])
```

```python
import functools
import math

import jax
import jax.numpy as jnp
import numpy as np
from jax import lax
from jax.experimental import pallas as pl
from jax.experimental.pallas import tpu as pltpu

F32 = jnp.float32
BF16 = jnp.bfloat16

D_MODEL = 1024
CHUNK = 64
D_FF = 2816
EPS = 1e-6
M_HEADS = 4
M_DH = 128
M_W = M_HEADS * M_DH
A_HEADS = 4
A_NOPE = 128
A_ROPE = 64
A_VDIM = 128
Q_LORA = 384
KV_LORA = 256
ROPE_THETA = 10000.0
MLA_SCALE = (A_NOPE + A_ROPE) ** -0.5
S5_W = 512
S5_GSIZE = 16
S5_GROUPS = S5_W // S5_GSIZE
S5_P = 64
S5_STATE = S5_GROUPS * S5_P
SB_HEADS = 8
SB_DH = 64
SB_W = SB_HEADS * SB_DH

LANES = 128
SUBLANES = 8
VMEM_LIMIT_BYTES = 56 * 1024 * 1024

TOKEN_TILE = 512
FF_CHUNK = 256
MLA_TILE = 512
SB_TILE = 128
SB_TILES_PER_STEP = 4
S5_TT = 128
MLSTM_STREAMS = 4
SB_LOG_FLOOR = -24.0
LOG2E = math.log2(math.e)

HIGHEST = lax.Precision.HIGHEST


def _cparams(*sem):
    return pltpu.CompilerParams(dimension_semantics=sem, vmem_limit_bytes=VMEM_LIMIT_BYTES)


def _bdot(a, b):
    return jnp.dot(a.astype(BF16), b.astype(BF16), preferred_element_type=F32)


def _bdot_nt(a, b):
    return lax.dot_general(a.astype(BF16), b.astype(BF16), (((1,), (1,)), ((), ())),
                           preferred_element_type=F32)


def _rms(x, g):
    return x * lax.rsqrt(jnp.mean(x * x, axis=-1, keepdims=True) + EPS) * g


def _sigmoid(x):
    return 1.0 / (1.0 + jnp.exp(-x))


def _log_sigmoid(x):
    return jnp.minimum(x, 0.0) - jnp.log(1.0 + jnp.exp(-jnp.abs(x)))


def _const_spec(shape):
    nd = len(shape)
    return pl.BlockSpec(shape, lambda *_: (0,) * nd)


def _ffn_kernel(*refs, with_mix, with_final):
    it = iter(refs)
    x_ref = next(it)
    if with_mix:
        a_ref, b_ref, wa_ref, wb_ref = next(it), next(it), next(it), next(it)
    g_ref, wg_ref, wu_ref, wd_ref = next(it), next(it), next(it), next(it)
    gf_ref = next(it) if with_final else None
    o_ref = next(it)
    acc_ref = next(it)

    x = x_ref[...]
    if with_mix:
        x = x + _bdot(a_ref[...], wa_ref[...]) + _bdot(b_ref[...], wb_ref[...])
    h = _rms(x, g_ref[...]).astype(BF16)
    acc_ref[...] = jnp.zeros_like(acc_ref)
    for c in range(D_FF // FF_CHUNK):
        sl = slice(c * FF_CHUNK, (c + 1) * FF_CHUNK)
        g = jnp.dot(h, wg_ref[:, sl], preferred_element_type=F32)
        u = jnp.dot(h, wu_ref[:, sl], preferred_element_type=F32)
        a = (g * _sigmoid(g) * u).astype(BF16)
        acc_ref[...] += jnp.dot(a, wd_ref[sl, :], preferred_element_type=F32)
    y = x + 0.5 * acc_ref[...]
    if with_final:
        y = _rms(y, gf_ref[...])
    o_ref[...] = y


def _ffn(x, gain, wg, wu, wd, *, mix=None, final_gain=None, tm):
    n = x.shape[0]
    row = lambda i: (i, 0)
    args, specs = [x], [pl.BlockSpec((tm, D_MODEL), row)]
    if mix is not None:
        a, b, wa, wb = mix
        args += [a, b, wa, wb]
        specs += [pl.BlockSpec((tm, a.shape[-1]), row), pl.BlockSpec((tm, b.shape[-1]), row),
                  _const_spec(wa.shape), _const_spec(wb.shape)]
    args += [gain.reshape(1, D_MODEL), wg, wu, wd]
    specs += [_const_spec((1, D_MODEL)), _const_spec(wg.shape), _const_spec(wu.shape),
              _const_spec(wd.shape)]
    if final_gain is not None:
        args.append(final_gain.reshape(1, D_MODEL))
        specs.append(_const_spec((1, D_MODEL)))
    return pl.pallas_call(
        functools.partial(_ffn_kernel, with_mix=mix is not None, with_final=final_gain is not None),
        grid=(n // tm,),
        in_specs=specs,
        out_specs=pl.BlockSpec((tm, D_MODEL), row),
        out_shape=jax.ShapeDtypeStruct((n, D_MODEL), F32),
        scratch_shapes=[pltpu.VMEM((tm, D_MODEL), F32)],
        compiler_params=_cparams("parallel"),
        name="ffn",
    )(*args)


def _even_in_kernel(x_ref, gain_ref, wm_ref, wg_ref, bg_ref, wcq_ref, wckv_ref, wkra_ref, wkrb_ref,
                    qn_ref, kvn_ref, wq_ref, ws_ref, wuk_ref, wuv_ref, cos_ref, sin_ref,
                    qm_o, km_o, vm_o, om_o, gates_o, q_o, kn_o, v_o, kr128_o, lat_o, kr_o,
                    *, v_transposed):
    h = _rms(x_ref[...], gain_ref[...]).astype(BF16)
    m = jnp.dot(h, wm_ref[...], preferred_element_type=F32)
    qm_o[...] = m[:, :M_W].astype(BF16)
    km_o[...] = (m[:, M_W:2 * M_W] * M_DH ** -0.5).astype(BF16)
    vm_o[...] = m[:, 2 * M_W:3 * M_W].astype(BF16)
    om_o[...] = m[:, 3 * M_W:]

    g = jnp.dot(h, wg_ref[...], preferred_element_type=F32) + bg_ref[...]
    lane = lax.broadcasted_iota(jnp.int32, g.shape, 1)
    gates_o[...] = jnp.where(lane < M_HEADS, g, _log_sigmoid(g))

    cos, sin = cos_ref[...], sin_ref[...]
    cq = jnp.dot(h, wcq_ref[...], preferred_element_type=F32)
    cqn = _rms(cq, qn_ref[...]).astype(BF16)
    qf = jnp.dot(cqn, wq_ref[...], preferred_element_type=F32)
    qs = jnp.dot(cqn, ws_ref[...], preferred_element_type=F32)
    q_scale = MLA_SCALE * LOG2E
    for hh in range(A_HEADS):
        base = hh * 2 * LANES
        q_o[:, base:base + LANES] = (qf[:, base:base + LANES] * q_scale).astype(BF16)
        rot = qf[:, base + LANES:base + 2 * LANES] * cos + qs[:, hh * LANES:(hh + 1) * LANES] * sin
        q_o[:, base + LANES:base + 2 * LANES] = (rot * q_scale).astype(BF16)

    ckv = jnp.dot(h, wckv_ref[...], preferred_element_type=F32)
    lat = _rms(ckv, kvn_ref[...])
    lat_o[...] = lat
    latb = lat.astype(BF16)
    kn_o[...] = jnp.dot(latb, wuk_ref[...], preferred_element_type=F32).astype(BF16)
    if v_transposed:
        v_o[0] = _bdot_nt(wuv_ref[...], latb).astype(BF16)
    else:
        v_o[...] = jnp.dot(latb, wuv_ref[...], preferred_element_type=F32).astype(BF16)

    kr = (jnp.dot(h, wkra_ref[...], preferred_element_type=F32) * cos
          + jnp.dot(h, wkrb_ref[...], preferred_element_type=F32) * sin)
    kr128_o[...] = kr.astype(BF16)
    kr_o[...] = kr[:, :A_ROPE]


def _even_in(x, gain, w, cos, sin, *, tm, seq_len):
    n = x.shape[0]
    row = lambda i: (i, 0)
    n_pos_tiles = cos.shape[0] // tm
    pos = lambda i: (i % n_pos_tiles, 0)
    v_transposed = seq_len % tm == 0
    n_time_tiles = max(seq_len // tm, 1)
    weights = [w["wm"], w["wg"], w["bg"], w["wcq"], w["wckv"], w["wkra"], w["wkrb"], w["qn"], w["kvn"],
               w["wq"], w["ws"], w["wuk"], w["wuv"].T if v_transposed else w["wuv"]]
    in_specs = ([pl.BlockSpec((tm, D_MODEL), row), _const_spec((1, D_MODEL))]
                + [_const_spec(a.shape) for a in weights]
                + [pl.BlockSpec((tm, LANES), pos), pl.BlockSpec((tm, LANES), pos)])
    outs = [(M_W, BF16), (M_W, BF16), (M_W, BF16), (M_W, F32), (LANES, F32),
            (A_HEADS * 2 * LANES, BF16), (A_HEADS * A_NOPE, BF16), (A_HEADS * A_VDIM, BF16),
            (LANES, BF16), (KV_LORA, F32), (A_ROPE, F32)]
    out_specs = [pl.BlockSpec((tm, c), row) for c, _ in outs]
    out_shape = [jax.ShapeDtypeStruct((n, c), dt) for c, dt in outs]
    if v_transposed:
        v_idx = 7
        out_specs[v_idx] = pl.BlockSpec((1, A_HEADS * A_VDIM, tm),
                                        lambda i: (i // n_time_tiles, 0, i % n_time_tiles))
        out_shape[v_idx] = jax.ShapeDtypeStruct((n // seq_len, A_HEADS * A_VDIM, seq_len), BF16)
    return pl.pallas_call(
        functools.partial(_even_in_kernel, v_transposed=v_transposed),
        grid=(n // tm,),
        in_specs=in_specs,
        out_specs=out_specs,
        out_shape=out_shape,
        compiler_params=_cparams("parallel"),
        name="even_in",
    )(x, gain.reshape(1, D_MODEL), *weights, cos, sin)


def _kv_expand_kernel(lat_ref, wuk_ref, wuv_ref, kn_o, v_o):
    latb = lat_ref[...].astype(BF16)
    kn_o[...] = jnp.dot(latb, wuk_ref[...], preferred_element_type=F32).astype(BF16)
    v_o[...] = jnp.dot(latb, wuv_ref[...], preferred_element_type=F32).astype(BF16)


def _kv_expand(lat, wuk, wuv):
    n = lat.shape[0]
    tm = min(4 * TOKEN_TILE, n)
    assert n % tm == 0
    row = lambda i: (i, 0)
    return pl.pallas_call(
        _kv_expand_kernel,
        grid=(n // tm,),
        in_specs=[pl.BlockSpec((tm, KV_LORA), row), _const_spec(wuk.shape), _const_spec(wuv.shape)],
        out_specs=[pl.BlockSpec((tm, wuk.shape[1]), row), pl.BlockSpec((tm, wuv.shape[1]), row)],
        out_shape=[jax.ShapeDtypeStruct((n, wuk.shape[1]), BF16),
                   jax.ShapeDtypeStruct((n, wuv.shape[1]), BF16)],
        compiler_params=_cparams("parallel"),
        name="kv_expand",
    )(lat, wuk, wuv)


def _mlstm_kernel(qm_ref, km_ref, vm_ref, om_ref, g_ref, onorm_ref, c0_ref, n0_ref, m0_ref,
                  h_ref, cout_ref, nout_ref, mout_ref, c_sc, n_sc, m_sc, *, n_chunks, n_streams):
    lt = pl.program_id(1)

    @pl.when(lt == 0)
    def _():
        c_sc[...] = c0_ref[...]
        n_sc[...] = n0_ref[...]
        m_sc[...] = m0_ref[...]

    row = lax.broadcasted_iota(jnp.int32, (CHUNK, CHUNK), 0)
    col = lax.broadcasted_iota(jnp.int32, (CHUNK, CHUNK), 1)
    causal = col <= row
    ltri = causal.astype(BF16)
    utri = (row <= col).astype(BF16)
    ones_kl = jnp.ones((CHUNK, LANES), BF16)

    def split3(a):
        hi = a.astype(BF16)
        r1 = a - hi.astype(F32)
        mid = r1.astype(BF16)
        return hi, mid, (r1 - mid.astype(F32)).astype(BF16)

    def chunk(c, carry):
        r0 = pl.multiple_of(c * CHUNK, CHUNK)
        rows = pl.ds(r0, CHUNK)
        chains = [(bb, hh) for bb in range(n_streams) for hh in range(M_HEADS)]
        hsl = lambda hh: slice(hh * M_DH, (hh + 1) * M_DH)
        gate = {}
        for bb in range(n_streams):
            gates = g_ref[bb, rows, :]
            gates_t = gates.T
            cc = jnp.dot(ltri, jnp.concatenate(split3(gates), axis=1), preferred_element_type=F32)
            cum_col = cc[:, :LANES] + cc[:, LANES:2 * LANES] + cc[:, 2 * LANES:]
            cr = jnp.dot(jnp.concatenate(split3(gates_t), axis=0), utri, preferred_element_type=F32)
            cum_row = cr[:LANES] + cr[LANES:2 * LANES] + cr[2 * LANES:]
            gate[bb] = (gates, gates_t, cum_col, cum_row)
        qk, qc, qn = {}, {}, {}
        for ch in chains:
            bb, hh = ch
            rhs = jnp.concatenate([c_sc[bb, hh].astype(BF16),
                                   jnp.broadcast_to(n_sc[bb, hh], (LANES, M_DH)).astype(BF16),
                                   km_ref[bb, rows, hsl(hh)]], axis=0)
            r = _bdot_nt(qm_ref[bb, rows, hsl(hh)], rhs)
            qc[ch], qn[ch], qk[ch] = r[:, :M_DH], r[:, M_DH:M_DH + LANES], r[:, M_DH + LANES:]
        st = {}
        for ch in chains:
            bb, hh = ch
            gates, gates_t, cum_col, cum_row = gate[bb]
            ig_row = gates_t[hh:hh + 1, :]
            ig_col = gates[:, hh:hh + 1]
            b_row = cum_row[M_HEADS + hh:M_HEADS + hh + 1, :]
            b_col = cum_col[:, M_HEADS + hh:M_HEADS + hh + 1]
            m_prev = m_sc[bb, hh][:, :1]
            b_lanes = jnp.broadcast_to(b_col, (CHUNK, LANES))
            log_d = jnp.where(causal, b_lanes[:, :CHUNK] - b_row + ig_row, -jnp.inf)
            m_t = jnp.maximum(b_col + m_prev, jnp.max(log_d, axis=-1, keepdims=True))
            m_lanes = jnp.broadcast_to(m_t, (CHUNK, LANES))
            s = (qk[ch] * jnp.exp(log_d - m_lanes[:, :CHUNK])).astype(BF16)
            w_inter = jnp.exp(b_lanes + m_prev - m_lanes)
            m_new = m_t[CHUNK - 1:CHUNK, :]
            b_last = b_col[CHUNK - 1:CHUNK, :]
            w_s = jnp.exp(b_last - b_col + ig_col - m_new)
            decay = jnp.exp(b_last + m_prev - m_new)
            r = jnp.dot(s, jnp.concatenate([vm_ref[bb, rows, hsl(hh)], ones_kl], axis=1),
                        preferred_element_type=F32)
            st[ch] = (r[:, :M_DH], r[:, M_DH:], w_inter, m_lanes, m_new, w_s, decay)
        for ch in chains:
            bb, hh = ch
            sv, s_sum, w_inter, m_lanes, m_new, w_s, decay = st[ch]
            k = km_ref[bb, rows, hsl(hh)]
            v = vm_ref[bb, rows, hsl(hh)]
            c_h = c_sc[bb, hh]
            n_h = n_sc[bb, hh]
            num = w_inter * qc[ch] + sv
            den = w_inter * qn[ch] + s_sum
            hv = num / jnp.maximum(jnp.abs(den), jnp.exp(-m_lanes))
            wv_t = (jnp.broadcast_to(w_s, (CHUNK, M_DH)) * v).T
            c_sc[bb, hh] = decay * c_h + _bdot(wv_t, k)
            n_sc[bb, hh] = decay * n_h + jnp.sum(w_s * k.astype(F32), axis=0, keepdims=True)
            m_sc[bb, hh] = jnp.broadcast_to(m_new, (1, LANES))
            hn = _rms(hv, onorm_ref[:, hsl(hh)])
            h_ref[bb, rows, hsl(hh)] = (_sigmoid(om_ref[bb, rows, hsl(hh)]) * hn).astype(BF16)
        return carry

    lax.fori_loop(0, n_chunks, chunk, 0)

    @pl.when(lt == pl.num_programs(1) - 1)
    def _():
        cout_ref[...] = c_sc[...]
        nout_ref[...] = n_sc[...]
        mout_ref[...] = m_sc[...]


def _mlstm(qm, km, vm, om, gates, onorm, c0, n0, m0, *, tl):
    b, l, _ = qm.shape
    ns = MLSTM_STREAMS
    tile = lambda i, j: (i, j, 0)
    st4 = lambda i, j: (i, 0, 0, 0)
    state = [((ns, M_HEADS, M_DH, M_DH), (b, M_HEADS, M_DH, M_DH)),
             ((ns, M_HEADS, 1, M_DH), (b, M_HEADS, 1, M_DH)),
             ((ns, M_HEADS, 1, LANES), (b, M_HEADS, 1, LANES))]
    return pl.pallas_call(
        functools.partial(_mlstm_kernel, n_chunks=tl // CHUNK, n_streams=ns),
        grid=(b // ns, l // tl),
        in_specs=[pl.BlockSpec((ns, tl, M_W), tile)] * 4
        + [pl.BlockSpec((ns, tl, LANES), tile), _const_spec((1, M_W))]
        + [pl.BlockSpec(blk, st4) for blk, _ in state],
        out_specs=[pl.BlockSpec((ns, tl, M_W), tile)] + [pl.BlockSpec(blk, st4) for blk, _ in state],
        out_shape=[jax.ShapeDtypeStruct((b, l, M_W), BF16)]
        + [jax.ShapeDtypeStruct(full, F32) for _, full in state],
        scratch_shapes=[pltpu.VMEM(blk, F32) for blk, _ in state],
        compiler_params=_cparams("parallel", "arbitrary"),
        name="mlstm",
    )(qm, km, vm, om, gates, onorm.reshape(1, M_W), c0, n0, m0)


MLA_SUM_ROWS = 16


def _mla_kernel(qi_ref, kj_ref, flag_ref, q_ref, kn_ref, kr_ref, vt_ref, o_ref, m_sc, acc_sc,
                *, tq, tkb, offset, variants):
    p = pl.program_id(1)
    i = qi_ref[p]
    j = kj_ref[p]
    flags = flag_ref[p]

    @pl.when(j == 0)
    def _():
        m_sc[...] = jnp.full_like(m_sc, -jnp.inf)
        acc_sc[...] = jnp.zeros_like(acc_sc)

    def sweep(nk, masked):
        kr = kr_ref[0, :nk, :]
        ones = jnp.ones((MLA_SUM_ROWS, nk), BF16)
        scores = []
        for hh in range(A_HEADS):
            q = q_ref[0, :, hh * 2 * LANES:(hh + 1) * 2 * LANES]
            k = jnp.concatenate([kn_ref[0, :nk, hh * A_NOPE:(hh + 1) * A_NOPE], kr], axis=-1)
            scores.append(_bdot_nt(k, q))
        if masked:
            k_chunk = (j * tkb + lax.broadcasted_iota(jnp.int32, (nk, tq), 0)) // CHUNK
            q_chunk = (offset + i * tq + lax.broadcasted_iota(jnp.int32, (nk, tq), 1)) // CHUNK
            visible = k_chunk <= q_chunk
            scores = [jnp.where(visible, s, -jnp.inf) for s in scores]
        for hh in range(A_HEADS):
            s = scores[hh]
            m_prev = m_sc[hh]
            m_new = jnp.maximum(m_prev, jnp.max(s, axis=0, keepdims=True))
            alpha = jnp.exp2(m_prev - m_new)
            pr = jnp.exp2(s - m_new).astype(BF16)
            vt = jnp.concatenate([vt_ref[0, hh * A_VDIM:(hh + 1) * A_VDIM, :nk], ones], axis=0)
            acc_sc[hh] = alpha * acc_sc[hh] + jnp.dot(vt, pr, preferred_element_type=F32)
            m_sc[hh] = m_new

    for vi, (nk, masked) in enumerate(variants):
        pl.when((flags >> 1) == vi)(functools.partial(sweep, nk, masked))

    @pl.when((flags & 1) != 0)
    def _():
        for hh in range(A_HEADS):
            acc = acc_sc[hh]
            o_ref[0, :, hh * A_VDIM:(hh + 1) * A_VDIM] = (
                acc[:A_VDIM] / acc[A_VDIM:A_VDIM + 1]).T.astype(BF16)


def _transpose_kernel(x_ref, o_ref):
    o_ref[0] = x_ref[0].astype(F32).T.astype(o_ref.dtype)


def _transpose_tiles(x, t):
    b, l, c = x.shape
    return pl.pallas_call(
        _transpose_kernel,
        grid=(b, l // t),
        in_specs=[pl.BlockSpec((1, t, c), lambda bb, i: (bb, i, 0))],
        out_specs=pl.BlockSpec((1, c, t), lambda bb, i: (bb, 0, i)),
        out_shape=jax.ShapeDtypeStruct((b, c, l), x.dtype),
        compiler_params=_cparams("parallel", "parallel"),
        name="transpose_tiles",
    )(x)


def _mla_pairs(lq, lk, tq, tkb, sub, offset):
    qi, kj, flags, variants = [], [], [], []
    for i in range(lq // tq):
        first_row, last_row = offset + i * tq, offset + (i + 1) * tq - 1
        k_end = min((last_row // CHUNK + 1) * CHUNK, lk)
        k_all = (first_row // CHUNK + 1) * CHUNK
        nj = -(-k_end // tkb)
        for j in range(nj):
            nk = -(-(min(k_end, (j + 1) * tkb) - j * tkb) // sub) * sub
            variant = (nk, j * tkb + nk > k_all)
            if variant not in variants:
                variants.append(variant)
            qi.append(i)
            kj.append(j)
            flags.append(int(j == nj - 1) + 2 * variants.index(variant))
    return (np.asarray(qi, np.int32), np.asarray(kj, np.int32), np.asarray(flags, np.int32),
            tuple(variants))


def _mla_attention(q, kn, kr, vt, *, tq, tkb, sub, offset):
    b, lq, _ = q.shape
    lk = kn.shape[1]
    assert lk % tkb == 0 and tkb % sub == 0 and vt.shape[2] == lk
    qi, kj, flags, variants = _mla_pairs(lq, lk, tq, tkb, sub, offset)
    qmap = lambda bb, p, qi_r, kj_r, flag_r: (bb, qi_r[p], 0)
    kmap = lambda bb, p, qi_r, kj_r, flag_r: (bb, kj_r[p], 0)
    grid_spec = pltpu.PrefetchScalarGridSpec(
        num_scalar_prefetch=3,
        grid=(b, len(qi)),
        in_specs=[pl.BlockSpec((1, tq, q.shape[-1]), qmap),
                  pl.BlockSpec((1, tkb, kn.shape[-1]), kmap),
                  pl.BlockSpec((1, tkb, kr.shape[-1]), kmap),
                  pl.BlockSpec((1, vt.shape[1], tkb), lambda bb, p, qi_r, kj_r, flag_r: (bb, 0, kj_r[p]))],
        out_specs=pl.BlockSpec((1, tq, A_HEADS * A_VDIM), qmap),
        scratch_shapes=[pltpu.VMEM((A_HEADS, 1, tq), F32),
                        pltpu.VMEM((A_HEADS, A_VDIM + MLA_SUM_ROWS, tq), F32)],
    )
    return pl.pallas_call(
        functools.partial(_mla_kernel, tq=tq, tkb=tkb, offset=offset, variants=variants),
        grid_spec=grid_spec,
        out_shape=jax.ShapeDtypeStruct((b, lq, A_HEADS * A_VDIM), BF16),
        compiler_params=_cparams("parallel", "arbitrary"),
        name="mla_attention",
    )(jnp.asarray(qi), jnp.asarray(kj), jnp.asarray(flags), q, kn, kr, vt)


def _odd_in_kernel(x_ref, gain_ref, wu_ref, wq_ref, wk_ref, wv_ref, u_o, q_o, k_o, v_o, kb_o, vb_o):
    h = _rms(x_ref[...], gain_ref[...]).astype(BF16)
    u_o[...] = jnp.dot(h, wu_ref[...], preferred_element_type=F32)
    q_o[...] = (jnp.dot(h, wq_ref[...], preferred_element_type=F32) * SB_DH ** -0.5).astype(BF16)
    k = jnp.dot(h, wk_ref[...], preferred_element_type=F32)
    v = jnp.dot(h, wv_ref[...], preferred_element_type=F32)
    k_o[...] = pltpu.einshape("t(hd)->thd", k, h=SB_HEADS)
    v_o[...] = pltpu.einshape("t(hd)->thd", v, h=SB_HEADS)
    kb_o[...] = k.astype(BF16)
    vb_o[...] = v.astype(BF16)


def _odd_in(x, gain, w, *, tm):
    n = x.shape[0]
    row = lambda i: (i, 0)
    weights = [w["wu"], w["wq"], w["wk"], w["wv"]]
    return pl.pallas_call(
        _odd_in_kernel,
        grid=(n // tm,),
        in_specs=[pl.BlockSpec((tm, D_MODEL), row), _const_spec((1, D_MODEL))]
        + [_const_spec(a.shape) for a in weights],
        out_specs=[pl.BlockSpec((tm, S5_W), row), pl.BlockSpec((tm, SB_W), row)]
        + [pl.BlockSpec((tm, SB_HEADS, SB_DH), lambda i: (i, 0, 0))] * 2
        + [pl.BlockSpec((tm, SB_W), row)] * 2,
        out_shape=[jax.ShapeDtypeStruct((n, S5_W), F32),
                   jax.ShapeDtypeStruct((n, SB_W), BF16),
                   jax.ShapeDtypeStruct((n, SB_HEADS, SB_DH), F32),
                   jax.ShapeDtypeStruct((n, SB_HEADS, SB_DH), F32),
                   jax.ShapeDtypeStruct((n, SB_W), BF16),
                   jax.ShapeDtypeStruct((n, SB_W), BF16)],
        compiler_params=_cparams("parallel"),
        name="odd_in",
    )(x, gain.reshape(1, D_MODEL), *weights)


S5_BLOCKS = 4
S5_BLK_IN = S5_W // S5_BLOCKS
S5_BLK_ST = S5_STATE // S5_BLOCKS


def _s5_kernel(u_ref, bbr_ref, bbi_ref, ar_ref, ai_ref, ccr_ref, cci_ref, d_ref, wglu_ref,
               x0r_ref, x0i_ref, hs_o, xr_o, xi_o, u_tm, hs_tm, xr_buf, xi_buf, xr_st, xi_st,
               *, tt, nb):
    t = pl.program_id(0)

    @pl.when(t == 0)
    def _():
        xr_st[...] = x0r_ref[...]
        xi_st[...] = x0i_ref[...]

    for bb in range(nb):
        for k in range(S5_BLOCKS):
            u_tm[k, pl.ds(bb, tt, stride=nb), :] = u_ref[bb, :, k * S5_BLK_IN:(k + 1) * S5_BLK_IN]

    for k in range(S5_BLOCKS):
        cs = slice(k * S5_BLK_ST, (k + 1) * S5_BLK_ST)
        ub = u_tm[k].astype(BF16)
        xr_buf[:, cs] = jnp.dot(ub, bbr_ref[k], preferred_element_type=F32)
        xi_buf[:, cs] = jnp.dot(ub, bbi_ref[k], preferred_element_type=F32)

    for k in range(S5_BLOCKS):
        cs = slice(k * S5_BLK_ST, (k + 1) * S5_BLK_ST)
        ar = jnp.broadcast_to(ar_ref[:, cs], (nb, S5_BLK_ST))
        ai = jnp.broadcast_to(ai_ref[:, cs], (nb, S5_BLK_ST))

        def step(s, carry, cs=cs, ar=ar, ai=ai):
            xr, xi = carry
            rows = pl.ds(pl.multiple_of(s * nb, nb), nb)
            nxr = ar * xr - ai * xi + xr_buf[rows, cs]
            nxi = ar * xi + ai * xr + xi_buf[rows, cs]
            xr_buf[rows, cs] = nxr
            xi_buf[rows, cs] = nxi
            return nxr, nxi

        xr, xi = lax.fori_loop(0, tt, step, (xr_st[:, cs], xi_st[:, cs]), unroll=8)
        xr_st[:, cs] = xr
        xi_st[:, cs] = xi

    gs = []
    for k in range(S5_BLOCKS):
        cs = slice(k * S5_BLK_ST, (k + 1) * S5_BLK_ST)
        y = (_bdot(xr_buf[:, cs], ccr_ref[k]) + _bdot(xi_buf[:, cs], cci_ref[k])
             + d_ref[:, k * S5_BLK_IN:(k + 1) * S5_BLK_IN] * u_tm[k])
        gs.append(jax.nn.gelu(y, approximate=True))
    g = jnp.concatenate(gs, axis=-1)
    hs = g * _sigmoid(_bdot(g, wglu_ref[...]))
    for k in range(S5_BLOCKS):
        hs_tm[k] = hs[:, k * S5_BLK_IN:(k + 1) * S5_BLK_IN]
    for bb in range(nb):
        for k in range(S5_BLOCKS):
            hs_o[bb, :, k * S5_BLK_IN:(k + 1) * S5_BLK_IN] = (
                hs_tm[k, pl.ds(bb, tt, stride=nb), :].astype(BF16))

    @pl.when(t == pl.num_programs(0) - 1)
    def _():
        xr_o[...] = xr_st[...]
        xi_o[...] = xi_st[...]


def _s5(u, w, x0r, x0i, *, tt):
    nb, l, _ = u.shape
    blk = lambda t: (0, t, 0)
    consts = [w["bbr"], w["bbi"], w["ar"], w["ai"], w["ccr"], w["cci"], w["d"], w["wglu"], x0r, x0i]
    return pl.pallas_call(
        functools.partial(_s5_kernel, tt=tt, nb=nb),
        grid=(l // tt,),
        in_specs=[pl.BlockSpec((nb, tt, S5_W), blk)] + [_const_spec(a.shape) for a in consts],
        out_specs=[pl.BlockSpec((nb, tt, S5_W), blk), _const_spec((nb, S5_STATE)),
                   _const_spec((nb, S5_STATE))],
        out_shape=[jax.ShapeDtypeStruct((nb, l, S5_W), BF16),
                   jax.ShapeDtypeStruct((nb, S5_STATE), F32),
                   jax.ShapeDtypeStruct((nb, S5_STATE), F32)],
        scratch_shapes=[pltpu.VMEM((S5_BLOCKS, tt * nb, S5_BLK_IN), F32),
                        pltpu.VMEM((S5_BLOCKS, tt * nb, S5_BLK_IN), F32),
                        pltpu.VMEM((tt * nb, S5_STATE), F32), pltpu.VMEM((tt * nb, S5_STATE), F32),
                        pltpu.VMEM((nb, S5_STATE), F32), pltpu.VMEM((nb, S5_STATE), F32)],
        compiler_params=_cparams("arbitrary"),
        name="s5",
    )(u, *consts)


def _sb_kernel(q_ref, k_ref, v_ref, o_ref, r_sc, acc_sc, qh_sc, *, tq, tk, offset, n_sub):
    for su in range(n_sub):
        _sb_query_tile(pl.program_id(1) * n_sub + su, slice(su * tq, (su + 1) * tq),
                       q_ref, k_ref, v_ref, o_ref, r_sc, acc_sc, qh_sc, tq=tq, tk=tk, offset=offset)


def _sb_query_tile(i, rs, q_ref, k_ref, v_ref, o_ref, r_sc, acc_sc, qh_sc, *, tq, tk, offset):
    n_pairs = SB_HEADS // 2
    half = lax.broadcasted_iota(jnp.int32, (1, LANES), 1) // SB_DH
    for pr in range(n_pairs):
        qq = q_ref[0, rs, pr * LANES:(pr + 1) * LANES]
        qh_sc[pr, :tq] = jnp.where(half == 0, qq, jnp.zeros_like(qq))
        qh_sc[pr, tq:] = jnp.where(half == 1, qq, jnp.zeros_like(qq))
    r_sc[...] = jnp.zeros_like(r_sc)
    acc_sc[...] = jnp.zeros_like(acc_sc)

    jrow = lax.broadcasted_iota(jnp.int32, (tk, tk), 0)
    jcol = lax.broadcasted_iota(jnp.int32, (tk, tk), 1)
    later = (jrow > jcol).astype(BF16)
    j_diag = (offset + i * tq) // tk

    def sweep(tiles):
        zs, vv, causal = {}, {}, {}
        for t, (j, masked, live) in enumerate(tiles):
            k0 = pl.multiple_of(j * tk, tk)
            if masked:
                q_pos = offset + i * tq + lax.broadcasted_iota(jnp.int32, (tq, tk), 0)
                causal[t] = (k0 + lax.broadcasted_iota(jnp.int32, (tq, tk), 1)) < q_pos
            for pr in range(n_pairs):
                ps = slice(pr * LANES, (pr + 1) * LANES)
                zs[t, pr] = _bdot_nt(qh_sc[pr], k_ref[0, pl.ds(k0, tk), ps])
                vv[t, pr] = v_ref[0, pl.ds(k0, tk), ps]
        l1s, lbs, cums = {}, {}, {}
        for t, (j, masked, live) in enumerate(tiles):
            for pr in range(n_pairs):
                parts = []
                for sub in range(2):
                    hh = 2 * pr + sub
                    z = zs[t, pr][sub * tq:(sub + 1) * tq]
                    sp = jnp.maximum(z, 0.0) + jnp.log(1.0 + jnp.exp(-jnp.abs(z)))
                    l1 = -sp
                    if masked:
                        l1 = jnp.where(causal[t], l1, 0.0)
                    hi = l1.astype(BF16)
                    parts += [hi, (l1 - hi.astype(F32)).astype(BF16)]
                    l1s[t, hh] = l1
                    lbs[t, hh] = z - sp
                c = jnp.dot(jnp.concatenate(parts, axis=0), later, preferred_element_type=F32)
                cums[t, 2 * pr] = c[:tq] + c[tq:2 * tq]
                cums[t, 2 * pr + 1] = c[2 * tq:3 * tq] + c[3 * tq:]
        for t, (j, masked, live) in enumerate(tiles):
            r_max = jnp.full((1, 1), -jnp.inf, F32)
            dead_shift = 0.0 if live is None else jnp.where(live, 0.0, -1e30)
            keep = 1.0 if live is None else jnp.where(live, 1.0, 0.0)
            for pr in range(n_pairs):
                ws = []
                for sub in range(2):
                    hh = 2 * pr + sub
                    r_old = r_sc[hh]
                    log_w = lbs[t, hh] + cums[t, hh] + (r_old + dead_shift)
                    if masked:
                        log_w = jnp.where(causal[t], log_w, -jnp.inf)
                    ws.append(jnp.exp(log_w).astype(BF16))
                    r_new = r_old + keep * jnp.sum(l1s[t, hh], axis=-1, keepdims=True)
                    r_sc[hh] = r_new
                    r_max = jnp.maximum(r_max, jnp.max(r_new, axis=0, keepdims=True))
                v2 = jnp.concatenate([jnp.where(half == 0, vv[t, pr], jnp.zeros_like(vv[t, pr])),
                                      jnp.where(half == 1, vv[t, pr], jnp.zeros_like(vv[t, pr]))],
                                     axis=0)
                acc_sc[:, pr * LANES:(pr + 1) * LANES] += jnp.dot(
                    jnp.concatenate(ws, axis=1), v2, preferred_element_type=F32)
        return r_max[0, 0]

    has_prev = j_diag >= 1
    r_first = sweep([(j_diag, True, None), (jnp.maximum(j_diag - 1, 0), False, has_prev)])

    def cond(carry):
        j, r_max = carry
        return jnp.logical_and(j >= 0, r_max > SB_LOG_FLOOR)

    def body(carry):
        j, _ = carry
        return j - 1, sweep([(j, False, None)])

    lax.while_loop(cond, body, (j_diag - 2, r_first))
    o_ref[0, rs] = acc_sc[...].astype(BF16)


def _sb_attention(q, k, v, *, tq, tk, offset):
    b, lq, _ = q.shape
    lk = k.shape[1]
    assert tk % tq == 0 and offset % tk == 0 and lk % tk == 0
    n_sub = math.gcd(lq // tq, SB_TILES_PER_STEP)
    return pl.pallas_call(
        functools.partial(_sb_kernel, tq=tq, tk=tk, offset=offset, n_sub=n_sub),
        grid=(b, lq // (tq * n_sub)),
        in_specs=[pl.BlockSpec((1, tq * n_sub, SB_W), lambda bb, i: (bb, i, 0)),
                  pl.BlockSpec((1, lk, SB_W), lambda bb, i: (bb, 0, 0)),
                  pl.BlockSpec((1, lk, SB_W), lambda bb, i: (bb, 0, 0))],
        out_specs=pl.BlockSpec((1, tq * n_sub, SB_W), lambda bb, i: (bb, i, 0)),
        out_shape=jax.ShapeDtypeStruct((b, lq, SB_W), BF16),
        scratch_shapes=[pltpu.VMEM((SB_HEADS, tq, 1), F32),
                        pltpu.VMEM((tq, SB_W), F32),
                        pltpu.VMEM((SB_HEADS // 2, 2 * tq, LANES), BF16)],
        compiler_params=_cparams("parallel", "arbitrary"),
        name="sb_attention",
    )(q, k, v)


def _pad_cols(w, width):
    return jnp.pad(w, ((0, 0), (0, width - w.shape[1])))


def _even_weights(w_in, b_ig, b_fg, q_norm, kv_norm, w_uq, w_ukv):
    o = 4 * M_W
    w_gate = w_in[:, o:o + 2 * M_HEADS]
    o += 2 * M_HEADS
    w_cq = w_in[:, o:o + Q_LORA]
    o += Q_LORA
    w_ckv = w_in[:, o:o + KV_LORA]
    o += KV_LORA
    w_kr = w_in[:, o:o + A_ROPE]
    hr = A_ROPE // 2
    wkra = _pad_cols(w_kr, LANES)
    wkrb = _pad_cols(jnp.concatenate([w_kr[:, hr:], w_kr[:, :hr]], axis=1), LANES)
    uq = w_uq.reshape(Q_LORA, A_HEADS, A_NOPE + A_ROPE)
    zeros = jnp.zeros((Q_LORA, A_HEADS, LANES - A_ROPE), w_uq.dtype)
    wq = jnp.concatenate([uq, zeros], axis=-1).reshape(Q_LORA, A_HEADS * 2 * LANES)
    ws = jnp.concatenate([uq[..., A_NOPE + hr:], uq[..., A_NOPE:A_NOPE + hr], zeros],
                         axis=-1).reshape(Q_LORA, A_HEADS * LANES)
    ukv = w_ukv.reshape(KV_LORA, A_HEADS, A_NOPE + A_VDIM)
    return dict(
        wm=w_in[:, :4 * M_W].astype(BF16),
        wg=_pad_cols(w_gate, LANES).astype(BF16),
        bg=_pad_cols(jnp.concatenate([b_ig, b_fg]).reshape(1, 2 * M_HEADS).astype(F32), LANES),
        wcq=w_cq.astype(BF16), wckv=w_ckv.astype(BF16),
        wkra=wkra.astype(BF16), wkrb=wkrb.astype(BF16),
        qn=q_norm.reshape(1, Q_LORA).astype(F32), kvn=kv_norm.reshape(1, KV_LORA).astype(F32),
        wq=wq.astype(BF16), ws=ws.astype(BF16),
        wuk=ukv[..., :A_NOPE].reshape(KV_LORA, A_HEADS * A_NOPE).astype(BF16),
        wuv=ukv[..., A_NOPE:].reshape(KV_LORA, A_HEADS * A_VDIM).astype(BF16),
    )


def _rope_tables(pos):
    half = A_ROPE // 2
    inv_freq = ROPE_THETA ** (-jnp.arange(half, dtype=F32) / half)
    ang = pos.astype(F32)[:, None] * inv_freq[None, :]
    cos, sin = jnp.cos(ang), jnp.sin(ang)
    pad = jnp.zeros((pos.shape[0], LANES - A_ROPE), F32)
    return (jnp.concatenate([cos, cos, pad], axis=1), jnp.concatenate([-sin, sin, pad], axis=1))


def _s5_weights(a_re, a_im, b_re, b_im, c_re, c_im, d, log_dt, w_glu):
    dt = jnp.exp(log_dt.astype(F32))[:, None]
    ar, ai = a_re.astype(F32), a_im.astype(F32)
    mag = jnp.exp(dt * ar)
    ab_r, ab_i = mag * jnp.cos(dt * ai), mag * jnp.sin(dt * ai)
    den = ar * ar + ai * ai
    c_r = ((ab_r - 1.0) * ar + ab_i * ai) / den
    c_i = (ab_i * ar - (ab_r - 1.0) * ai) / den
    br, bi = b_re.astype(F32), b_im.astype(F32)
    bb_r = c_r[..., None] * br - c_i[..., None] * bi
    bb_i = c_r[..., None] * bi + c_i[..., None] * br
    gb = S5_GROUPS // S5_BLOCKS
    eye = jnp.eye(gb, dtype=F32)
    bd_in = lambda m: jnp.einsum("kgpc,gh->kgchp", m.reshape(S5_BLOCKS, gb, S5_P, S5_GSIZE),
                                 eye).reshape(S5_BLOCKS, S5_BLK_IN, S5_BLK_ST)
    bd_out = lambda m: jnp.einsum("kgcp,gh->kgphc", m.reshape(S5_BLOCKS, gb, S5_GSIZE, S5_P),
                                  eye).reshape(S5_BLOCKS, S5_BLK_ST, S5_BLK_IN)
    return dict(
        bbr=bd_in(bb_r).astype(BF16), bbi=bd_in(bb_i).astype(BF16),
        ar=ab_r.reshape(1, S5_STATE), ai=ab_i.reshape(1, S5_STATE),
        ccr=bd_out(c_re.astype(F32)).astype(BF16), cci=bd_out(-c_im.astype(F32)).astype(BF16),
        d=d.reshape(1, S5_W).astype(F32), wglu=w_glu.astype(BF16),
    )


def _pad_keys(a, mult):
    pad = (-a.shape[1]) % mult
    return jnp.pad(a, ((0, 0), (0, pad), (0, 0))) if pad else a


def _trunk(x, pos0, cache, p):
    nb, l, _ = x.shape
    assert nb == SUBLANES, "the S5 layout puts the streams of one time step on the sublanes"
    assert l % CHUNK == 0
    n = nb * l
    tm = min(TOKEN_TILE, n)
    tl = min(tm, l)
    assert l % tl == 0 and tm % tl == 0
    past = 0 if cache is None else cache["sb_k"].shape[1]
    assert pos0 == past
    xf = x.reshape(n, D_MODEL)
    bf = lambda a: a.astype(BF16)

    xf = _ffn(xf, p["norm_ffn"][0, 0], bf(p["ffn_w_gate"][0, 0]), bf(p["ffn_w_up"][0, 0]),
              bf(p["ffn_w_down"][0, 0]), tm=tm)
    ew = _even_weights(p["even_w_in"][0], p["mlstm_b_igate"][0], p["mlstm_b_fgate"][0],
                       p["mla_q_norm"][0], p["mla_kv_norm"][0], p["mla_w_uq"][0], p["mla_w_ukv"][0])
    cos, sin = _rope_tables(pos0 + jnp.arange(l))
    if tm > l:
        cos, sin = jnp.tile(cos, (tm // l, 1)), jnp.tile(sin, (tm // l, 1))
    qm, km, vm, om, gates, q, kn, v, kr128, lat, kr = _even_in(
        xf, p["norm_mix"][0], ew, cos, sin, tm=tm, seq_len=l)
    v_transposed = v.ndim == 3

    if cache is None:
        c0 = jnp.zeros((nb, M_HEADS, M_DH, M_DH), F32)
        n0 = jnp.zeros((nb, M_HEADS, 1, M_DH), F32)
        m0 = jnp.zeros((nb, M_HEADS, 1, LANES), F32)
    else:
        c0 = cache["mlstm_c"].astype(F32)
        n0 = cache["mlstm_n"].astype(F32).reshape(nb, M_HEADS, 1, M_DH)
        m0 = jnp.broadcast_to(cache["mlstm_m"].astype(F32)[..., None, None], (nb, M_HEADS, 1, LANES))
    r3 = lambda a: a.reshape(nb, l, a.shape[-1])
    h_m, c_new, n_new, m_new = _mlstm(r3(qm), r3(km), r3(vm), r3(om), r3(gates),
                                      p["mlstm_out_norm"][0], c0, n0, m0, tl=tl)

    mla_tq = min(MLA_TILE, -(-l // LANES) * LANES)
    if cache is None:
        mla_sub = mla_tq
        mla_tkb = min(2 * mla_tq, l)
    else:
        mla_sub = mla_tkb = -(-(past + max(l, mla_tq)) // LANES) * LANES
    kn3, kr3 = r3(kn), r3(kr128)
    v3, vt = (None, v) if v_transposed else (r3(v), None)
    if cache is not None:
        ckn, cv = _kv_expand(cache["mla_latent"].astype(F32).reshape(nb * past, KV_LORA),
                             ew["wuk"], ew["wuv"])
        ckr = jnp.pad(cache["mla_krope"], ((0, 0), (0, 0), (0, LANES - A_ROPE))).astype(BF16)
        kn3 = jnp.concatenate([ckn.reshape(nb, past, -1), kn3], axis=1)
        kr3 = jnp.concatenate([ckr, kr3], axis=1)
        cv3 = cv.reshape(nb, past, -1)
        if v_transposed:
            vt = jnp.concatenate([_transpose_tiles(cv3, math.gcd(past, TOKEN_TILE)), vt], axis=2)
        else:
            v3 = jnp.concatenate([cv3, v3], axis=1)
    if vt is None:
        vt = _transpose_tiles(_pad_keys(v3, mla_tkb), mla_sub)
    else:
        vt = jnp.pad(vt, ((0, 0), (0, 0), (0, (-vt.shape[2]) % mla_tkb)))
    h_a = _mla_attention(_pad_keys(r3(q), mla_tq), _pad_keys(kn3, mla_tkb), _pad_keys(kr3, mla_tkb),
                         vt, tq=mla_tq, tkb=mla_tkb, sub=mla_sub, offset=past)[:, :l]

    w_out = bf(p["even_w_out"][0])
    xf = _ffn(xf, p["norm_ffn"][0, 1], bf(p["ffn_w_gate"][0, 1]), bf(p["ffn_w_up"][0, 1]),
              bf(p["ffn_w_down"][0, 1]),
              mix=(h_m.reshape(n, M_W), h_a.reshape(n, -1), w_out[:M_W], w_out[M_W:]), tm=tm)

    xf = _ffn(xf, p["norm_ffn"][1, 0], bf(p["ffn_w_gate"][1, 0]), bf(p["ffn_w_up"][1, 0]),
              bf(p["ffn_w_down"][1, 0]), tm=tm)
    w_in = p["odd_w_in"][0]
    ow = dict(wu=bf(w_in[:, :S5_W]), wq=bf(w_in[:, S5_W:S5_W + SB_W]),
              wk=bf(w_in[:, S5_W + SB_W:S5_W + 2 * SB_W]), wv=bf(w_in[:, S5_W + 2 * SB_W:]))
    u, qs, ks, vs, ksb, vsb = _odd_in(xf, p["norm_mix"][1], ow, tm=tm)

    sw = _s5_weights(p["s5_A_re"][0], p["s5_A_im"][0], p["s5_B_re"][0], p["s5_B_im"][0],
                     p["s5_C_re"][0], p["s5_C_im"][0], p["s5_D"][0], p["s5_log_dt"][0], p["s5_w_glu"][0])
    if cache is None:
        x0r = jnp.zeros((nb, S5_STATE), F32)
        x0i = jnp.zeros((nb, S5_STATE), F32)
    else:
        x0r = cache["s5_re"].astype(F32).reshape(nb, S5_STATE)
        x0i = cache["s5_im"].astype(F32).reshape(nb, S5_STATE)
    h_s, xr_new, xi_new = _s5(r3(u), sw, x0r, x0i, tt=min(S5_TT, l))

    k3, v3 = r3(ksb), r3(vsb)
    if cache is not None:
        k3 = jnp.concatenate([bf(cache["sb_k"].reshape(nb, past, SB_W)), k3], axis=1)
        v3 = jnp.concatenate([bf(cache["sb_v"].reshape(nb, past, SB_W)), v3], axis=1)
    h_b = _sb_attention(r3(qs), _pad_keys(k3, SB_TILE), _pad_keys(v3, SB_TILE), tq=min(SB_TILE, l),
                        tk=SB_TILE, offset=past)

    w_out = bf(p["odd_w_out"][0])
    y = _ffn(xf, p["norm_ffn"][1, 1], bf(p["ffn_w_gate"][1, 1]), bf(p["ffn_w_up"][1, 1]),
             bf(p["ffn_w_down"][1, 1]),
             mix=(h_s.reshape(n, S5_W), h_b.reshape(n, SB_W), w_out[:S5_W], w_out[S5_W:]),
             final_gain=p["norm_final"], tm=tm)

    return (y.reshape(nb, l, D_MODEL),
            lat.reshape(1, nb, l, KV_LORA), kr.reshape(1, nb, l, A_ROPE),
            c_new[None], n_new.reshape(1, nb, M_HEADS, M_DH), m_new[:, :, 0, 0][None],
            xr_new.reshape(1, nb, S5_GROUPS, S5_P), xi_new.reshape(1, nb, S5_GROUPS, S5_P),
            ks.reshape(1, nb, l, SB_HEADS, SB_DH), vs.reshape(1, nb, l, SB_HEADS, SB_DH))


def kernel(x_prompt, x_sample, cache_mla_latent, cache_mla_krope, state_mlstm_C, state_mlstm_n, state_mlstm_m, state_s5_re, state_s5_im, cache_sb_k, cache_sb_v, norm_ffn, norm_mix, norm_final, ffn_w_gate, ffn_w_up, ffn_w_down, even_w_in, even_w_out, mlstm_b_igate, mlstm_b_fgate, mlstm_out_norm, mla_q_norm, mla_kv_norm, mla_w_uq, mla_w_ukv, odd_w_in, odd_w_out, s5_A_re, s5_A_im, s5_B_re, s5_B_im, s5_C_re, s5_C_im, s5_D, s5_log_dt, s5_w_glu):
    p = dict(norm_ffn=norm_ffn, norm_mix=norm_mix, norm_final=norm_final, ffn_w_gate=ffn_w_gate,
             ffn_w_up=ffn_w_up, ffn_w_down=ffn_w_down, even_w_in=even_w_in, even_w_out=even_w_out,
             mlstm_b_igate=mlstm_b_igate, mlstm_b_fgate=mlstm_b_fgate, mlstm_out_norm=mlstm_out_norm,
             mla_q_norm=mla_q_norm, mla_kv_norm=mla_kv_norm, mla_w_uq=mla_w_uq, mla_w_ukv=mla_w_ukv,
             odd_w_in=odd_w_in, odd_w_out=odd_w_out, s5_A_re=s5_A_re, s5_A_im=s5_A_im, s5_B_re=s5_B_re,
             s5_B_im=s5_B_im, s5_C_re=s5_C_re, s5_C_im=s5_C_im, s5_D=s5_D, s5_log_dt=s5_log_dt,
             s5_w_glu=s5_w_glu)
    past = cache_sb_k.shape[2]
    cache = dict(mla_latent=cache_mla_latent[0], mla_krope=cache_mla_krope[0],
                 mlstm_c=state_mlstm_C[0], mlstm_n=state_mlstm_n[0], mlstm_m=state_mlstm_m[0],
                 s5_re=state_s5_re[0], s5_im=state_s5_im[0],
                 sb_k=cache_sb_k[0], sb_v=cache_sb_v[0])
    out_p = _trunk(x_prompt, 0, None, p)
    out_s = _trunk(x_sample, past, cache, p)
    return (out_p[0], out_s[0]) + tuple(out_p[1:]) + tuple(out_s[1:])
```

```python
import functools
import math

import jax
import jax.numpy as jnp
import numpy as np
from jax import lax
from jax.experimental import pallas as pl
from jax.experimental.pallas import tpu as pltpu

F32 = jnp.float32
BF16 = jnp.bfloat16

D_MODEL = 1024
CHUNK = 64
D_FF = 2816
EPS = 1e-6
M_HEADS = 4
M_DH = 128
M_W = M_HEADS * M_DH
A_HEADS = 4
A_NOPE = 128
A_ROPE = 64
A_VDIM = 128
Q_LORA = 384
KV_LORA = 256
ROPE_THETA = 10000.0
MLA_SCALE = (A_NOPE + A_ROPE) ** -0.5
S5_W = 512
S5_GSIZE = 16
S5_GROUPS = S5_W // S5_GSIZE
S5_P = 64
S5_STATE = S5_GROUPS * S5_P
SB_HEADS = 8
SB_DH = 64
SB_W = SB_HEADS * SB_DH

LANES = 128
SUBLANES = 8
VMEM_LIMIT_BYTES = 56 * 1024 * 1024

TOKEN_TILE = 512
FF_CHUNK = 256
MLA_TILE = 512
SB_TILE = 128
SB_TILES_PER_STEP = 4
S5_TT = 128
MLSTM_STREAMS = 4
SB_LOG_FLOOR = -24.0
LOG2E = math.log2(math.e)

HIGHEST = lax.Precision.HIGHEST


def _cparams(*sem):
    return pltpu.CompilerParams(dimension_semantics=sem, vmem_limit_bytes=VMEM_LIMIT_BYTES)


def _bdot(a, b):
    return jnp.dot(a.astype(BF16), b.astype(BF16), preferred_element_type=F32)


def _bdot_nt(a, b):
    return lax.dot_general(a.astype(BF16), b.astype(BF16), (((1,), (1,)), ((), ())),
                           preferred_element_type=F32)


def _rms(x, g):
    return x * lax.rsqrt(jnp.mean(x * x, axis=-1, keepdims=True) + EPS) * g


def _sigmoid(x):
    return 1.0 / (1.0 + jnp.exp(-x))


def _log_sigmoid(x):
    return jnp.minimum(x, 0.0) - jnp.log(1.0 + jnp.exp(-jnp.abs(x)))


def _const_spec(shape):
    nd = len(shape)
    return pl.BlockSpec(shape, lambda *_: (0,) * nd)


def _ffn_kernel(*refs, with_mix, with_final):
    it = iter(refs)
    x_ref = next(it)
    if with_mix:
        a_ref, b_ref, wa_ref, wb_ref = next(it), next(it), next(it), next(it)
    g_ref, wg_ref, wu_ref, wd_ref = next(it), next(it), next(it), next(it)
    gf_ref = next(it) if with_final else None
    o_ref = next(it)
    acc_ref = next(it)

    x = x_ref[...]
    if with_mix:
        x = x + _bdot(a_ref[...], wa_ref[...]) + _bdot(b_ref[...], wb_ref[...])
    h = _rms(x, g_ref[...]).astype(BF16)
    acc_ref[...] = jnp.zeros_like(acc_ref)
    for c in range(D_FF // FF_CHUNK):
        sl = slice(c * FF_CHUNK, (c + 1) * FF_CHUNK)
        g = jnp.dot(h, wg_ref[:, sl], preferred_element_type=F32)
        u = jnp.dot(h, wu_ref[:, sl], preferred_element_type=F32)
        a = (g * _sigmoid(g) * u).astype(BF16)
        acc_ref[...] += jnp.dot(a, wd_ref[sl, :], preferred_element_type=F32)
    y = x + 0.5 * acc_ref[...]
    if with_final:
        y = _rms(y, gf_ref[...])
    o_ref[...] = y


def _ffn(x, gain, wg, wu, wd, *, mix=None, final_gain=None, tm):
    n = x.shape[0]
    row = lambda i: (i, 0)
    args, specs = [x], [pl.BlockSpec((tm, D_MODEL), row)]
    if mix is not None:
        a, b, wa, wb = mix
        args += [a, b, wa, wb]
        specs += [pl.BlockSpec((tm, a.shape[-1]), row), pl.BlockSpec((tm, b.shape[-1]), row),
                  _const_spec(wa.shape), _const_spec(wb.shape)]
    args += [gain.reshape(1, D_MODEL), wg, wu, wd]
    specs += [_const_spec((1, D_MODEL)), _const_spec(wg.shape), _const_spec(wu.shape),
              _const_spec(wd.shape)]
    if final_gain is not None:
        args.append(final_gain.reshape(1, D_MODEL))
        specs.append(_const_spec((1, D_MODEL)))
    return pl.pallas_call(
        functools.partial(_ffn_kernel, with_mix=mix is not None, with_final=final_gain is not None),
        grid=(n // tm,),
        in_specs=specs,
        out_specs=pl.BlockSpec((tm, D_MODEL), row),
        out_shape=jax.ShapeDtypeStruct((n, D_MODEL), F32),
        scratch_shapes=[pltpu.VMEM((tm, D_MODEL), F32)],
        compiler_params=_cparams("parallel"),
        name="ffn",
    )(*args)


def _even_in_kernel(x_ref, gain_ref, wm_ref, wg_ref, bg_ref, wcq_ref, wckv_ref, wkra_ref, wkrb_ref,
                    qn_ref, kvn_ref, wq_ref, ws_ref, wuk_ref, wuv_ref, cos_ref, sin_ref,
                    qm_o, km_o, vm_o, om_o, gates_o, q_o, kn_o, v_o, kr128_o, lat_o, kr_o,
                    *, v_transposed):
    h = _rms(x_ref[...], gain_ref[...]).astype(BF16)
    m = jnp.dot(h, wm_ref[...], preferred_element_type=F32)
    qm_o[...] = m[:, :M_W].astype(BF16)
    km_o[...] = (m[:, M_W:2 * M_W] * M_DH ** -0.5).astype(BF16)
    vm_o[...] = m[:, 2 * M_W:3 * M_W].astype(BF16)
    om_o[...] = m[:, 3 * M_W:]

    g = jnp.dot(h, wg_ref[...], preferred_element_type=F32) + bg_ref[...]
    lane = lax.broadcasted_iota(jnp.int32, g.shape, 1)
    gates_o[...] = jnp.where(lane < M_HEADS, g, _log_sigmoid(g))

    cos, sin = cos_ref[...], sin_ref[...]
    cq = jnp.dot(h, wcq_ref[...], preferred_element_type=F32)
    cqn = _rms(cq, qn_ref[...]).astype(BF16)
    qf = jnp.dot(cqn, wq_ref[...], preferred_element_type=F32)
    qs = jnp.dot(cqn, ws_ref[...], preferred_element_type=F32)
    q_scale = MLA_SCALE * LOG2E
    for hh in range(A_HEADS):
        base = hh * 2 * LANES
        q_o[:, base:base + LANES] = (qf[:, base:base + LANES] * q_scale).astype(BF16)
        rot = qf[:, base + LANES:base + 2 * LANES] * cos + qs[:, hh * LANES:(hh + 1) * LANES] * sin
        q_o[:, base + LANES:base + 2 * LANES] = (rot * q_scale).astype(BF16)

    ckv = jnp.dot(h, wckv_ref[...], preferred_element_type=F32)
    lat = _rms(ckv, kvn_ref[...])
    lat_o[...] = lat
    latb = lat.astype(BF16)
    kn_o[...] = jnp.dot(latb, wuk_ref[...], preferred_element_type=F32).astype(BF16)
    if v_transposed:
        v_o[0] = _bdot_nt(wuv_ref[...], latb).astype(BF16)
    else:
        v_o[...] = jnp.dot(latb, wuv_ref[...], preferred_element_type=F32).astype(BF16)

    kr = (jnp.dot(h, wkra_ref[...], preferred_element_type=F32) * cos
          + jnp.dot(h, wkrb_ref[...], preferred_element_type=F32) * sin)
    kr128_o[...] = kr.astype(BF16)
    kr_o[...] = kr[:, :A_ROPE]


def _even_in(x, gain, w, cos, sin, *, tm, seq_len):
    n = x.shape[0]
    row = lambda i: (i, 0)
    n_pos_tiles = cos.shape[0] // tm
    pos = lambda i: (i % n_pos_tiles, 0)
    v_transposed = seq_len % tm == 0
    n_time_tiles = max(seq_len // tm, 1)
    weights = [w["wm"], w["wg"], w["bg"], w["wcq"], w["wckv"], w["wkra"], w["wkrb"], w["qn"], w["kvn"],
               w["wq"], w["ws"], w["wuk"], w["wuv"].T if v_transposed else w["wuv"]]
    in_specs = ([pl.BlockSpec((tm, D_MODEL), row), _const_spec((1, D_MODEL))]
                + [_const_spec(a.shape) for a in weights]
                + [pl.BlockSpec((tm, LANES), pos), pl.BlockSpec((tm, LANES), pos)])
    outs = [(M_W, BF16), (M_W, BF16), (M_W, BF16), (M_W, F32), (LANES, F32),
            (A_HEADS * 2 * LANES, BF16), (A_HEADS * A_NOPE, BF16), (A_HEADS * A_VDIM, BF16),
            (LANES, BF16), (KV_LORA, F32), (A_ROPE, F32)]
    out_specs = [pl.BlockSpec((tm, c), row) for c, _ in outs]
    out_shape = [jax.ShapeDtypeStruct((n, c), dt) for c, dt in outs]
    if v_transposed:
        v_idx = 7
        out_specs[v_idx] = pl.BlockSpec((1, A_HEADS * A_VDIM, tm),
                                        lambda i: (i // n_time_tiles, 0, i % n_time_tiles))
        out_shape[v_idx] = jax.ShapeDtypeStruct((n // seq_len, A_HEADS * A_VDIM, seq_len), BF16)
    return pl.pallas_call(
        functools.partial(_even_in_kernel, v_transposed=v_transposed),
        grid=(n // tm,),
        in_specs=in_specs,
        out_specs=out_specs,
        out_shape=out_shape,
        compiler_params=_cparams("parallel"),
        name="even_in",
    )(x, gain.reshape(1, D_MODEL), *weights, cos, sin)


def _kv_expand_kernel(lat_ref, wuk_ref, wuv_ref, kn_o, v_o):
    latb = lat_ref[...].astype(BF16)
    kn_o[...] = jnp.dot(latb, wuk_ref[...], preferred_element_type=F32).astype(BF16)
    v_o[...] = jnp.dot(latb, wuv_ref[...], preferred_element_type=F32).astype(BF16)


def _kv_expand(lat, wuk, wuv):
    n = lat.shape[0]
    tm = min(4 * TOKEN_TILE, n)
    assert n % tm == 0
    row = lambda i: (i, 0)
    return pl.pallas_call(
        _kv_expand_kernel,
        grid=(n // tm,),
        in_specs=[pl.BlockSpec((tm, KV_LORA), row), _const_spec(wuk.shape), _const_spec(wuv.shape)],
        out_specs=[pl.BlockSpec((tm, wuk.shape[1]), row), pl.BlockSpec((tm, wuv.shape[1]), row)],
        out_shape=[jax.ShapeDtypeStruct((n, wuk.shape[1]), BF16),
                   jax.ShapeDtypeStruct((n, wuv.shape[1]), BF16)],
        compiler_params=_cparams("parallel"),
        name="kv_expand",
    )(lat, wuk, wuv)


def _mlstm_kernel(qm_ref, km_ref, vm_ref, om_ref, g_ref, onorm_ref, c0_ref, n0_ref, m0_ref,
                  h_ref, cout_ref, nout_ref, mout_ref, c_sc, n_sc, m_sc, *, n_chunks, n_streams):
    lt = pl.program_id(1)

    @pl.when(lt == 0)
    def _():
        c_sc[...] = c0_ref[...]
        n_sc[...] = n0_ref[...]
        m_sc[...] = m0_ref[...]

    row = lax.broadcasted_iota(jnp.int32, (CHUNK, CHUNK), 0)
    col = lax.broadcasted_iota(jnp.int32, (CHUNK, CHUNK), 1)
    causal = col <= row
    ltri = causal.astype(BF16)
    utri = (row <= col).astype(BF16)
    ones_kl = jnp.ones((CHUNK, LANES), BF16)

    def split3(a):
        hi = a.astype(BF16)
        r1 = a - hi.astype(F32)
        mid = r1.astype(BF16)
        return hi, mid, (r1 - mid.astype(F32)).astype(BF16)

    def chunk(c, carry):
        r0 = pl.multiple_of(c * CHUNK, CHUNK)
        rows = pl.ds(r0, CHUNK)
        chains = [(bb, hh) for bb in range(n_streams) for hh in range(M_HEADS)]
        hsl = lambda hh: slice(hh * M_DH, (hh + 1) * M_DH)
        gate = {}
        for bb in range(n_streams):
            gates = g_ref[bb, rows, :]
            gates_t = gates.T
            cc = jnp.dot(ltri, jnp.concatenate(split3(gates), axis=1), preferred_element_type=F32)
            cum_col = cc[:, :LANES] + cc[:, LANES:2 * LANES] + cc[:, 2 * LANES:]
            cr = jnp.dot(jnp.concatenate(split3(gates_t), axis=0), utri, preferred_element_type=F32)
            cum_row = cr[:LANES] + cr[LANES:2 * LANES] + cr[2 * LANES:]
            gate[bb] = (gates, gates_t, cum_col, cum_row)
        qk, qc, qn = {}, {}, {}
        for ch in chains:
            bb, hh = ch
            rhs = jnp.concatenate([c_sc[bb, hh].astype(BF16),
                                   jnp.broadcast_to(n_sc[bb, hh], (LANES, M_DH)).astype(BF16),
                                   km_ref[bb, rows, hsl(hh)]], axis=0)
            r = _bdot_nt(qm_ref[bb, rows, hsl(hh)], rhs)
            qc[ch], qn[ch], qk[ch] = r[:, :M_DH], r[:, M_DH:M_DH + LANES], r[:, M_DH + LANES:]
        st = {}
        for ch in chains:
            bb, hh = ch
            gates, gates_t, cum_col, cum_row = gate[bb]
            ig_row = gates_t[hh:hh + 1, :]
            ig_col = gates[:, hh:hh + 1]
            b_row = cum_row[M_HEADS + hh:M_HEADS + hh + 1, :]
            b_col = cum_col[:, M_HEADS + hh:M_HEADS + hh + 1]
            m_prev = m_sc[bb, hh][:, :1]
            b_lanes = jnp.broadcast_to(b_col, (CHUNK, LANES))
            log_d = jnp.where(causal, b_lanes[:, :CHUNK] - b_row + ig_row, -jnp.inf)
            m_t = jnp.maximum(b_col + m_prev, jnp.max(log_d, axis=-1, keepdims=True))
            m_lanes = jnp.broadcast_to(m_t, (CHUNK, LANES))
            s = (qk[ch] * jnp.exp(log_d - m_lanes[:, :CHUNK])).astype(BF16)
            w_inter = jnp.exp(b_lanes + m_prev - m_lanes)
            m_new = m_t[CHUNK - 1:CHUNK, :]
            b_last = b_col[CHUNK - 1:CHUNK, :]
            w_s = jnp.exp(b_last - b_col + ig_col - m_new)
            decay = jnp.exp(b_last + m_prev - m_new)
            r = jnp.dot(s, jnp.concatenate([vm_ref[bb, rows, hsl(hh)], ones_kl], axis=1),
                        preferred_element_type=F32)
            st[ch] = (r[:, :M_DH], r[:, M_DH:], w_inter, m_lanes, m_new, w_s, decay)
        for ch in chains:
            bb, hh = ch
            sv, s_sum, w_inter, m_lanes, m_new, w_s, decay = st[ch]
            k = km_ref[bb, rows, hsl(hh)]
            v = vm_ref[bb, rows, hsl(hh)]
            c_h = c_sc[bb, hh]
            n_h = n_sc[bb, hh]
            num = w_inter * qc[ch] + sv
            den = w_inter * qn[ch] + s_sum
            hv = num / jnp.maximum(jnp.abs(den), jnp.exp(-m_lanes))
            wv_t = (jnp.broadcast_to(w_s, (CHUNK, M_DH)) * v).T
            c_sc[bb, hh] = decay * c_h + _bdot(wv_t, k)
            n_sc[bb, hh] = decay * n_h + jnp.sum(w_s * k.astype(F32), axis=0, keepdims=True)
            m_sc[bb, hh] = jnp.broadcast_to(m_new, (1, LANES))
            hn = _rms(hv, onorm_ref[:, hsl(hh)])
            h_ref[bb, rows, hsl(hh)] = (_sigmoid(om_ref[bb, rows, hsl(hh)]) * hn).astype(BF16)
        return carry

    lax.fori_loop(0, n_chunks, chunk, 0)

    @pl.when(lt == pl.num_programs(1) - 1)
    def _():
        cout_ref[...] = c_sc[...]
        nout_ref[...] = n_sc[...]
        mout_ref[...] = m_sc[...]


def _mlstm(qm, km, vm, om, gates, onorm, c0, n0, m0, *, tl):
    b, l, _ = qm.shape
    ns = MLSTM_STREAMS
    tile = lambda i, j: (i, j, 0)
    st4 = lambda i, j: (i, 0, 0, 0)
    state = [((ns, M_HEADS, M_DH, M_DH), (b, M_HEADS, M_DH, M_DH)),
             ((ns, M_HEADS, 1, M_DH), (b, M_HEADS, 1, M_DH)),
             ((ns, M_HEADS, 1, LANES), (b, M_HEADS, 1, LANES))]
    return pl.pallas_call(
        functools.partial(_mlstm_kernel, n_chunks=tl // CHUNK, n_streams=ns),
        grid=(b // ns, l // tl),
        in_specs=[pl.BlockSpec((ns, tl, M_W), tile)] * 4
        + [pl.BlockSpec((ns, tl, LANES), tile), _const_spec((1, M_W))]
        + [pl.BlockSpec(blk, st4) for blk, _ in state],
        out_specs=[pl.BlockSpec((ns, tl, M_W), tile)] + [pl.BlockSpec(blk, st4) for blk, _ in state],
        out_shape=[jax.ShapeDtypeStruct((b, l, M_W), BF16)]
        + [jax.ShapeDtypeStruct(full, F32) for _, full in state],
        scratch_shapes=[pltpu.VMEM(blk, F32) for blk, _ in state],
        compiler_params=_cparams("parallel", "arbitrary"),
        name="mlstm",
    )(qm, km, vm, om, gates, onorm.reshape(1, M_W), c0, n0, m0)


MLA_SUM_ROWS = 16


MLA_EXP_RANGE = 60.0


def _mla_kernel(qi_ref, kj_ref, flag_ref, q_ref, kn_ref, kr_ref, vt_ref, o_ref, m_sc, acc_sc, bad_sc,
                redo_sm, *, tq, tkb, offset, variants):
    p = pl.program_id(1)
    i = qi_ref[p]
    j = kj_ref[p]
    flags = flag_ref[p]

    @pl.when(j == 0)
    def _():
        m_sc[...] = jnp.zeros_like(m_sc)
        acc_sc[...] = jnp.zeros_like(acc_sc)

    def score_tiles(nk, masked):
        kr = kr_ref[0, :nk, :]
        scores = []
        for hh in range(A_HEADS):
            q = q_ref[0, :, hh * 2 * LANES:(hh + 1) * 2 * LANES]
            k = jnp.concatenate([kn_ref[0, :nk, hh * A_NOPE:(hh + 1) * A_NOPE], kr], axis=-1)
            scores.append(_bdot_nt(k, q))
        if masked:
            k_chunk = (j * tkb + lax.broadcasted_iota(jnp.int32, (nk, tq), 0)) // CHUNK
            q_chunk = (offset + i * tq + lax.broadcasted_iota(jnp.int32, (nk, tq), 1)) // CHUNK
            visible = k_chunk <= q_chunk
            scores = [jnp.where(visible, s, -jnp.inf) for s in scores]
        return scores

    def values_t(hh, nk):
        return jnp.concatenate([vt_ref[0, hh * A_VDIM:(hh + 1) * A_VDIM, :nk],
                                jnp.ones((MLA_SUM_ROWS, nk), BF16)], axis=0)

    def add_block(nk, masked):
        scores = score_tiles(nk, masked)
        any_bad = jnp.zeros((1, tq), F32)
        for hh in range(A_HEADS):
            s = scores[hh]
            m_ref = m_sc[hh]
            d = jnp.max(s, axis=0, keepdims=True) - m_ref
            ok = jnp.logical_and(d <= MLA_EXP_RANGE, jnp.logical_or(d >= -MLA_EXP_RANGE, j > 0))
            pr = jnp.exp2(s - m_ref).astype(BF16)
            acc = acc_sc[hh]
            acc_sc[hh] = jnp.where(ok, acc + jnp.dot(values_t(hh, nk), pr, preferred_element_type=F32),
                                   acc)
            bad = jnp.where(ok, 0.0, 1.0)
            bad_sc[hh] = bad
            any_bad = jnp.maximum(any_bad, bad)
        redo_sm[0] = (jnp.max(any_bad) > 0.0).astype(jnp.int32)

    def redo_block(nk, masked):
        scores = score_tiles(nk, masked)
        for hh in range(A_HEADS):
            s = scores[hh]
            bad = bad_sc[hh] > 0.0
            m_prev = jnp.where(j == 0, -jnp.inf, m_sc[hh])
            m_new = jnp.maximum(m_prev, jnp.max(s, axis=0, keepdims=True))
            alpha = jnp.exp2(m_prev - m_new)
            pr = jnp.exp2(s - m_new).astype(BF16)
            acc = acc_sc[hh]
            acc_sc[hh] = jnp.where(
                bad, alpha * acc + jnp.dot(values_t(hh, nk), pr, preferred_element_type=F32), acc)
            m_sc[hh] = jnp.where(bad, m_new, m_sc[hh])

    for vi, (nk, masked) in enumerate(variants):
        pl.when((flags >> 1) == vi)(functools.partial(add_block, nk, masked))
    for vi, (nk, masked) in enumerate(variants):
        pl.when(jnp.logical_and((flags >> 1) == vi, redo_sm[0] != 0))(
            functools.partial(redo_block, nk, masked))

    @pl.when((flags & 1) != 0)
    def _():
        for hh in range(A_HEADS):
            acc = acc_sc[hh]
            o_ref[0, :, hh * A_VDIM:(hh + 1) * A_VDIM] = (
                acc[:A_VDIM] / acc[A_VDIM:A_VDIM + 1]).T.astype(BF16)


def _transpose_kernel(x_ref, o_ref):
    o_ref[0] = x_ref[0].astype(F32).T.astype(o_ref.dtype)


def _transpose_tiles(x, t):
    b, l, c = x.shape
    return pl.pallas_call(
        _transpose_kernel,
        grid=(b, l // t),
        in_specs=[pl.BlockSpec((1, t, c), lambda bb, i: (bb, i, 0))],
        out_specs=pl.BlockSpec((1, c, t), lambda bb, i: (bb, 0, i)),
        out_shape=jax.ShapeDtypeStruct((b, c, l), x.dtype),
        compiler_params=_cparams("parallel", "parallel"),
        name="transpose_tiles",
    )(x)


def _mla_pairs(lq, lk, tq, tkb, sub, offset):
    qi, kj, flags, variants = [], [], [], []
    for i in range(lq // tq):
        first_row, last_row = offset + i * tq, offset + (i + 1) * tq - 1
        k_end = min((last_row // CHUNK + 1) * CHUNK, lk)
        k_all = (first_row // CHUNK + 1) * CHUNK
        nj = -(-k_end // tkb)
        for j in range(nj):
            nk = -(-(min(k_end, (j + 1) * tkb) - j * tkb) // sub) * sub
            variant = (nk, j * tkb + nk > k_all)
            if variant not in variants:
                variants.append(variant)
            qi.append(i)
            kj.append(j)
            flags.append(int(j == nj - 1) + 2 * variants.index(variant))
    return (np.asarray(qi, np.int32), np.asarray(kj, np.int32), np.asarray(flags, np.int32),
            tuple(variants))


def _mla_attention(q, kn, kr, vt, *, tq, tkb, sub, offset):
    b, lq, _ = q.shape
    lk = kn.shape[1]
    assert lk % tkb == 0 and tkb % sub == 0 and vt.shape[2] == lk
    qi, kj, flags, variants = _mla_pairs(lq, lk, tq, tkb, sub, offset)
    qmap = lambda bb, p, qi_r, kj_r, flag_r: (bb, qi_r[p], 0)
    kmap = lambda bb, p, qi_r, kj_r, flag_r: (bb, kj_r[p], 0)
    grid_spec = pltpu.PrefetchScalarGridSpec(
        num_scalar_prefetch=3,
        grid=(b, len(qi)),
        in_specs=[pl.BlockSpec((1, tq, q.shape[-1]), qmap),
                  pl.BlockSpec((1, tkb, kn.shape[-1]), kmap),
                  pl.BlockSpec((1, tkb, kr.shape[-1]), kmap),
                  pl.BlockSpec((1, vt.shape[1], tkb), lambda bb, p, qi_r, kj_r, flag_r: (bb, 0, kj_r[p]))],
        out_specs=pl.BlockSpec((1, tq, A_HEADS * A_VDIM), qmap),
        scratch_shapes=[pltpu.VMEM((A_HEADS, 1, tq), F32),
                        pltpu.VMEM((A_HEADS, A_VDIM + MLA_SUM_ROWS, tq), F32),
                        pltpu.VMEM((A_HEADS, 1, tq), F32),
                        pltpu.SMEM((1,), jnp.int32)],
    )
    return pl.pallas_call(
        functools.partial(_mla_kernel, tq=tq, tkb=tkb, offset=offset, variants=variants),
        grid_spec=grid_spec,
        out_shape=jax.ShapeDtypeStruct((b, lq, A_HEADS * A_VDIM), BF16),
        compiler_params=_cparams("parallel", "arbitrary"),
        name="mla_attention",
    )(jnp.asarray(qi), jnp.asarray(kj), jnp.asarray(flags), q, kn, kr, vt)


def _odd_in_kernel(x_ref, gain_ref, wu_ref, wq_ref, wk_ref, wv_ref, u_o, q_o, k_o, v_o, kb_o, vb_o):
    h = _rms(x_ref[...], gain_ref[...]).astype(BF16)
    u_o[...] = jnp.dot(h, wu_ref[...], preferred_element_type=F32)
    q_o[...] = (jnp.dot(h, wq_ref[...], preferred_element_type=F32) * SB_DH ** -0.5).astype(BF16)
    k = jnp.dot(h, wk_ref[...], preferred_element_type=F32)
    v = jnp.dot(h, wv_ref[...], preferred_element_type=F32)
    k_o[...] = pltpu.einshape("t(hd)->thd", k, h=SB_HEADS)
    v_o[...] = pltpu.einshape("t(hd)->thd", v, h=SB_HEADS)
    kb_o[...] = k.astype(BF16)
    vb_o[...] = v.astype(BF16)


def _odd_in(x, gain, w, *, tm):
    n = x.shape[0]
    row = lambda i: (i, 0)
    weights = [w["wu"], w["wq"], w["wk"], w["wv"]]
    return pl.pallas_call(
        _odd_in_kernel,
        grid=(n // tm,),
        in_specs=[pl.BlockSpec((tm, D_MODEL), row), _const_spec((1, D_MODEL))]
        + [_const_spec(a.shape) for a in weights],
        out_specs=[pl.BlockSpec((tm, S5_W), row), pl.BlockSpec((tm, SB_W), row)]
        + [pl.BlockSpec((tm, SB_HEADS, SB_DH), lambda i: (i, 0, 0))] * 2
        + [pl.BlockSpec((tm, SB_W), row)] * 2,
        out_shape=[jax.ShapeDtypeStruct((n, S5_W), F32),
                   jax.ShapeDtypeStruct((n, SB_W), BF16),
                   jax.ShapeDtypeStruct((n, SB_HEADS, SB_DH), F32),
                   jax.ShapeDtypeStruct((n, SB_HEADS, SB_DH), F32),
                   jax.ShapeDtypeStruct((n, SB_W), BF16),
                   jax.ShapeDtypeStruct((n, SB_W), BF16)],
        compiler_params=_cparams("parallel"),
        name="odd_in",
    )(x, gain.reshape(1, D_MODEL), *weights)


S5_BLOCKS = 4
S5_BLK_IN = S5_W // S5_BLOCKS
S5_BLK_ST = S5_STATE // S5_BLOCKS


def _s5_kernel(u_ref, bbr_ref, bbi_ref, ar_ref, ai_ref, ccr_ref, cci_ref, d_ref, wglu_ref,
               x0r_ref, x0i_ref, hs_o, xr_o, xi_o, u_tm, hs_tm, xr_buf, xi_buf, xr_st, xi_st,
               *, tt, nb):
    t = pl.program_id(0)

    @pl.when(t == 0)
    def _():
        xr_st[...] = x0r_ref[...]
        xi_st[...] = x0i_ref[...]

    for bb in range(nb):
        for k in range(S5_BLOCKS):
            u_tm[k, pl.ds(bb, tt, stride=nb), :] = u_ref[bb, :, k * S5_BLK_IN:(k + 1) * S5_BLK_IN]

    for k in range(S5_BLOCKS):
        cs = slice(k * S5_BLK_ST, (k + 1) * S5_BLK_ST)
        ub = u_tm[k].astype(BF16)
        xr_buf[:, cs] = jnp.dot(ub, bbr_ref[k], preferred_element_type=F32)
        xi_buf[:, cs] = jnp.dot(ub, bbi_ref[k], preferred_element_type=F32)

    for k in range(S5_BLOCKS):
        cs = slice(k * S5_BLK_ST, (k + 1) * S5_BLK_ST)
        ar = jnp.broadcast_to(ar_ref[:, cs], (nb, S5_BLK_ST))
        ai = jnp.broadcast_to(ai_ref[:, cs], (nb, S5_BLK_ST))

        def step(s, carry, cs=cs, ar=ar, ai=ai):
            xr, xi = carry
            rows = pl.ds(pl.multiple_of(s * nb, nb), nb)
            nxr = ar * xr - ai * xi + xr_buf[rows, cs]
            nxi = ar * xi + ai * xr + xi_buf[rows, cs]
            xr_buf[rows, cs] = nxr
            xi_buf[rows, cs] = nxi
            return nxr, nxi

        xr, xi = lax.fori_loop(0, tt, step, (xr_st[:, cs], xi_st[:, cs]), unroll=8)
        xr_st[:, cs] = xr
        xi_st[:, cs] = xi

    gs = []
    for k in range(S5_BLOCKS):
        cs = slice(k * S5_BLK_ST, (k + 1) * S5_BLK_ST)
        y = (_bdot(xr_buf[:, cs], ccr_ref[k]) + _bdot(xi_buf[:, cs], cci_ref[k])
             + d_ref[:, k * S5_BLK_IN:(k + 1) * S5_BLK_IN] * u_tm[k])
        gs.append(jax.nn.gelu(y, approximate=True))
    g = jnp.concatenate(gs, axis=-1)
    hs = g * _sigmoid(_bdot(g, wglu_ref[...]))
    for k in range(S5_BLOCKS):
        hs_tm[k] = hs[:, k * S5_BLK_IN:(k + 1) * S5_BLK_IN]
    for bb in range(nb):
        for k in range(S5_BLOCKS):
            hs_o[bb, :, k * S5_BLK_IN:(k + 1) * S5_BLK_IN] = (
                hs_tm[k, pl.ds(bb, tt, stride=nb), :].astype(BF16))

    @pl.when(t == pl.num_programs(0) - 1)
    def _():
        xr_o[...] = xr_st[...]
        xi_o[...] = xi_st[...]


def _s5(u, w, x0r, x0i, *, tt):
    nb, l, _ = u.shape
    blk = lambda t: (0, t, 0)
    consts = [w["bbr"], w["bbi"], w["ar"], w["ai"], w["ccr"], w["cci"], w["d"], w["wglu"], x0r, x0i]
    return pl.pallas_call(
        functools.partial(_s5_kernel, tt=tt, nb=nb),
        grid=(l // tt,),
        in_specs=[pl.BlockSpec((nb, tt, S5_W), blk)] + [_const_spec(a.shape) for a in consts],
        out_specs=[pl.BlockSpec((nb, tt, S5_W), blk), _const_spec((nb, S5_STATE)),
                   _const_spec((nb, S5_STATE))],
        out_shape=[jax.ShapeDtypeStruct((nb, l, S5_W), BF16),
                   jax.ShapeDtypeStruct((nb, S5_STATE), F32),
                   jax.ShapeDtypeStruct((nb, S5_STATE), F32)],
        scratch_shapes=[pltpu.VMEM((S5_BLOCKS, tt * nb, S5_BLK_IN), F32),
                        pltpu.VMEM((S5_BLOCKS, tt * nb, S5_BLK_IN), F32),
                        pltpu.VMEM((tt * nb, S5_STATE), F32), pltpu.VMEM((tt * nb, S5_STATE), F32),
                        pltpu.VMEM((nb, S5_STATE), F32), pltpu.VMEM((nb, S5_STATE), F32)],
        compiler_params=_cparams("arbitrary"),
        name="s5",
    )(u, *consts)


def _sb_kernel(q_ref, k_ref, v_ref, o_ref, r_sc, acc_sc, qh_sc, *, tq, tk, offset, n_sub):
    for su in range(n_sub):
        _sb_query_tile(pl.program_id(1) * n_sub + su, slice(su * tq, (su + 1) * tq),
                       q_ref, k_ref, v_ref, o_ref, r_sc, acc_sc, qh_sc, tq=tq, tk=tk, offset=offset)


def _sb_query_tile(i, rs, q_ref, k_ref, v_ref, o_ref, r_sc, acc_sc, qh_sc, *, tq, tk, offset):
    n_pairs = SB_HEADS // 2
    half = lax.broadcasted_iota(jnp.int32, (1, LANES), 1) // SB_DH
    for pr in range(n_pairs):
        qq = q_ref[0, rs, pr * LANES:(pr + 1) * LANES]
        qh_sc[pr, :tq] = jnp.where(half == 0, qq, jnp.zeros_like(qq))
        qh_sc[pr, tq:] = jnp.where(half == 1, qq, jnp.zeros_like(qq))
    r_sc[...] = jnp.zeros_like(r_sc)
    acc_sc[...] = jnp.zeros_like(acc_sc)

    jrow = lax.broadcasted_iota(jnp.int32, (tk, tk), 0)
    jcol = lax.broadcasted_iota(jnp.int32, (tk, tk), 1)
    later = (jrow > jcol).astype(BF16)
    j_diag = (offset + i * tq) // tk

    def sweep(tiles):
        zs, vv, causal = {}, {}, {}
        for t, (j, masked, live) in enumerate(tiles):
            k0 = pl.multiple_of(j * tk, tk)
            if masked:
                q_pos = offset + i * tq + lax.broadcasted_iota(jnp.int32, (tq, tk), 0)
                causal[t] = (k0 + lax.broadcasted_iota(jnp.int32, (tq, tk), 1)) < q_pos
            for pr in range(n_pairs):
                ps = slice(pr * LANES, (pr + 1) * LANES)
                zs[t, pr] = _bdot_nt(qh_sc[pr], k_ref[0, pl.ds(k0, tk), ps])
                vv[t, pr] = v_ref[0, pl.ds(k0, tk), ps]
        l1s, lbs, cums = {}, {}, {}
        for t, (j, masked, live) in enumerate(tiles):
            for pr in range(n_pairs):
                parts = []
                for sub in range(2):
                    hh = 2 * pr + sub
                    z = zs[t, pr][sub * tq:(sub + 1) * tq]
                    sp = jnp.maximum(z, 0.0) + jnp.log(1.0 + jnp.exp(-jnp.abs(z)))
                    l1 = -sp
                    if masked:
                        l1 = jnp.where(causal[t], l1, 0.0)
                    hi = l1.astype(BF16)
                    parts += [hi, (l1 - hi.astype(F32)).astype(BF16)]
                    l1s[t, hh] = l1
                    lbs[t, hh] = z - sp
                c = jnp.dot(jnp.concatenate(parts, axis=0), later, preferred_element_type=F32)
                cums[t, 2 * pr] = c[:tq] + c[tq:2 * tq]
                cums[t, 2 * pr + 1] = c[2 * tq:3 * tq] + c[3 * tq:]
        for t, (j, masked, live) in enumerate(tiles):
            r_max = jnp.full((1, 1), -jnp.inf, F32)
            dead_shift = 0.0 if live is None else jnp.where(live, 0.0, -1e30)
            keep = 1.0 if live is None else jnp.where(live, 1.0, 0.0)
            for pr in range(n_pairs):
                ws = []
                for sub in range(2):
                    hh = 2 * pr + sub
                    r_old = r_sc[hh]
                    log_w = lbs[t, hh] + cums[t, hh] + (r_old + dead_shift)
                    if masked:
                        log_w = jnp.where(causal[t], log_w, -jnp.inf)
                    ws.append(jnp.exp(log_w).astype(BF16))
                    r_new = r_old + keep * jnp.sum(l1s[t, hh], axis=-1, keepdims=True)
                    r_sc[hh] = r_new
                    r_max = jnp.maximum(r_max, jnp.max(r_new, axis=0, keepdims=True))
                v2 = jnp.concatenate([jnp.where(half == 0, vv[t, pr], jnp.zeros_like(vv[t, pr])),
                                      jnp.where(half == 1, vv[t, pr], jnp.zeros_like(vv[t, pr]))],
                                     axis=0)
                acc_sc[:, pr * LANES:(pr + 1) * LANES] += jnp.dot(
                    jnp.concatenate(ws, axis=1), v2, preferred_element_type=F32)
        return r_max[0, 0]

    has_prev = j_diag >= 1
    r_first = sweep([(j_diag, True, None), (jnp.maximum(j_diag - 1, 0), False, has_prev)])

    def cond(carry):
        j, r_max = carry
        return jnp.logical_and(j >= 0, r_max > SB_LOG_FLOOR)

    def body(carry):
        j, _ = carry
        return j - 1, sweep([(j, False, None)])

    lax.while_loop(cond, body, (j_diag - 2, r_first))
    o_ref[0, rs] = acc_sc[...].astype(BF16)


def _sb_attention(q, k, v, *, tq, tk, offset):
    b, lq, _ = q.shape
    lk = k.shape[1]
    assert tk % tq == 0 and offset % tk == 0 and lk % tk == 0
    n_sub = math.gcd(lq // tq, SB_TILES_PER_STEP)
    return pl.pallas_call(
        functools.partial(_sb_kernel, tq=tq, tk=tk, offset=offset, n_sub=n_sub),
        grid=(b, lq // (tq * n_sub)),
        in_specs=[pl.BlockSpec((1, tq * n_sub, SB_W), lambda bb, i: (bb, i, 0)),
                  pl.BlockSpec((1, lk, SB_W), lambda bb, i: (bb, 0, 0)),
                  pl.BlockSpec((1, lk, SB_W), lambda bb, i: (bb, 0, 0))],
        out_specs=pl.BlockSpec((1, tq * n_sub, SB_W), lambda bb, i: (bb, i, 0)),
        out_shape=jax.ShapeDtypeStruct((b, lq, SB_W), BF16),
        scratch_shapes=[pltpu.VMEM((SB_HEADS, tq, 1), F32),
                        pltpu.VMEM((tq, SB_W), F32),
                        pltpu.VMEM((SB_HEADS // 2, 2 * tq, LANES), BF16)],
        compiler_params=_cparams("parallel", "arbitrary"),
        name="sb_attention",
    )(q, k, v)


def _pad_cols(w, width):
    return jnp.pad(w, ((0, 0), (0, width - w.shape[1])))


def _even_weights(w_in, b_ig, b_fg, q_norm, kv_norm, w_uq, w_ukv):
    o = 4 * M_W
    w_gate = w_in[:, o:o + 2 * M_HEADS]
    o += 2 * M_HEADS
    w_cq = w_in[:, o:o + Q_LORA]
    o += Q_LORA
    w_ckv = w_in[:, o:o + KV_LORA]
    o += KV_LORA
    w_kr = w_in[:, o:o + A_ROPE]
    hr = A_ROPE // 2
    wkra = _pad_cols(w_kr, LANES)
    wkrb = _pad_cols(jnp.concatenate([w_kr[:, hr:], w_kr[:, :hr]], axis=1), LANES)
    uq = w_uq.reshape(Q_LORA, A_HEADS, A_NOPE + A_ROPE)
    zeros = jnp.zeros((Q_LORA, A_HEADS, LANES - A_ROPE), w_uq.dtype)
    wq = jnp.concatenate([uq, zeros], axis=-1).reshape(Q_LORA, A_HEADS * 2 * LANES)
    ws = jnp.concatenate([uq[..., A_NOPE + hr:], uq[..., A_NOPE:A_NOPE + hr], zeros],
                         axis=-1).reshape(Q_LORA, A_HEADS * LANES)
    ukv = w_ukv.reshape(KV_LORA, A_HEADS, A_NOPE + A_VDIM)
    return dict(
        wm=w_in[:, :4 * M_W].astype(BF16),
        wg=_pad_cols(w_gate, LANES).astype(BF16),
        bg=_pad_cols(jnp.concatenate([b_ig, b_fg]).reshape(1, 2 * M_HEADS).astype(F32), LANES),
        wcq=w_cq.astype(BF16), wckv=w_ckv.astype(BF16),
        wkra=wkra.astype(BF16), wkrb=wkrb.astype(BF16),
        qn=q_norm.reshape(1, Q_LORA).astype(F32), kvn=kv_norm.reshape(1, KV_LORA).astype(F32),
        wq=wq.astype(BF16), ws=ws.astype(BF16),
        wuk=ukv[..., :A_NOPE].reshape(KV_LORA, A_HEADS * A_NOPE).astype(BF16),
        wuv=ukv[..., A_NOPE:].reshape(KV_LORA, A_HEADS * A_VDIM).astype(BF16),
    )


def _rope_tables(pos):
    half = A_ROPE // 2
    inv_freq = ROPE_THETA ** (-jnp.arange(half, dtype=F32) / half)
    ang = pos.astype(F32)[:, None] * inv_freq[None, :]
    cos, sin = jnp.cos(ang), jnp.sin(ang)
    pad = jnp.zeros((pos.shape[0], LANES - A_ROPE), F32)
    return (jnp.concatenate([cos, cos, pad], axis=1), jnp.concatenate([-sin, sin, pad], axis=1))


def _s5_weights(a_re, a_im, b_re, b_im, c_re, c_im, d, log_dt, w_glu):
    dt = jnp.exp(log_dt.astype(F32))[:, None]
    ar, ai = a_re.astype(F32), a_im.astype(F32)
    mag = jnp.exp(dt * ar)
    ab_r, ab_i = mag * jnp.cos(dt * ai), mag * jnp.sin(dt * ai)
    den = ar * ar + ai * ai
    c_r = ((ab_r - 1.0) * ar + ab_i * ai) / den
    c_i = (ab_i * ar - (ab_r - 1.0) * ai) / den
    br, bi = b_re.astype(F32), b_im.astype(F32)
    bb_r = c_r[..., None] * br - c_i[..., None] * bi
    bb_i = c_r[..., None] * bi + c_i[..., None] * br
    gb = S5_GROUPS // S5_BLOCKS
    eye = jnp.eye(gb, dtype=F32)
    bd_in = lambda m: jnp.einsum("kgpc,gh->kgchp", m.reshape(S5_BLOCKS, gb, S5_P, S5_GSIZE),
                                 eye).reshape(S5_BLOCKS, S5_BLK_IN, S5_BLK_ST)
    bd_out = lambda m: jnp.einsum("kgcp,gh->kgphc", m.reshape(S5_BLOCKS, gb, S5_GSIZE, S5_P),
                                  eye).reshape(S5_BLOCKS, S5_BLK_ST, S5_BLK_IN)
    return dict(
        bbr=bd_in(bb_r).astype(BF16), bbi=bd_in(bb_i).astype(BF16),
        ar=ab_r.reshape(1, S5_STATE), ai=ab_i.reshape(1, S5_STATE),
        ccr=bd_out(c_re.astype(F32)).astype(BF16), cci=bd_out(-c_im.astype(F32)).astype(BF16),
        d=d.reshape(1, S5_W).astype(F32), wglu=w_glu.astype(BF16),
    )


def _pad_keys(a, mult):
    pad = (-a.shape[1]) % mult
    return jnp.pad(a, ((0, 0), (0, pad), (0, 0))) if pad else a


def _trunk(x, pos0, cache, p):
    nb, l, _ = x.shape
    assert nb == SUBLANES, "the S5 layout puts the streams of one time step on the sublanes"
    assert l % CHUNK == 0
    n = nb * l
    tm = min(TOKEN_TILE, n)
    tl = min(tm, l)
    assert l % tl == 0 and tm % tl == 0
    past = 0 if cache is None else cache["sb_k"].shape[1]
    assert pos0 == past
    xf = x.reshape(n, D_MODEL)
    bf = lambda a: a.astype(BF16)

    xf = _ffn(xf, p["norm_ffn"][0, 0], bf(p["ffn_w_gate"][0, 0]), bf(p["ffn_w_up"][0, 0]),
              bf(p["ffn_w_down"][0, 0]), tm=tm)
    ew = _even_weights(p["even_w_in"][0], p["mlstm_b_igate"][0], p["mlstm_b_fgate"][0],
                       p["mla_q_norm"][0], p["mla_kv_norm"][0], p["mla_w_uq"][0], p["mla_w_ukv"][0])
    cos, sin = _rope_tables(pos0 + jnp.arange(l))
    if tm > l:
        cos, sin = jnp.tile(cos, (tm // l, 1)), jnp.tile(sin, (tm // l, 1))
    qm, km, vm, om, gates, q, kn, v, kr128, lat, kr = _even_in(
        xf, p["norm_mix"][0], ew, cos, sin, tm=tm, seq_len=l)
    v_transposed = v.ndim == 3

    if cache is None:
        c0 = jnp.zeros((nb, M_HEADS, M_DH, M_DH), F32)
        n0 = jnp.zeros((nb, M_HEADS, 1, M_DH), F32)
        m0 = jnp.zeros((nb, M_HEADS, 1, LANES), F32)
    else:
        c0 = cache["mlstm_c"].astype(F32)
        n0 = cache["mlstm_n"].astype(F32).reshape(nb, M_HEADS, 1, M_DH)
        m0 = jnp.broadcast_to(cache["mlstm_m"].astype(F32)[..., None, None], (nb, M_HEADS, 1, LANES))
    r3 = lambda a: a.reshape(nb, l, a.shape[-1])
    h_m, c_new, n_new, m_new = _mlstm(r3(qm), r3(km), r3(vm), r3(om), r3(gates),
                                      p["mlstm_out_norm"][0], c0, n0, m0, tl=tl)

    mla_tq = min(MLA_TILE, -(-l // LANES) * LANES)
    if cache is None:
        mla_sub = mla_tq
        mla_tkb = min(2 * mla_tq, l)
    else:
        mla_sub = mla_tkb = -(-(past + max(l, mla_tq)) // LANES) * LANES
    kn3, kr3 = r3(kn), r3(kr128)
    v3, vt = (None, v) if v_transposed else (r3(v), None)
    if cache is not None:
        ckn, cv = _kv_expand(cache["mla_latent"].astype(F32).reshape(nb * past, KV_LORA),
                             ew["wuk"], ew["wuv"])
        ckr = jnp.pad(cache["mla_krope"], ((0, 0), (0, 0), (0, LANES - A_ROPE))).astype(BF16)
        kn3 = jnp.concatenate([ckn.reshape(nb, past, -1), kn3], axis=1)
        kr3 = jnp.concatenate([ckr, kr3], axis=1)
        cv3 = cv.reshape(nb, past, -1)
        if v_transposed:
            vt = jnp.concatenate([_transpose_tiles(cv3, math.gcd(past, TOKEN_TILE)), vt], axis=2)
        else:
            v3 = jnp.concatenate([cv3, v3], axis=1)
    if vt is None:
        vt = _transpose_tiles(_pad_keys(v3, mla_tkb), mla_sub)
    else:
        vt = jnp.pad(vt, ((0, 0), (0, 0), (0, (-vt.shape[2]) % mla_tkb)))
    h_a = _mla_attention(_pad_keys(r3(q), mla_tq), _pad_keys(kn3, mla_tkb), _pad_keys(kr3, mla_tkb),
                         vt, tq=mla_tq, tkb=mla_tkb, sub=mla_sub, offset=past)[:, :l]

    w_out = bf(p["even_w_out"][0])
    xf = _ffn(xf, p["norm_ffn"][0, 1], bf(p["ffn_w_gate"][0, 1]), bf(p["ffn_w_up"][0, 1]),
              bf(p["ffn_w_down"][0, 1]),
              mix=(h_m.reshape(n, M_W), h_a.reshape(n, -1), w_out[:M_W], w_out[M_W:]), tm=tm)

    xf = _ffn(xf, p["norm_ffn"][1, 0], bf(p["ffn_w_gate"][1, 0]), bf(p["ffn_w_up"][1, 0]),
              bf(p["ffn_w_down"][1, 0]), tm=tm)
    w_in = p["odd_w_in"][0]
    ow = dict(wu=bf(w_in[:, :S5_W]), wq=bf(w_in[:, S5_W:S5_W + SB_W]),
              wk=bf(w_in[:, S5_W + SB_W:S5_W + 2 * SB_W]), wv=bf(w_in[:, S5_W + 2 * SB_W:]))
    u, qs, ks, vs, ksb, vsb = _odd_in(xf, p["norm_mix"][1], ow, tm=tm)

    sw = _s5_weights(p["s5_A_re"][0], p["s5_A_im"][0], p["s5_B_re"][0], p["s5_B_im"][0],
                     p["s5_C_re"][0], p["s5_C_im"][0], p["s5_D"][0], p["s5_log_dt"][0], p["s5_w_glu"][0])
    if cache is None:
        x0r = jnp.zeros((nb, S5_STATE), F32)
        x0i = jnp.zeros((nb, S5_STATE), F32)
    else:
        x0r = cache["s5_re"].astype(F32).reshape(nb, S5_STATE)
        x0i = cache["s5_im"].astype(F32).reshape(nb, S5_STATE)
    h_s, xr_new, xi_new = _s5(r3(u), sw, x0r, x0i, tt=min(S5_TT, l))

    k3, v3 = r3(ksb), r3(vsb)
    if cache is not None:
        k3 = jnp.concatenate([bf(cache["sb_k"].reshape(nb, past, SB_W)), k3], axis=1)
        v3 = jnp.concatenate([bf(cache["sb_v"].reshape(nb, past, SB_W)), v3], axis=1)
    h_b = _sb_attention(r3(qs), _pad_keys(k3, SB_TILE), _pad_keys(v3, SB_TILE), tq=min(SB_TILE, l),
                        tk=SB_TILE, offset=past)

    w_out = bf(p["odd_w_out"][0])
    y = _ffn(xf, p["norm_ffn"][1, 1], bf(p["ffn_w_gate"][1, 1]), bf(p["ffn_w_up"][1, 1]),
             bf(p["ffn_w_down"][1, 1]),
             mix=(h_s.reshape(n, S5_W), h_b.reshape(n, SB_W), w_out[:S5_W], w_out[S5_W:]),
             final_gain=p["norm_final"], tm=tm)

    return (y.reshape(nb, l, D_MODEL),
            lat.reshape(1, nb, l, KV_LORA), kr.reshape(1, nb, l, A_ROPE),
            c_new[None], n_new.reshape(1, nb, M_HEADS, M_DH), m_new[:, :, 0, 0][None],
            xr_new.reshape(1, nb, S5_GROUPS, S5_P), xi_new.reshape(1, nb, S5_GROUPS, S5_P),
            ks.reshape(1, nb, l, SB_HEADS, SB_DH), vs.reshape(1, nb, l, SB_HEADS, SB_DH))


def kernel(x_prompt, x_sample, cache_mla_latent, cache_mla_krope, state_mlstm_C, state_mlstm_n, state_mlstm_m, state_s5_re, state_s5_im, cache_sb_k, cache_sb_v, norm_ffn, norm_mix, norm_final, ffn_w_gate, ffn_w_up, ffn_w_down, even_w_in, even_w_out, mlstm_b_igate, mlstm_b_fgate, mlstm_out_norm, mla_q_norm, mla_kv_norm, mla_w_uq, mla_w_ukv, odd_w_in, odd_w_out, s5_A_re, s5_A_im, s5_B_re, s5_B_im, s5_C_re, s5_C_im, s5_D, s5_log_dt, s5_w_glu):
    p = dict(norm_ffn=norm_ffn, norm_mix=norm_mix, norm_final=norm_final, ffn_w_gate=ffn_w_gate,
             ffn_w_up=ffn_w_up, ffn_w_down=ffn_w_down, even_w_in=even_w_in, even_w_out=even_w_out,
             mlstm_b_igate=mlstm_b_igate, mlstm_b_fgate=mlstm_b_fgate, mlstm_out_norm=mlstm_out_norm,
             mla_q_norm=mla_q_norm, mla_kv_norm=mla_kv_norm, mla_w_uq=mla_w_uq, mla_w_ukv=mla_w_ukv,
             odd_w_in=odd_w_in, odd_w_out=odd_w_out, s5_A_re=s5_A_re, s5_A_im=s5_A_im, s5_B_re=s5_B_re,
             s5_B_im=s5_B_im, s5_C_re=s5_C_re, s5_C_im=s5_C_im, s5_D=s5_D, s5_log_dt=s5_log_dt,
             s5_w_glu=s5_w_glu)
    past = cache_sb_k.shape[2]
    cache = dict(mla_latent=cache_mla_latent[0], mla_krope=cache_mla_krope[0],
                 mlstm_c=state_mlstm_C[0], mlstm_n=state_mlstm_n[0], mlstm_m=state_mlstm_m[0],
                 s5_re=state_s5_re[0], s5_im=state_s5_im[0],
                 sb_k=cache_sb_k[0], sb_v=cache_sb_v[0])
    out_p = _trunk(x_prompt, 0, None, p)
    out_s = _trunk(x_sample, past, cache, p)
    return (out_p[0], out_s[0]) + tuple(out_p[1:]) + tuple(out_s[1:])
```

```python
import functools
import math

import jax
import jax.numpy as jnp
import numpy as np
from jax import lax
from jax.experimental import pallas as pl
from jax.experimental.pallas import tpu as pltpu

F32 = jnp.float32
BF16 = jnp.bfloat16

D_MODEL = 1024
CHUNK = 64
D_FF = 2816
EPS = 1e-6
M_HEADS = 4
M_DH = 128
M_W = M_HEADS * M_DH
A_HEADS = 4
A_NOPE = 128
A_ROPE = 64
A_VDIM = 128
Q_LORA = 384
KV_LORA = 256
ROPE_THETA = 10000.0
MLA_SCALE = (A_NOPE + A_ROPE) ** -0.5
S5_W = 512
S5_GSIZE = 16
S5_GROUPS = S5_W // S5_GSIZE
S5_P = 64
S5_STATE = S5_GROUPS * S5_P
SB_HEADS = 8
SB_DH = 64
SB_W = SB_HEADS * SB_DH

LANES = 128
SUBLANES = 8
VMEM_LIMIT_BYTES = 56 * 1024 * 1024

TOKEN_TILE = 512
FF_CHUNK = 256
MLA_TILE = 512
SB_TILE = 128
SB_TILES_PER_STEP = 4
S5_TT = 128
MLSTM_STREAMS = 4
SB_LOG_FLOOR = -24.0
LOG2E = math.log2(math.e)

HIGHEST = lax.Precision.HIGHEST


def _cparams(*sem):
    return pltpu.CompilerParams(dimension_semantics=sem, vmem_limit_bytes=VMEM_LIMIT_BYTES)


def _bdot(a, b):
    return jnp.dot(a.astype(BF16), b.astype(BF16), preferred_element_type=F32)


def _bdot_nt(a, b):
    return lax.dot_general(a.astype(BF16), b.astype(BF16), (((1,), (1,)), ((), ())),
                           preferred_element_type=F32)


def _rms(x, g):
    return x * lax.rsqrt(jnp.mean(x * x, axis=-1, keepdims=True) + EPS) * g


def _sigmoid(x):
    return 1.0 / (1.0 + jnp.exp(-x))


def _log_sigmoid(x):
    return jnp.minimum(x, 0.0) - jnp.log(1.0 + jnp.exp(-jnp.abs(x)))


def _const_spec(shape):
    nd = len(shape)
    return pl.BlockSpec(shape, lambda *_: (0,) * nd)


def _ffn_kernel(*refs, with_mix, with_final):
    it = iter(refs)
    x_ref = next(it)
    if with_mix:
        a_ref, b_ref, wa_ref, wb_ref = next(it), next(it), next(it), next(it)
    g_ref, wg_ref, wu_ref, wd_ref = next(it), next(it), next(it), next(it)
    gf_ref = next(it) if with_final else None
    o_ref = next(it)
    acc_ref = next(it)

    x = x_ref[...]
    if with_mix:
        x = x + _bdot(a_ref[...], wa_ref[...]) + _bdot(b_ref[...], wb_ref[...])
    h = _rms(x, g_ref[...]).astype(BF16)
    acc_ref[...] = jnp.zeros_like(acc_ref)
    for c in range(D_FF // FF_CHUNK):
        sl = slice(c * FF_CHUNK, (c + 1) * FF_CHUNK)
        g = jnp.dot(h, wg_ref[:, sl], preferred_element_type=F32)
        u = jnp.dot(h, wu_ref[:, sl], preferred_element_type=F32)
        a = (g * _sigmoid(g) * u).astype(BF16)
        acc_ref[...] += jnp.dot(a, wd_ref[sl, :], preferred_element_type=F32)
    y = x + 0.5 * acc_ref[...]
    if with_final:
        y = _rms(y, gf_ref[...])
    o_ref[...] = y


def _ffn(x, gain, wg, wu, wd, which, *, mix=None, final_gain=None, tm):
    n = x.shape[0]
    row = lambda i: (i, 0)
    pick = lambda i: which + (0, 0)
    args, specs = [x], [pl.BlockSpec((tm, D_MODEL), row)]
    if mix is not None:
        a, b, wa, wb = mix
        args += [a, b, wa, wb]
        specs += [pl.BlockSpec((tm, a.shape[-1]), row), pl.BlockSpec((tm, b.shape[-1]), row),
                  _const_spec(wa.shape), _const_spec(wb.shape)]
    args += [gain.reshape(1, D_MODEL), wg, wu, wd]
    specs += [_const_spec((1, D_MODEL))] + [pl.BlockSpec((None, None) + w.shape[2:], pick)
                                            for w in (wg, wu, wd)]
    if final_gain is not None:
        args.append(final_gain.reshape(1, D_MODEL))
        specs.append(_const_spec((1, D_MODEL)))
    return pl.pallas_call(
        functools.partial(_ffn_kernel, with_mix=mix is not None, with_final=final_gain is not None),
        grid=(n // tm,),
        in_specs=specs,
        out_specs=pl.BlockSpec((tm, D_MODEL), row),
        out_shape=jax.ShapeDtypeStruct((n, D_MODEL), F32),
        scratch_shapes=[pltpu.VMEM((tm, D_MODEL), F32)],
        compiler_params=_cparams("parallel"),
        name="ffn",
    )(*args)


def _even_in_kernel(x_ref, gain_ref, wm_ref, wg_ref, bg_ref, wcq_ref, wckv_ref, wkra_ref, wkrb_ref,
                    qn_ref, kvn_ref, wq_ref, ws_ref, wuk_ref, wuv_ref, cos_ref, sin_ref,
                    qm_o, km_o, vm_o, om_o, gates_o, q_o, kn_o, v_o, kr128_o, lat_o, kr_o,
                    *, v_transposed):
    h = _rms(x_ref[...], gain_ref[...]).astype(BF16)
    m = jnp.dot(h, wm_ref[...], preferred_element_type=F32)
    qm_o[...] = m[:, :M_W].astype(BF16)
    km_o[...] = (m[:, M_W:2 * M_W] * M_DH ** -0.5).astype(BF16)
    vm_o[...] = m[:, 2 * M_W:3 * M_W].astype(BF16)
    om_o[...] = m[:, 3 * M_W:]

    g = jnp.dot(h, wg_ref[...], preferred_element_type=F32) + bg_ref[...]
    lane = lax.broadcasted_iota(jnp.int32, g.shape, 1)
    gates_o[...] = jnp.where(lane < M_HEADS, g, _log_sigmoid(g))

    cos, sin = cos_ref[...], sin_ref[...]
    cq = jnp.dot(h, wcq_ref[...], preferred_element_type=F32)
    cqn = _rms(cq, qn_ref[...]).astype(BF16)
    qf = jnp.dot(cqn, wq_ref[...], preferred_element_type=F32)
    qs = jnp.dot(cqn, ws_ref[...], preferred_element_type=F32)
    q_scale = MLA_SCALE * LOG2E
    for hh in range(A_HEADS):
        base = hh * 2 * LANES
        q_o[:, base:base + LANES] = (qf[:, base:base + LANES] * q_scale).astype(BF16)
        rot = qf[:, base + LANES:base + 2 * LANES] * cos + qs[:, hh * LANES:(hh + 1) * LANES] * sin
        q_o[:, base + LANES:base + 2 * LANES] = (rot * q_scale).astype(BF16)

    ckv = jnp.dot(h, wckv_ref[...], preferred_element_type=F32)
    lat = _rms(ckv, kvn_ref[...])
    lat_o[...] = lat
    latb = lat.astype(BF16)
    kn_o[...] = jnp.dot(latb, wuk_ref[...], preferred_element_type=F32).astype(BF16)
    if v_transposed:
        v_o[0] = _bdot_nt(wuv_ref[...], latb).astype(BF16)
    else:
        v_o[...] = jnp.dot(latb, wuv_ref[...], preferred_element_type=F32).astype(BF16)

    kr = (jnp.dot(h, wkra_ref[...], preferred_element_type=F32) * cos
          + jnp.dot(h, wkrb_ref[...], preferred_element_type=F32) * sin)
    kr128_o[...] = kr.astype(BF16)
    kr_o[...] = kr[:, :A_ROPE]


def _even_in(x, gain, w, cos, sin, *, tm, seq_len):
    n = x.shape[0]
    row = lambda i: (i, 0)
    n_pos_tiles = cos.shape[0] // tm
    pos = lambda i: (i % n_pos_tiles, 0)
    v_transposed = seq_len % tm == 0
    n_time_tiles = max(seq_len // tm, 1)
    weights = [w["wm"], w["wg"], w["bg"], w["wcq"], w["wckv"], w["wkra"], w["wkrb"], w["qn"], w["kvn"],
               w["wq"], w["ws"], w["wuk"], w["wuv"].T if v_transposed else w["wuv"]]
    in_specs = ([pl.BlockSpec((tm, D_MODEL), row), _const_spec((1, D_MODEL))]
                + [_const_spec(a.shape) for a in weights]
                + [pl.BlockSpec((tm, LANES), pos), pl.BlockSpec((tm, LANES), pos)])
    outs = [(M_W, BF16), (M_W, BF16), (M_W, BF16), (M_W, F32), (LANES, F32),
            (A_HEADS * 2 * LANES, BF16), (A_HEADS * A_NOPE, BF16), (A_HEADS * A_VDIM, BF16),
            (LANES, BF16), (KV_LORA, F32), (A_ROPE, F32)]
    out_specs = [pl.BlockSpec((tm, c), row) for c, _ in outs]
    out_shape = [jax.ShapeDtypeStruct((n, c), dt) for c, dt in outs]
    if v_transposed:
        v_idx = 7
        out_specs[v_idx] = pl.BlockSpec((1, A_HEADS * A_VDIM, tm),
                                        lambda i: (i // n_time_tiles, 0, i % n_time_tiles))
        out_shape[v_idx] = jax.ShapeDtypeStruct((n // seq_len, A_HEADS * A_VDIM, seq_len), BF16)
    return pl.pallas_call(
        functools.partial(_even_in_kernel, v_transposed=v_transposed),
        grid=(n // tm,),
        in_specs=in_specs,
        out_specs=out_specs,
        out_shape=out_shape,
        compiler_params=_cparams("parallel"),
        name="even_in",
    )(x, gain.reshape(1, D_MODEL), *weights, cos, sin)


def _kv_expand_kernel(lat_ref, wuk_ref, wuv_ref, kn_o, v_o):
    latb = lat_ref[...].astype(BF16)
    kn_o[...] = jnp.dot(latb, wuk_ref[...], preferred_element_type=F32).astype(BF16)
    v_o[...] = jnp.dot(latb, wuv_ref[...], preferred_element_type=F32).astype(BF16)


def _kv_expand(lat, wuk, wuv):
    n = lat.shape[0]
    tm = min(4 * TOKEN_TILE, n)
    assert n % tm == 0
    row = lambda i: (i, 0)
    return pl.pallas_call(
        _kv_expand_kernel,
        grid=(n // tm,),
        in_specs=[pl.BlockSpec((tm, KV_LORA), row), _const_spec(wuk.shape), _const_spec(wuv.shape)],
        out_specs=[pl.BlockSpec((tm, wuk.shape[1]), row), pl.BlockSpec((tm, wuv.shape[1]), row)],
        out_shape=[jax.ShapeDtypeStruct((n, wuk.shape[1]), BF16),
                   jax.ShapeDtypeStruct((n, wuv.shape[1]), BF16)],
        compiler_params=_cparams("parallel"),
        name="kv_expand",
    )(lat, wuk, wuv)


def _mlstm_kernel(qm_ref, km_ref, vm_ref, om_ref, g_ref, onorm_ref, c0_ref, n0_ref, m0_ref,
                  h_ref, cout_ref, nout_ref, mout_ref, c_sc, n_sc, m_sc, *, n_chunks, n_streams):
    lt = pl.program_id(1)

    @pl.when(lt == 0)
    def _():
        c_sc[...] = c0_ref[...]
        n_sc[...] = n0_ref[...]
        m_sc[...] = m0_ref[...]

    row = lax.broadcasted_iota(jnp.int32, (CHUNK, CHUNK), 0)
    col = lax.broadcasted_iota(jnp.int32, (CHUNK, CHUNK), 1)
    causal = col <= row
    ltri = causal.astype(BF16)
    utri = (row <= col).astype(BF16)
    ones_kl = jnp.ones((CHUNK, LANES), BF16)

    def split3(a):
        hi = a.astype(BF16)
        r1 = a - hi.astype(F32)
        mid = r1.astype(BF16)
        return hi, mid, (r1 - mid.astype(F32)).astype(BF16)

    def chunk(c, carry):
        r0 = pl.multiple_of(c * CHUNK, CHUNK)
        rows = pl.ds(r0, CHUNK)
        chains = [(bb, hh) for bb in range(n_streams) for hh in range(M_HEADS)]
        hsl = lambda hh: slice(hh * M_DH, (hh + 1) * M_DH)
        gate = {}
        for bb in range(n_streams):
            gates = g_ref[bb, rows, :]
            gates_t = gates.T
            cc = jnp.dot(ltri, jnp.concatenate(split3(gates), axis=1), preferred_element_type=F32)
            cum_col = cc[:, :LANES] + cc[:, LANES:2 * LANES] + cc[:, 2 * LANES:]
            cr = jnp.dot(jnp.concatenate(split3(gates_t), axis=0), utri, preferred_element_type=F32)
            cum_row = cr[:LANES] + cr[LANES:2 * LANES] + cr[2 * LANES:]
            gate[bb] = (gates, gates_t, cum_col, cum_row)
        qk, qc, qn = {}, {}, {}
        for ch in chains:
            bb, hh = ch
            rhs = jnp.concatenate([c_sc[bb, hh].astype(BF16),
                                   jnp.broadcast_to(n_sc[bb, hh], (LANES, M_DH)).astype(BF16),
                                   km_ref[bb, rows, hsl(hh)]], axis=0)
            r = _bdot_nt(qm_ref[bb, rows, hsl(hh)], rhs)
            qc[ch], qn[ch], qk[ch] = r[:, :M_DH], r[:, M_DH:M_DH + LANES], r[:, M_DH + LANES:]
        st = {}
        for ch in chains:
            bb, hh = ch
            gates, gates_t, cum_col, cum_row = gate[bb]
            ig_row = gates_t[hh:hh + 1, :]
            ig_col = gates[:, hh:hh + 1]
            b_row = cum_row[M_HEADS + hh:M_HEADS + hh + 1, :]
            b_col = cum_col[:, M_HEADS + hh:M_HEADS + hh + 1]
            m_prev = m_sc[bb, hh][:, :1]
            b_lanes = jnp.broadcast_to(b_col, (CHUNK, LANES))
            log_d = jnp.where(causal, b_lanes[:, :CHUNK] - b_row + ig_row, -jnp.inf)
            m_t = jnp.maximum(b_col + m_prev, jnp.max(log_d, axis=-1, keepdims=True))
            m_lanes = jnp.broadcast_to(m_t, (CHUNK, LANES))
            s = (qk[ch] * jnp.exp(log_d - m_lanes[:, :CHUNK])).astype(BF16)
            w_inter = jnp.exp(b_lanes + m_prev - m_lanes)
            m_new = m_t[CHUNK - 1:CHUNK, :]
            b_last = b_col[CHUNK - 1:CHUNK, :]
            w_s = jnp.exp(b_last - b_col + ig_col - m_new)
            decay = jnp.exp(b_last + m_prev - m_new)
            r = jnp.dot(s, jnp.concatenate([vm_ref[bb, rows, hsl(hh)], ones_kl], axis=1),
                        preferred_element_type=F32)
            st[ch] = (r[:, :M_DH], r[:, M_DH:], w_inter, m_lanes, m_new, w_s, decay)
        for ch in chains:
            bb, hh = ch
            sv, s_sum, w_inter, m_lanes, m_new, w_s, decay = st[ch]
            k = km_ref[bb, rows, hsl(hh)]
            v = vm_ref[bb, rows, hsl(hh)]
            c_h = c_sc[bb, hh]
            n_h = n_sc[bb, hh]
            num = w_inter * qc[ch] + sv
            den = w_inter * qn[ch] + s_sum
            hv = num / jnp.maximum(jnp.abs(den), jnp.exp(-m_lanes))
            wv_t = (jnp.broadcast_to(w_s, (CHUNK, M_DH)) * v).T
            c_sc[bb, hh] = decay * c_h + _bdot(wv_t, k)
            n_sc[bb, hh] = decay * n_h + jnp.sum(w_s * k.astype(F32), axis=0, keepdims=True)
            m_sc[bb, hh] = jnp.broadcast_to(m_new, (1, LANES))
            hn = _rms(hv, onorm_ref[:, hsl(hh)])
            h_ref[bb, rows, hsl(hh)] = (_sigmoid(om_ref[bb, rows, hsl(hh)]) * hn).astype(BF16)
        return carry

    lax.fori_loop(0, n_chunks, chunk, 0)

    @pl.when(lt == pl.num_programs(1) - 1)
    def _():
        cout_ref[...] = c_sc[...]
        nout_ref[...] = n_sc[...]
        mout_ref[...] = m_sc[...]


def _mlstm(qm, km, vm, om, gates, onorm, c0, n0, m0, *, tl):
    b, l, _ = qm.shape
    ns = MLSTM_STREAMS
    tile = lambda i, j: (i, j, 0)
    st4 = lambda i, j: (i, 0, 0, 0)
    state = [((ns, M_HEADS, M_DH, M_DH), (b, M_HEADS, M_DH, M_DH)),
             ((ns, M_HEADS, 1, M_DH), (b, M_HEADS, 1, M_DH)),
             ((ns, M_HEADS, 1, LANES), (b, M_HEADS, 1, LANES))]
    return pl.pallas_call(
        functools.partial(_mlstm_kernel, n_chunks=tl // CHUNK, n_streams=ns),
        grid=(b // ns, l // tl),
        in_specs=[pl.BlockSpec((ns, tl, M_W), tile)] * 4
        + [pl.BlockSpec((ns, tl, LANES), tile), _const_spec((1, M_W))]
        + [pl.BlockSpec(blk, st4) for blk, _ in state],
        out_specs=[pl.BlockSpec((ns, tl, M_W), tile)] + [pl.BlockSpec(blk, st4) for blk, _ in state],
        out_shape=[jax.ShapeDtypeStruct((b, l, M_W), BF16)]
        + [jax.ShapeDtypeStruct(full, F32) for _, full in state],
        scratch_shapes=[pltpu.VMEM(blk, F32) for blk, _ in state],
        compiler_params=_cparams("parallel", "arbitrary"),
        name="mlstm",
    )(qm, km, vm, om, gates, onorm.reshape(1, M_W), c0, n0, m0)


MLA_SUM_ROWS = 16


MLA_EXP_RANGE = 60.0


def _mla_kernel(qi_ref, kj_ref, flag_ref, q_ref, kn_ref, kr_ref, vt_ref, o_ref, m_sc, acc_sc, bad_sc,
                redo_sm, *, tq, tkb, offset, variants):
    p = pl.program_id(1)
    i = qi_ref[p]
    j = kj_ref[p]
    flags = flag_ref[p]

    @pl.when(j == 0)
    def _():
        m_sc[...] = jnp.zeros_like(m_sc)
        acc_sc[...] = jnp.zeros_like(acc_sc)

    def score_tiles(nk, masked):
        kr = kr_ref[0, :nk, :]
        scores = []
        for hh in range(A_HEADS):
            q = q_ref[0, :, hh * 2 * LANES:(hh + 1) * 2 * LANES]
            k = jnp.concatenate([kn_ref[0, :nk, hh * A_NOPE:(hh + 1) * A_NOPE], kr], axis=-1)
            scores.append(_bdot_nt(k, q))
        if masked:
            k_chunk = (j * tkb + lax.broadcasted_iota(jnp.int32, (nk, tq), 0)) // CHUNK
            q_chunk = (offset + i * tq + lax.broadcasted_iota(jnp.int32, (nk, tq), 1)) // CHUNK
            visible = k_chunk <= q_chunk
            scores = [jnp.where(visible, s, -jnp.inf) for s in scores]
        return scores

    def values_t(hh, nk):
        return jnp.concatenate([vt_ref[0, hh * A_VDIM:(hh + 1) * A_VDIM, :nk],
                                jnp.ones((MLA_SUM_ROWS, nk), BF16)], axis=0)

    def add_block(nk, masked):
        scores = score_tiles(nk, masked)
        any_bad = jnp.zeros((1, tq), F32)
        for hh in range(A_HEADS):
            s = scores[hh]
            m_ref = m_sc[hh]
            d = jnp.max(s, axis=0, keepdims=True) - m_ref
            ok = jnp.logical_and(d <= MLA_EXP_RANGE, jnp.logical_or(d >= -MLA_EXP_RANGE, j > 0))
            pr = jnp.exp2(s - m_ref).astype(BF16)
            acc = acc_sc[hh]
            acc_sc[hh] = jnp.where(ok, acc + jnp.dot(values_t(hh, nk), pr, preferred_element_type=F32),
                                   acc)
            bad = jnp.where(ok, 0.0, 1.0)
            bad_sc[hh] = bad
            any_bad = jnp.maximum(any_bad, bad)
        redo_sm[0] = (jnp.max(any_bad) > 0.0).astype(jnp.int32)

    def redo_block(nk, masked):
        scores = score_tiles(nk, masked)
        for hh in range(A_HEADS):
            s = scores[hh]
            bad = bad_sc[hh] > 0.0
            m_prev = jnp.where(j == 0, -jnp.inf, m_sc[hh])
            m_new = jnp.maximum(m_prev, jnp.max(s, axis=0, keepdims=True))
            alpha = jnp.exp2(m_prev - m_new)
            pr = jnp.exp2(s - m_new).astype(BF16)
            acc = acc_sc[hh]
            acc_sc[hh] = jnp.where(
                bad, alpha * acc + jnp.dot(values_t(hh, nk), pr, preferred_element_type=F32), acc)
            m_sc[hh] = jnp.where(bad, m_new, m_sc[hh])

    for vi, (nk, masked) in enumerate(variants):
        pl.when((flags >> 1) == vi)(functools.partial(add_block, nk, masked))
    for vi, (nk, masked) in enumerate(variants):
        pl.when(jnp.logical_and((flags >> 1) == vi, redo_sm[0] != 0))(
            functools.partial(redo_block, nk, masked))

    @pl.when((flags & 1) != 0)
    def _():
        for hh in range(A_HEADS):
            acc = acc_sc[hh]
            o_ref[0, :, hh * A_VDIM:(hh + 1) * A_VDIM] = (
                acc[:A_VDIM] / acc[A_VDIM:A_VDIM + 1]).T.astype(BF16)


def _transpose_kernel(x_ref, o_ref):
    o_ref[0] = x_ref[0].astype(F32).T.astype(o_ref.dtype)


def _transpose_tiles(x, t):
    b, l, c = x.shape
    return pl.pallas_call(
        _transpose_kernel,
        grid=(b, l // t),
        in_specs=[pl.BlockSpec((1, t, c), lambda bb, i: (bb, i, 0))],
        out_specs=pl.BlockSpec((1, c, t), lambda bb, i: (bb, 0, i)),
        out_shape=jax.ShapeDtypeStruct((b, c, l), x.dtype),
        compiler_params=_cparams("parallel", "parallel"),
        name="transpose_tiles",
    )(x)


def _mla_pairs(lq, lk, tq, tkb, sub, offset):
    qi, kj, flags, variants = [], [], [], []
    for i in range(lq // tq):
        first_row, last_row = offset + i * tq, offset + (i + 1) * tq - 1
        k_end = min((last_row // CHUNK + 1) * CHUNK, lk)
        k_all = (first_row // CHUNK + 1) * CHUNK
        nj = -(-k_end // tkb)
        for j in range(nj):
            nk = -(-(min(k_end, (j + 1) * tkb) - j * tkb) // sub) * sub
            variant = (nk, j * tkb + nk > k_all)
            if variant not in variants:
                variants.append(variant)
            qi.append(i)
            kj.append(j)
            flags.append(int(j == nj - 1) + 2 * variants.index(variant))
    return (np.asarray(qi, np.int32), np.asarray(kj, np.int32), np.asarray(flags, np.int32),
            tuple(variants))


def _mla_attention(q, kn, kr, vt, *, tq, tkb, sub, offset):
    b, lq, _ = q.shape
    lk = kn.shape[1]
    assert lk % tkb == 0 and tkb % sub == 0 and vt.shape[2] == lk
    qi, kj, flags, variants = _mla_pairs(lq, lk, tq, tkb, sub, offset)
    qmap = lambda bb, p, qi_r, kj_r, flag_r: (bb, qi_r[p], 0)
    kmap = lambda bb, p, qi_r, kj_r, flag_r: (bb, kj_r[p], 0)
    grid_spec = pltpu.PrefetchScalarGridSpec(
        num_scalar_prefetch=3,
        grid=(b, len(qi)),
        in_specs=[pl.BlockSpec((1, tq, q.shape[-1]), qmap),
                  pl.BlockSpec((1, tkb, kn.shape[-1]), kmap),
                  pl.BlockSpec((1, tkb, kr.shape[-1]), kmap),
                  pl.BlockSpec((1, vt.shape[1], tkb), lambda bb, p, qi_r, kj_r, flag_r: (bb, 0, kj_r[p]))],
        out_specs=pl.BlockSpec((1, tq, A_HEADS * A_VDIM), qmap),
        scratch_shapes=[pltpu.VMEM((A_HEADS, 1, tq), F32),
                        pltpu.VMEM((A_HEADS, A_VDIM + MLA_SUM_ROWS, tq), F32),
                        pltpu.VMEM((A_HEADS, 1, tq), F32),
                        pltpu.SMEM((1,), jnp.int32)],
    )
    return pl.pallas_call(
        functools.partial(_mla_kernel, tq=tq, tkb=tkb, offset=offset, variants=variants),
        grid_spec=grid_spec,
        out_shape=jax.ShapeDtypeStruct((b, lq, A_HEADS * A_VDIM), BF16),
        compiler_params=_cparams("parallel", "arbitrary"),
        name="mla_attention",
    )(jnp.asarray(qi), jnp.asarray(kj), jnp.asarray(flags), q, kn, kr, vt)


def _odd_in_kernel(x_ref, gain_ref, wu_ref, wq_ref, wk_ref, wv_ref, u_o, q_o, k_o, v_o, kb_o, vb_o):
    h = _rms(x_ref[...], gain_ref[...]).astype(BF16)
    u_o[...] = jnp.dot(h, wu_ref[...], preferred_element_type=F32)
    q_o[...] = (jnp.dot(h, wq_ref[...], preferred_element_type=F32) * (SB_DH ** -0.5 * LOG2E)).astype(BF16)
    k = jnp.dot(h, wk_ref[...], preferred_element_type=F32)
    v = jnp.dot(h, wv_ref[...], preferred_element_type=F32)
    k_o[...] = pltpu.einshape("t(hd)->thd", k, h=SB_HEADS)
    v_o[...] = pltpu.einshape("t(hd)->thd", v, h=SB_HEADS)
    kb_o[...] = k.astype(BF16)
    vb_o[...] = v.astype(BF16)


def _odd_in(x, gain, w, *, tm):
    n = x.shape[0]
    row = lambda i: (i, 0)
    weights = [w["wu"], w["wq"], w["wk"], w["wv"]]
    return pl.pallas_call(
        _odd_in_kernel,
        grid=(n // tm,),
        in_specs=[pl.BlockSpec((tm, D_MODEL), row), _const_spec((1, D_MODEL))]
        + [_const_spec(a.shape) for a in weights],
        out_specs=[pl.BlockSpec((tm, S5_W), row), pl.BlockSpec((tm, SB_W), row)]
        + [pl.BlockSpec((tm, SB_HEADS, SB_DH), lambda i: (i, 0, 0))] * 2
        + [pl.BlockSpec((tm, SB_W), row)] * 2,
        out_shape=[jax.ShapeDtypeStruct((n, S5_W), F32),
                   jax.ShapeDtypeStruct((n, SB_W), BF16),
                   jax.ShapeDtypeStruct((n, SB_HEADS, SB_DH), F32),
                   jax.ShapeDtypeStruct((n, SB_HEADS, SB_DH), F32),
                   jax.ShapeDtypeStruct((n, SB_W), BF16),
                   jax.ShapeDtypeStruct((n, SB_W), BF16)],
        compiler_params=_cparams("parallel"),
        name="odd_in",
    )(x, gain.reshape(1, D_MODEL), *weights)


S5_BLOCKS = 4
S5_BLK_IN = S5_W // S5_BLOCKS
S5_BLK_ST = S5_STATE // S5_BLOCKS


def _s5_kernel(u_ref, bbr_ref, bbi_ref, ar_ref, ai_ref, ccr_ref, cci_ref, d_ref, wglu_ref,
               x0r_ref, x0i_ref, hs_o, xr_o, xi_o, u_tm, hs_tm, xr_buf, xi_buf, xr_st, xi_st,
               *, tt, nb):
    t = pl.program_id(0)

    @pl.when(t == 0)
    def _():
        xr_st[...] = x0r_ref[...]
        xi_st[...] = x0i_ref[...]

    for bb in range(nb):
        for k in range(S5_BLOCKS):
            u_tm[k, pl.ds(bb, tt, stride=nb), :] = u_ref[bb, :, k * S5_BLK_IN:(k + 1) * S5_BLK_IN]

    for k in range(S5_BLOCKS):
        cs = slice(k * S5_BLK_ST, (k + 1) * S5_BLK_ST)
        ub = u_tm[k].astype(BF16)
        xr_buf[:, cs] = jnp.dot(ub, bbr_ref[k], preferred_element_type=F32)
        xi_buf[:, cs] = jnp.dot(ub, bbi_ref[k], preferred_element_type=F32)

    for k in range(0, S5_BLOCKS, 2):
        cols = [slice((k + d) * S5_BLK_ST, (k + d + 1) * S5_BLK_ST) for d in range(2)]
        decay = [(jnp.broadcast_to(ar_ref[:, cs], (nb, S5_BLK_ST)),
                  jnp.broadcast_to(ai_ref[:, cs], (nb, S5_BLK_ST))) for cs in cols]

        def step(s, carry, cols=cols, decay=decay):
            rows = pl.ds(pl.multiple_of(s * nb, nb), nb)
            out = []
            for (xr, xi), (ar, ai), cs in zip(carry, decay, cols):
                nxr = ar * xr - ai * xi + xr_buf[rows, cs]
                nxi = ar * xi + ai * xr + xi_buf[rows, cs]
                xr_buf[rows, cs] = nxr
                xi_buf[rows, cs] = nxi
                out.append((nxr, nxi))
            return tuple(out)

        final = lax.fori_loop(0, tt, step, tuple((xr_st[:, cs], xi_st[:, cs]) for cs in cols), unroll=8)
        for (xr, xi), cs in zip(final, cols):
            xr_st[:, cs] = xr
            xi_st[:, cs] = xi

    gs = []
    for k in range(S5_BLOCKS):
        cs = slice(k * S5_BLK_ST, (k + 1) * S5_BLK_ST)
        y = (_bdot(xr_buf[:, cs], ccr_ref[k]) + _bdot(xi_buf[:, cs], cci_ref[k])
             + d_ref[:, k * S5_BLK_IN:(k + 1) * S5_BLK_IN] * u_tm[k])
        gs.append(jax.nn.gelu(y, approximate=True))
    g = jnp.concatenate(gs, axis=-1)
    hs = g * _sigmoid(_bdot(g, wglu_ref[...]))
    for k in range(S5_BLOCKS):
        hs_tm[k] = hs[:, k * S5_BLK_IN:(k + 1) * S5_BLK_IN]
    for bb in range(nb):
        for k in range(S5_BLOCKS):
            hs_o[bb, :, k * S5_BLK_IN:(k + 1) * S5_BLK_IN] = (
                hs_tm[k, pl.ds(bb, tt, stride=nb), :].astype(BF16))

    @pl.when(t == pl.num_programs(0) - 1)
    def _():
        xr_o[...] = xr_st[...]
        xi_o[...] = xi_st[...]


def _s5(u, w, x0r, x0i, *, tt):
    nb, l, _ = u.shape
    blk = lambda t: (0, t, 0)
    consts = [w["bbr"], w["bbi"], w["ar"], w["ai"], w["ccr"], w["cci"], w["d"], w["wglu"], x0r, x0i]
    return pl.pallas_call(
        functools.partial(_s5_kernel, tt=tt, nb=nb),
        grid=(l // tt,),
        in_specs=[pl.BlockSpec((nb, tt, S5_W), blk)] + [_const_spec(a.shape) for a in consts],
        out_specs=[pl.BlockSpec((nb, tt, S5_W), blk), _const_spec((nb, S5_STATE)),
                   _const_spec((nb, S5_STATE))],
        out_shape=[jax.ShapeDtypeStruct((nb, l, S5_W), BF16),
                   jax.ShapeDtypeStruct((nb, S5_STATE), F32),
                   jax.ShapeDtypeStruct((nb, S5_STATE), F32)],
        scratch_shapes=[pltpu.VMEM((S5_BLOCKS, tt * nb, S5_BLK_IN), F32),
                        pltpu.VMEM((S5_BLOCKS, tt * nb, S5_BLK_IN), F32),
                        pltpu.VMEM((tt * nb, S5_STATE), F32), pltpu.VMEM((tt * nb, S5_STATE), F32),
                        pltpu.VMEM((nb, S5_STATE), F32), pltpu.VMEM((nb, S5_STATE), F32)],
        compiler_params=_cparams("arbitrary"),
        name="s5",
    )(u, *consts)


def _sb_kernel(q_ref, k_ref, v_ref, o_ref, r_sc, acc_sc, qh_sc, *, tq, tk, offset, n_sub):
    for su in range(n_sub):
        _sb_query_tile(pl.program_id(1) * n_sub + su, slice(su * tq, (su + 1) * tq),
                       q_ref, k_ref, v_ref, o_ref, r_sc, acc_sc, qh_sc, tq=tq, tk=tk, offset=offset)


def _sb_query_tile(i, rs, q_ref, k_ref, v_ref, o_ref, r_sc, acc_sc, qh_sc, *, tq, tk, offset):
    n_pairs = SB_HEADS // 2
    half = lax.broadcasted_iota(jnp.int32, (1, LANES), 1) // SB_DH
    for pr in range(n_pairs):
        qq = q_ref[0, rs, pr * LANES:(pr + 1) * LANES]
        qh_sc[pr, :tq] = jnp.where(half == 0, qq, jnp.zeros_like(qq))
        qh_sc[pr, tq:] = jnp.where(half == 1, qq, jnp.zeros_like(qq))
    r_sc[...] = jnp.zeros_like(r_sc)
    acc_sc[...] = jnp.zeros_like(acc_sc)

    jrow = lax.broadcasted_iota(jnp.int32, (tk, tk), 0)
    jcol = lax.broadcasted_iota(jnp.int32, (tk, tk), 1)
    later = (jrow > jcol).astype(BF16)
    j_diag = (offset + i * tq) // tk

    def sweep(tiles):
        zs, vv, causal = {}, {}, {}
        for t, (j, masked, live) in enumerate(tiles):
            k0 = pl.multiple_of(j * tk, tk)
            if masked:
                q_pos = offset + i * tq + lax.broadcasted_iota(jnp.int32, (tq, tk), 0)
                causal[t] = (k0 + lax.broadcasted_iota(jnp.int32, (tq, tk), 1)) < q_pos
            for pr in range(n_pairs):
                ps = slice(pr * LANES, (pr + 1) * LANES)
                zs[t, pr] = _bdot_nt(qh_sc[pr], k_ref[0, pl.ds(k0, tk), ps])
                vv[t, pr] = v_ref[0, pl.ds(k0, tk), ps]
        l1s, lbs, cums = {}, {}, {}
        for t, (j, masked, live) in enumerate(tiles):
            for pr in range(n_pairs):
                parts = []
                for sub in range(2):
                    hh = 2 * pr + sub
                    z = zs[t, pr][sub * tq:(sub + 1) * tq]
                    sp = jnp.maximum(z, 0.0) + jnp.log2(1.0 + jnp.exp2(-jnp.abs(z)))
                    l1 = -sp
                    if masked:
                        l1 = jnp.where(causal[t], l1, 0.0)
                    hi = l1.astype(BF16)
                    parts += [hi, (l1 - hi.astype(F32)).astype(BF16)]
                    l1s[t, hh] = l1
                    lbs[t, hh] = z - sp
                c = jnp.dot(jnp.concatenate(parts, axis=0), later, preferred_element_type=F32)
                cums[t, 2 * pr] = c[:tq] + c[tq:2 * tq]
                cums[t, 2 * pr + 1] = c[2 * tq:3 * tq] + c[3 * tq:]
        for t, (j, masked, live) in enumerate(tiles):
            r_max = jnp.full((1, 1), -jnp.inf, F32)
            dead_shift = 0.0 if live is None else jnp.where(live, 0.0, -1e30)
            keep = 1.0 if live is None else jnp.where(live, 1.0, 0.0)
            for pr in range(n_pairs):
                ws = []
                for sub in range(2):
                    hh = 2 * pr + sub
                    r_old = r_sc[hh]
                    log_w = lbs[t, hh] + cums[t, hh] + (r_old + dead_shift)
                    if masked:
                        log_w = jnp.where(causal[t], log_w, -jnp.inf)
                    ws.append(jnp.exp2(log_w).astype(BF16))
                    r_new = r_old + keep * jnp.sum(l1s[t, hh], axis=-1, keepdims=True)
                    r_sc[hh] = r_new
                    r_max = jnp.maximum(r_max, jnp.max(r_new, axis=0, keepdims=True))
                v2 = jnp.concatenate([jnp.where(half == 0, vv[t, pr], jnp.zeros_like(vv[t, pr])),
                                      jnp.where(half == 1, vv[t, pr], jnp.zeros_like(vv[t, pr]))],
                                     axis=0)
                acc_sc[:, pr * LANES:(pr + 1) * LANES] += jnp.dot(
                    jnp.concatenate(ws, axis=1), v2, preferred_element_type=F32)
        return r_max[0, 0]

    has_prev = j_diag >= 1
    r_first = sweep([(j_diag, True, None), (jnp.maximum(j_diag - 1, 0), False, has_prev)])

    def cond(carry):
        j, r_max = carry
        return jnp.logical_and(j >= 0, r_max > SB_LOG_FLOOR * LOG2E)

    def body(carry):
        j, _ = carry
        return j - 1, sweep([(j, False, None)])

    lax.while_loop(cond, body, (j_diag - 2, r_first))
    o_ref[0, rs] = acc_sc[...].astype(BF16)


def _sb_attention(q, k, v, *, tq, tk, offset):
    b, lq, _ = q.shape
    lk = k.shape[1]
    assert tk % tq == 0 and offset % tk == 0 and lk % tk == 0
    n_sub = math.gcd(lq // tq, SB_TILES_PER_STEP)
    return pl.pallas_call(
        functools.partial(_sb_kernel, tq=tq, tk=tk, offset=offset, n_sub=n_sub),
        grid=(b, lq // (tq * n_sub)),
        in_specs=[pl.BlockSpec((1, tq * n_sub, SB_W), lambda bb, i: (bb, i, 0)),
                  pl.BlockSpec((1, lk, SB_W), lambda bb, i: (bb, 0, 0)),
                  pl.BlockSpec((1, lk, SB_W), lambda bb, i: (bb, 0, 0))],
        out_specs=pl.BlockSpec((1, tq * n_sub, SB_W), lambda bb, i: (bb, i, 0)),
        out_shape=jax.ShapeDtypeStruct((b, lq, SB_W), BF16),
        scratch_shapes=[pltpu.VMEM((SB_HEADS, tq, 1), F32),
                        pltpu.VMEM((tq, SB_W), F32),
                        pltpu.VMEM((SB_HEADS // 2, 2 * tq, LANES), BF16)],
        compiler_params=_cparams("parallel", "arbitrary"),
        name="sb_attention",
    )(q, k, v)


def _pad_cols(w, width):
    return jnp.pad(w, ((0, 0), (0, width - w.shape[1])))


def _even_weights(w_in, b_ig, b_fg, q_norm, kv_norm, w_uq, w_ukv):
    o = 4 * M_W
    w_gate = w_in[:, o:o + 2 * M_HEADS]
    o += 2 * M_HEADS
    w_cq = w_in[:, o:o + Q_LORA]
    o += Q_LORA
    w_ckv = w_in[:, o:o + KV_LORA]
    o += KV_LORA
    w_kr = w_in[:, o:o + A_ROPE]
    hr = A_ROPE // 2
    wkra = _pad_cols(w_kr, LANES)
    wkrb = _pad_cols(jnp.concatenate([w_kr[:, hr:], w_kr[:, :hr]], axis=1), LANES)
    uq = w_uq.reshape(Q_LORA, A_HEADS, A_NOPE + A_ROPE)
    zeros = jnp.zeros((Q_LORA, A_HEADS, LANES - A_ROPE), w_uq.dtype)
    wq = jnp.concatenate([uq, zeros], axis=-1).reshape(Q_LORA, A_HEADS * 2 * LANES)
    ws = jnp.concatenate([uq[..., A_NOPE + hr:], uq[..., A_NOPE:A_NOPE + hr], zeros],
                         axis=-1).reshape(Q_LORA, A_HEADS * LANES)
    ukv = w_ukv.reshape(KV_LORA, A_HEADS, A_NOPE + A_VDIM)
    return dict(
        wm=w_in[:, :4 * M_W].astype(BF16),
        wg=_pad_cols(w_gate, LANES).astype(BF16),
        bg=_pad_cols(jnp.concatenate([b_ig, b_fg]).reshape(1, 2 * M_HEADS).astype(F32), LANES),
        wcq=w_cq.astype(BF16), wckv=w_ckv.astype(BF16),
        wkra=wkra.astype(BF16), wkrb=wkrb.astype(BF16),
        qn=q_norm.reshape(1, Q_LORA).astype(F32), kvn=kv_norm.reshape(1, KV_LORA).astype(F32),
        wq=wq.astype(BF16), ws=ws.astype(BF16),
        wuk=ukv[..., :A_NOPE].reshape(KV_LORA, A_HEADS * A_NOPE).astype(BF16),
        wuv=ukv[..., A_NOPE:].reshape(KV_LORA, A_HEADS * A_VDIM).astype(BF16),
    )


def _rope_tables(pos):
    half = A_ROPE // 2
    inv_freq = ROPE_THETA ** (-jnp.arange(half, dtype=F32) / half)
    ang = pos.astype(F32)[:, None] * inv_freq[None, :]
    cos, sin = jnp.cos(ang), jnp.sin(ang)
    pad = jnp.zeros((pos.shape[0], LANES - A_ROPE), F32)
    return (jnp.concatenate([cos, cos, pad], axis=1), jnp.concatenate([-sin, sin, pad], axis=1))


def _s5_weights(a_re, a_im, b_re, b_im, c_re, c_im, d, log_dt, w_glu):
    dt = jnp.exp(log_dt.astype(F32))[:, None]
    ar, ai = a_re.astype(F32), a_im.astype(F32)
    mag = jnp.exp(dt * ar)
    ab_r, ab_i = mag * jnp.cos(dt * ai), mag * jnp.sin(dt * ai)
    den = ar * ar + ai * ai
    c_r = ((ab_r - 1.0) * ar + ab_i * ai) / den
    c_i = (ab_i * ar - (ab_r - 1.0) * ai) / den
    br, bi = b_re.astype(F32), b_im.astype(F32)
    bb_r = c_r[..., None] * br - c_i[..., None] * bi
    bb_i = c_r[..., None] * bi + c_i[..., None] * br
    gb = S5_GROUPS // S5_BLOCKS
    eye = jnp.eye(gb, dtype=F32)
    bd_in = lambda m: jnp.einsum("kgpc,gh->kgchp", m.reshape(S5_BLOCKS, gb, S5_P, S5_GSIZE),
                                 eye).reshape(S5_BLOCKS, S5_BLK_IN, S5_BLK_ST)
    bd_out = lambda m: jnp.einsum("kgcp,gh->kgphc", m.reshape(S5_BLOCKS, gb, S5_GSIZE, S5_P),
                                  eye).reshape(S5_BLOCKS, S5_BLK_ST, S5_BLK_IN)
    return dict(
        bbr=bd_in(bb_r).astype(BF16), bbi=bd_in(bb_i).astype(BF16),
        ar=ab_r.reshape(1, S5_STATE), ai=ab_i.reshape(1, S5_STATE),
        ccr=bd_out(c_re.astype(F32)).astype(BF16), cci=bd_out(-c_im.astype(F32)).astype(BF16),
        d=d.reshape(1, S5_W).astype(F32), wglu=w_glu.astype(BF16),
    )


def _pad_keys(a, mult):
    pad = (-a.shape[1]) % mult
    return jnp.pad(a, ((0, 0), (0, pad), (0, 0))) if pad else a


def _trunk(x, pos0, cache, p):
    nb, l, _ = x.shape
    assert nb == SUBLANES, "the S5 layout puts the streams of one time step on the sublanes"
    assert l % CHUNK == 0
    n = nb * l
    tm = min(TOKEN_TILE, n)
    tl = min(tm, l)
    assert l % tl == 0 and tm % tl == 0
    past = 0 if cache is None else cache["sb_k"].shape[1]
    assert pos0 == past
    xf = x.reshape(n, D_MODEL)
    bf = lambda a: a.astype(BF16)
    ffn_w = (bf(p["ffn_w_gate"]), bf(p["ffn_w_up"]), bf(p["ffn_w_down"]))

    xf = _ffn(xf, p["norm_ffn"][0, 0], *ffn_w, (0, 0), tm=tm)
    ew = _even_weights(p["even_w_in"][0], p["mlstm_b_igate"][0], p["mlstm_b_fgate"][0],
                       p["mla_q_norm"][0], p["mla_kv_norm"][0], p["mla_w_uq"][0], p["mla_w_ukv"][0])
    cos, sin = _rope_tables(pos0 + jnp.arange(l))
    if tm > l:
        cos, sin = jnp.tile(cos, (tm // l, 1)), jnp.tile(sin, (tm // l, 1))
    qm, km, vm, om, gates, q, kn, v, kr128, lat, kr = _even_in(
        xf, p["norm_mix"][0], ew, cos, sin, tm=tm, seq_len=l)
    v_transposed = v.ndim == 3

    if cache is None:
        c0 = jnp.zeros((nb, M_HEADS, M_DH, M_DH), F32)
        n0 = jnp.zeros((nb, M_HEADS, 1, M_DH), F32)
        m0 = jnp.zeros((nb, M_HEADS, 1, LANES), F32)
    else:
        c0 = cache["mlstm_c"].astype(F32)
        n0 = cache["mlstm_n"].astype(F32).reshape(nb, M_HEADS, 1, M_DH)
        m0 = jnp.broadcast_to(cache["mlstm_m"].astype(F32)[..., None, None], (nb, M_HEADS, 1, LANES))
    r3 = lambda a: a.reshape(nb, l, a.shape[-1])
    h_m, c_new, n_new, m_new = _mlstm(r3(qm), r3(km), r3(vm), r3(om), r3(gates),
                                      p["mlstm_out_norm"][0], c0, n0, m0, tl=tl)

    mla_tq = min(MLA_TILE, -(-l // LANES) * LANES)
    if cache is None:
        mla_sub = mla_tq
        mla_tkb = min(2 * mla_tq, l)
    else:
        mla_sub = mla_tkb = -(-(past + max(l, mla_tq)) // LANES) * LANES
    kn3, kr3 = r3(kn), r3(kr128)
    v3, vt = (None, v) if v_transposed else (r3(v), None)
    if cache is not None:
        ckn, cv = _kv_expand(cache["mla_latent"].astype(F32).reshape(nb * past, KV_LORA),
                             ew["wuk"], ew["wuv"])
        ckr = jnp.pad(cache["mla_krope"], ((0, 0), (0, 0), (0, LANES - A_ROPE))).astype(BF16)
        kn3 = jnp.concatenate([ckn.reshape(nb, past, -1), kn3], axis=1)
        kr3 = jnp.concatenate([ckr, kr3], axis=1)
        cv3 = cv.reshape(nb, past, -1)
        if v_transposed:
            vt = jnp.concatenate([_transpose_tiles(cv3, math.gcd(past, TOKEN_TILE)), vt], axis=2)
        else:
            v3 = jnp.concatenate([cv3, v3], axis=1)
    if vt is None:
        vt = _transpose_tiles(_pad_keys(v3, mla_tkb), mla_sub)
    else:
        vt = jnp.pad(vt, ((0, 0), (0, 0), (0, (-vt.shape[2]) % mla_tkb)))
    h_a = _mla_attention(_pad_keys(r3(q), mla_tq), _pad_keys(kn3, mla_tkb), _pad_keys(kr3, mla_tkb),
                         vt, tq=mla_tq, tkb=mla_tkb, sub=mla_sub, offset=past)[:, :l]

    w_out = bf(p["even_w_out"][0])
    xf = _ffn(xf, p["norm_ffn"][0, 1], *ffn_w, (0, 1),
              mix=(h_m.reshape(n, M_W), h_a.reshape(n, -1), w_out[:M_W], w_out[M_W:]), tm=tm)

    xf = _ffn(xf, p["norm_ffn"][1, 0], *ffn_w, (1, 0), tm=tm)
    w_in = p["odd_w_in"][0]
    ow = dict(wu=bf(w_in[:, :S5_W]), wq=bf(w_in[:, S5_W:S5_W + SB_W]),
              wk=bf(w_in[:, S5_W + SB_W:S5_W + 2 * SB_W]), wv=bf(w_in[:, S5_W + 2 * SB_W:]))
    u, qs, ks, vs, ksb, vsb = _odd_in(xf, p["norm_mix"][1], ow, tm=tm)

    sw = _s5_weights(p["s5_A_re"][0], p["s5_A_im"][0], p["s5_B_re"][0], p["s5_B_im"][0],
                     p["s5_C_re"][0], p["s5_C_im"][0], p["s5_D"][0], p["s5_log_dt"][0], p["s5_w_glu"][0])
    if cache is None:
        x0r = jnp.zeros((nb, S5_STATE), F32)
        x0i = jnp.zeros((nb, S5_STATE), F32)
    else:
        x0r = cache["s5_re"].astype(F32).reshape(nb, S5_STATE)
        x0i = cache["s5_im"].astype(F32).reshape(nb, S5_STATE)
    h_s, xr_new, xi_new = _s5(r3(u), sw, x0r, x0i, tt=min(S5_TT, l))

    k3, v3 = r3(ksb), r3(vsb)
    if cache is not None:
        k3 = jnp.concatenate([bf(cache["sb_k"].reshape(nb, past, SB_W)), k3], axis=1)
        v3 = jnp.concatenate([bf(cache["sb_v"].reshape(nb, past, SB_W)), v3], axis=1)
    h_b = _sb_attention(r3(qs), _pad_keys(k3, SB_TILE), _pad_keys(v3, SB_TILE), tq=min(SB_TILE, l),
                        tk=SB_TILE, offset=past)

    w_out = bf(p["odd_w_out"][0])
    y = _ffn(xf, p["norm_ffn"][1, 1], *ffn_w, (1, 1),
             mix=(h_s.reshape(n, S5_W), h_b.reshape(n, SB_W), w_out[:S5_W], w_out[S5_W:]),
             final_gain=p["norm_final"], tm=tm)

    return (y.reshape(nb, l, D_MODEL),
            lat.reshape(1, nb, l, KV_LORA), kr.reshape(1, nb, l, A_ROPE),
            c_new[None], n_new.reshape(1, nb, M_HEADS, M_DH), m_new[:, :, 0, 0][None],
            xr_new.reshape(1, nb, S5_GROUPS, S5_P), xi_new.reshape(1, nb, S5_GROUPS, S5_P),
            ks.reshape(1, nb, l, SB_HEADS, SB_DH), vs.reshape(1, nb, l, SB_HEADS, SB_DH))


def kernel(x_prompt, x_sample, cache_mla_latent, cache_mla_krope, state_mlstm_C, state_mlstm_n, state_mlstm_m, state_s5_re, state_s5_im, cache_sb_k, cache_sb_v, norm_ffn, norm_mix, norm_final, ffn_w_gate, ffn_w_up, ffn_w_down, even_w_in, even_w_out, mlstm_b_igate, mlstm_b_fgate, mlstm_out_norm, mla_q_norm, mla_kv_norm, mla_w_uq, mla_w_ukv, odd_w_in, odd_w_out, s5_A_re, s5_A_im, s5_B_re, s5_B_im, s5_C_re, s5_C_im, s5_D, s5_log_dt, s5_w_glu):
    p = dict(norm_ffn=norm_ffn, norm_mix=norm_mix, norm_final=norm_final, ffn_w_gate=ffn_w_gate,
             ffn_w_up=ffn_w_up, ffn_w_down=ffn_w_down, even_w_in=even_w_in, even_w_out=even_w_out,
             mlstm_b_igate=mlstm_b_igate, mlstm_b_fgate=mlstm_b_fgate, mlstm_out_norm=mlstm_out_norm,
             mla_q_norm=mla_q_norm, mla_kv_norm=mla_kv_norm, mla_w_uq=mla_w_uq, mla_w_ukv=mla_w_ukv,
             odd_w_in=odd_w_in, odd_w_out=odd_w_out, s5_A_re=s5_A_re, s5_A_im=s5_A_im, s5_B_re=s5_B_re,
             s5_B_im=s5_B_im, s5_C_re=s5_C_re, s5_C_im=s5_C_im, s5_D=s5_D, s5_log_dt=s5_log_dt,
             s5_w_glu=s5_w_glu)
    past = cache_sb_k.shape[2]
    cache = dict(mla_latent=cache_mla_latent[0], mla_krope=cache_mla_krope[0],
                 mlstm_c=state_mlstm_C[0], mlstm_n=state_mlstm_n[0], mlstm_m=state_mlstm_m[0],
                 s5_re=state_s5_re[0], s5_im=state_s5_im[0],
                 sb_k=cache_sb_k[0], sb_v=cache_sb_v[0])
    out_p = _trunk(x_prompt, 0, None, p)
    out_s = _trunk(x_sample, past, cache, p)
    return (out_p[0], out_s[0]) + tuple(out_p[1:]) + tuple(out_s[1:])
```

```python
import functools
import math

import jax
import jax.numpy as jnp
import numpy as np
from jax import lax
from jax.experimental import pallas as pl
from jax.experimental.pallas import tpu as pltpu

F32 = jnp.float32
BF16 = jnp.bfloat16

D_MODEL = 1024
CHUNK = 64
D_FF = 2816
EPS = 1e-6
M_HEADS = 4
M_DH = 128
M_W = M_HEADS * M_DH
A_HEADS = 4
A_NOPE = 128
A_ROPE = 64
A_VDIM = 128
Q_LORA = 384
KV_LORA = 256
ROPE_THETA = 10000.0
MLA_SCALE = (A_NOPE + A_ROPE) ** -0.5
S5_W = 512
S5_GSIZE = 16
S5_GROUPS = S5_W // S5_GSIZE
S5_P = 64
S5_STATE = S5_GROUPS * S5_P
SB_HEADS = 8
SB_DH = 64
SB_W = SB_HEADS * SB_DH

LANES = 128
SUBLANES = 8
VMEM_LIMIT_BYTES = 56 * 1024 * 1024

TOKEN_TILE = 512
FF_CHUNK = 256
MLA_TILE = 512
SB_TILE = 128
SB_TILES_PER_STEP = 4
S5_TT = 128
MLSTM_STREAMS = 4
SB_LOG_FLOOR = -24.0
LOG2E = math.log2(math.e)

HIGHEST = lax.Precision.HIGHEST


def _cparams(*sem):
    return pltpu.CompilerParams(dimension_semantics=sem, vmem_limit_bytes=VMEM_LIMIT_BYTES)


def _bdot(a, b):
    return jnp.dot(a.astype(BF16), b.astype(BF16), preferred_element_type=F32)


def _bdot_nt(a, b):
    return lax.dot_general(a.astype(BF16), b.astype(BF16), (((1,), (1,)), ((), ())),
                           preferred_element_type=F32)


def _rms(x, g):
    return x * lax.rsqrt(jnp.mean(x * x, axis=-1, keepdims=True) + EPS) * g


def _sigmoid(x):
    return 1.0 / (1.0 + jnp.exp(-x))


def _log_sigmoid(x):
    return jnp.minimum(x, 0.0) - jnp.log(1.0 + jnp.exp(-jnp.abs(x)))


def _const_spec(shape):
    nd = len(shape)
    return pl.BlockSpec(shape, lambda *_: (0,) * nd)


def _ffn_kernel(*refs, with_mix, with_final):
    it = iter(refs)
    x_ref = next(it)
    if with_mix:
        a_ref, b_ref, wa_ref, wb_ref = next(it), next(it), next(it), next(it)
    g_ref, wg_ref, wu_ref, wd_ref = next(it), next(it), next(it), next(it)
    gf_ref = next(it) if with_final else None
    o_ref = next(it)
    acc_ref = next(it)

    x = x_ref[...]
    if with_mix:
        x = x + _bdot(a_ref[...], wa_ref[...]) + _bdot(b_ref[...], wb_ref[...])
    h = _rms(x, g_ref[...]).astype(BF16)
    acc_ref[...] = jnp.zeros_like(acc_ref)
    for c in range(D_FF // FF_CHUNK):
        sl = slice(c * FF_CHUNK, (c + 1) * FF_CHUNK)
        g = jnp.dot(h, wg_ref[:, sl], preferred_element_type=F32)
        u = jnp.dot(h, wu_ref[:, sl], preferred_element_type=F32)
        a = (g * _sigmoid(g) * u).astype(BF16)
        acc_ref[...] += jnp.dot(a, wd_ref[sl, :], preferred_element_type=F32)
    y = x + 0.5 * acc_ref[...]
    if with_final:
        y = _rms(y, gf_ref[...])
    o_ref[...] = y


def _ffn(x, gain, wg, wu, wd, which, *, mix=None, final_gain=None, tm):
    n = x.shape[0]
    row = lambda i: (i, 0)
    pick = lambda i: which + (0, 0)
    args, specs = [x], [pl.BlockSpec((tm, D_MODEL), row)]
    if mix is not None:
        a, b, wa, wb = mix
        args += [a, b, wa, wb]
        specs += [pl.BlockSpec((tm, a.shape[-1]), row), pl.BlockSpec((tm, b.shape[-1]), row),
                  _const_spec(wa.shape), _const_spec(wb.shape)]
    args += [gain.reshape(1, D_MODEL), wg, wu, wd]
    specs += [_const_spec((1, D_MODEL))] + [pl.BlockSpec((None, None) + w.shape[2:], pick)
                                            for w in (wg, wu, wd)]
    if final_gain is not None:
        args.append(final_gain.reshape(1, D_MODEL))
        specs.append(_const_spec((1, D_MODEL)))
    return pl.pallas_call(
        functools.partial(_ffn_kernel, with_mix=mix is not None, with_final=final_gain is not None),
        grid=(n // tm,),
        in_specs=specs,
        out_specs=pl.BlockSpec((tm, D_MODEL), row),
        out_shape=jax.ShapeDtypeStruct((n, D_MODEL), F32),
        scratch_shapes=[pltpu.VMEM((tm, D_MODEL), F32)],
        compiler_params=_cparams("parallel"),
        name="ffn",
    )(*args)


def _even_in_kernel(x_ref, gain_ref, wm_ref, wg_ref, bg_ref, wcq_ref, wckv_ref, wkra_ref, wkrb_ref,
                    qn_ref, kvn_ref, wq_ref, ws_ref, wuk_ref, wuv_ref, cos_ref, sin_ref,
                    qm_o, km_o, vm_o, om_o, gates_o, q_o, kn_o, v_o, kr128_o, lat_o, kr_o,
                    *, v_transposed):
    h = _rms(x_ref[...], gain_ref[...]).astype(BF16)
    m = jnp.dot(h, wm_ref[...], preferred_element_type=F32)
    qm_o[...] = m[:, :M_W].astype(BF16)
    km_o[...] = (m[:, M_W:2 * M_W] * M_DH ** -0.5).astype(BF16)
    vm_o[...] = m[:, 2 * M_W:3 * M_W].astype(BF16)
    om_o[...] = m[:, 3 * M_W:]

    g = jnp.dot(h, wg_ref[...], preferred_element_type=F32) + bg_ref[...]
    lane = lax.broadcasted_iota(jnp.int32, g.shape, 1)
    gates_o[...] = jnp.where(lane < M_HEADS, g, _log_sigmoid(g))

    cos, sin = cos_ref[...], sin_ref[...]
    cq = jnp.dot(h, wcq_ref[...], preferred_element_type=F32)
    cqn = _rms(cq, qn_ref[...]).astype(BF16)
    qf = jnp.dot(cqn, wq_ref[...], preferred_element_type=F32)
    qs = jnp.dot(cqn, ws_ref[...], preferred_element_type=F32)
    q_scale = MLA_SCALE * LOG2E
    for hh in range(A_HEADS):
        base = hh * 2 * LANES
        q_o[:, base:base + LANES] = (qf[:, base:base + LANES] * q_scale).astype(BF16)
        rot = qf[:, base + LANES:base + 2 * LANES] * cos + qs[:, hh * LANES:(hh + 1) * LANES] * sin
        q_o[:, base + LANES:base + 2 * LANES] = (rot * q_scale).astype(BF16)

    ckv = jnp.dot(h, wckv_ref[...], preferred_element_type=F32)
    lat = _rms(ckv, kvn_ref[...])
    lat_o[...] = lat
    latb = lat.astype(BF16)
    kn_o[...] = jnp.dot(latb, wuk_ref[...], preferred_element_type=F32).astype(BF16)
    if v_transposed:
        v_o[0] = _bdot_nt(wuv_ref[...], latb).astype(BF16)
    else:
        v_o[...] = jnp.dot(latb, wuv_ref[...], preferred_element_type=F32).astype(BF16)

    kr = (jnp.dot(h, wkra_ref[...], preferred_element_type=F32) * cos
          + jnp.dot(h, wkrb_ref[...], preferred_element_type=F32) * sin)
    kr128_o[...] = kr.astype(BF16)
    kr_o[...] = kr[:, :A_ROPE]


def _even_in(x, gain, w, cos, sin, *, tm, seq_len):
    n = x.shape[0]
    row = lambda i: (i, 0)
    n_pos_tiles = cos.shape[0] // tm
    pos = lambda i: (i % n_pos_tiles, 0)
    v_transposed = seq_len % tm == 0
    n_time_tiles = max(seq_len // tm, 1)
    weights = [w["wm"], w["wg"], w["bg"], w["wcq"], w["wckv"], w["wkra"], w["wkrb"], w["qn"], w["kvn"],
               w["wq"], w["ws"], w["wuk"], w["wuv"].T if v_transposed else w["wuv"]]
    in_specs = ([pl.BlockSpec((tm, D_MODEL), row), _const_spec((1, D_MODEL))]
                + [_const_spec(a.shape) for a in weights]
                + [pl.BlockSpec((tm, LANES), pos), pl.BlockSpec((tm, LANES), pos)])
    outs = [(M_W, BF16), (M_W, BF16), (M_W, BF16), (M_W, F32), (LANES, F32),
            (A_HEADS * 2 * LANES, BF16), (A_HEADS * A_NOPE, BF16), (A_HEADS * A_VDIM, BF16),
            (LANES, BF16), (KV_LORA, F32), (A_ROPE, F32)]
    out_specs = [pl.BlockSpec((tm, c), row) for c, _ in outs]
    out_shape = [jax.ShapeDtypeStruct((n, c), dt) for c, dt in outs]
    if v_transposed:
        v_idx = 7
        out_specs[v_idx] = pl.BlockSpec((1, A_HEADS * A_VDIM, tm),
                                        lambda i: (i // n_time_tiles, 0, i % n_time_tiles))
        out_shape[v_idx] = jax.ShapeDtypeStruct((n // seq_len, A_HEADS * A_VDIM, seq_len), BF16)
    return pl.pallas_call(
        functools.partial(_even_in_kernel, v_transposed=v_transposed),
        grid=(n // tm,),
        in_specs=in_specs,
        out_specs=out_specs,
        out_shape=out_shape,
        compiler_params=_cparams("parallel"),
        name="even_in",
    )(x, gain.reshape(1, D_MODEL), *weights, cos, sin)


def _kv_expand_kernel(lat_ref, wuk_ref, wuv_ref, kn_o, v_o):
    latb = lat_ref[...].astype(BF16)
    kn_o[...] = jnp.dot(latb, wuk_ref[...], preferred_element_type=F32).astype(BF16)
    v_o[...] = jnp.dot(latb, wuv_ref[...], preferred_element_type=F32).astype(BF16)


def _kv_expand(lat, wuk, wuv):
    n = lat.shape[0]
    tm = min(4 * TOKEN_TILE, n)
    assert n % tm == 0
    row = lambda i: (i, 0)
    return pl.pallas_call(
        _kv_expand_kernel,
        grid=(n // tm,),
        in_specs=[pl.BlockSpec((tm, KV_LORA), row), _const_spec(wuk.shape), _const_spec(wuv.shape)],
        out_specs=[pl.BlockSpec((tm, wuk.shape[1]), row), pl.BlockSpec((tm, wuv.shape[1]), row)],
        out_shape=[jax.ShapeDtypeStruct((n, wuk.shape[1]), BF16),
                   jax.ShapeDtypeStruct((n, wuv.shape[1]), BF16)],
        compiler_params=_cparams("parallel"),
        name="kv_expand",
    )(lat, wuk, wuv)


def _mlstm_kernel(qm_ref, km_ref, vm_ref, om_ref, g_ref, onorm_ref, c0_ref, n0_ref, m0_ref,
                  h_ref, cout_ref, nout_ref, mout_ref, c_sc, n_sc, m_sc, *, n_chunks, n_streams):
    lt = pl.program_id(1)

    @pl.when(lt == 0)
    def _():
        c_sc[...] = c0_ref[...]
        n_sc[...] = n0_ref[...]
        m_sc[...] = m0_ref[...]

    row = lax.broadcasted_iota(jnp.int32, (CHUNK, CHUNK), 0)
    col = lax.broadcasted_iota(jnp.int32, (CHUNK, CHUNK), 1)
    causal = col <= row
    ltri = causal.astype(BF16)
    utri = (row <= col).astype(BF16)
    ones_kl = jnp.ones((CHUNK, LANES), BF16)

    def split3(a):
        hi = a.astype(BF16)
        r1 = a - hi.astype(F32)
        mid = r1.astype(BF16)
        return hi, mid, (r1 - mid.astype(F32)).astype(BF16)

    def chunk(c, carry):
        r0 = pl.multiple_of(c * CHUNK, CHUNK)
        rows = pl.ds(r0, CHUNK)
        chains = [(bb, hh) for bb in range(n_streams) for hh in range(M_HEADS)]
        hsl = lambda hh: slice(hh * M_DH, (hh + 1) * M_DH)
        gate = {}
        for bb in range(n_streams):
            gates = g_ref[bb, rows, :]
            gates_t = gates.T
            cc = jnp.dot(ltri, jnp.concatenate(split3(gates), axis=1), preferred_element_type=F32)
            cum_col = cc[:, :LANES] + cc[:, LANES:2 * LANES] + cc[:, 2 * LANES:]
            cr = jnp.dot(jnp.concatenate(split3(gates_t), axis=0), utri, preferred_element_type=F32)
            cum_row = cr[:LANES] + cr[LANES:2 * LANES] + cr[2 * LANES:]
            gate[bb] = (gates, gates_t, cum_col, cum_row)
        qk, qc, qn = {}, {}, {}
        for ch in chains:
            bb, hh = ch
            rhs = jnp.concatenate([c_sc[bb, hh].astype(BF16),
                                   jnp.broadcast_to(n_sc[bb, hh], (LANES, M_DH)).astype(BF16),
                                   km_ref[bb, rows, hsl(hh)]], axis=0)
            r = _bdot_nt(qm_ref[bb, rows, hsl(hh)], rhs)
            qc[ch], qn[ch], qk[ch] = r[:, :M_DH], r[:, M_DH:M_DH + LANES], r[:, M_DH + LANES:]
        st = {}
        for ch in chains:
            bb, hh = ch
            gates, gates_t, cum_col, cum_row = gate[bb]
            ig_row = gates_t[hh:hh + 1, :]
            ig_col = gates[:, hh:hh + 1]
            b_row = cum_row[M_HEADS + hh:M_HEADS + hh + 1, :]
            b_col = cum_col[:, M_HEADS + hh:M_HEADS + hh + 1]
            m_prev = m_sc[bb, hh][:, :1]
            b_lanes = jnp.broadcast_to(b_col, (CHUNK, LANES))
            log_d = jnp.where(causal, b_lanes[:, :CHUNK] - b_row + ig_row, -jnp.inf)
            m_t = jnp.maximum(b_col + m_prev, jnp.max(log_d, axis=-1, keepdims=True))
            m_lanes = jnp.broadcast_to(m_t, (CHUNK, LANES))
            s = (qk[ch] * jnp.exp(log_d - m_lanes[:, :CHUNK])).astype(BF16)
            w_inter = jnp.exp(b_lanes + m_prev - m_lanes)
            m_new = m_t[CHUNK - 1:CHUNK, :]
            b_last = b_col[CHUNK - 1:CHUNK, :]
            w_s = jnp.exp(b_last - b_col + ig_col - m_new)
            decay = jnp.exp(b_last + m_prev - m_new)
            r = jnp.dot(s, jnp.concatenate([vm_ref[bb, rows, hsl(hh)], ones_kl], axis=1),
                        preferred_element_type=F32)
            st[ch] = (r[:, :M_DH], r[:, M_DH:], w_inter, m_lanes, m_new, w_s, decay)
        for ch in chains:
            bb, hh = ch
            sv, s_sum, w_inter, m_lanes, m_new, w_s, decay = st[ch]
            k = km_ref[bb, rows, hsl(hh)]
            v = vm_ref[bb, rows, hsl(hh)]
            c_h = c_sc[bb, hh]
            n_h = n_sc[bb, hh]
            num = w_inter * qc[ch] + sv
            den = w_inter * qn[ch] + s_sum
            hv = num / jnp.maximum(jnp.abs(den), jnp.exp(-m_lanes))
            wv_t = (jnp.broadcast_to(w_s, (CHUNK, M_DH)) * v).T
            c_sc[bb, hh] = decay * c_h + _bdot(wv_t, k)
            n_sc[bb, hh] = decay * n_h + jnp.sum(w_s * k.astype(F32), axis=0, keepdims=True)
            m_sc[bb, hh] = jnp.broadcast_to(m_new, (1, LANES))
            hn = _rms(hv, onorm_ref[:, hsl(hh)])
            h_ref[bb, rows, hsl(hh)] = (_sigmoid(om_ref[bb, rows, hsl(hh)]) * hn).astype(BF16)
        return carry

    lax.fori_loop(0, n_chunks, chunk, 0)

    @pl.when(lt == pl.num_programs(1) - 1)
    def _():
        cout_ref[...] = c_sc[...]
        nout_ref[...] = n_sc[...]
        mout_ref[...] = m_sc[...]


def _mlstm(qm, km, vm, om, gates, onorm, c0, n0, m0, *, tl):
    b, l, _ = qm.shape
    ns = MLSTM_STREAMS
    tile = lambda i, j: (i, j, 0)
    st4 = lambda i, j: (i, 0, 0, 0)
    state = [((ns, M_HEADS, M_DH, M_DH), (b, M_HEADS, M_DH, M_DH)),
             ((ns, M_HEADS, 1, M_DH), (b, M_HEADS, 1, M_DH)),
             ((ns, M_HEADS, 1, LANES), (b, M_HEADS, 1, LANES))]
    return pl.pallas_call(
        functools.partial(_mlstm_kernel, n_chunks=tl // CHUNK, n_streams=ns),
        grid=(b // ns, l // tl),
        in_specs=[pl.BlockSpec((ns, tl, M_W), tile)] * 4
        + [pl.BlockSpec((ns, tl, LANES), tile), _const_spec((1, M_W))]
        + [pl.BlockSpec(blk, st4) for blk, _ in state],
        out_specs=[pl.BlockSpec((ns, tl, M_W), tile)] + [pl.BlockSpec(blk, st4) for blk, _ in state],
        out_shape=[jax.ShapeDtypeStruct((b, l, M_W), BF16)]
        + [jax.ShapeDtypeStruct(full, F32) for _, full in state],
        scratch_shapes=[pltpu.VMEM(blk, F32) for blk, _ in state],
        compiler_params=_cparams("parallel", "arbitrary"),
        name="mlstm",
    )(qm, km, vm, om, gates, onorm.reshape(1, M_W), c0, n0, m0)


MLA_SUM_ROWS = 16


MLA_EXP_RANGE = 60.0


def _mla_kernel(qi_ref, kj_ref, flag_ref, q_ref, kn_ref, kr_ref, vt_ref, o_ref, m_sc, acc_sc, bad_sc,
                redo_sm, *, tq, tkb, offset, variants):
    p = pl.program_id(1)
    i = qi_ref[p]
    j = kj_ref[p]
    flags = flag_ref[p]

    @pl.when(j == 0)
    def _():
        m_sc[...] = jnp.zeros_like(m_sc)
        acc_sc[...] = jnp.zeros_like(acc_sc)

    def score_tiles(nk, masked):
        kr = kr_ref[0, :nk, :]
        scores = []
        for hh in range(A_HEADS):
            q = q_ref[0, :, hh * 2 * LANES:(hh + 1) * 2 * LANES]
            k = jnp.concatenate([kn_ref[0, :nk, hh * A_NOPE:(hh + 1) * A_NOPE], kr], axis=-1)
            scores.append(_bdot_nt(k, q))
        if masked:
            k_chunk = (j * tkb + lax.broadcasted_iota(jnp.int32, (nk, tq), 0)) // CHUNK
            q_chunk = (offset + i * tq + lax.broadcasted_iota(jnp.int32, (nk, tq), 1)) // CHUNK
            visible = k_chunk <= q_chunk
            scores = [jnp.where(visible, s, -jnp.inf) for s in scores]
        return scores

    def values_t(hh, nk):
        return jnp.concatenate([vt_ref[0, hh * A_VDIM:(hh + 1) * A_VDIM, :nk],
                                jnp.ones((MLA_SUM_ROWS, nk), BF16)], axis=0)

    def add_block(nk, masked):
        scores = score_tiles(nk, masked)
        any_bad = jnp.zeros((1, tq), F32)
        for hh in range(A_HEADS):
            s = scores[hh]
            m_ref = m_sc[hh]
            d = jnp.max(s, axis=0, keepdims=True) - m_ref
            ok = jnp.logical_and(d <= MLA_EXP_RANGE, jnp.logical_or(d >= -MLA_EXP_RANGE, j > 0))
            pr = jnp.exp2(s - m_ref).astype(BF16)
            acc = acc_sc[hh]
            acc_sc[hh] = jnp.where(ok, acc + jnp.dot(values_t(hh, nk), pr, preferred_element_type=F32),
                                   acc)
            bad = jnp.where(ok, 0.0, 1.0)
            bad_sc[hh] = bad
            any_bad = jnp.maximum(any_bad, bad)
        redo_sm[0] = (jnp.max(any_bad) > 0.0).astype(jnp.int32)

    def redo_block(nk, masked):
        scores = score_tiles(nk, masked)
        for hh in range(A_HEADS):
            s = scores[hh]
            bad = bad_sc[hh] > 0.0
            m_prev = jnp.where(j == 0, -jnp.inf, m_sc[hh])
            m_new = jnp.maximum(m_prev, jnp.max(s, axis=0, keepdims=True))
            alpha = jnp.exp2(m_prev - m_new)
            pr = jnp.exp2(s - m_new).astype(BF16)
            acc = acc_sc[hh]
            acc_sc[hh] = jnp.where(
                bad, alpha * acc + jnp.dot(values_t(hh, nk), pr, preferred_element_type=F32), acc)
            m_sc[hh] = jnp.where(bad, m_new, m_sc[hh])

    for vi, (nk, masked) in enumerate(variants):
        pl.when((flags >> 1) == vi)(functools.partial(add_block, nk, masked))
    for vi, (nk, masked) in enumerate(variants):
        pl.when(jnp.logical_and((flags >> 1) == vi, redo_sm[0] != 0))(
            functools.partial(redo_block, nk, masked))

    @pl.when((flags & 1) != 0)
    def _():
        for hh in range(A_HEADS):
            acc = acc_sc[hh]
            o_ref[0, :, hh * A_VDIM:(hh + 1) * A_VDIM] = (
                acc[:A_VDIM] / acc[A_VDIM:A_VDIM + 1]).T.astype(BF16)


def _transpose_kernel(x_ref, o_ref):
    o_ref[0] = x_ref[0].astype(F32).T.astype(o_ref.dtype)


def _transpose_tiles(x, t):
    b, l, c = x.shape
    return pl.pallas_call(
        _transpose_kernel,
        grid=(b, l // t),
        in_specs=[pl.BlockSpec((1, t, c), lambda bb, i: (bb, i, 0))],
        out_specs=pl.BlockSpec((1, c, t), lambda bb, i: (bb, 0, i)),
        out_shape=jax.ShapeDtypeStruct((b, c, l), x.dtype),
        compiler_params=_cparams("parallel", "parallel"),
        name="transpose_tiles",
    )(x)


def _mla_pairs(lq, lk, tq, tkb, sub, offset):
    qi, kj, flags, variants = [], [], [], []
    for i in range(lq // tq):
        first_row, last_row = offset + i * tq, offset + (i + 1) * tq - 1
        k_end = min((last_row // CHUNK + 1) * CHUNK, lk)
        k_all = (first_row // CHUNK + 1) * CHUNK
        nj = -(-k_end // tkb)
        for j in range(nj):
            nk = -(-(min(k_end, (j + 1) * tkb) - j * tkb) // sub) * sub
            variant = (nk, j * tkb + nk > k_all)
            if variant not in variants:
                variants.append(variant)
            qi.append(i)
            kj.append(j)
            flags.append(int(j == nj - 1) + 2 * variants.index(variant))
    return (np.asarray(qi, np.int32), np.asarray(kj, np.int32), np.asarray(flags, np.int32),
            tuple(variants))


def _mla_attention(q, kn, kr, vt, *, tq, tkb, sub, offset):
    b, lq, _ = q.shape
    lk = kn.shape[1]
    assert lk % tkb == 0 and tkb % sub == 0 and vt.shape[2] == lk
    qi, kj, flags, variants = _mla_pairs(lq, lk, tq, tkb, sub, offset)
    qmap = lambda bb, p, qi_r, kj_r, flag_r: (bb, qi_r[p], 0)
    kmap = lambda bb, p, qi_r, kj_r, flag_r: (bb, kj_r[p], 0)
    grid_spec = pltpu.PrefetchScalarGridSpec(
        num_scalar_prefetch=3,
        grid=(b, len(qi)),
        in_specs=[pl.BlockSpec((1, tq, q.shape[-1]), qmap),
                  pl.BlockSpec((1, tkb, kn.shape[-1]), kmap),
                  pl.BlockSpec((1, tkb, kr.shape[-1]), kmap),
                  pl.BlockSpec((1, vt.shape[1], tkb), lambda bb, p, qi_r, kj_r, flag_r: (bb, 0, kj_r[p]))],
        out_specs=pl.BlockSpec((1, tq, A_HEADS * A_VDIM), qmap),
        scratch_shapes=[pltpu.VMEM((A_HEADS, 1, tq), F32),
                        pltpu.VMEM((A_HEADS, A_VDIM + MLA_SUM_ROWS, tq), F32),
                        pltpu.VMEM((A_HEADS, 1, tq), F32),
                        pltpu.SMEM((1,), jnp.int32)],
    )
    return pl.pallas_call(
        functools.partial(_mla_kernel, tq=tq, tkb=tkb, offset=offset, variants=variants),
        grid_spec=grid_spec,
        out_shape=jax.ShapeDtypeStruct((b, lq, A_HEADS * A_VDIM), BF16),
        compiler_params=_cparams("parallel", "arbitrary"),
        name="mla_attention",
    )(jnp.asarray(qi), jnp.asarray(kj), jnp.asarray(flags), q, kn, kr, vt)


def _odd_in_kernel(x_ref, gain_ref, wu_ref, wq_ref, wk_ref, wv_ref, u_o, q_o, k_o, v_o, kb_o, vb_o):
    h = _rms(x_ref[...], gain_ref[...]).astype(BF16)
    u_o[...] = jnp.dot(h, wu_ref[...], preferred_element_type=F32)
    q_o[...] = (jnp.dot(h, wq_ref[...], preferred_element_type=F32) * (SB_DH ** -0.5 * LOG2E)).astype(BF16)
    k = jnp.dot(h, wk_ref[...], preferred_element_type=F32)
    v = jnp.dot(h, wv_ref[...], preferred_element_type=F32)
    k_o[...] = pltpu.einshape("t(hd)->thd", k, h=SB_HEADS)
    v_o[...] = pltpu.einshape("t(hd)->thd", v, h=SB_HEADS)
    kb_o[...] = k.astype(BF16)
    vb_o[...] = v.astype(BF16)


def _odd_in(x, gain, w, *, tm):
    n = x.shape[0]
    row = lambda i: (i, 0)
    weights = [w["wu"], w["wq"], w["wk"], w["wv"]]
    return pl.pallas_call(
        _odd_in_kernel,
        grid=(n // tm,),
        in_specs=[pl.BlockSpec((tm, D_MODEL), row), _const_spec((1, D_MODEL))]
        + [_const_spec(a.shape) for a in weights],
        out_specs=[pl.BlockSpec((tm, S5_W), row), pl.BlockSpec((tm, SB_W), row)]
        + [pl.BlockSpec((tm, SB_HEADS, SB_DH), lambda i: (i, 0, 0))] * 2
        + [pl.BlockSpec((tm, SB_W), row)] * 2,
        out_shape=[jax.ShapeDtypeStruct((n, S5_W), F32),
                   jax.ShapeDtypeStruct((n, SB_W), BF16),
                   jax.ShapeDtypeStruct((n, SB_HEADS, SB_DH), F32),
                   jax.ShapeDtypeStruct((n, SB_HEADS, SB_DH), F32),
                   jax.ShapeDtypeStruct((n, SB_W), BF16),
                   jax.ShapeDtypeStruct((n, SB_W), BF16)],
        compiler_params=_cparams("parallel"),
        name="odd_in",
    )(x, gain.reshape(1, D_MODEL), *weights)


S5_BLOCKS = 4
S5_BLK_IN = S5_W // S5_BLOCKS
S5_BLK_ST = S5_STATE // S5_BLOCKS


def _s5_kernel(u_ref, bbr_ref, bbi_ref, ar_ref, ai_ref, ccr_ref, cci_ref, d_ref, wglu_ref,
               x0r_ref, x0i_ref, hs_o, xr_o, xi_o, u_tm, hs_tm, xr_buf, xi_buf, xr_st, xi_st,
               *, tt, nb):
    t = pl.program_id(0)

    @pl.when(t == 0)
    def _():
        xr_st[...] = x0r_ref[...]
        xi_st[...] = x0i_ref[...]

    for bb in range(nb):
        for k in range(S5_BLOCKS):
            u_tm[k, pl.ds(bb, tt, stride=nb), :] = u_ref[bb, :, k * S5_BLK_IN:(k + 1) * S5_BLK_IN]

    for k in range(S5_BLOCKS):
        cs = slice(k * S5_BLK_ST, (k + 1) * S5_BLK_ST)
        ub = u_tm[k].astype(BF16)
        xr_buf[:, cs] = jnp.dot(ub, bbr_ref[k], preferred_element_type=F32)
        xi_buf[:, cs] = jnp.dot(ub, bbi_ref[k], preferred_element_type=F32)

    for k in range(0, S5_BLOCKS, 2):
        cols = [slice((k + d) * S5_BLK_ST, (k + d + 1) * S5_BLK_ST) for d in range(2)]
        decay = [(jnp.broadcast_to(ar_ref[:, cs], (nb, S5_BLK_ST)),
                  jnp.broadcast_to(ai_ref[:, cs], (nb, S5_BLK_ST))) for cs in cols]

        def step(s, carry, cols=cols, decay=decay):
            rows = pl.ds(pl.multiple_of(s * nb, nb), nb)
            out = []
            for (xr, xi), (ar, ai), cs in zip(carry, decay, cols):
                nxr = ar * xr - ai * xi + xr_buf[rows, cs]
                nxi = ar * xi + ai * xr + xi_buf[rows, cs]
                xr_buf[rows, cs] = nxr
                xi_buf[rows, cs] = nxi
                out.append((nxr, nxi))
            return tuple(out)

        final = lax.fori_loop(0, tt, step, tuple((xr_st[:, cs], xi_st[:, cs]) for cs in cols), unroll=8)
        for (xr, xi), cs in zip(final, cols):
            xr_st[:, cs] = xr
            xi_st[:, cs] = xi

    gs = []
    for k in range(S5_BLOCKS):
        cs = slice(k * S5_BLK_ST, (k + 1) * S5_BLK_ST)
        y = (_bdot(xr_buf[:, cs], ccr_ref[k]) + _bdot(xi_buf[:, cs], cci_ref[k])
             + d_ref[:, k * S5_BLK_IN:(k + 1) * S5_BLK_IN] * u_tm[k])
        gs.append(jax.nn.gelu(y, approximate=True))
    g = jnp.concatenate(gs, axis=-1)
    hs = g * _sigmoid(_bdot(g, wglu_ref[...]))
    for k in range(S5_BLOCKS):
        hs_tm[k] = hs[:, k * S5_BLK_IN:(k + 1) * S5_BLK_IN]
    for bb in range(nb):
        for k in range(S5_BLOCKS):
            hs_o[bb, :, k * S5_BLK_IN:(k + 1) * S5_BLK_IN] = (
                hs_tm[k, pl.ds(bb, tt, stride=nb), :].astype(BF16))

    @pl.when(t == pl.num_programs(0) - 1)
    def _():
        xr_o[...] = xr_st[...]
        xi_o[...] = xi_st[...]


def _s5(u, w, x0r, x0i, *, tt):
    nb, l, _ = u.shape
    blk = lambda t: (0, t, 0)
    consts = [w["bbr"], w["bbi"], w["ar"], w["ai"], w["ccr"], w["cci"], w["d"], w["wglu"], x0r, x0i]
    return pl.pallas_call(
        functools.partial(_s5_kernel, tt=tt, nb=nb),
        grid=(l // tt,),
        in_specs=[pl.BlockSpec((nb, tt, S5_W), blk)] + [_const_spec(a.shape) for a in consts],
        out_specs=[pl.BlockSpec((nb, tt, S5_W), blk), _const_spec((nb, S5_STATE)),
                   _const_spec((nb, S5_STATE))],
        out_shape=[jax.ShapeDtypeStruct((nb, l, S5_W), BF16),
                   jax.ShapeDtypeStruct((nb, S5_STATE), F32),
                   jax.ShapeDtypeStruct((nb, S5_STATE), F32)],
        scratch_shapes=[pltpu.VMEM((S5_BLOCKS, tt * nb, S5_BLK_IN), F32),
                        pltpu.VMEM((S5_BLOCKS, tt * nb, S5_BLK_IN), F32),
                        pltpu.VMEM((tt * nb, S5_STATE), F32), pltpu.VMEM((tt * nb, S5_STATE), F32),
                        pltpu.VMEM((nb, S5_STATE), F32), pltpu.VMEM((nb, S5_STATE), F32)],
        compiler_params=_cparams("arbitrary"),
        name="s5",
    )(u, *consts)


def _sb_kernel(q_ref, k_ref, v_ref, o_ref, r_sc, acc_sc, qh_sc, *, tq, tk, offset, n_sub):
    for su in range(n_sub):
        _sb_query_tile(pl.program_id(1) * n_sub + su, slice(su * tq, (su + 1) * tq),
                       q_ref, k_ref, v_ref, o_ref, r_sc, acc_sc, qh_sc, tq=tq, tk=tk, offset=offset)


def _sb_query_tile(i, rs, q_ref, k_ref, v_ref, o_ref, r_sc, acc_sc, qh_sc, *, tq, tk, offset):
    n_pairs = SB_HEADS // 2
    half = lax.broadcasted_iota(jnp.int32, (1, LANES), 1) // SB_DH
    for pr in range(n_pairs):
        qq = q_ref[0, rs, pr * LANES:(pr + 1) * LANES]
        qh_sc[pr, :tq] = jnp.where(half == 0, qq, jnp.zeros_like(qq))
        qh_sc[pr, tq:] = jnp.where(half == 1, qq, jnp.zeros_like(qq))
    r_sc[...] = jnp.zeros_like(r_sc)
    acc_sc[...] = jnp.zeros_like(acc_sc)

    hk = tk // 2
    jrow = lax.broadcasted_iota(jnp.int32, (tk, tk), 0)
    jcol = lax.broadcasted_iota(jnp.int32, (tk, tk), 1)
    later = (jrow > jcol).astype(BF16)
    later_pk = jnp.logical_and(jrow > jcol, (jrow // hk) == (jcol // hk)).astype(BF16)
    first_half = lax.broadcasted_iota(jnp.int32, (1, tk), 1) < hk
    j_diag = (offset + i * tq) // tk

    def softplus2(z):
        return jnp.maximum(z, 0.0) + jnp.log2(1.0 + jnp.exp2(-jnp.abs(z)))

    def split2(a):
        hi = a.astype(BF16)
        return [hi, (a - hi.astype(F32)).astype(BF16)]

    def by_head(x):
        zero = jnp.zeros_like(x)
        return [jnp.where(half == 0, x, zero), jnp.where(half == 1, x, zero)]

    def sweep(tiles):
        zs, vv, keep_key = {}, {}, {}
        for t, (j, kind, live, limit) in enumerate(tiles):
            k0 = pl.multiple_of(j * tk, tk)
            if kind == "masked":
                q_pos = offset + i * tq + lax.broadcasted_iota(jnp.int32, (tq, tk), 0)
                keep_key[t] = (k0 + lax.broadcasted_iota(jnp.int32, (tq, tk), 1)) < q_pos
            elif limit is not None:
                keep_key[t] = lax.broadcasted_iota(jnp.int32, (tq, tk), 1) < limit
            for pr in range(n_pairs):
                ps = slice(pr * LANES, (pr + 1) * LANES)
                if kind == "half":
                    kh = k_ref[0, pl.ds(k0 + hk, hk), ps]
                    zero = jnp.zeros_like(kh)
                    k2 = jnp.concatenate([jnp.concatenate([kh, zero], axis=1),
                                          jnp.concatenate([zero, kh], axis=1)], axis=0)
                    q2 = jnp.concatenate([qh_sc[pr, :tq], qh_sc[pr, tq:]], axis=1)
                    zs[t, pr] = _bdot_nt(q2, k2)
                    vv[t, pr] = jnp.concatenate(by_head(v_ref[0, pl.ds(k0 + hk, hk), ps]), axis=0)
                else:
                    zs[t, pr] = _bdot_nt(qh_sc[pr], k_ref[0, pl.ds(k0, tk), ps])
                    vv[t, pr] = jnp.concatenate(by_head(v_ref[0, pl.ds(k0, tk), ps]), axis=0)
        l1s, lbs, cums = {}, {}, {}
        for t, (j, kind, live, limit) in enumerate(tiles):
            parts, slots = [], []
            for pr in range(n_pairs):
                for sub in range(1 if kind == "half" else 2):
                    slot = pr if kind == "half" else 2 * pr + sub
                    z = zs[t, pr] if kind == "half" else zs[t, pr][sub * tq:(sub + 1) * tq]
                    sp = softplus2(z)
                    l1 = -sp
                    if t in keep_key:
                        l1 = jnp.where(keep_key[t], l1, 0.0)
                    parts += split2(l1)
                    slots.append(slot)
                    l1s[t, slot], lbs[t, slot] = l1, z - sp
            c = jnp.dot(jnp.concatenate(parts, axis=0), later_pk if kind == "half" else later,
                        preferred_element_type=F32)
            for n, slot in enumerate(slots):
                cums[t, slot] = c[2 * n * tq:(2 * n + 1) * tq] + c[(2 * n + 1) * tq:(2 * n + 2) * tq]
        for t, (j, kind, live, limit) in enumerate(tiles):
            r_max = jnp.full((1, 1), -jnp.inf, F32)
            dead_shift = 0.0 if live is None else jnp.where(live, 0.0, -1e30)
            keep = 1.0 if live is None else jnp.where(live, 1.0, 0.0)
            for pr in range(n_pairs):
                if kind == "half":
                    r0, r1 = r_sc[2 * pr], r_sc[2 * pr + 1]
                    log_w = lbs[t, pr] + cums[t, pr] + (jnp.where(first_half, r0, r1) + dead_shift)
                    w = jnp.exp2(log_w).astype(BF16)
                    s_all = jnp.sum(l1s[t, pr], axis=-1, keepdims=True)
                    s0 = jnp.sum(jnp.where(first_half, l1s[t, pr], 0.0), axis=-1, keepdims=True)
                    r_new = [r0 + keep * s0, r1 + keep * (s_all - s0)]
                else:
                    ws, r_new = [], []
                    for sub in range(2):
                        hh = 2 * pr + sub
                        r_old = r_sc[hh]
                        log_w = lbs[t, hh] + cums[t, hh] + (r_old + dead_shift)
                        if t in keep_key:
                            log_w = jnp.where(keep_key[t], log_w, -jnp.inf)
                        ws.append(jnp.exp2(log_w).astype(BF16))
                        r_new.append(r_old + keep * jnp.sum(l1s[t, hh], axis=-1, keepdims=True))
                    w = jnp.concatenate(ws, axis=1)
                for sub in range(2):
                    r_sc[2 * pr + sub] = r_new[sub]
                    r_max = jnp.maximum(r_max, jnp.max(r_new[sub], axis=0, keepdims=True))
                acc_sc[:, pr * LANES:(pr + 1) * LANES] += jnp.dot(w, vv[t, pr], preferred_element_type=F32)
        return r_max[0, 0]

    has_prev = j_diag >= 1
    r_first = sweep([(j_diag, "masked", None, None), (jnp.maximum(j_diag - 1, 0), "half", has_prev, None)])

    def cond(carry):
        j, r_max, _ = carry
        return jnp.logical_and(j >= 0, r_max > SB_LOG_FLOOR * LOG2E)

    def body(carry):
        j, _, limit = carry
        return j - 1, sweep([(j, "full", None, limit)]), jnp.int32(tk)

    lax.while_loop(cond, body, (j_diag - 1, r_first, jnp.int32(hk)))
    o_ref[0, rs] = acc_sc[...].astype(BF16)


def _sb_attention(q, k, v, *, tq, tk, offset):
    b, lq, _ = q.shape
    lk = k.shape[1]
    assert tk % tq == 0 and offset % tk == 0 and lk % tk == 0
    n_sub = math.gcd(lq // tq, SB_TILES_PER_STEP)
    return pl.pallas_call(
        functools.partial(_sb_kernel, tq=tq, tk=tk, offset=offset, n_sub=n_sub),
        grid=(b, lq // (tq * n_sub)),
        in_specs=[pl.BlockSpec((1, tq * n_sub, SB_W), lambda bb, i: (bb, i, 0)),
                  pl.BlockSpec((1, lk, SB_W), lambda bb, i: (bb, 0, 0)),
                  pl.BlockSpec((1, lk, SB_W), lambda bb, i: (bb, 0, 0))],
        out_specs=pl.BlockSpec((1, tq * n_sub, SB_W), lambda bb, i: (bb, i, 0)),
        out_shape=jax.ShapeDtypeStruct((b, lq, SB_W), BF16),
        scratch_shapes=[pltpu.VMEM((SB_HEADS, tq, 1), F32),
                        pltpu.VMEM((tq, SB_W), F32),
                        pltpu.VMEM((SB_HEADS // 2, 2 * tq, LANES), BF16)],
        compiler_params=_cparams("parallel", "arbitrary"),
        name="sb_attention",
    )(q, k, v)


def _pad_cols(w, width):
    return jnp.pad(w, ((0, 0), (0, width - w.shape[1])))


def _even_weights(w_in, b_ig, b_fg, q_norm, kv_norm, w_uq, w_ukv):
    o = 4 * M_W
    w_gate = w_in[:, o:o + 2 * M_HEADS]
    o += 2 * M_HEADS
    w_cq = w_in[:, o:o + Q_LORA]
    o += Q_LORA
    w_ckv = w_in[:, o:o + KV_LORA]
    o += KV_LORA
    w_kr = w_in[:, o:o + A_ROPE]
    hr = A_ROPE // 2
    wkra = _pad_cols(w_kr, LANES)
    wkrb = _pad_cols(jnp.concatenate([w_kr[:, hr:], w_kr[:, :hr]], axis=1), LANES)
    uq = w_uq.reshape(Q_LORA, A_HEADS, A_NOPE + A_ROPE)
    zeros = jnp.zeros((Q_LORA, A_HEADS, LANES - A_ROPE), w_uq.dtype)
    wq = jnp.concatenate([uq, zeros], axis=-1).reshape(Q_LORA, A_HEADS * 2 * LANES)
    ws = jnp.concatenate([uq[..., A_NOPE + hr:], uq[..., A_NOPE:A_NOPE + hr], zeros],
                         axis=-1).reshape(Q_LORA, A_HEADS * LANES)
    ukv = w_ukv.reshape(KV_LORA, A_HEADS, A_NOPE + A_VDIM)
    return dict(
        wm=w_in[:, :4 * M_W].astype(BF16),
        wg=_pad_cols(w_gate, LANES).astype(BF16),
        bg=_pad_cols(jnp.concatenate([b_ig, b_fg]).reshape(1, 2 * M_HEADS).astype(F32), LANES),
        wcq=w_cq.astype(BF16), wckv=w_ckv.astype(BF16),
        wkra=wkra.astype(BF16), wkrb=wkrb.astype(BF16),
        qn=q_norm.reshape(1, Q_LORA).astype(F32), kvn=kv_norm.reshape(1, KV_LORA).astype(F32),
        wq=wq.astype(BF16), ws=ws.astype(BF16),
        wuk=ukv[..., :A_NOPE].reshape(KV_LORA, A_HEADS * A_NOPE).astype(BF16),
        wuv=ukv[..., A_NOPE:].reshape(KV_LORA, A_HEADS * A_VDIM).astype(BF16),
    )


def _rope_tables(pos):
    half = A_ROPE // 2
    inv_freq = ROPE_THETA ** (-jnp.arange(half, dtype=F32) / half)
    ang = pos.astype(F32)[:, None] * inv_freq[None, :]
    cos, sin = jnp.cos(ang), jnp.sin(ang)
    pad = jnp.zeros((pos.shape[0], LANES - A_ROPE), F32)
    return (jnp.concatenate([cos, cos, pad], axis=1), jnp.concatenate([-sin, sin, pad], axis=1))


def _s5_weights(a_re, a_im, b_re, b_im, c_re, c_im, d, log_dt, w_glu):
    dt = jnp.exp(log_dt.astype(F32))[:, None]
    ar, ai = a_re.astype(F32), a_im.astype(F32)
    mag = jnp.exp(dt * ar)
    ab_r, ab_i = mag * jnp.cos(dt * ai), mag * jnp.sin(dt * ai)
    den = ar * ar + ai * ai
    c_r = ((ab_r - 1.0) * ar + ab_i * ai) / den
    c_i = (ab_i * ar - (ab_r - 1.0) * ai) / den
    br, bi = b_re.astype(F32), b_im.astype(F32)
    bb_r = c_r[..., None] * br - c_i[..., None] * bi
    bb_i = c_r[..., None] * bi + c_i[..., None] * br
    gb = S5_GROUPS // S5_BLOCKS
    eye = jnp.eye(gb, dtype=F32)
    bd_in = lambda m: jnp.einsum("kgpc,gh->kgchp", m.reshape(S5_BLOCKS, gb, S5_P, S5_GSIZE),
                                 eye).reshape(S5_BLOCKS, S5_BLK_IN, S5_BLK_ST)
    bd_out = lambda m: jnp.einsum("kgcp,gh->kgphc", m.reshape(S5_BLOCKS, gb, S5_GSIZE, S5_P),
                                  eye).reshape(S5_BLOCKS, S5_BLK_ST, S5_BLK_IN)
    return dict(
        bbr=bd_in(bb_r).astype(BF16), bbi=bd_in(bb_i).astype(BF16),
        ar=ab_r.reshape(1, S5_STATE), ai=ab_i.reshape(1, S5_STATE),
        ccr=bd_out(c_re.astype(F32)).astype(BF16), cci=bd_out(-c_im.astype(F32)).astype(BF16),
        d=d.reshape(1, S5_W).astype(F32), wglu=w_glu.astype(BF16),
    )


def _pad_keys(a, mult):
    pad = (-a.shape[1]) % mult
    return jnp.pad(a, ((0, 0), (0, pad), (0, 0))) if pad else a


def _trunk(x, pos0, cache, p):
    nb, l, _ = x.shape
    assert nb == SUBLANES, "the S5 layout puts the streams of one time step on the sublanes"
    assert l % CHUNK == 0
    n = nb * l
    tm = min(TOKEN_TILE, n)
    tl = min(tm, l)
    assert l % tl == 0 and tm % tl == 0
    past = 0 if cache is None else cache["sb_k"].shape[1]
    assert pos0 == past
    xf = x.reshape(n, D_MODEL)
    bf = lambda a: a.astype(BF16)
    ffn_w = (bf(p["ffn_w_gate"]), bf(p["ffn_w_up"]), bf(p["ffn_w_down"]))

    xf = _ffn(xf, p["norm_ffn"][0, 0], *ffn_w, (0, 0), tm=tm)
    ew = _even_weights(p["even_w_in"][0], p["mlstm_b_igate"][0], p["mlstm_b_fgate"][0],
                       p["mla_q_norm"][0], p["mla_kv_norm"][0], p["mla_w_uq"][0], p["mla_w_ukv"][0])
    cos, sin = _rope_tables(pos0 + jnp.arange(l))
    if tm > l:
        cos, sin = jnp.tile(cos, (tm // l, 1)), jnp.tile(sin, (tm // l, 1))
    qm, km, vm, om, gates, q, kn, v, kr128, lat, kr = _even_in(
        xf, p["norm_mix"][0], ew, cos, sin, tm=tm, seq_len=l)
    v_transposed = v.ndim == 3

    if cache is None:
        c0 = jnp.zeros((nb, M_HEADS, M_DH, M_DH), F32)
        n0 = jnp.zeros((nb, M_HEADS, 1, M_DH), F32)
        m0 = jnp.zeros((nb, M_HEADS, 1, LANES), F32)
    else:
        c0 = cache["mlstm_c"].astype(F32)
        n0 = cache["mlstm_n"].astype(F32).reshape(nb, M_HEADS, 1, M_DH)
        m0 = jnp.broadcast_to(cache["mlstm_m"].astype(F32)[..., None, None], (nb, M_HEADS, 1, LANES))
    r3 = lambda a: a.reshape(nb, l, a.shape[-1])
    h_m, c_new, n_new, m_new = _mlstm(r3(qm), r3(km), r3(vm), r3(om), r3(gates),
                                      p["mlstm_out_norm"][0], c0, n0, m0, tl=tl)

    mla_tq = min(MLA_TILE, -(-l // LANES) * LANES)
    if cache is None:
        mla_sub = mla_tq
        mla_tkb = min(2 * mla_tq, l)
    else:
        mla_sub = mla_tkb = -(-(past + max(l, mla_tq)) // LANES) * LANES
    kn3, kr3 = r3(kn), r3(kr128)
    v3, vt = (None, v) if v_transposed else (r3(v), None)
    if cache is not None:
        ckn, cv = _kv_expand(cache["mla_latent"].astype(F32).reshape(nb * past, KV_LORA),
                             ew["wuk"], ew["wuv"])
        ckr = jnp.pad(cache["mla_krope"], ((0, 0), (0, 0), (0, LANES - A_ROPE))).astype(BF16)
        kn3 = jnp.concatenate([ckn.reshape(nb, past, -1), kn3], axis=1)
        kr3 = jnp.concatenate([ckr, kr3], axis=1)
        cv3 = cv.reshape(nb, past, -1)
        if v_transposed:
            vt = jnp.concatenate([_transpose_tiles(cv3, math.gcd(past, TOKEN_TILE)), vt], axis=2)
        else:
            v3 = jnp.concatenate([cv3, v3], axis=1)
    if vt is None:
        vt = _transpose_tiles(_pad_keys(v3, mla_tkb), mla_sub)
    else:
        vt = jnp.pad(vt, ((0, 0), (0, 0), (0, (-vt.shape[2]) % mla_tkb)))
    h_a = _mla_attention(_pad_keys(r3(q), mla_tq), _pad_keys(kn3, mla_tkb), _pad_keys(kr3, mla_tkb),
                         vt, tq=mla_tq, tkb=mla_tkb, sub=mla_sub, offset=past)[:, :l]

    w_out = bf(p["even_w_out"][0])
    xf = _ffn(xf, p["norm_ffn"][0, 1], *ffn_w, (0, 1),
              mix=(h_m.reshape(n, M_W), h_a.reshape(n, -1), w_out[:M_W], w_out[M_W:]), tm=tm)

    xf = _ffn(xf, p["norm_ffn"][1, 0], *ffn_w, (1, 0), tm=tm)
    w_in = p["odd_w_in"][0]
    ow = dict(wu=bf(w_in[:, :S5_W]), wq=bf(w_in[:, S5_W:S5_W + SB_W]),
              wk=bf(w_in[:, S5_W + SB_W:S5_W + 2 * SB_W]), wv=bf(w_in[:, S5_W + 2 * SB_W:]))
    u, qs, ks, vs, ksb, vsb = _odd_in(xf, p["norm_mix"][1], ow, tm=tm)

    sw = _s5_weights(p["s5_A_re"][0], p["s5_A_im"][0], p["s5_B_re"][0], p["s5_B_im"][0],
                     p["s5_C_re"][0], p["s5_C_im"][0], p["s5_D"][0], p["s5_log_dt"][0], p["s5_w_glu"][0])
    if cache is None:
        x0r = jnp.zeros((nb, S5_STATE), F32)
        x0i = jnp.zeros((nb, S5_STATE), F32)
    else:
        x0r = cache["s5_re"].astype(F32).reshape(nb, S5_STATE)
        x0i = cache["s5_im"].astype(F32).reshape(nb, S5_STATE)
    h_s, xr_new, xi_new = _s5(r3(u), sw, x0r, x0i, tt=min(S5_TT, l))

    k3, v3 = r3(ksb), r3(vsb)
    if cache is not None:
        k3 = jnp.concatenate([bf(cache["sb_k"].reshape(nb, past, SB_W)), k3], axis=1)
        v3 = jnp.concatenate([bf(cache["sb_v"].reshape(nb, past, SB_W)), v3], axis=1)
    h_b = _sb_attention(r3(qs), _pad_keys(k3, SB_TILE), _pad_keys(v3, SB_TILE), tq=min(SB_TILE, l),
                        tk=SB_TILE, offset=past)

    w_out = bf(p["odd_w_out"][0])
    y = _ffn(xf, p["norm_ffn"][1, 1], *ffn_w, (1, 1),
             mix=(h_s.reshape(n, S5_W), h_b.reshape(n, SB_W), w_out[:S5_W], w_out[S5_W:]),
             final_gain=p["norm_final"], tm=tm)

    return (y.reshape(nb, l, D_MODEL),
            lat.reshape(1, nb, l, KV_LORA), kr.reshape(1, nb, l, A_ROPE),
            c_new[None], n_new.reshape(1, nb, M_HEADS, M_DH), m_new[:, :, 0, 0][None],
            xr_new.reshape(1, nb, S5_GROUPS, S5_P), xi_new.reshape(1, nb, S5_GROUPS, S5_P),
            ks.reshape(1, nb, l, SB_HEADS, SB_DH), vs.reshape(1, nb, l, SB_HEADS, SB_DH))


def kernel(x_prompt, x_sample, cache_mla_latent, cache_mla_krope, state_mlstm_C, state_mlstm_n, state_mlstm_m, state_s5_re, state_s5_im, cache_sb_k, cache_sb_v, norm_ffn, norm_mix, norm_final, ffn_w_gate, ffn_w_up, ffn_w_down, even_w_in, even_w_out, mlstm_b_igate, mlstm_b_fgate, mlstm_out_norm, mla_q_norm, mla_kv_norm, mla_w_uq, mla_w_ukv, odd_w_in, odd_w_out, s5_A_re, s5_A_im, s5_B_re, s5_B_im, s5_C_re, s5_C_im, s5_D, s5_log_dt, s5_w_glu):
    p = dict(norm_ffn=norm_ffn, norm_mix=norm_mix, norm_final=norm_final, ffn_w_gate=ffn_w_gate,
             ffn_w_up=ffn_w_up, ffn_w_down=ffn_w_down, even_w_in=even_w_in, even_w_out=even_w_out,
             mlstm_b_igate=mlstm_b_igate, mlstm_b_fgate=mlstm_b_fgate, mlstm_out_norm=mlstm_out_norm,
             mla_q_norm=mla_q_norm, mla_kv_norm=mla_kv_norm, mla_w_uq=mla_w_uq, mla_w_ukv=mla_w_ukv,
             odd_w_in=odd_w_in, odd_w_out=odd_w_out, s5_A_re=s5_A_re, s5_A_im=s5_A_im, s5_B_re=s5_B_re,
             s5_B_im=s5_B_im, s5_C_re=s5_C_re, s5_C_im=s5_C_im, s5_D=s5_D, s5_log_dt=s5_log_dt,
             s5_w_glu=s5_w_glu)
    past = cache_sb_k.shape[2]
    cache = dict(mla_latent=cache_mla_latent[0], mla_krope=cache_mla_krope[0],
                 mlstm_c=state_mlstm_C[0], mlstm_n=state_mlstm_n[0], mlstm_m=state_mlstm_m[0],
                 s5_re=state_s5_re[0], s5_im=state_s5_im[0],
                 sb_k=cache_sb_k[0], sb_v=cache_sb_v[0])
    out_p = _trunk(x_prompt, 0, None, p)
    out_s = _trunk(x_sample, past, cache, p)
    return (out_p[0], out_s[0]) + tuple(out_p[1:]) + tuple(out_s[1:])
```

```python
import functools
import math

import jax
import jax.numpy as jnp
import numpy as np
from jax import lax
from jax.experimental import pallas as pl
from jax.experimental.pallas import tpu as pltpu

F32 = jnp.float32
BF16 = jnp.bfloat16

D_MODEL = 1024
CHUNK = 64
D_FF = 2816
EPS = 1e-6
M_HEADS = 4
M_DH = 128
M_W = M_HEADS * M_DH
A_HEADS = 4
A_NOPE = 128
A_ROPE = 64
A_VDIM = 128
Q_LORA = 384
KV_LORA = 256
ROPE_THETA = 10000.0
MLA_SCALE = (A_NOPE + A_ROPE) ** -0.5
S5_W = 512
S5_GSIZE = 16
S5_GROUPS = S5_W // S5_GSIZE
S5_P = 64
S5_STATE = S5_GROUPS * S5_P
SB_HEADS = 8
SB_DH = 64
SB_W = SB_HEADS * SB_DH

LANES = 128
SUBLANES = 8
VMEM_LIMIT_BYTES = 56 * 1024 * 1024

TOKEN_TILE = 512
FFN_SUB_TILE = 512
FFN_TILES_PER_STEP = 2
FF_CHUNK = 256
MLA_TILE = 512
SB_TILE = 128
SB_TILES_PER_STEP = 4
S5_TT = 128
MLSTM_STREAMS = 4
SB_LOG_FLOOR = -24.0
LOG2E = math.log2(math.e)

HIGHEST = lax.Precision.HIGHEST


def _cparams(*sem):
    return pltpu.CompilerParams(dimension_semantics=sem, vmem_limit_bytes=VMEM_LIMIT_BYTES)


def _bdot(a, b):
    return jnp.dot(a.astype(BF16), b.astype(BF16), preferred_element_type=F32)


def _bdot_nt(a, b):
    return lax.dot_general(a.astype(BF16), b.astype(BF16), (((1,), (1,)), ((), ())),
                           preferred_element_type=F32)


def _rms(x, g):
    return x * lax.rsqrt(jnp.mean(x * x, axis=-1, keepdims=True) + EPS) * g


def _sigmoid(x):
    return 1.0 / (1.0 + jnp.exp(-x))


def _log_sigmoid(x):
    return jnp.minimum(x, 0.0) - jnp.log(1.0 + jnp.exp(-jnp.abs(x)))


def _const_spec(shape):
    nd = len(shape)
    return pl.BlockSpec(shape, lambda *_: (0,) * nd)


def _ffn_kernel(*refs, with_mix, with_final):
    it = iter(refs)
    x_ref = next(it)
    if with_mix:
        a_ref, b_ref, wa_ref, wb_ref = next(it), next(it), next(it), next(it)
    g_ref, wg_ref, wu_ref, wd_ref = next(it), next(it), next(it), next(it)
    gf_ref = next(it) if with_final else None
    o_ref = next(it)
    acc_ref = next(it)

    for r0 in range(0, x_ref.shape[0], FFN_SUB_TILE):
        rs = slice(r0, min(r0 + FFN_SUB_TILE, x_ref.shape[0]))
        x = x_ref[rs, :]
        if with_mix:
            x = x + _bdot(a_ref[rs, :], wa_ref[...]) + _bdot(b_ref[rs, :], wb_ref[...])
        h = _rms(x, g_ref[...]).astype(BF16)
        acc_ref[rs, :] = jnp.zeros((rs.stop - rs.start, D_MODEL), F32)
        for c in range(D_FF // FF_CHUNK):
            sl = slice(c * FF_CHUNK, (c + 1) * FF_CHUNK)
            g = jnp.dot(h, wg_ref[:, sl], preferred_element_type=F32)
            u = jnp.dot(h, wu_ref[:, sl], preferred_element_type=F32)
            a = (g * _sigmoid(g) * u).astype(BF16)
            acc_ref[rs, :] += jnp.dot(a, wd_ref[sl, :], preferred_element_type=F32)
        y = x + 0.5 * acc_ref[rs, :]
        if with_final:
            y = _rms(y, gf_ref[...])
        o_ref[rs, :] = y


def _ffn(x, gain, wg, wu, wd, which, *, mix=None, final_gain=None, tm):
    n = x.shape[0]
    if n % (FFN_TILES_PER_STEP * tm) == 0:
        tm = FFN_TILES_PER_STEP * tm
    row = lambda i: (i, 0)
    pick = lambda i: which + (0, 0)
    once = pl.Buffered(1)
    args, specs = [x], [pl.BlockSpec((tm, D_MODEL), row)]
    if mix is not None:
        a, b, wa, wb = mix
        args += [a, b, wa, wb]
        specs += [pl.BlockSpec((tm, a.shape[-1]), row), pl.BlockSpec((tm, b.shape[-1]), row),
                  pl.BlockSpec(wa.shape, lambda i: (0, 0), pipeline_mode=once),
                  pl.BlockSpec(wb.shape, lambda i: (0, 0), pipeline_mode=once)]
    args += [gain.reshape(1, D_MODEL), wg, wu, wd]
    specs += [_const_spec((1, D_MODEL))] + [
        pl.BlockSpec((None, None) + w.shape[2:], pick, pipeline_mode=once) for w in (wg, wu, wd)]
    if final_gain is not None:
        args.append(final_gain.reshape(1, D_MODEL))
        specs.append(_const_spec((1, D_MODEL)))
    return pl.pallas_call(
        functools.partial(_ffn_kernel, with_mix=mix is not None, with_final=final_gain is not None),
        grid=(n // tm,),
        in_specs=specs,
        out_specs=pl.BlockSpec((tm, D_MODEL), row),
        out_shape=jax.ShapeDtypeStruct((n, D_MODEL), F32),
        scratch_shapes=[pltpu.VMEM((tm, D_MODEL), F32)],
        compiler_params=_cparams("parallel"),
        name="ffn",
    )(*args)


def _even_in_kernel(x_ref, gain_ref, wm_ref, wg_ref, bg_ref, wcq_ref, wckv_ref, wkra_ref, wkrb_ref,
                    qn_ref, kvn_ref, wq_ref, ws_ref, wuk_ref, wuv_ref, cos_ref, sin_ref,
                    qm_o, km_o, vm_o, om_o, gates_o, q_o, kn_o, v_o, kr128_o, lat_o, kr_o,
                    *, v_transposed):
    h = _rms(x_ref[...], gain_ref[...]).astype(BF16)
    m = jnp.dot(h, wm_ref[...], preferred_element_type=F32)
    qm_o[...] = m[:, :M_W].astype(BF16)
    km_o[...] = (m[:, M_W:2 * M_W] * M_DH ** -0.5).astype(BF16)
    vm_o[...] = m[:, 2 * M_W:3 * M_W].astype(BF16)
    om_o[...] = m[:, 3 * M_W:]

    g = jnp.dot(h, wg_ref[...], preferred_element_type=F32) + bg_ref[...]
    lane = lax.broadcasted_iota(jnp.int32, g.shape, 1)
    gates_o[...] = jnp.where(lane < M_HEADS, g, _log_sigmoid(g))

    cos, sin = cos_ref[...], sin_ref[...]
    cq = jnp.dot(h, wcq_ref[...], preferred_element_type=F32)
    cqn = _rms(cq, qn_ref[...]).astype(BF16)
    qf = jnp.dot(cqn, wq_ref[...], preferred_element_type=F32)
    qs = jnp.dot(cqn, ws_ref[...], preferred_element_type=F32)
    q_scale = MLA_SCALE * LOG2E
    for hh in range(A_HEADS):
        base = hh * 2 * LANES
        q_o[:, base:base + LANES] = (qf[:, base:base + LANES] * q_scale).astype(BF16)
        rot = qf[:, base + LANES:base + 2 * LANES] * cos + qs[:, hh * LANES:(hh + 1) * LANES] * sin
        q_o[:, base + LANES:base + 2 * LANES] = (rot * q_scale).astype(BF16)

    ckv = jnp.dot(h, wckv_ref[...], preferred_element_type=F32)
    lat = _rms(ckv, kvn_ref[...])
    lat_o[...] = lat
    latb = lat.astype(BF16)
    kn_o[...] = jnp.dot(latb, wuk_ref[...], preferred_element_type=F32).astype(BF16)
    if v_transposed:
        v_o[0] = _bdot_nt(wuv_ref[...], latb).astype(BF16)
    else:
        v_o[...] = jnp.dot(latb, wuv_ref[...], preferred_element_type=F32).astype(BF16)

    kr = (jnp.dot(h, wkra_ref[...], preferred_element_type=F32) * cos
          + jnp.dot(h, wkrb_ref[...], preferred_element_type=F32) * sin)
    kr128_o[...] = kr.astype(BF16)
    kr_o[...] = kr[:, :A_ROPE]


def _even_in(x, gain, w, cos, sin, *, tm, seq_len):
    n = x.shape[0]
    row = lambda i: (i, 0)
    n_pos_tiles = cos.shape[0] // tm
    pos = lambda i: (i % n_pos_tiles, 0)
    v_transposed = seq_len % tm == 0
    n_time_tiles = max(seq_len // tm, 1)
    weights = [w["wm"], w["wg"], w["bg"], w["wcq"], w["wckv"], w["wkra"], w["wkrb"], w["qn"], w["kvn"],
               w["wq"], w["ws"], w["wuk"], w["wuv"].T if v_transposed else w["wuv"]]
    in_specs = ([pl.BlockSpec((tm, D_MODEL), row), _const_spec((1, D_MODEL))]
                + [_const_spec(a.shape) for a in weights]
                + [pl.BlockSpec((tm, LANES), pos), pl.BlockSpec((tm, LANES), pos)])
    outs = [(M_W, BF16), (M_W, BF16), (M_W, BF16), (M_W, F32), (LANES, F32),
            (A_HEADS * 2 * LANES, BF16), (A_HEADS * A_NOPE, BF16), (A_HEADS * A_VDIM, BF16),
            (LANES, BF16), (KV_LORA, F32), (A_ROPE, F32)]
    out_specs = [pl.BlockSpec((tm, c), row) for c, _ in outs]
    out_shape = [jax.ShapeDtypeStruct((n, c), dt) for c, dt in outs]
    if v_transposed:
        v_idx = 7
        out_specs[v_idx] = pl.BlockSpec((1, A_HEADS * A_VDIM, tm),
                                        lambda i: (i // n_time_tiles, 0, i % n_time_tiles))
        out_shape[v_idx] = jax.ShapeDtypeStruct((n // seq_len, A_HEADS * A_VDIM, seq_len), BF16)
    return pl.pallas_call(
        functools.partial(_even_in_kernel, v_transposed=v_transposed),
        grid=(n // tm,),
        in_specs=in_specs,
        out_specs=out_specs,
        out_shape=out_shape,
        compiler_params=_cparams("parallel"),
        name="even_in",
    )(x, gain.reshape(1, D_MODEL), *weights, cos, sin)


def _kv_expand_kernel(lat_ref, wuk_ref, wuv_ref, kn_o, v_o):
    latb = lat_ref[...].astype(BF16)
    kn_o[...] = jnp.dot(latb, wuk_ref[...], preferred_element_type=F32).astype(BF16)
    v_o[...] = jnp.dot(latb, wuv_ref[...], preferred_element_type=F32).astype(BF16)


def _kv_expand(lat, wuk, wuv):
    n = lat.shape[0]
    tm = min(4 * TOKEN_TILE, n)
    assert n % tm == 0
    row = lambda i: (i, 0)
    return pl.pallas_call(
        _kv_expand_kernel,
        grid=(n // tm,),
        in_specs=[pl.BlockSpec((tm, KV_LORA), row), _const_spec(wuk.shape), _const_spec(wuv.shape)],
        out_specs=[pl.BlockSpec((tm, wuk.shape[1]), row), pl.BlockSpec((tm, wuv.shape[1]), row)],
        out_shape=[jax.ShapeDtypeStruct((n, wuk.shape[1]), BF16),
                   jax.ShapeDtypeStruct((n, wuv.shape[1]), BF16)],
        compiler_params=_cparams("parallel"),
        name="kv_expand",
    )(lat, wuk, wuv)


def _mlstm_kernel(qm_ref, km_ref, vm_ref, om_ref, g_ref, onorm_ref, c0_ref, n0_ref, m0_ref,
                  h_ref, cout_ref, nout_ref, mout_ref, c_sc, n_sc, m_sc, *, n_chunks, n_streams):
    lt = pl.program_id(1)

    @pl.when(lt == 0)
    def _():
        c_sc[...] = c0_ref[...]
        n_sc[...] = n0_ref[...]
        m_sc[...] = m0_ref[...]

    row = lax.broadcasted_iota(jnp.int32, (CHUNK, CHUNK), 0)
    col = lax.broadcasted_iota(jnp.int32, (CHUNK, CHUNK), 1)
    causal = col <= row
    ltri = causal.astype(BF16)
    utri = (row <= col).astype(BF16)
    ones_kl = jnp.ones((CHUNK, LANES), BF16)

    def split3(a):
        hi = a.astype(BF16)
        r1 = a - hi.astype(F32)
        mid = r1.astype(BF16)
        return hi, mid, (r1 - mid.astype(F32)).astype(BF16)

    def chunk(c, carry):
        r0 = pl.multiple_of(c * CHUNK, CHUNK)
        rows = pl.ds(r0, CHUNK)
        chains = [(bb, hh) for bb in range(n_streams) for hh in range(M_HEADS)]
        hsl = lambda hh: slice(hh * M_DH, (hh + 1) * M_DH)
        gate = {}
        for bb in range(n_streams):
            gates = g_ref[bb, rows, :]
            gates_t = gates.T
            cc = jnp.dot(ltri, jnp.concatenate(split3(gates), axis=1), preferred_element_type=F32)
            cum_col = cc[:, :LANES] + cc[:, LANES:2 * LANES] + cc[:, 2 * LANES:]
            cr = jnp.dot(jnp.concatenate(split3(gates_t), axis=0), utri, preferred_element_type=F32)
            cum_row = cr[:LANES] + cr[LANES:2 * LANES] + cr[2 * LANES:]
            gate[bb] = (gates, gates_t, cum_col, cum_row)
        qk, qc, qn = {}, {}, {}
        for ch in chains:
            bb, hh = ch
            rhs = jnp.concatenate([c_sc[bb, hh].astype(BF16),
                                   jnp.broadcast_to(n_sc[bb, hh], (LANES, M_DH)).astype(BF16),
                                   km_ref[bb, rows, hsl(hh)]], axis=0)
            r = _bdot_nt(qm_ref[bb, rows, hsl(hh)], rhs)
            qc[ch], qn[ch], qk[ch] = r[:, :M_DH], r[:, M_DH:M_DH + LANES], r[:, M_DH + LANES:]
        st = {}
        for ch in chains:
            bb, hh = ch
            gates, gates_t, cum_col, cum_row = gate[bb]
            ig_row = gates_t[hh:hh + 1, :]
            ig_col = gates[:, hh:hh + 1]
            b_row = cum_row[M_HEADS + hh:M_HEADS + hh + 1, :]
            b_col = cum_col[:, M_HEADS + hh:M_HEADS + hh + 1]
            m_prev = m_sc[bb, hh][:, :1]
            b_lanes = jnp.broadcast_to(b_col, (CHUNK, LANES))
            log_d = jnp.where(causal, b_lanes[:, :CHUNK] - b_row + ig_row, -jnp.inf)
            m_t = jnp.maximum(b_col + m_prev, jnp.max(log_d, axis=-1, keepdims=True))
            m_lanes = jnp.broadcast_to(m_t, (CHUNK, LANES))
            s = (qk[ch] * jnp.exp(log_d - m_lanes[:, :CHUNK])).astype(BF16)
            w_inter = jnp.exp(b_lanes + m_prev - m_lanes)
            m_new = m_t[CHUNK - 1:CHUNK, :]
            b_last = b_col[CHUNK - 1:CHUNK, :]
            w_s = jnp.exp(b_last - b_col + ig_col - m_new)
            decay = jnp.exp(b_last + m_prev - m_new)
            r = jnp.dot(s, jnp.concatenate([vm_ref[bb, rows, hsl(hh)], ones_kl], axis=1),
                        preferred_element_type=F32)
            st[ch] = (r[:, :M_DH], r[:, M_DH:], w_inter, m_lanes, m_new, w_s, decay)
        for ch in chains:
            bb, hh = ch
            sv, s_sum, w_inter, m_lanes, m_new, w_s, decay = st[ch]
            k = km_ref[bb, rows, hsl(hh)]
            v = vm_ref[bb, rows, hsl(hh)]
            c_h = c_sc[bb, hh]
            n_h = n_sc[bb, hh]
            num = w_inter * qc[ch] + sv
            den = w_inter * qn[ch] + s_sum
            hv = num / jnp.maximum(jnp.abs(den), jnp.exp(-m_lanes))
            wv_t = (jnp.broadcast_to(w_s, (CHUNK, M_DH)) * v).T
            c_sc[bb, hh] = decay * c_h + _bdot(wv_t, k)
            n_sc[bb, hh] = decay * n_h + jnp.sum(w_s * k.astype(F32), axis=0, keepdims=True)
            m_sc[bb, hh] = jnp.broadcast_to(m_new, (1, LANES))
            hn = _rms(hv, onorm_ref[:, hsl(hh)])
            h_ref[bb, rows, hsl(hh)] = (_sigmoid(om_ref[bb, rows, hsl(hh)]) * hn).astype(BF16)
        return carry

    lax.fori_loop(0, n_chunks, chunk, 0)

    @pl.when(lt == pl.num_programs(1) - 1)
    def _():
        cout_ref[...] = c_sc[...]
        nout_ref[...] = n_sc[...]
        mout_ref[...] = m_sc[...]


def _mlstm(qm, km, vm, om, gates, onorm, c0, n0, m0, *, tl):
    b, l, _ = qm.shape
    ns = MLSTM_STREAMS
    tile = lambda i, j: (i, j, 0)
    st4 = lambda i, j: (i, 0, 0, 0)
    state = [((ns, M_HEADS, M_DH, M_DH), (b, M_HEADS, M_DH, M_DH)),
             ((ns, M_HEADS, 1, M_DH), (b, M_HEADS, 1, M_DH)),
             ((ns, M_HEADS, 1, LANES), (b, M_HEADS, 1, LANES))]
    return pl.pallas_call(
        functools.partial(_mlstm_kernel, n_chunks=tl // CHUNK, n_streams=ns),
        grid=(b // ns, l // tl),
        in_specs=[pl.BlockSpec((ns, tl, M_W), tile)] * 4
        + [pl.BlockSpec((ns, tl, LANES), tile), _const_spec((1, M_W))]
        + [pl.BlockSpec(blk, st4) for blk, _ in state],
        out_specs=[pl.BlockSpec((ns, tl, M_W), tile)] + [pl.BlockSpec(blk, st4) for blk, _ in state],
        out_shape=[jax.ShapeDtypeStruct((b, l, M_W), BF16)]
        + [jax.ShapeDtypeStruct(full, F32) for _, full in state],
        scratch_shapes=[pltpu.VMEM(blk, F32) for blk, _ in state],
        compiler_params=_cparams("parallel", "arbitrary"),
        name="mlstm",
    )(qm, km, vm, om, gates, onorm.reshape(1, M_W), c0, n0, m0)


MLA_SUM_ROWS = 16


MLA_EXP_RANGE = 60.0


def _mla_kernel(qi_ref, kj_ref, flag_ref, q_ref, kn_ref, kr_ref, vt_ref, o_ref, m_sc, acc_sc, bad_sc,
                redo_sm, *, tq, tkb, offset, variants):
    p = pl.program_id(1)
    i = qi_ref[p]
    j = kj_ref[p]
    flags = flag_ref[p]

    @pl.when(j == 0)
    def _():
        m_sc[...] = jnp.zeros_like(m_sc)
        acc_sc[...] = jnp.zeros_like(acc_sc)

    def score_tiles(nk, masked):
        kr = kr_ref[0, :nk, :]
        scores = []
        for hh in range(A_HEADS):
            q = q_ref[0, :, hh * 2 * LANES:(hh + 1) * 2 * LANES]
            k = jnp.concatenate([kn_ref[0, :nk, hh * A_NOPE:(hh + 1) * A_NOPE], kr], axis=-1)
            scores.append(_bdot_nt(k, q))
        if masked:
            k_chunk = (j * tkb + lax.broadcasted_iota(jnp.int32, (nk, tq), 0)) // CHUNK
            q_chunk = (offset + i * tq + lax.broadcasted_iota(jnp.int32, (nk, tq), 1)) // CHUNK
            visible = k_chunk <= q_chunk
            scores = [jnp.where(visible, s, -jnp.inf) for s in scores]
        return scores

    def values_t(hh, nk):
        return jnp.concatenate([vt_ref[0, hh * A_VDIM:(hh + 1) * A_VDIM, :nk],
                                jnp.ones((MLA_SUM_ROWS, nk), BF16)], axis=0)

    def add_block(nk, masked):
        scores = score_tiles(nk, masked)
        any_bad = jnp.zeros((1, tq), F32)
        for hh in range(A_HEADS):
            s = scores[hh]
            m_ref = m_sc[hh]
            d = jnp.max(s, axis=0, keepdims=True) - m_ref
            ok = jnp.logical_and(d <= MLA_EXP_RANGE, jnp.logical_or(d >= -MLA_EXP_RANGE, j > 0))
            pr = jnp.exp2(s - m_ref).astype(BF16)
            acc = acc_sc[hh]
            acc_sc[hh] = jnp.where(ok, acc + jnp.dot(values_t(hh, nk), pr, preferred_element_type=F32),
                                   acc)
            bad = jnp.where(ok, 0.0, 1.0)
            bad_sc[hh] = bad
            any_bad = jnp.maximum(any_bad, bad)
        redo_sm[0] = (jnp.max(any_bad) > 0.0).astype(jnp.int32)

    def redo_block(nk, masked):
        scores = score_tiles(nk, masked)
        for hh in range(A_HEADS):
            s = scores[hh]
            bad = bad_sc[hh] > 0.0
            m_prev = jnp.where(j == 0, -jnp.inf, m_sc[hh])
            m_new = jnp.maximum(m_prev, jnp.max(s, axis=0, keepdims=True))
            alpha = jnp.exp2(m_prev - m_new)
            pr = jnp.exp2(s - m_new).astype(BF16)
            acc = acc_sc[hh]
            acc_sc[hh] = jnp.where(
                bad, alpha * acc + jnp.dot(values_t(hh, nk), pr, preferred_element_type=F32), acc)
            m_sc[hh] = jnp.where(bad, m_new, m_sc[hh])

    for vi, (nk, masked) in enumerate(variants):
        pl.when((flags >> 1) == vi)(functools.partial(add_block, nk, masked))
    for vi, (nk, masked) in enumerate(variants):
        pl.when(jnp.logical_and((flags >> 1) == vi, redo_sm[0] != 0))(
            functools.partial(redo_block, nk, masked))

    @pl.when((flags & 1) != 0)
    def _():
        for hh in range(A_HEADS):
            acc = acc_sc[hh]
            o_ref[0, :, hh * A_VDIM:(hh + 1) * A_VDIM] = (
                acc[:A_VDIM] / acc[A_VDIM:A_VDIM + 1]).T.astype(BF16)


def _transpose_kernel(x_ref, o_ref):
    o_ref[0] = x_ref[0].astype(F32).T.astype(o_ref.dtype)


def _transpose_tiles(x, t):
    b, l, c = x.shape
    return pl.pallas_call(
        _transpose_kernel,
        grid=(b, l // t),
        in_specs=[pl.BlockSpec((1, t, c), lambda bb, i: (bb, i, 0))],
        out_specs=pl.BlockSpec((1, c, t), lambda bb, i: (bb, 0, i)),
        out_shape=jax.ShapeDtypeStruct((b, c, l), x.dtype),
        compiler_params=_cparams("parallel", "parallel"),
        name="transpose_tiles",
    )(x)


def _mla_pairs(lq, lk, tq, tkb, sub, offset):
    qi, kj, flags, variants = [], [], [], []
    for i in range(lq // tq):
        first_row, last_row = offset + i * tq, offset + (i + 1) * tq - 1
        k_end = min((last_row // CHUNK + 1) * CHUNK, lk)
        k_all = (first_row // CHUNK + 1) * CHUNK
        nj = -(-k_end // tkb)
        for j in range(nj):
            nk = -(-(min(k_end, (j + 1) * tkb) - j * tkb) // sub) * sub
            variant = (nk, j * tkb + nk > k_all)
            if variant not in variants:
                variants.append(variant)
            qi.append(i)
            kj.append(j)
            flags.append(int(j == nj - 1) + 2 * variants.index(variant))
    return (np.asarray(qi, np.int32), np.asarray(kj, np.int32), np.asarray(flags, np.int32),
            tuple(variants))


def _mla_attention(q, kn, kr, vt, *, tq, tkb, sub, offset):
    b, lq, _ = q.shape
    lk = kn.shape[1]
    assert lk % tkb == 0 and tkb % sub == 0 and vt.shape[2] == lk
    qi, kj, flags, variants = _mla_pairs(lq, lk, tq, tkb, sub, offset)
    qmap = lambda bb, p, qi_r, kj_r, flag_r: (bb, qi_r[p], 0)
    kmap = lambda bb, p, qi_r, kj_r, flag_r: (bb, kj_r[p], 0)
    grid_spec = pltpu.PrefetchScalarGridSpec(
        num_scalar_prefetch=3,
        grid=(b, len(qi)),
        in_specs=[pl.BlockSpec((1, tq, q.shape[-1]), qmap),
                  pl.BlockSpec((1, tkb, kn.shape[-1]), kmap),
                  pl.BlockSpec((1, tkb, kr.shape[-1]), kmap),
                  pl.BlockSpec((1, vt.shape[1], tkb), lambda bb, p, qi_r, kj_r, flag_r: (bb, 0, kj_r[p]))],
        out_specs=pl.BlockSpec((1, tq, A_HEADS * A_VDIM), qmap),
        scratch_shapes=[pltpu.VMEM((A_HEADS, 1, tq), F32),
                        pltpu.VMEM((A_HEADS, A_VDIM + MLA_SUM_ROWS, tq), F32),
                        pltpu.VMEM((A_HEADS, 1, tq), F32),
                        pltpu.SMEM((1,), jnp.int32)],
    )
    return pl.pallas_call(
        functools.partial(_mla_kernel, tq=tq, tkb=tkb, offset=offset, variants=variants),
        grid_spec=grid_spec,
        out_shape=jax.ShapeDtypeStruct((b, lq, A_HEADS * A_VDIM), BF16),
        compiler_params=_cparams("parallel", "arbitrary"),
        name="mla_attention",
    )(jnp.asarray(qi), jnp.asarray(kj), jnp.asarray(flags), q, kn, kr, vt)


def _odd_in_kernel(x_ref, gain_ref, wu_ref, wq_ref, wk_ref, wv_ref, u_o, q_o, k_o, v_o, kb_o, vb_o):
    h = _rms(x_ref[...], gain_ref[...]).astype(BF16)
    u_o[...] = jnp.dot(h, wu_ref[...], preferred_element_type=F32)
    q_o[...] = (jnp.dot(h, wq_ref[...], preferred_element_type=F32) * (SB_DH ** -0.5 * LOG2E)).astype(BF16)
    k = jnp.dot(h, wk_ref[...], preferred_element_type=F32)
    v = jnp.dot(h, wv_ref[...], preferred_element_type=F32)
    k_o[...] = pltpu.einshape("t(hd)->thd", k, h=SB_HEADS)
    v_o[...] = pltpu.einshape("t(hd)->thd", v, h=SB_HEADS)
    kb_o[...] = k.astype(BF16)
    vb_o[...] = v.astype(BF16)


def _odd_in(x, gain, w, *, tm):
    n = x.shape[0]
    row = lambda i: (i, 0)
    weights = [w["wu"], w["wq"], w["wk"], w["wv"]]
    return pl.pallas_call(
        _odd_in_kernel,
        grid=(n // tm,),
        in_specs=[pl.BlockSpec((tm, D_MODEL), row), _const_spec((1, D_MODEL))]
        + [_const_spec(a.shape) for a in weights],
        out_specs=[pl.BlockSpec((tm, S5_W), row), pl.BlockSpec((tm, SB_W), row)]
        + [pl.BlockSpec((tm, SB_HEADS, SB_DH), lambda i: (i, 0, 0))] * 2
        + [pl.BlockSpec((tm, SB_W), row)] * 2,
        out_shape=[jax.ShapeDtypeStruct((n, S5_W), F32),
                   jax.ShapeDtypeStruct((n, SB_W), BF16),
                   jax.ShapeDtypeStruct((n, SB_HEADS, SB_DH), F32),
                   jax.ShapeDtypeStruct((n, SB_HEADS, SB_DH), F32),
                   jax.ShapeDtypeStruct((n, SB_W), BF16),
                   jax.ShapeDtypeStruct((n, SB_W), BF16)],
        compiler_params=_cparams("parallel"),
        name="odd_in",
    )(x, gain.reshape(1, D_MODEL), *weights)


S5_BLOCKS = 4
S5_BLK_IN = S5_W // S5_BLOCKS
S5_BLK_ST = S5_STATE // S5_BLOCKS


def _s5_kernel(u_ref, bbr_ref, bbi_ref, ar_ref, ai_ref, ccr_ref, cci_ref, d_ref, wglu_ref,
               x0r_ref, x0i_ref, hs_o, xr_o, xi_o, u_tm, hs_tm, xr_buf, xi_buf, xr_st, xi_st,
               *, tt, nb):
    t = pl.program_id(0)

    @pl.when(t == 0)
    def _():
        xr_st[...] = x0r_ref[...]
        xi_st[...] = x0i_ref[...]

    for bb in range(nb):
        for k in range(S5_BLOCKS):
            u_tm[k, pl.ds(bb, tt, stride=nb), :] = u_ref[bb, :, k * S5_BLK_IN:(k + 1) * S5_BLK_IN]

    for k in range(S5_BLOCKS):
        cs = slice(k * S5_BLK_ST, (k + 1) * S5_BLK_ST)
        ub = u_tm[k].astype(BF16)
        xr_buf[:, cs] = jnp.dot(ub, bbr_ref[k], preferred_element_type=F32)
        xi_buf[:, cs] = jnp.dot(ub, bbi_ref[k], preferred_element_type=F32)

    for k in range(0, S5_BLOCKS, 2):
        cols = [slice((k + d) * S5_BLK_ST, (k + d + 1) * S5_BLK_ST) for d in range(2)]
        decay = [(jnp.broadcast_to(ar_ref[:, cs], (nb, S5_BLK_ST)),
                  jnp.broadcast_to(ai_ref[:, cs], (nb, S5_BLK_ST))) for cs in cols]

        def step(s, carry, cols=cols, decay=decay):
            rows = pl.ds(pl.multiple_of(s * nb, nb), nb)
            out = []
            for (xr, xi), (ar, ai), cs in zip(carry, decay, cols):
                nxr = ar * xr - ai * xi + xr_buf[rows, cs]
                nxi = ar * xi + ai * xr + xi_buf[rows, cs]
                xr_buf[rows, cs] = nxr
                xi_buf[rows, cs] = nxi
                out.append((nxr, nxi))
            return tuple(out)

        final = lax.fori_loop(0, tt, step, tuple((xr_st[:, cs], xi_st[:, cs]) for cs in cols), unroll=8)
        for (xr, xi), cs in zip(final, cols):
            xr_st[:, cs] = xr
            xi_st[:, cs] = xi

    gs = []
    for k in range(S5_BLOCKS):
        cs = slice(k * S5_BLK_ST, (k + 1) * S5_BLK_ST)
        y = (_bdot(xr_buf[:, cs], ccr_ref[k]) + _bdot(xi_buf[:, cs], cci_ref[k])
             + d_ref[:, k * S5_BLK_IN:(k + 1) * S5_BLK_IN] * u_tm[k])
        gs.append(jax.nn.gelu(y, approximate=True))
    g = jnp.concatenate(gs, axis=-1)
    hs = g * _sigmoid(_bdot(g, wglu_ref[...]))
    for k in range(S5_BLOCKS):
        hs_tm[k] = hs[:, k * S5_BLK_IN:(k + 1) * S5_BLK_IN]
    for bb in range(nb):
        for k in range(S5_BLOCKS):
            hs_o[bb, :, k * S5_BLK_IN:(k + 1) * S5_BLK_IN] = (
                hs_tm[k, pl.ds(bb, tt, stride=nb), :].astype(BF16))

    @pl.when(t == pl.num_programs(0) - 1)
    def _():
        xr_o[...] = xr_st[...]
        xi_o[...] = xi_st[...]


def _s5(u, w, x0r, x0i, *, tt):
    nb, l, _ = u.shape
    blk = lambda t: (0, t, 0)
    consts = [w["bbr"], w["bbi"], w["ar"], w["ai"], w["ccr"], w["cci"], w["d"], w["wglu"], x0r, x0i]
    return pl.pallas_call(
        functools.partial(_s5_kernel, tt=tt, nb=nb),
        grid=(l // tt,),
        in_specs=[pl.BlockSpec((nb, tt, S5_W), blk)] + [_const_spec(a.shape) for a in consts],
        out_specs=[pl.BlockSpec((nb, tt, S5_W), blk), _const_spec((nb, S5_STATE)),
                   _const_spec((nb, S5_STATE))],
        out_shape=[jax.ShapeDtypeStruct((nb, l, S5_W), BF16),
                   jax.ShapeDtypeStruct((nb, S5_STATE), F32),
                   jax.ShapeDtypeStruct((nb, S5_STATE), F32)],
        scratch_shapes=[pltpu.VMEM((S5_BLOCKS, tt * nb, S5_BLK_IN), F32),
                        pltpu.VMEM((S5_BLOCKS, tt * nb, S5_BLK_IN), F32),
                        pltpu.VMEM((tt * nb, S5_STATE), F32), pltpu.VMEM((tt * nb, S5_STATE), F32),
                        pltpu.VMEM((nb, S5_STATE), F32), pltpu.VMEM((nb, S5_STATE), F32)],
        compiler_params=_cparams("arbitrary"),
        name="s5",
    )(u, *consts)


def _sb_kernel(q_ref, k_ref, v_ref, o_ref, r_sc, acc_sc, qh_sc, *, tq, tk, offset, n_sub):
    for su in range(n_sub):
        _sb_query_tile(pl.program_id(1) * n_sub + su, slice(su * tq, (su + 1) * tq),
                       q_ref, k_ref, v_ref, o_ref, r_sc, acc_sc, qh_sc, tq=tq, tk=tk, offset=offset)


def _sb_query_tile(i, rs, q_ref, k_ref, v_ref, o_ref, r_sc, acc_sc, qh_sc, *, tq, tk, offset):
    n_pairs = SB_HEADS // 2
    half = lax.broadcasted_iota(jnp.int32, (1, LANES), 1) // SB_DH
    for pr in range(n_pairs):
        qq = q_ref[0, rs, pr * LANES:(pr + 1) * LANES]
        qh_sc[pr, :tq] = jnp.where(half == 0, qq, jnp.zeros_like(qq))
        qh_sc[pr, tq:] = jnp.where(half == 1, qq, jnp.zeros_like(qq))
    r_sc[...] = jnp.zeros_like(r_sc)
    acc_sc[...] = jnp.zeros_like(acc_sc)

    hk = tk // 2
    jrow = lax.broadcasted_iota(jnp.int32, (tk, tk), 0)
    jcol = lax.broadcasted_iota(jnp.int32, (tk, tk), 1)
    later = (jrow > jcol).astype(BF16)
    later_pk = jnp.logical_and(jrow > jcol, (jrow // hk) == (jcol // hk)).astype(BF16)
    first_half = lax.broadcasted_iota(jnp.int32, (1, tk), 1) < hk
    j_diag = (offset + i * tq) // tk

    def softplus2(z):
        return jnp.maximum(z, 0.0) + jnp.log2(1.0 + jnp.exp2(-jnp.abs(z)))

    def split2(a):
        hi = a.astype(BF16)
        return [hi, (a - hi.astype(F32)).astype(BF16)]

    def by_head(x):
        zero = jnp.zeros_like(x)
        return [jnp.where(half == 0, x, zero), jnp.where(half == 1, x, zero)]

    def sweep(tiles):
        zs, vv, keep_key = {}, {}, {}
        for t, (j, kind, live, limit) in enumerate(tiles):
            k0 = pl.multiple_of(j * tk, tk)
            if kind == "masked":
                q_pos = offset + i * tq + lax.broadcasted_iota(jnp.int32, (tq, tk), 0)
                keep_key[t] = (k0 + lax.broadcasted_iota(jnp.int32, (tq, tk), 1)) < q_pos
            elif limit is not None:
                keep_key[t] = lax.broadcasted_iota(jnp.int32, (tq, tk), 1) < limit
            for pr in range(n_pairs):
                ps = slice(pr * LANES, (pr + 1) * LANES)
                if kind == "half":
                    kh = k_ref[0, pl.ds(k0 + hk, hk), ps]
                    zero = jnp.zeros_like(kh)
                    k2 = jnp.concatenate([jnp.concatenate([kh, zero], axis=1),
                                          jnp.concatenate([zero, kh], axis=1)], axis=0)
                    q2 = jnp.concatenate([qh_sc[pr, :tq], qh_sc[pr, tq:]], axis=1)
                    zs[t, pr] = _bdot_nt(q2, k2)
                    vv[t, pr] = jnp.concatenate(by_head(v_ref[0, pl.ds(k0 + hk, hk), ps]), axis=0)
                else:
                    zs[t, pr] = _bdot_nt(qh_sc[pr], k_ref[0, pl.ds(k0, tk), ps])
                    vv[t, pr] = jnp.concatenate(by_head(v_ref[0, pl.ds(k0, tk), ps]), axis=0)
        l1s, lbs, cums = {}, {}, {}
        for t, (j, kind, live, limit) in enumerate(tiles):
            parts, slots = [], []
            for pr in range(n_pairs):
                for sub in range(1 if kind == "half" else 2):
                    slot = pr if kind == "half" else 2 * pr + sub
                    z = zs[t, pr] if kind == "half" else zs[t, pr][sub * tq:(sub + 1) * tq]
                    sp = softplus2(z)
                    l1 = -sp
                    if t in keep_key:
                        l1 = jnp.where(keep_key[t], l1, 0.0)
                    parts += split2(l1)
                    slots.append(slot)
                    l1s[t, slot], lbs[t, slot] = l1, z - sp
            c = jnp.dot(jnp.concatenate(parts, axis=0), later_pk if kind == "half" else later,
                        preferred_element_type=F32)
            for n, slot in enumerate(slots):
                cums[t, slot] = c[2 * n * tq:(2 * n + 1) * tq] + c[(2 * n + 1) * tq:(2 * n + 2) * tq]
        for t, (j, kind, live, limit) in enumerate(tiles):
            r_max = jnp.full((1, 1), -jnp.inf, F32)
            dead_shift = 0.0 if live is None else jnp.where(live, 0.0, -1e30)
            keep = 1.0 if live is None else jnp.where(live, 1.0, 0.0)
            for pr in range(n_pairs):
                if kind == "half":
                    r0, r1 = r_sc[2 * pr], r_sc[2 * pr + 1]
                    log_w = lbs[t, pr] + cums[t, pr] + (jnp.where(first_half, r0, r1) + dead_shift)
                    w = jnp.exp2(log_w).astype(BF16)
                    s_all = jnp.sum(l1s[t, pr], axis=-1, keepdims=True)
                    s0 = jnp.sum(jnp.where(first_half, l1s[t, pr], 0.0), axis=-1, keepdims=True)
                    r_new = [r0 + keep * s0, r1 + keep * (s_all - s0)]
                else:
                    ws, r_new = [], []
                    for sub in range(2):
                        hh = 2 * pr + sub
                        r_old = r_sc[hh]
                        log_w = lbs[t, hh] + cums[t, hh] + (r_old + dead_shift)
                        if t in keep_key:
                            log_w = jnp.where(keep_key[t], log_w, -jnp.inf)
                        ws.append(jnp.exp2(log_w).astype(BF16))
                        r_new.append(r_old + keep * jnp.sum(l1s[t, hh], axis=-1, keepdims=True))
                    w = jnp.concatenate(ws, axis=1)
                for sub in range(2):
                    r_sc[2 * pr + sub] = r_new[sub]
                    r_max = jnp.maximum(r_max, jnp.max(r_new[sub], axis=0, keepdims=True))
                acc_sc[:, pr * LANES:(pr + 1) * LANES] += jnp.dot(w, vv[t, pr], preferred_element_type=F32)
        return r_max[0, 0]

    has_prev = j_diag >= 1
    r_first = sweep([(j_diag, "masked", None, None), (jnp.maximum(j_diag - 1, 0), "half", has_prev, None)])

    def cond(carry):
        j, r_max, _ = carry
        return jnp.logical_and(j >= 0, r_max > SB_LOG_FLOOR * LOG2E)

    def body(carry):
        j, _, limit = carry
        return j - 1, sweep([(j, "full", None, limit)]), jnp.int32(tk)

    lax.while_loop(cond, body, (j_diag - 1, r_first, jnp.int32(hk)))
    o_ref[0, rs] = acc_sc[...].astype(BF16)


def _sb_attention(q, k, v, *, tq, tk, offset):
    b, lq, _ = q.shape
    lk = k.shape[1]
    assert tk % tq == 0 and offset % tk == 0 and lk % tk == 0
    n_sub = math.gcd(lq // tq, SB_TILES_PER_STEP)
    return pl.pallas_call(
        functools.partial(_sb_kernel, tq=tq, tk=tk, offset=offset, n_sub=n_sub),
        grid=(b, lq // (tq * n_sub)),
        in_specs=[pl.BlockSpec((1, tq * n_sub, SB_W), lambda bb, i: (bb, i, 0)),
                  pl.BlockSpec((1, lk, SB_W), lambda bb, i: (bb, 0, 0)),
                  pl.BlockSpec((1, lk, SB_W), lambda bb, i: (bb, 0, 0))],
        out_specs=pl.BlockSpec((1, tq * n_sub, SB_W), lambda bb, i: (bb, i, 0)),
        out_shape=jax.ShapeDtypeStruct((b, lq, SB_W), BF16),
        scratch_shapes=[pltpu.VMEM((SB_HEADS, tq, 1), F32),
                        pltpu.VMEM((tq, SB_W), F32),
                        pltpu.VMEM((SB_HEADS // 2, 2 * tq, LANES), BF16)],
        compiler_params=_cparams("parallel", "arbitrary"),
        name="sb_attention",
    )(q, k, v)


def _pad_cols(w, width):
    return jnp.pad(w, ((0, 0), (0, width - w.shape[1])))


def _even_weights(w_in, b_ig, b_fg, q_norm, kv_norm, w_uq, w_ukv):
    o = 4 * M_W
    w_gate = w_in[:, o:o + 2 * M_HEADS]
    o += 2 * M_HEADS
    w_cq = w_in[:, o:o + Q_LORA]
    o += Q_LORA
    w_ckv = w_in[:, o:o + KV_LORA]
    o += KV_LORA
    w_kr = w_in[:, o:o + A_ROPE]
    hr = A_ROPE // 2
    wkra = _pad_cols(w_kr, LANES)
    wkrb = _pad_cols(jnp.concatenate([w_kr[:, hr:], w_kr[:, :hr]], axis=1), LANES)
    uq = w_uq.reshape(Q_LORA, A_HEADS, A_NOPE + A_ROPE)
    zeros = jnp.zeros((Q_LORA, A_HEADS, LANES - A_ROPE), w_uq.dtype)
    wq = jnp.concatenate([uq, zeros], axis=-1).reshape(Q_LORA, A_HEADS * 2 * LANES)
    ws = jnp.concatenate([uq[..., A_NOPE + hr:], uq[..., A_NOPE:A_NOPE + hr], zeros],
                         axis=-1).reshape(Q_LORA, A_HEADS * LANES)
    ukv = w_ukv.reshape(KV_LORA, A_HEADS, A_NOPE + A_VDIM)
    return dict(
        wm=w_in[:, :4 * M_W].astype(BF16),
        wg=_pad_cols(w_gate, LANES).astype(BF16),
        bg=_pad_cols(jnp.concatenate([b_ig, b_fg]).reshape(1, 2 * M_HEADS).astype(F32), LANES),
        wcq=w_cq.astype(BF16), wckv=w_ckv.astype(BF16),
        wkra=wkra.astype(BF16), wkrb=wkrb.astype(BF16),
        qn=q_norm.reshape(1, Q_LORA).astype(F32), kvn=kv_norm.reshape(1, KV_LORA).astype(F32),
        wq=wq.astype(BF16), ws=ws.astype(BF16),
        wuk=ukv[..., :A_NOPE].reshape(KV_LORA, A_HEADS * A_NOPE).astype(BF16),
        wuv=ukv[..., A_NOPE:].reshape(KV_LORA, A_HEADS * A_VDIM).astype(BF16),
    )


def _rope_tables(pos):
    half = A_ROPE // 2
    inv_freq = ROPE_THETA ** (-jnp.arange(half, dtype=F32) / half)
    ang = pos.astype(F32)[:, None] * inv_freq[None, :]
    cos, sin = jnp.cos(ang), jnp.sin(ang)
    pad = jnp.zeros((pos.shape[0], LANES - A_ROPE), F32)
    return (jnp.concatenate([cos, cos, pad], axis=1), jnp.concatenate([-sin, sin, pad], axis=1))


def _s5_weights(a_re, a_im, b_re, b_im, c_re, c_im, d, log_dt, w_glu):
    dt = jnp.exp(log_dt.astype(F32))[:, None]
    ar, ai = a_re.astype(F32), a_im.astype(F32)
    mag = jnp.exp(dt * ar)
    ab_r, ab_i = mag * jnp.cos(dt * ai), mag * jnp.sin(dt * ai)
    den = ar * ar + ai * ai
    c_r = ((ab_r - 1.0) * ar + ab_i * ai) / den
    c_i = (ab_i * ar - (ab_r - 1.0) * ai) / den
    br, bi = b_re.astype(F32), b_im.astype(F32)
    bb_r = c_r[..., None] * br - c_i[..., None] * bi
    bb_i = c_r[..., None] * bi + c_i[..., None] * br
    gb = S5_GROUPS // S5_BLOCKS
    eye = jnp.eye(gb, dtype=F32)
    bd_in = lambda m: jnp.einsum("kgpc,gh->kgchp", m.reshape(S5_BLOCKS, gb, S5_P, S5_GSIZE),
                                 eye).reshape(S5_BLOCKS, S5_BLK_IN, S5_BLK_ST)
    bd_out = lambda m: jnp.einsum("kgcp,gh->kgphc", m.reshape(S5_BLOCKS, gb, S5_GSIZE, S5_P),
                                  eye).reshape(S5_BLOCKS, S5_BLK_ST, S5_BLK_IN)
    return dict(
        bbr=bd_in(bb_r).astype(BF16), bbi=bd_in(bb_i).astype(BF16),
        ar=ab_r.reshape(1, S5_STATE), ai=ab_i.reshape(1, S5_STATE),
        ccr=bd_out(c_re.astype(F32)).astype(BF16), cci=bd_out(-c_im.astype(F32)).astype(BF16),
        d=d.reshape(1, S5_W).astype(F32), wglu=w_glu.astype(BF16),
    )


def _pad_keys(a, mult):
    pad = (-a.shape[1]) % mult
    return jnp.pad(a, ((0, 0), (0, pad), (0, 0))) if pad else a


def _trunk(x, pos0, cache, p):
    nb, l, _ = x.shape
    assert nb == SUBLANES, "the S5 layout puts the streams of one time step on the sublanes"
    assert l % CHUNK == 0
    n = nb * l
    tm = min(TOKEN_TILE, n)
    tl = min(tm, l)
    assert l % tl == 0 and tm % tl == 0
    past = 0 if cache is None else cache["sb_k"].shape[1]
    assert pos0 == past
    xf = x.reshape(n, D_MODEL)
    bf = lambda a: a.astype(BF16)
    ffn_w = (bf(p["ffn_w_gate"]), bf(p["ffn_w_up"]), bf(p["ffn_w_down"]))

    xf = _ffn(xf, p["norm_ffn"][0, 0], *ffn_w, (0, 0), tm=tm)
    ew = _even_weights(p["even_w_in"][0], p["mlstm_b_igate"][0], p["mlstm_b_fgate"][0],
                       p["mla_q_norm"][0], p["mla_kv_norm"][0], p["mla_w_uq"][0], p["mla_w_ukv"][0])
    cos, sin = _rope_tables(pos0 + jnp.arange(l))
    if tm > l:
        cos, sin = jnp.tile(cos, (tm // l, 1)), jnp.tile(sin, (tm // l, 1))
    qm, km, vm, om, gates, q, kn, v, kr128, lat, kr = _even_in(
        xf, p["norm_mix"][0], ew, cos, sin, tm=tm, seq_len=l)
    v_transposed = v.ndim == 3

    if cache is None:
        c0 = jnp.zeros((nb, M_HEADS, M_DH, M_DH), F32)
        n0 = jnp.zeros((nb, M_HEADS, 1, M_DH), F32)
        m0 = jnp.zeros((nb, M_HEADS, 1, LANES), F32)
    else:
        c0 = cache["mlstm_c"].astype(F32)
        n0 = cache["mlstm_n"].astype(F32).reshape(nb, M_HEADS, 1, M_DH)
        m0 = jnp.broadcast_to(cache["mlstm_m"].astype(F32)[..., None, None], (nb, M_HEADS, 1, LANES))
    r3 = lambda a: a.reshape(nb, l, a.shape[-1])
    h_m, c_new, n_new, m_new = _mlstm(r3(qm), r3(km), r3(vm), r3(om), r3(gates),
                                      p["mlstm_out_norm"][0], c0, n0, m0, tl=tl)

    mla_tq = min(MLA_TILE, -(-l // LANES) * LANES)
    if cache is None:
        mla_sub = mla_tq
        mla_tkb = min(2 * mla_tq, l)
    else:
        mla_sub = mla_tkb = -(-(past + max(l, mla_tq)) // LANES) * LANES
    kn3, kr3 = r3(kn), r3(kr128)
    v3, vt = (None, v) if v_transposed else (r3(v), None)
    if cache is not None:
        ckn, cv = _kv_expand(cache["mla_latent"].astype(F32).reshape(nb * past, KV_LORA),
                             ew["wuk"], ew["wuv"])
        ckr = jnp.pad(cache["mla_krope"], ((0, 0), (0, 0), (0, LANES - A_ROPE))).astype(BF16)
        kn3 = jnp.concatenate([ckn.reshape(nb, past, -1), kn3], axis=1)
        kr3 = jnp.concatenate([ckr, kr3], axis=1)
        cv3 = cv.reshape(nb, past, -1)
        if v_transposed:
            vt = jnp.concatenate([_transpose_tiles(cv3, math.gcd(past, TOKEN_TILE)), vt], axis=2)
        else:
            v3 = jnp.concatenate([cv3, v3], axis=1)
    if vt is None:
        vt = _transpose_tiles(_pad_keys(v3, mla_tkb), mla_sub)
    else:
        vt = jnp.pad(vt, ((0, 0), (0, 0), (0, (-vt.shape[2]) % mla_tkb)))
    h_a = _mla_attention(_pad_keys(r3(q), mla_tq), _pad_keys(kn3, mla_tkb), _pad_keys(kr3, mla_tkb),
                         vt, tq=mla_tq, tkb=mla_tkb, sub=mla_sub, offset=past)[:, :l]

    w_out = bf(p["even_w_out"][0])
    xf = _ffn(xf, p["norm_ffn"][0, 1], *ffn_w, (0, 1),
              mix=(h_m.reshape(n, M_W), h_a.reshape(n, -1), w_out[:M_W], w_out[M_W:]), tm=tm)

    xf = _ffn(xf, p["norm_ffn"][1, 0], *ffn_w, (1, 0), tm=tm)
    w_in = p["odd_w_in"][0]
    ow = dict(wu=bf(w_in[:, :S5_W]), wq=bf(w_in[:, S5_W:S5_W + SB_W]),
              wk=bf(w_in[:, S5_W + SB_W:S5_W + 2 * SB_W]), wv=bf(w_in[:, S5_W + 2 * SB_W:]))
    u, qs, ks, vs, ksb, vsb = _odd_in(xf, p["norm_mix"][1], ow, tm=tm)

    sw = _s5_weights(p["s5_A_re"][0], p["s5_A_im"][0], p["s5_B_re"][0], p["s5_B_im"][0],
                     p["s5_C_re"][0], p["s5_C_im"][0], p["s5_D"][0], p["s5_log_dt"][0], p["s5_w_glu"][0])
    if cache is None:
        x0r = jnp.zeros((nb, S5_STATE), F32)
        x0i = jnp.zeros((nb, S5_STATE), F32)
    else:
        x0r = cache["s5_re"].astype(F32).reshape(nb, S5_STATE)
        x0i = cache["s5_im"].astype(F32).reshape(nb, S5_STATE)
    h_s, xr_new, xi_new = _s5(r3(u), sw, x0r, x0i, tt=min(S5_TT, l))

    k3, v3 = r3(ksb), r3(vsb)
    if cache is not None:
        k3 = jnp.concatenate([bf(cache["sb_k"].reshape(nb, past, SB_W)), k3], axis=1)
        v3 = jnp.concatenate([bf(cache["sb_v"].reshape(nb, past, SB_W)), v3], axis=1)
    h_b = _sb_attention(r3(qs), _pad_keys(k3, SB_TILE), _pad_keys(v3, SB_TILE), tq=min(SB_TILE, l),
                        tk=SB_TILE, offset=past)

    w_out = bf(p["odd_w_out"][0])
    y = _ffn(xf, p["norm_ffn"][1, 1], *ffn_w, (1, 1),
             mix=(h_s.reshape(n, S5_W), h_b.reshape(n, SB_W), w_out[:S5_W], w_out[S5_W:]),
             final_gain=p["norm_final"], tm=tm)

    return (y.reshape(nb, l, D_MODEL),
            lat.reshape(1, nb, l, KV_LORA), kr.reshape(1, nb, l, A_ROPE),
            c_new[None], n_new.reshape(1, nb, M_HEADS, M_DH), m_new[:, :, 0, 0][None],
            xr_new.reshape(1, nb, S5_GROUPS, S5_P), xi_new.reshape(1, nb, S5_GROUPS, S5_P),
            ks.reshape(1, nb, l, SB_HEADS, SB_DH), vs.reshape(1, nb, l, SB_HEADS, SB_DH))


def kernel(x_prompt, x_sample, cache_mla_latent, cache_mla_krope, state_mlstm_C, state_mlstm_n, state_mlstm_m, state_s5_re, state_s5_im, cache_sb_k, cache_sb_v, norm_ffn, norm_mix, norm_final, ffn_w_gate, ffn_w_up, ffn_w_down, even_w_in, even_w_out, mlstm_b_igate, mlstm_b_fgate, mlstm_out_norm, mla_q_norm, mla_kv_norm, mla_w_uq, mla_w_ukv, odd_w_in, odd_w_out, s5_A_re, s5_A_im, s5_B_re, s5_B_im, s5_C_re, s5_C_im, s5_D, s5_log_dt, s5_w_glu):
    p = dict(norm_ffn=norm_ffn, norm_mix=norm_mix, norm_final=norm_final, ffn_w_gate=ffn_w_gate,
             ffn_w_up=ffn_w_up, ffn_w_down=ffn_w_down, even_w_in=even_w_in, even_w_out=even_w_out,
             mlstm_b_igate=mlstm_b_igate, mlstm_b_fgate=mlstm_b_fgate, mlstm_out_norm=mlstm_out_norm,
             mla_q_norm=mla_q_norm, mla_kv_norm=mla_kv_norm, mla_w_uq=mla_w_uq, mla_w_ukv=mla_w_ukv,
             odd_w_in=odd_w_in, odd_w_out=odd_w_out, s5_A_re=s5_A_re, s5_A_im=s5_A_im, s5_B_re=s5_B_re,
             s5_B_im=s5_B_im, s5_C_re=s5_C_re, s5_C_im=s5_C_im, s5_D=s5_D, s5_log_dt=s5_log_dt,
             s5_w_glu=s5_w_glu)
    past = cache_sb_k.shape[2]
    cache = dict(mla_latent=cache_mla_latent[0], mla_krope=cache_mla_krope[0],
                 mlstm_c=state_mlstm_C[0], mlstm_n=state_mlstm_n[0], mlstm_m=state_mlstm_m[0],
                 s5_re=state_s5_re[0], s5_im=state_s5_im[0],
                 sb_k=cache_sb_k[0], sb_v=cache_sb_v[0])
    out_p = _trunk(x_prompt, 0, None, p)
    out_s = _trunk(x_sample, past, cache, p)
    return (out_p[0], out_s[0]) + tuple(out_p[1:]) + tuple(out_s[1:])
```

```python
import functools
import math

import jax
import jax.numpy as jnp
import numpy as np
from jax import lax
from jax.experimental import pallas as pl
from jax.experimental.pallas import tpu as pltpu

F32 = jnp.float32
BF16 = jnp.bfloat16

D_MODEL = 1024
CHUNK = 64
D_FF = 2816
EPS = 1e-6
M_HEADS = 4
M_DH = 128
M_W = M_HEADS * M_DH
A_HEADS = 4
A_NOPE = 128
A_ROPE = 64
A_VDIM = 128
Q_LORA = 384
KV_LORA = 256
ROPE_THETA = 10000.0
MLA_SCALE = (A_NOPE + A_ROPE) ** -0.5
S5_W = 512
S5_GSIZE = 16
S5_GROUPS = S5_W // S5_GSIZE
S5_P = 64
S5_STATE = S5_GROUPS * S5_P
SB_HEADS = 8
SB_DH = 64
SB_W = SB_HEADS * SB_DH

LANES = 128
SUBLANES = 8
VMEM_LIMIT_BYTES = 56 * 1024 * 1024

TOKEN_TILE = 512
FFN_SUB_TILE = 512
FFN_TILES_PER_STEP = 2
FF_CHUNK = 256
MLA_TILE = 512
SB_TILE = 128
SB_TILES_PER_STEP = 4
S5_TT = 128
MLSTM_STREAMS = 4
SB_LOG_FLOOR = -24.0
LOG2E = math.log2(math.e)

HIGHEST = lax.Precision.HIGHEST


def _cparams(*sem):
    return pltpu.CompilerParams(dimension_semantics=sem, vmem_limit_bytes=VMEM_LIMIT_BYTES)


def _bdot(a, b):
    return jnp.dot(a.astype(BF16), b.astype(BF16), preferred_element_type=F32)


def _bdot_nt(a, b):
    return lax.dot_general(a.astype(BF16), b.astype(BF16), (((1,), (1,)), ((), ())),
                           preferred_element_type=F32)


def _rms(x, g):
    return x * lax.rsqrt(jnp.mean(x * x, axis=-1, keepdims=True) + EPS) * g


def _sigmoid(x):
    return 1.0 / (1.0 + jnp.exp(-x))


def _log_sigmoid(x):
    return jnp.minimum(x, 0.0) - jnp.log(1.0 + jnp.exp(-jnp.abs(x)))


def _const_spec(shape):
    nd = len(shape)
    return pl.BlockSpec(shape, lambda *_: (0,) * nd)


def _ffn_kernel(*refs, with_mix, with_final):
    it = iter(refs)
    x_ref = next(it)
    if with_mix:
        a_ref, b_ref, wa_ref, wb_ref = next(it), next(it), next(it), next(it)
    g_ref, wg_ref, wu_ref, wd_ref = next(it), next(it), next(it), next(it)
    gf_ref = next(it) if with_final else None
    o_ref = next(it)
    acc_ref = next(it)

    for r0 in range(0, x_ref.shape[0], FFN_SUB_TILE):
        rs = slice(r0, min(r0 + FFN_SUB_TILE, x_ref.shape[0]))
        x = x_ref[rs, :]
        if with_mix:
            x = x + _bdot(a_ref[rs, :], wa_ref[...]) + _bdot(b_ref[rs, :], wb_ref[...])
        h = _rms(x, g_ref[...]).astype(BF16)
        acc_ref[rs, :] = jnp.zeros((rs.stop - rs.start, D_MODEL), F32)
        for c in range(D_FF // FF_CHUNK):
            sl = slice(c * FF_CHUNK, (c + 1) * FF_CHUNK)
            g = jnp.dot(h, wg_ref[:, sl], preferred_element_type=F32)
            u = jnp.dot(h, wu_ref[:, sl], preferred_element_type=F32)
            a = (g * _sigmoid(g) * u).astype(BF16)
            acc_ref[rs, :] += jnp.dot(a, wd_ref[sl, :], preferred_element_type=F32)
        y = x + 0.5 * acc_ref[rs, :]
        if with_final:
            y = _rms(y, gf_ref[...])
        o_ref[rs, :] = y


def _ffn(x, gain, wg, wu, wd, which, *, mix=None, final_gain=None, tm):
    n = x.shape[0]
    if n % (FFN_TILES_PER_STEP * tm) == 0:
        tm = FFN_TILES_PER_STEP * tm
    row = lambda i: (i, 0)
    pick = lambda i: which + (0, 0)
    once = pl.Buffered(1)
    args, specs = [x], [pl.BlockSpec((tm, D_MODEL), row)]
    if mix is not None:
        a, b, wa, wb = mix
        args += [a, b, wa, wb]
        specs += [pl.BlockSpec((tm, a.shape[-1]), row), pl.BlockSpec((tm, b.shape[-1]), row),
                  pl.BlockSpec(wa.shape, lambda i: (0, 0), pipeline_mode=once),
                  pl.BlockSpec(wb.shape, lambda i: (0, 0), pipeline_mode=once)]
    args += [gain.reshape(1, D_MODEL), wg, wu, wd]
    specs += [_const_spec((1, D_MODEL))] + [
        pl.BlockSpec((None, None) + w.shape[2:], pick, pipeline_mode=once) for w in (wg, wu, wd)]
    if final_gain is not None:
        args.append(final_gain.reshape(1, D_MODEL))
        specs.append(_const_spec((1, D_MODEL)))
    return pl.pallas_call(
        functools.partial(_ffn_kernel, with_mix=mix is not None, with_final=final_gain is not None),
        grid=(n // tm,),
        in_specs=specs,
        out_specs=pl.BlockSpec((tm, D_MODEL), row),
        out_shape=jax.ShapeDtypeStruct((n, D_MODEL), F32),
        scratch_shapes=[pltpu.VMEM((tm, D_MODEL), F32)],
        compiler_params=_cparams("parallel"),
        name="ffn",
    )(*args)


def _even_in_kernel(x_ref, gain_ref, wm_ref, wside_ref, bg_ref, qn_ref, kvn_ref, wq_ref, ws_ref,
                    wuk_ref, wuv_ref, cos_ref, sin_ref,
                    qm_o, km_o, vm_o, om_o, gates_o, q_o, kn_o, v_o, kr128_o, lat_o, kr_o,
                    *, v_transposed):
    h = _rms(x_ref[...], gain_ref[...]).astype(BF16)
    m = jnp.dot(h, wm_ref[...], preferred_element_type=F32)
    qm_o[...] = m[:, :M_W].astype(BF16)
    km_o[...] = (m[:, M_W:2 * M_W] * M_DH ** -0.5).astype(BF16)
    vm_o[...] = m[:, 2 * M_W:3 * M_W].astype(BF16)
    om_o[...] = m[:, 3 * M_W:]

    side = jnp.dot(h, wside_ref[...], preferred_element_type=F32)
    o_cq, o_ckv, o_kra, o_krb = LANES, LANES + Q_LORA, LANES + Q_LORA + KV_LORA, 2 * LANES + Q_LORA + KV_LORA
    g = side[:, :LANES] + bg_ref[...]
    lane = lax.broadcasted_iota(jnp.int32, g.shape, 1)
    gates_o[...] = jnp.where(lane < M_HEADS, g, _log_sigmoid(g))

    cos, sin = cos_ref[...], sin_ref[...]
    cq = side[:, o_cq:o_ckv]
    cqn = _rms(cq, qn_ref[...]).astype(BF16)
    qf = jnp.dot(cqn, wq_ref[...], preferred_element_type=F32)
    qs = jnp.dot(cqn, ws_ref[...], preferred_element_type=F32)
    q_scale = MLA_SCALE * LOG2E
    for hh in range(A_HEADS):
        base = hh * 2 * LANES
        q_o[:, base:base + LANES] = (qf[:, base:base + LANES] * q_scale).astype(BF16)
        rot = qf[:, base + LANES:base + 2 * LANES] * cos + qs[:, hh * LANES:(hh + 1) * LANES] * sin
        q_o[:, base + LANES:base + 2 * LANES] = (rot * q_scale).astype(BF16)

    ckv = side[:, o_ckv:o_kra]
    lat = _rms(ckv, kvn_ref[...])
    lat_o[...] = lat
    latb = lat.astype(BF16)
    kn_o[...] = jnp.dot(latb, wuk_ref[...], preferred_element_type=F32).astype(BF16)
    if v_transposed:
        v_o[0] = _bdot_nt(wuv_ref[...], latb).astype(BF16)
    else:
        v_o[...] = jnp.dot(latb, wuv_ref[...], preferred_element_type=F32).astype(BF16)

    kr = side[:, o_kra:o_krb] * cos + side[:, o_krb:] * sin
    kr128_o[...] = kr.astype(BF16)
    kr_o[...] = kr[:, :A_ROPE]


def _even_in(x, gain, w, cos, sin, *, tm, seq_len):
    n = x.shape[0]
    row = lambda i: (i, 0)
    n_pos_tiles = cos.shape[0] // tm
    pos = lambda i: (i % n_pos_tiles, 0)
    v_transposed = seq_len % tm == 0
    n_time_tiles = max(seq_len // tm, 1)
    weights = [w["wm"], w["wside"], w["bg"], w["qn"], w["kvn"],
               w["wq"], w["ws"], w["wuk"], w["wuv"].T if v_transposed else w["wuv"]]
    in_specs = ([pl.BlockSpec((tm, D_MODEL), row), _const_spec((1, D_MODEL))]
                + [_const_spec(a.shape) for a in weights]
                + [pl.BlockSpec((tm, LANES), pos), pl.BlockSpec((tm, LANES), pos)])
    outs = [(M_W, BF16), (M_W, BF16), (M_W, BF16), (M_W, F32), (LANES, F32),
            (A_HEADS * 2 * LANES, BF16), (A_HEADS * A_NOPE, BF16), (A_HEADS * A_VDIM, BF16),
            (LANES, BF16), (KV_LORA, F32), (A_ROPE, F32)]
    out_specs = [pl.BlockSpec((tm, c), row) for c, _ in outs]
    out_shape = [jax.ShapeDtypeStruct((n, c), dt) for c, dt in outs]
    if v_transposed:
        v_idx = 7
        out_specs[v_idx] = pl.BlockSpec((1, A_HEADS * A_VDIM, tm),
                                        lambda i: (i // n_time_tiles, 0, i % n_time_tiles))
        out_shape[v_idx] = jax.ShapeDtypeStruct((n // seq_len, A_HEADS * A_VDIM, seq_len), BF16)
    return pl.pallas_call(
        functools.partial(_even_in_kernel, v_transposed=v_transposed),
        grid=(n // tm,),
        in_specs=in_specs,
        out_specs=out_specs,
        out_shape=out_shape,
        compiler_params=_cparams("parallel"),
        name="even_in",
    )(x, gain.reshape(1, D_MODEL), *weights, cos, sin)


def _kv_expand_kernel(lat_ref, wuk_ref, wuv_ref, kn_o, v_o):
    latb = lat_ref[...].astype(BF16)
    kn_o[...] = jnp.dot(latb, wuk_ref[...], preferred_element_type=F32).astype(BF16)
    v_o[...] = jnp.dot(latb, wuv_ref[...], preferred_element_type=F32).astype(BF16)


def _kv_expand(lat, wuk, wuv):
    n = lat.shape[0]
    tm = min(4 * TOKEN_TILE, n)
    assert n % tm == 0
    row = lambda i: (i, 0)
    return pl.pallas_call(
        _kv_expand_kernel,
        grid=(n // tm,),
        in_specs=[pl.BlockSpec((tm, KV_LORA), row), _const_spec(wuk.shape), _const_spec(wuv.shape)],
        out_specs=[pl.BlockSpec((tm, wuk.shape[1]), row), pl.BlockSpec((tm, wuv.shape[1]), row)],
        out_shape=[jax.ShapeDtypeStruct((n, wuk.shape[1]), BF16),
                   jax.ShapeDtypeStruct((n, wuv.shape[1]), BF16)],
        compiler_params=_cparams("parallel"),
        name="kv_expand",
    )(lat, wuk, wuv)


def _mlstm_kernel(qm_ref, km_ref, vm_ref, om_ref, g_ref, onorm_ref, c0_ref, n0_ref, m0_ref,
                  h_ref, cout_ref, nout_ref, mout_ref, c_sc, n_sc, m_sc, *, n_chunks, n_streams):
    lt = pl.program_id(1)

    @pl.when(lt == 0)
    def _():
        c_sc[...] = c0_ref[...]
        n_sc[...] = n0_ref[...]
        m_sc[...] = m0_ref[...]

    row = lax.broadcasted_iota(jnp.int32, (CHUNK, CHUNK), 0)
    col = lax.broadcasted_iota(jnp.int32, (CHUNK, CHUNK), 1)
    causal = col <= row
    ltri = causal.astype(BF16)
    utri = (row <= col).astype(BF16)
    ones_kl = jnp.ones((CHUNK, LANES), BF16)

    def split3(a):
        hi = a.astype(BF16)
        r1 = a - hi.astype(F32)
        mid = r1.astype(BF16)
        return hi, mid, (r1 - mid.astype(F32)).astype(BF16)

    def chunk(c, carry):
        r0 = pl.multiple_of(c * CHUNK, CHUNK)
        rows = pl.ds(r0, CHUNK)
        chains = [(bb, hh) for bb in range(n_streams) for hh in range(M_HEADS)]
        hsl = lambda hh: slice(hh * M_DH, (hh + 1) * M_DH)
        gate = {}
        for bb in range(n_streams):
            gates = g_ref[bb, rows, :]
            gates_t = gates.T
            cc = jnp.dot(ltri, jnp.concatenate(split3(gates), axis=1), preferred_element_type=F32)
            cum_col = cc[:, :LANES] + cc[:, LANES:2 * LANES] + cc[:, 2 * LANES:]
            cr = jnp.dot(jnp.concatenate(split3(gates_t), axis=0), utri, preferred_element_type=F32)
            cum_row = cr[:LANES] + cr[LANES:2 * LANES] + cr[2 * LANES:]
            gate[bb] = (gates, gates_t, cum_col, cum_row)
        qk, qc, qn = {}, {}, {}
        for ch in chains:
            bb, hh = ch
            rhs = jnp.concatenate([c_sc[bb, hh].astype(BF16),
                                   jnp.broadcast_to(n_sc[bb, hh], (LANES, M_DH)).astype(BF16),
                                   km_ref[bb, rows, hsl(hh)]], axis=0)
            r = _bdot_nt(qm_ref[bb, rows, hsl(hh)], rhs)
            qc[ch], qn[ch], qk[ch] = r[:, :M_DH], r[:, M_DH:M_DH + LANES], r[:, M_DH + LANES:]
        st = {}
        for ch in chains:
            bb, hh = ch
            gates, gates_t, cum_col, cum_row = gate[bb]
            ig_row = gates_t[hh:hh + 1, :]
            ig_col = gates[:, hh:hh + 1]
            b_row = cum_row[M_HEADS + hh:M_HEADS + hh + 1, :]
            b_col = cum_col[:, M_HEADS + hh:M_HEADS + hh + 1]
            m_prev = m_sc[bb, hh][:, :1]
            b_lanes = jnp.broadcast_to(b_col, (CHUNK, LANES))
            log_d = jnp.where(causal, b_lanes[:, :CHUNK] - b_row + ig_row, -jnp.inf)
            m_t = jnp.maximum(b_col + m_prev, jnp.max(log_d, axis=-1, keepdims=True))
            m_lanes = jnp.broadcast_to(m_t, (CHUNK, LANES))
            s = (qk[ch] * jnp.exp(log_d - m_lanes[:, :CHUNK])).astype(BF16)
            w_inter = jnp.exp(b_lanes + m_prev - m_lanes)
            m_new = m_t[CHUNK - 1:CHUNK, :]
            b_last = b_col[CHUNK - 1:CHUNK, :]
            w_s = jnp.exp(b_last - b_col + ig_col - m_new)
            decay = jnp.exp(b_last + m_prev - m_new)
            r = jnp.dot(s, jnp.concatenate([vm_ref[bb, rows, hsl(hh)], ones_kl], axis=1),
                        preferred_element_type=F32)
            st[ch] = (r[:, :M_DH], r[:, M_DH:], w_inter, m_lanes, m_new, w_s, decay)
        for ch in chains:
            bb, hh = ch
            sv, s_sum, w_inter, m_lanes, m_new, w_s, decay = st[ch]
            k = km_ref[bb, rows, hsl(hh)]
            v = vm_ref[bb, rows, hsl(hh)]
            c_h = c_sc[bb, hh]
            n_h = n_sc[bb, hh]
            num = w_inter * qc[ch] + sv
            den = w_inter * qn[ch] + s_sum
            hv = num / jnp.maximum(jnp.abs(den), jnp.exp(-m_lanes))
            wv_t = (jnp.broadcast_to(w_s, (CHUNK, M_DH)) * v).T
            c_sc[bb, hh] = decay * c_h + _bdot(wv_t, k)
            n_sc[bb, hh] = decay * n_h + jnp.sum(w_s * k.astype(F32), axis=0, keepdims=True)
            m_sc[bb, hh] = jnp.broadcast_to(m_new, (1, LANES))
            hn = _rms(hv, onorm_ref[:, hsl(hh)])
            h_ref[bb, rows, hsl(hh)] = (_sigmoid(om_ref[bb, rows, hsl(hh)]) * hn).astype(BF16)
        return carry

    lax.fori_loop(0, n_chunks, chunk, 0)

    @pl.when(lt == pl.num_programs(1) - 1)
    def _():
        cout_ref[...] = c_sc[...]
        nout_ref[...] = n_sc[...]
        mout_ref[...] = m_sc[...]


def _mlstm(qm, km, vm, om, gates, onorm, c0, n0, m0, *, tl):
    b, l, _ = qm.shape
    ns = MLSTM_STREAMS
    tile = lambda i, j: (i, j, 0)
    st4 = lambda i, j: (i, 0, 0, 0)
    state = [((ns, M_HEADS, M_DH, M_DH), (b, M_HEADS, M_DH, M_DH)),
             ((ns, M_HEADS, 1, M_DH), (b, M_HEADS, 1, M_DH)),
             ((ns, M_HEADS, 1, LANES), (b, M_HEADS, 1, LANES))]
    return pl.pallas_call(
        functools.partial(_mlstm_kernel, n_chunks=tl // CHUNK, n_streams=ns),
        grid=(b // ns, l // tl),
        in_specs=[pl.BlockSpec((ns, tl, M_W), tile)] * 4
        + [pl.BlockSpec((ns, tl, LANES), tile), _const_spec((1, M_W))]
        + [pl.BlockSpec(blk, st4) for blk, _ in state],
        out_specs=[pl.BlockSpec((ns, tl, M_W), tile)] + [pl.BlockSpec(blk, st4) for blk, _ in state],
        out_shape=[jax.ShapeDtypeStruct((b, l, M_W), BF16)]
        + [jax.ShapeDtypeStruct(full, F32) for _, full in state],
        scratch_shapes=[pltpu.VMEM(blk, F32) for blk, _ in state],
        compiler_params=_cparams("parallel", "arbitrary"),
        name="mlstm",
    )(qm, km, vm, om, gates, onorm.reshape(1, M_W), c0, n0, m0)


MLA_SUM_ROWS = 16


MLA_EXP_RANGE = 60.0


def _mla_kernel(qi_ref, kj_ref, flag_ref, q_ref, kn_ref, kr_ref, vt_ref, o_ref, m_sc, acc_sc, bad_sc,
                redo_sm, *, tq, tkb, offset, variants):
    p = pl.program_id(1)
    i = qi_ref[p]
    j = kj_ref[p]
    flags = flag_ref[p]

    @pl.when(j == 0)
    def _():
        m_sc[...] = jnp.zeros_like(m_sc)
        acc_sc[...] = jnp.zeros_like(acc_sc)

    def score_tiles(nk, masked):
        kr = kr_ref[0, :nk, :]
        scores = []
        for hh in range(A_HEADS):
            q = q_ref[0, :, hh * 2 * LANES:(hh + 1) * 2 * LANES]
            k = jnp.concatenate([kn_ref[0, :nk, hh * A_NOPE:(hh + 1) * A_NOPE], kr], axis=-1)
            scores.append(_bdot_nt(k, q))
        if masked:
            k_chunk = (j * tkb + lax.broadcasted_iota(jnp.int32, (nk, tq), 0)) // CHUNK
            q_chunk = (offset + i * tq + lax.broadcasted_iota(jnp.int32, (nk, tq), 1)) // CHUNK
            visible = k_chunk <= q_chunk
            scores = [jnp.where(visible, s, -jnp.inf) for s in scores]
        return scores

    def values_t(hh, nk):
        return jnp.concatenate([vt_ref[0, hh * A_VDIM:(hh + 1) * A_VDIM, :nk],
                                jnp.ones((MLA_SUM_ROWS, nk), BF16)], axis=0)

    def add_block(nk, masked):
        scores = score_tiles(nk, masked)
        any_bad = jnp.zeros((1, tq), F32)
        for hh in range(A_HEADS):
            s = scores[hh]
            m_ref = m_sc[hh]
            d = jnp.max(s, axis=0, keepdims=True) - m_ref
            ok = jnp.logical_and(d <= MLA_EXP_RANGE, jnp.logical_or(d >= -MLA_EXP_RANGE, j > 0))
            pr = jnp.exp2(s - m_ref).astype(BF16)
            acc = acc_sc[hh]
            acc_sc[hh] = jnp.where(ok, acc + jnp.dot(values_t(hh, nk), pr, preferred_element_type=F32),
                                   acc)
            bad = jnp.where(ok, 0.0, 1.0)
            bad_sc[hh] = bad
            any_bad = jnp.maximum(any_bad, bad)
        redo_sm[0] = (jnp.max(any_bad) > 0.0).astype(jnp.int32)

    def redo_block(nk, masked):
        scores = score_tiles(nk, masked)
        for hh in range(A_HEADS):
            s = scores[hh]
            bad = bad_sc[hh] > 0.0
            m_prev = jnp.where(j == 0, -jnp.inf, m_sc[hh])
            m_new = jnp.maximum(m_prev, jnp.max(s, axis=0, keepdims=True))
            alpha = jnp.exp2(m_prev - m_new)
            pr = jnp.exp2(s - m_new).astype(BF16)
            acc = acc_sc[hh]
            acc_sc[hh] = jnp.where(
                bad, alpha * acc + jnp.dot(values_t(hh, nk), pr, preferred_element_type=F32), acc)
            m_sc[hh] = jnp.where(bad, m_new, m_sc[hh])

    for vi, (nk, masked) in enumerate(variants):
        pl.when((flags >> 1) == vi)(functools.partial(add_block, nk, masked))
    for vi, (nk, masked) in enumerate(variants):
        pl.when(jnp.logical_and((flags >> 1) == vi, redo_sm[0] != 0))(
            functools.partial(redo_block, nk, masked))

    @pl.when((flags & 1) != 0)
    def _():
        for hh in range(A_HEADS):
            acc = acc_sc[hh]
            o_ref[0, :, hh * A_VDIM:(hh + 1) * A_VDIM] = (
                acc[:A_VDIM] / acc[A_VDIM:A_VDIM + 1]).T.astype(BF16)


def _transpose_kernel(x_ref, o_ref):
    o_ref[0] = x_ref[0].astype(F32).T.astype(o_ref.dtype)


def _transpose_tiles(x, t):
    b, l, c = x.shape
    return pl.pallas_call(
        _transpose_kernel,
        grid=(b, l // t),
        in_specs=[pl.BlockSpec((1, t, c), lambda bb, i: (bb, i, 0))],
        out_specs=pl.BlockSpec((1, c, t), lambda bb, i: (bb, 0, i)),
        out_shape=jax.ShapeDtypeStruct((b, c, l), x.dtype),
        compiler_params=_cparams("parallel", "parallel"),
        name="transpose_tiles",
    )(x)


def _mla_pairs(lq, lk, tq, tkb, sub, offset):
    qi, kj, flags, variants = [], [], [], []
    for i in range(lq // tq):
        first_row, last_row = offset + i * tq, offset + (i + 1) * tq - 1
        k_end = min((last_row // CHUNK + 1) * CHUNK, lk)
        k_all = (first_row // CHUNK + 1) * CHUNK
        nj = -(-k_end // tkb)
        for j in range(nj):
            nk = -(-(min(k_end, (j + 1) * tkb) - j * tkb) // sub) * sub
            variant = (nk, j * tkb + nk > k_all)
            if variant not in variants:
                variants.append(variant)
            qi.append(i)
            kj.append(j)
            flags.append(int(j == nj - 1) + 2 * variants.index(variant))
    return (np.asarray(qi, np.int32), np.asarray(kj, np.int32), np.asarray(flags, np.int32),
            tuple(variants))


def _mla_attention(q, kn, kr, vt, *, tq, tkb, sub, offset):
    b, lq, _ = q.shape
    lk = kn.shape[1]
    assert lk % tkb == 0 and tkb % sub == 0 and vt.shape[2] == lk
    qi, kj, flags, variants = _mla_pairs(lq, lk, tq, tkb, sub, offset)
    qmap = lambda bb, p, qi_r, kj_r, flag_r: (bb, qi_r[p], 0)
    kmap = lambda bb, p, qi_r, kj_r, flag_r: (bb, kj_r[p], 0)
    grid_spec = pltpu.PrefetchScalarGridSpec(
        num_scalar_prefetch=3,
        grid=(b, len(qi)),
        in_specs=[pl.BlockSpec((1, tq, q.shape[-1]), qmap),
                  pl.BlockSpec((1, tkb, kn.shape[-1]), kmap),
                  pl.BlockSpec((1, tkb, kr.shape[-1]), kmap),
                  pl.BlockSpec((1, vt.shape[1], tkb), lambda bb, p, qi_r, kj_r, flag_r: (bb, 0, kj_r[p]))],
        out_specs=pl.BlockSpec((1, tq, A_HEADS * A_VDIM), qmap),
        scratch_shapes=[pltpu.VMEM((A_HEADS, 1, tq), F32),
                        pltpu.VMEM((A_HEADS, A_VDIM + MLA_SUM_ROWS, tq), F32),
                        pltpu.VMEM((A_HEADS, 1, tq), F32),
                        pltpu.SMEM((1,), jnp.int32)],
    )
    return pl.pallas_call(
        functools.partial(_mla_kernel, tq=tq, tkb=tkb, offset=offset, variants=variants),
        grid_spec=grid_spec,
        out_shape=jax.ShapeDtypeStruct((b, lq, A_HEADS * A_VDIM), BF16),
        compiler_params=_cparams("parallel", "arbitrary"),
        name="mla_attention",
    )(jnp.asarray(qi), jnp.asarray(kj), jnp.asarray(flags), q, kn, kr, vt)


def _odd_in_kernel(x_ref, gain_ref, wu_ref, wq_ref, wk_ref, wv_ref, u_o, q_o, k_o, v_o, kb_o, vb_o):
    h = _rms(x_ref[...], gain_ref[...]).astype(BF16)
    u_o[...] = jnp.dot(h, wu_ref[...], preferred_element_type=F32)
    q_o[...] = (jnp.dot(h, wq_ref[...], preferred_element_type=F32) * (SB_DH ** -0.5 * LOG2E)).astype(BF16)
    k = jnp.dot(h, wk_ref[...], preferred_element_type=F32)
    v = jnp.dot(h, wv_ref[...], preferred_element_type=F32)
    k_o[...] = pltpu.einshape("t(hd)->thd", k, h=SB_HEADS)
    v_o[...] = pltpu.einshape("t(hd)->thd", v, h=SB_HEADS)
    kb_o[...] = k.astype(BF16)
    vb_o[...] = v.astype(BF16)


def _odd_in(x, gain, w, *, tm):
    n = x.shape[0]
    row = lambda i: (i, 0)
    weights = [w["wu"], w["wq"], w["wk"], w["wv"]]
    return pl.pallas_call(
        _odd_in_kernel,
        grid=(n // tm,),
        in_specs=[pl.BlockSpec((tm, D_MODEL), row), _const_spec((1, D_MODEL))]
        + [_const_spec(a.shape) for a in weights],
        out_specs=[pl.BlockSpec((tm, S5_W), row), pl.BlockSpec((tm, SB_W), row)]
        + [pl.BlockSpec((tm, SB_HEADS, SB_DH), lambda i: (i, 0, 0))] * 2
        + [pl.BlockSpec((tm, SB_W), row)] * 2,
        out_shape=[jax.ShapeDtypeStruct((n, S5_W), F32),
                   jax.ShapeDtypeStruct((n, SB_W), BF16),
                   jax.ShapeDtypeStruct((n, SB_HEADS, SB_DH), F32),
                   jax.ShapeDtypeStruct((n, SB_HEADS, SB_DH), F32),
                   jax.ShapeDtypeStruct((n, SB_W), BF16),
                   jax.ShapeDtypeStruct((n, SB_W), BF16)],
        compiler_params=_cparams("parallel"),
        name="odd_in",
    )(x, gain.reshape(1, D_MODEL), *weights)


S5_BLOCKS = 4
S5_BLK_IN = S5_W // S5_BLOCKS
S5_BLK_ST = S5_STATE // S5_BLOCKS


def _s5_kernel(u_ref, bbr_ref, bbi_ref, ar_ref, ai_ref, ccr_ref, cci_ref, d_ref, wglu_ref,
               x0r_ref, x0i_ref, hs_o, xr_o, xi_o, u_tm, hs_tm, xr_buf, xi_buf, xr_st, xi_st,
               *, tt, nb):
    t = pl.program_id(0)

    @pl.when(t == 0)
    def _():
        xr_st[...] = x0r_ref[...]
        xi_st[...] = x0i_ref[...]

    for bb in range(nb):
        for k in range(S5_BLOCKS):
            u_tm[k, pl.ds(bb, tt, stride=nb), :] = u_ref[bb, :, k * S5_BLK_IN:(k + 1) * S5_BLK_IN]

    for k in range(S5_BLOCKS):
        cs = slice(k * S5_BLK_ST, (k + 1) * S5_BLK_ST)
        ub = u_tm[k].astype(BF16)
        xr_buf[:, cs] = jnp.dot(ub, bbr_ref[k], preferred_element_type=F32)
        xi_buf[:, cs] = jnp.dot(ub, bbi_ref[k], preferred_element_type=F32)

    for k in range(0, S5_BLOCKS, 2):
        cols = [slice((k + d) * S5_BLK_ST, (k + d + 1) * S5_BLK_ST) for d in range(2)]
        decay = [(jnp.broadcast_to(ar_ref[:, cs], (nb, S5_BLK_ST)),
                  jnp.broadcast_to(ai_ref[:, cs], (nb, S5_BLK_ST))) for cs in cols]

        def step(s, carry, cols=cols, decay=decay):
            rows = pl.ds(pl.multiple_of(s * nb, nb), nb)
            out = []
            for (xr, xi), (ar, ai), cs in zip(carry, decay, cols):
                nxr = ar * xr - ai * xi + xr_buf[rows, cs]
                nxi = ar * xi + ai * xr + xi_buf[rows, cs]
                xr_buf[rows, cs] = nxr
                xi_buf[rows, cs] = nxi
                out.append((nxr, nxi))
            return tuple(out)

        final = lax.fori_loop(0, tt, step, tuple((xr_st[:, cs], xi_st[:, cs]) for cs in cols), unroll=8)
        for (xr, xi), cs in zip(final, cols):
            xr_st[:, cs] = xr
            xi_st[:, cs] = xi

    gs = []
    for k in range(S5_BLOCKS):
        cs = slice(k * S5_BLK_ST, (k + 1) * S5_BLK_ST)
        y = (_bdot(xr_buf[:, cs], ccr_ref[k]) + _bdot(xi_buf[:, cs], cci_ref[k])
             + d_ref[:, k * S5_BLK_IN:(k + 1) * S5_BLK_IN] * u_tm[k])
        gs.append(jax.nn.gelu(y, approximate=True))
    g = jnp.concatenate(gs, axis=-1)
    hs = g * _sigmoid(_bdot(g, wglu_ref[...]))
    for k in range(S5_BLOCKS):
        hs_tm[k] = hs[:, k * S5_BLK_IN:(k + 1) * S5_BLK_IN]
    for bb in range(nb):
        for k in range(S5_BLOCKS):
            hs_o[bb, :, k * S5_BLK_IN:(k + 1) * S5_BLK_IN] = (
                hs_tm[k, pl.ds(bb, tt, stride=nb), :].astype(BF16))

    @pl.when(t == pl.num_programs(0) - 1)
    def _():
        xr_o[...] = xr_st[...]
        xi_o[...] = xi_st[...]


def _s5(u, w, x0r, x0i, *, tt):
    nb, l, _ = u.shape
    blk = lambda t: (0, t, 0)
    consts = [w["bbr"], w["bbi"], w["ar"], w["ai"], w["ccr"], w["cci"], w["d"], w["wglu"], x0r, x0i]
    return pl.pallas_call(
        functools.partial(_s5_kernel, tt=tt, nb=nb),
        grid=(l // tt,),
        in_specs=[pl.BlockSpec((nb, tt, S5_W), blk)] + [_const_spec(a.shape) for a in consts],
        out_specs=[pl.BlockSpec((nb, tt, S5_W), blk), _const_spec((nb, S5_STATE)),
                   _const_spec((nb, S5_STATE))],
        out_shape=[jax.ShapeDtypeStruct((nb, l, S5_W), BF16),
                   jax.ShapeDtypeStruct((nb, S5_STATE), F32),
                   jax.ShapeDtypeStruct((nb, S5_STATE), F32)],
        scratch_shapes=[pltpu.VMEM((S5_BLOCKS, tt * nb, S5_BLK_IN), F32),
                        pltpu.VMEM((S5_BLOCKS, tt * nb, S5_BLK_IN), F32),
                        pltpu.VMEM((tt * nb, S5_STATE), F32), pltpu.VMEM((tt * nb, S5_STATE), F32),
                        pltpu.VMEM((nb, S5_STATE), F32), pltpu.VMEM((nb, S5_STATE), F32)],
        compiler_params=_cparams("arbitrary"),
        name="s5",
    )(u, *consts)


def _sb_kernel(q_ref, k_ref, v_ref, o_ref, r_sc, acc_sc, qh_sc, *, tq, tk, offset, n_sub):
    for su in range(n_sub):
        _sb_query_tile(pl.program_id(1) * n_sub + su, slice(su * tq, (su + 1) * tq),
                       q_ref, k_ref, v_ref, o_ref, r_sc, acc_sc, qh_sc, tq=tq, tk=tk, offset=offset)


def _sb_query_tile(i, rs, q_ref, k_ref, v_ref, o_ref, r_sc, acc_sc, qh_sc, *, tq, tk, offset):
    n_pairs = SB_HEADS // 2
    half = lax.broadcasted_iota(jnp.int32, (1, LANES), 1) // SB_DH
    for pr in range(n_pairs):
        qq = q_ref[0, rs, pr * LANES:(pr + 1) * LANES]
        qh_sc[pr, :tq] = jnp.where(half == 0, qq, jnp.zeros_like(qq))
        qh_sc[pr, tq:] = jnp.where(half == 1, qq, jnp.zeros_like(qq))
    r_sc[...] = jnp.zeros_like(r_sc)
    acc_sc[...] = jnp.zeros_like(acc_sc)

    hk = tk // 2
    jrow = lax.broadcasted_iota(jnp.int32, (tk, tk), 0)
    jcol = lax.broadcasted_iota(jnp.int32, (tk, tk), 1)
    later = (jrow > jcol).astype(BF16)
    later_pk = jnp.logical_and(jrow > jcol, (jrow // hk) == (jcol // hk)).astype(BF16)
    first_half = lax.broadcasted_iota(jnp.int32, (1, tk), 1) < hk
    j_diag = (offset + i * tq) // tk

    def softplus2(z):
        return jnp.maximum(z, 0.0) + jnp.log2(1.0 + jnp.exp2(-jnp.abs(z)))

    def split2(a):
        hi = a.astype(BF16)
        return [hi, (a - hi.astype(F32)).astype(BF16)]

    def by_head(x):
        zero = jnp.zeros_like(x)
        return [jnp.where(half == 0, x, zero), jnp.where(half == 1, x, zero)]

    def sweep(tiles):
        zs, vv, keep_key = {}, {}, {}
        for t, (j, kind, live, limit) in enumerate(tiles):
            k0 = pl.multiple_of(j * tk, tk)
            if kind == "masked":
                q_pos = offset + i * tq + lax.broadcasted_iota(jnp.int32, (tq, tk), 0)
                keep_key[t] = (k0 + lax.broadcasted_iota(jnp.int32, (tq, tk), 1)) < q_pos
            elif limit is not None:
                keep_key[t] = lax.broadcasted_iota(jnp.int32, (tq, tk), 1) < limit
            for pr in range(n_pairs):
                ps = slice(pr * LANES, (pr + 1) * LANES)
                if kind == "half":
                    kh = k_ref[0, pl.ds(k0 + hk, hk), ps]
                    zero = jnp.zeros_like(kh)
                    k2 = jnp.concatenate([jnp.concatenate([kh, zero], axis=1),
                                          jnp.concatenate([zero, kh], axis=1)], axis=0)
                    q2 = jnp.concatenate([qh_sc[pr, :tq], qh_sc[pr, tq:]], axis=1)
                    zs[t, pr] = _bdot_nt(q2, k2)
                    vv[t, pr] = jnp.concatenate(by_head(v_ref[0, pl.ds(k0 + hk, hk), ps]), axis=0)
                else:
                    zs[t, pr] = _bdot_nt(qh_sc[pr], k_ref[0, pl.ds(k0, tk), ps])
                    vv[t, pr] = jnp.concatenate(by_head(v_ref[0, pl.ds(k0, tk), ps]), axis=0)
        l1s, lbs, cums = {}, {}, {}
        for t, (j, kind, live, limit) in enumerate(tiles):
            parts, slots = [], []
            for pr in range(n_pairs):
                for sub in range(1 if kind == "half" else 2):
                    slot = pr if kind == "half" else 2 * pr + sub
                    z = zs[t, pr] if kind == "half" else zs[t, pr][sub * tq:(sub + 1) * tq]
                    sp = softplus2(z)
                    l1 = -sp
                    if t in keep_key:
                        l1 = jnp.where(keep_key[t], l1, 0.0)
                    parts += split2(l1)
                    slots.append(slot)
                    l1s[t, slot], lbs[t, slot] = l1, z - sp
            c = jnp.dot(jnp.concatenate(parts, axis=0), later_pk if kind == "half" else later,
                        preferred_element_type=F32)
            for n, slot in enumerate(slots):
                cums[t, slot] = c[2 * n * tq:(2 * n + 1) * tq] + c[(2 * n + 1) * tq:(2 * n + 2) * tq]
        for t, (j, kind, live, limit) in enumerate(tiles):
            r_max = jnp.full((1, 1), -jnp.inf, F32)
            dead_shift = 0.0 if live is None else jnp.where(live, 0.0, -1e30)
            keep = 1.0 if live is None else jnp.where(live, 1.0, 0.0)
            for pr in range(n_pairs):
                if kind == "half":
                    r0, r1 = r_sc[2 * pr], r_sc[2 * pr + 1]
                    log_w = lbs[t, pr] + cums[t, pr] + (jnp.where(first_half, r0, r1) + dead_shift)
                    w = jnp.exp2(log_w).astype(BF16)
                    s_all = jnp.sum(l1s[t, pr], axis=-1, keepdims=True)
                    s0 = jnp.sum(jnp.where(first_half, l1s[t, pr], 0.0), axis=-1, keepdims=True)
                    r_new = [r0 + keep * s0, r1 + keep * (s_all - s0)]
                else:
                    ws, r_new = [], []
                    for sub in range(2):
                        hh = 2 * pr + sub
                        r_old = r_sc[hh]
                        log_w = lbs[t, hh] + cums[t, hh] + (r_old + dead_shift)
                        if t in keep_key:
                            log_w = jnp.where(keep_key[t], log_w, -jnp.inf)
                        ws.append(jnp.exp2(log_w).astype(BF16))
                        r_new.append(r_old + keep * jnp.sum(l1s[t, hh], axis=-1, keepdims=True))
                    w = jnp.concatenate(ws, axis=1)
                for sub in range(2):
                    r_sc[2 * pr + sub] = r_new[sub]
                    r_max = jnp.maximum(r_max, jnp.max(r_new[sub], axis=0, keepdims=True))
                acc_sc[:, pr * LANES:(pr + 1) * LANES] += jnp.dot(w, vv[t, pr], preferred_element_type=F32)
        return r_max[0, 0]

    has_prev = j_diag >= 1
    r_first = sweep([(j_diag, "masked", None, None), (jnp.maximum(j_diag - 1, 0), "half", has_prev, None)])

    def cond(carry):
        j, r_max, _ = carry
        return jnp.logical_and(j >= 0, r_max > SB_LOG_FLOOR * LOG2E)

    def body(carry):
        j, _, limit = carry
        return j - 1, sweep([(j, "full", None, limit)]), jnp.int32(tk)

    lax.while_loop(cond, body, (j_diag - 1, r_first, jnp.int32(hk)))
    o_ref[0, rs] = acc_sc[...].astype(BF16)


def _sb_attention(q, k, v, *, tq, tk, offset):
    b, lq, _ = q.shape
    lk = k.shape[1]
    assert tk % tq == 0 and offset % tk == 0 and lk % tk == 0
    n_sub = math.gcd(lq // tq, SB_TILES_PER_STEP)
    return pl.pallas_call(
        functools.partial(_sb_kernel, tq=tq, tk=tk, offset=offset, n_sub=n_sub),
        grid=(b, lq // (tq * n_sub)),
        in_specs=[pl.BlockSpec((1, tq * n_sub, SB_W), lambda bb, i: (bb, i, 0)),
                  pl.BlockSpec((1, lk, SB_W), lambda bb, i: (bb, 0, 0)),
                  pl.BlockSpec((1, lk, SB_W), lambda bb, i: (bb, 0, 0))],
        out_specs=pl.BlockSpec((1, tq * n_sub, SB_W), lambda bb, i: (bb, i, 0)),
        out_shape=jax.ShapeDtypeStruct((b, lq, SB_W), BF16),
        scratch_shapes=[pltpu.VMEM((SB_HEADS, tq, 1), F32),
                        pltpu.VMEM((tq, SB_W), F32),
                        pltpu.VMEM((SB_HEADS // 2, 2 * tq, LANES), BF16)],
        compiler_params=_cparams("parallel", "arbitrary"),
        name="sb_attention",
    )(q, k, v)


def _pad_cols(w, width):
    return jnp.pad(w, ((0, 0), (0, width - w.shape[1])))


def _even_weights(w_in, b_ig, b_fg, q_norm, kv_norm, w_uq, w_ukv):
    o = 4 * M_W
    w_gate = w_in[:, o:o + 2 * M_HEADS]
    o += 2 * M_HEADS
    w_cq = w_in[:, o:o + Q_LORA]
    o += Q_LORA
    w_ckv = w_in[:, o:o + KV_LORA]
    o += KV_LORA
    w_kr = w_in[:, o:o + A_ROPE]
    hr = A_ROPE // 2
    wkra = _pad_cols(w_kr, LANES)
    wkrb = _pad_cols(jnp.concatenate([w_kr[:, hr:], w_kr[:, :hr]], axis=1), LANES)
    uq = w_uq.reshape(Q_LORA, A_HEADS, A_NOPE + A_ROPE)
    zeros = jnp.zeros((Q_LORA, A_HEADS, LANES - A_ROPE), w_uq.dtype)
    wq = jnp.concatenate([uq, zeros], axis=-1).reshape(Q_LORA, A_HEADS * 2 * LANES)
    ws = jnp.concatenate([uq[..., A_NOPE + hr:], uq[..., A_NOPE:A_NOPE + hr], zeros],
                         axis=-1).reshape(Q_LORA, A_HEADS * LANES)
    ukv = w_ukv.reshape(KV_LORA, A_HEADS, A_NOPE + A_VDIM)
    return dict(
        wm=w_in[:, :4 * M_W].astype(BF16),
        wside=jnp.concatenate([_pad_cols(w_gate, LANES), w_cq, w_ckv, wkra, wkrb], axis=1).astype(BF16),
        bg=_pad_cols(jnp.concatenate([b_ig, b_fg]).reshape(1, 2 * M_HEADS).astype(F32), LANES),
        qn=q_norm.reshape(1, Q_LORA).astype(F32), kvn=kv_norm.reshape(1, KV_LORA).astype(F32),
        wq=wq.astype(BF16), ws=ws.astype(BF16),
        wuk=ukv[..., :A_NOPE].reshape(KV_LORA, A_HEADS * A_NOPE).astype(BF16),
        wuv=ukv[..., A_NOPE:].reshape(KV_LORA, A_HEADS * A_VDIM).astype(BF16),
    )


def _rope_tables(pos):
    half = A_ROPE // 2
    inv_freq = ROPE_THETA ** (-jnp.arange(half, dtype=F32) / half)
    ang = pos.astype(F32)[:, None] * inv_freq[None, :]
    cos, sin = jnp.cos(ang), jnp.sin(ang)
    pad = jnp.zeros((pos.shape[0], LANES - A_ROPE), F32)
    return (jnp.concatenate([cos, cos, pad], axis=1), jnp.concatenate([-sin, sin, pad], axis=1))


def _s5_weights(a_re, a_im, b_re, b_im, c_re, c_im, d, log_dt, w_glu):
    dt = jnp.exp(log_dt.astype(F32))[:, None]
    ar, ai = a_re.astype(F32), a_im.astype(F32)
    mag = jnp.exp(dt * ar)
    ab_r, ab_i = mag * jnp.cos(dt * ai), mag * jnp.sin(dt * ai)
    den = ar * ar + ai * ai
    c_r = ((ab_r - 1.0) * ar + ab_i * ai) / den
    c_i = (ab_i * ar - (ab_r - 1.0) * ai) / den
    br, bi = b_re.astype(F32), b_im.astype(F32)
    bb_r = c_r[..., None] * br - c_i[..., None] * bi
    bb_i = c_r[..., None] * bi + c_i[..., None] * br
    gb = S5_GROUPS // S5_BLOCKS
    eye = jnp.eye(gb, dtype=F32)
    bd_in = lambda m: jnp.einsum("kgpc,gh->kgchp", m.reshape(S5_BLOCKS, gb, S5_P, S5_GSIZE),
                                 eye).reshape(S5_BLOCKS, S5_BLK_IN, S5_BLK_ST)
    bd_out = lambda m: jnp.einsum("kgcp,gh->kgphc", m.reshape(S5_BLOCKS, gb, S5_GSIZE, S5_P),
                                  eye).reshape(S5_BLOCKS, S5_BLK_ST, S5_BLK_IN)
    return dict(
        bbr=bd_in(bb_r).astype(BF16), bbi=bd_in(bb_i).astype(BF16),
        ar=ab_r.reshape(1, S5_STATE), ai=ab_i.reshape(1, S5_STATE),
        ccr=bd_out(c_re.astype(F32)).astype(BF16), cci=bd_out(-c_im.astype(F32)).astype(BF16),
        d=d.reshape(1, S5_W).astype(F32), wglu=w_glu.astype(BF16),
    )


def _pad_keys(a, mult):
    pad = (-a.shape[1]) % mult
    return jnp.pad(a, ((0, 0), (0, pad), (0, 0))) if pad else a


def _trunk(x, pos0, cache, p):
    nb, l, _ = x.shape
    assert nb == SUBLANES, "the S5 layout puts the streams of one time step on the sublanes"
    assert l % CHUNK == 0
    n = nb * l
    tm = min(TOKEN_TILE, n)
    tl = min(tm, l)
    assert l % tl == 0 and tm % tl == 0
    past = 0 if cache is None else cache["sb_k"].shape[1]
    assert pos0 == past
    xf = x.reshape(n, D_MODEL)
    bf = lambda a: a.astype(BF16)
    ffn_w = (bf(p["ffn_w_gate"]), bf(p["ffn_w_up"]), bf(p["ffn_w_down"]))

    xf = _ffn(xf, p["norm_ffn"][0, 0], *ffn_w, (0, 0), tm=tm)
    ew = _even_weights(p["even_w_in"][0], p["mlstm_b_igate"][0], p["mlstm_b_fgate"][0],
                       p["mla_q_norm"][0], p["mla_kv_norm"][0], p["mla_w_uq"][0], p["mla_w_ukv"][0])
    cos, sin = _rope_tables(pos0 + jnp.arange(l))
    if tm > l:
        cos, sin = jnp.tile(cos, (tm // l, 1)), jnp.tile(sin, (tm // l, 1))
    qm, km, vm, om, gates, q, kn, v, kr128, lat, kr = _even_in(
        xf, p["norm_mix"][0], ew, cos, sin, tm=tm, seq_len=l)
    v_transposed = v.ndim == 3

    if cache is None:
        c0 = jnp.zeros((nb, M_HEADS, M_DH, M_DH), F32)
        n0 = jnp.zeros((nb, M_HEADS, 1, M_DH), F32)
        m0 = jnp.zeros((nb, M_HEADS, 1, LANES), F32)
    else:
        c0 = cache["mlstm_c"].astype(F32)
        n0 = cache["mlstm_n"].astype(F32).reshape(nb, M_HEADS, 1, M_DH)
        m0 = jnp.broadcast_to(cache["mlstm_m"].astype(F32)[..., None, None], (nb, M_HEADS, 1, LANES))
    r3 = lambda a: a.reshape(nb, l, a.shape[-1])
    h_m, c_new, n_new, m_new = _mlstm(r3(qm), r3(km), r3(vm), r3(om), r3(gates),
                                      p["mlstm_out_norm"][0], c0, n0, m0, tl=tl)

    mla_tq = min(MLA_TILE, -(-l // LANES) * LANES)
    if cache is None:
        mla_sub = mla_tq
        mla_tkb = min(2 * mla_tq, l)
    else:
        mla_sub = mla_tkb = -(-(past + max(l, mla_tq)) // LANES) * LANES
    kn3, kr3 = r3(kn), r3(kr128)
    v3, vt = (None, v) if v_transposed else (r3(v), None)
    if cache is not None:
        ckn, cv = _kv_expand(cache["mla_latent"].astype(F32).reshape(nb * past, KV_LORA),
                             ew["wuk"], ew["wuv"])
        ckr = jnp.pad(cache["mla_krope"], ((0, 0), (0, 0), (0, LANES - A_ROPE))).astype(BF16)
        kn3 = jnp.concatenate([ckn.reshape(nb, past, -1), kn3], axis=1)
        kr3 = jnp.concatenate([ckr, kr3], axis=1)
        cv3 = cv.reshape(nb, past, -1)
        if v_transposed:
            vt = jnp.concatenate([_transpose_tiles(cv3, math.gcd(past, TOKEN_TILE)), vt], axis=2)
        else:
            v3 = jnp.concatenate([cv3, v3], axis=1)
    if vt is None:
        vt = _transpose_tiles(_pad_keys(v3, mla_tkb), mla_sub)
    else:
        vt = jnp.pad(vt, ((0, 0), (0, 0), (0, (-vt.shape[2]) % mla_tkb)))
    h_a = _mla_attention(_pad_keys(r3(q), mla_tq), _pad_keys(kn3, mla_tkb), _pad_keys(kr3, mla_tkb),
                         vt, tq=mla_tq, tkb=mla_tkb, sub=mla_sub, offset=past)[:, :l]

    w_out = bf(p["even_w_out"][0])
    xf = _ffn(xf, p["norm_ffn"][0, 1], *ffn_w, (0, 1),
              mix=(h_m.reshape(n, M_W), h_a.reshape(n, -1), w_out[:M_W], w_out[M_W:]), tm=tm)

    xf = _ffn(xf, p["norm_ffn"][1, 0], *ffn_w, (1, 0), tm=tm)
    w_in = p["odd_w_in"][0]
    ow = dict(wu=bf(w_in[:, :S5_W]), wq=bf(w_in[:, S5_W:S5_W + SB_W]),
              wk=bf(w_in[:, S5_W + SB_W:S5_W + 2 * SB_W]), wv=bf(w_in[:, S5_W + 2 * SB_W:]))
    u, qs, ks, vs, ksb, vsb = _odd_in(xf, p["norm_mix"][1], ow, tm=tm)

    sw = _s5_weights(p["s5_A_re"][0], p["s5_A_im"][0], p["s5_B_re"][0], p["s5_B_im"][0],
                     p["s5_C_re"][0], p["s5_C_im"][0], p["s5_D"][0], p["s5_log_dt"][0], p["s5_w_glu"][0])
    if cache is None:
        x0r = jnp.zeros((nb, S5_STATE), F32)
        x0i = jnp.zeros((nb, S5_STATE), F32)
    else:
        x0r = cache["s5_re"].astype(F32).reshape(nb, S5_STATE)
        x0i = cache["s5_im"].astype(F32).reshape(nb, S5_STATE)
    h_s, xr_new, xi_new = _s5(r3(u), sw, x0r, x0i, tt=min(S5_TT, l))

    k3, v3 = r3(ksb), r3(vsb)
    if cache is not None:
        k3 = jnp.concatenate([bf(cache["sb_k"].reshape(nb, past, SB_W)), k3], axis=1)
        v3 = jnp.concatenate([bf(cache["sb_v"].reshape(nb, past, SB_W)), v3], axis=1)
    h_b = _sb_attention(r3(qs), _pad_keys(k3, SB_TILE), _pad_keys(v3, SB_TILE), tq=min(SB_TILE, l),
                        tk=SB_TILE, offset=past)

    w_out = bf(p["odd_w_out"][0])
    y = _ffn(xf, p["norm_ffn"][1, 1], *ffn_w, (1, 1),
             mix=(h_s.reshape(n, S5_W), h_b.reshape(n, SB_W), w_out[:S5_W], w_out[S5_W:]),
             final_gain=p["norm_final"], tm=tm)

    return (y.reshape(nb, l, D_MODEL),
            lat.reshape(1, nb, l, KV_LORA), kr.reshape(1, nb, l, A_ROPE),
            c_new[None], n_new.reshape(1, nb, M_HEADS, M_DH), m_new[:, :, 0, 0][None],
            xr_new.reshape(1, nb, S5_GROUPS, S5_P), xi_new.reshape(1, nb, S5_GROUPS, S5_P),
            ks.reshape(1, nb, l, SB_HEADS, SB_DH), vs.reshape(1, nb, l, SB_HEADS, SB_DH))


def kernel(x_prompt, x_sample, cache_mla_latent, cache_mla_krope, state_mlstm_C, state_mlstm_n, state_mlstm_m, state_s5_re, state_s5_im, cache_sb_k, cache_sb_v, norm_ffn, norm_mix, norm_final, ffn_w_gate, ffn_w_up, ffn_w_down, even_w_in, even_w_out, mlstm_b_igate, mlstm_b_fgate, mlstm_out_norm, mla_q_norm, mla_kv_norm, mla_w_uq, mla_w_ukv, odd_w_in, odd_w_out, s5_A_re, s5_A_im, s5_B_re, s5_B_im, s5_C_re, s5_C_im, s5_D, s5_log_dt, s5_w_glu):
    p = dict(norm_ffn=norm_ffn, norm_mix=norm_mix, norm_final=norm_final, ffn_w_gate=ffn_w_gate,
             ffn_w_up=ffn_w_up, ffn_w_down=ffn_w_down, even_w_in=even_w_in, even_w_out=even_w_out,
             mlstm_b_igate=mlstm_b_igate, mlstm_b_fgate=mlstm_b_fgate, mlstm_out_norm=mlstm_out_norm,
             mla_q_norm=mla_q_norm, mla_kv_norm=mla_kv_norm, mla_w_uq=mla_w_uq, mla_w_ukv=mla_w_ukv,
             odd_w_in=odd_w_in, odd_w_out=odd_w_out, s5_A_re=s5_A_re, s5_A_im=s5_A_im, s5_B_re=s5_B_re,
             s5_B_im=s5_B_im, s5_C_re=s5_C_re, s5_C_im=s5_C_im, s5_D=s5_D, s5_log_dt=s5_log_dt,
             s5_w_glu=s5_w_glu)
    past = cache_sb_k.shape[2]
    cache = dict(mla_latent=cache_mla_latent[0], mla_krope=cache_mla_krope[0],
                 mlstm_c=state_mlstm_C[0], mlstm_n=state_mlstm_n[0], mlstm_m=state_mlstm_m[0],
                 s5_re=state_s5_re[0], s5_im=state_s5_im[0],
                 sb_k=cache_sb_k[0], sb_v=cache_sb_v[0])
    out_p = _trunk(x_prompt, 0, None, p)
    out_s = _trunk(x_sample, past, cache, p)
    return (out_p[0], out_s[0]) + tuple(out_p[1:]) + tuple(out_s[1:])
```

```python
import functools
import math

import jax
import jax.numpy as jnp
import numpy as np
from jax import lax
from jax.experimental import pallas as pl
from jax.experimental.pallas import tpu as pltpu

F32 = jnp.float32
BF16 = jnp.bfloat16

D_MODEL = 1024
CHUNK = 64
D_FF = 2816
EPS = 1e-6
M_HEADS = 4
M_DH = 128
M_W = M_HEADS * M_DH
A_HEADS = 4
A_NOPE = 128
A_ROPE = 64
A_VDIM = 128
Q_LORA = 384
KV_LORA = 256
ROPE_THETA = 10000.0
MLA_SCALE = (A_NOPE + A_ROPE) ** -0.5
S5_W = 512
S5_GSIZE = 16
S5_GROUPS = S5_W // S5_GSIZE
S5_P = 64
S5_STATE = S5_GROUPS * S5_P
SB_HEADS = 8
SB_DH = 64
SB_W = SB_HEADS * SB_DH

LANES = 128
SUBLANES = 8
VMEM_LIMIT_BYTES = 56 * 1024 * 1024

TOKEN_TILE = 512
FFN_SUB_TILE = 512
FFN_TILES_PER_STEP = 2
FF_CHUNK = 256
MLA_TILE = 512
SB_TILE = 128
SB_TILES_PER_STEP = 4
S5_TT = 128
MLSTM_STREAMS = 4
SB_LOG_FLOOR = -24.0
LOG2E = math.log2(math.e)

HIGHEST = lax.Precision.HIGHEST


def _cparams(*sem):
    return pltpu.CompilerParams(dimension_semantics=sem, vmem_limit_bytes=VMEM_LIMIT_BYTES)


def _bdot(a, b):
    return jnp.dot(a.astype(BF16), b.astype(BF16), preferred_element_type=F32)


def _bdot_nt(a, b):
    return lax.dot_general(a.astype(BF16), b.astype(BF16), (((1,), (1,)), ((), ())),
                           preferred_element_type=F32)


def _rms(x, g):
    return x * lax.rsqrt(jnp.mean(x * x, axis=-1, keepdims=True) + EPS) * g


def _sigmoid(x):
    return 1.0 / (1.0 + jnp.exp(-x))


def _log_sigmoid(x):
    return jnp.minimum(x, 0.0) - jnp.log(1.0 + jnp.exp(-jnp.abs(x)))


def _const_spec(shape):
    nd = len(shape)
    return pl.BlockSpec(shape, lambda *_: (0,) * nd)


def _ffn_kernel(*refs, with_mix, with_final):
    it = iter(refs)
    x_ref = next(it)
    if with_mix:
        a_ref, b_ref, wa_ref, wb_ref = next(it), next(it), next(it), next(it)
    g_ref, wg_ref, wu_ref, wd_ref = next(it), next(it), next(it), next(it)
    gf_ref = next(it) if with_final else None
    o_ref = next(it)
    acc_ref = next(it)

    for r0 in range(0, x_ref.shape[0], FFN_SUB_TILE):
        rs = slice(r0, min(r0 + FFN_SUB_TILE, x_ref.shape[0]))
        x = x_ref[rs, :]
        if with_mix:
            x = x + _bdot(a_ref[rs, :], wa_ref[...]) + _bdot(b_ref[rs, :], wb_ref[...])
        h = _rms(x, g_ref[...]).astype(BF16)
        acc_ref[rs, :] = jnp.zeros((rs.stop - rs.start, D_MODEL), F32)
        for c in range(D_FF // FF_CHUNK):
            sl = slice(c * FF_CHUNK, (c + 1) * FF_CHUNK)
            g = jnp.dot(h, wg_ref[:, sl], preferred_element_type=F32)
            u = jnp.dot(h, wu_ref[:, sl], preferred_element_type=F32)
            a = (g * _sigmoid(g) * u).astype(BF16)
            acc_ref[rs, :] += jnp.dot(a, wd_ref[sl, :], preferred_element_type=F32)
        y = x + 0.5 * acc_ref[rs, :]
        if with_final:
            y = _rms(y, gf_ref[...])
        o_ref[rs, :] = y


def _ffn(x, gain, wg, wu, wd, which, *, mix=None, final_gain=None, tm):
    n = x.shape[0]
    if n % (FFN_TILES_PER_STEP * tm) == 0:
        tm = FFN_TILES_PER_STEP * tm
    row = lambda i: (i, 0)
    pick = lambda i: which + (0, 0)
    once = pl.Buffered(1)
    args, specs = [x], [pl.BlockSpec((tm, D_MODEL), row)]
    if mix is not None:
        a, b, wa, wb = mix
        args += [a, b, wa, wb]
        specs += [pl.BlockSpec((tm, a.shape[-1]), row), pl.BlockSpec((tm, b.shape[-1]), row),
                  pl.BlockSpec(wa.shape, lambda i: (0, 0), pipeline_mode=once),
                  pl.BlockSpec(wb.shape, lambda i: (0, 0), pipeline_mode=once)]
    args += [gain.reshape(1, D_MODEL), wg, wu, wd]
    specs += [_const_spec((1, D_MODEL))] + [
        pl.BlockSpec((None, None) + w.shape[2:], pick, pipeline_mode=once) for w in (wg, wu, wd)]
    if final_gain is not None:
        args.append(final_gain.reshape(1, D_MODEL))
        specs.append(_const_spec((1, D_MODEL)))
    return pl.pallas_call(
        functools.partial(_ffn_kernel, with_mix=mix is not None, with_final=final_gain is not None),
        grid=(n // tm,),
        in_specs=specs,
        out_specs=pl.BlockSpec((tm, D_MODEL), row),
        out_shape=jax.ShapeDtypeStruct((n, D_MODEL), F32),
        scratch_shapes=[pltpu.VMEM((tm, D_MODEL), F32)],
        compiler_params=_cparams("parallel"),
        name="ffn",
    )(*args)


def _even_in_kernel(x_ref, gain_ref, wm_ref, wside_ref, bg_ref, qn_ref, kvn_ref, wq_ref,
                    wuk_ref, wuv_ref, cos_ref, sin_ref,
                    qm_o, km_o, vm_o, om_o, gates_o, q_o, kn_o, v_o, kr128_o, lat_o, kr_o,
                    *, v_transposed):
    h = _rms(x_ref[...], gain_ref[...]).astype(BF16)
    m = jnp.dot(h, wm_ref[...], preferred_element_type=F32)
    qm_o[...] = m[:, :M_W].astype(BF16)
    km_o[...] = (m[:, M_W:2 * M_W] * M_DH ** -0.5).astype(BF16)
    vm_o[...] = m[:, 2 * M_W:3 * M_W].astype(BF16)
    om_o[...] = m[:, 3 * M_W:]

    side = jnp.dot(h, wside_ref[...], preferred_element_type=F32)
    o_cq, o_ckv, o_kra, o_krb = LANES, LANES + Q_LORA, LANES + Q_LORA + KV_LORA, 2 * LANES + Q_LORA + KV_LORA
    g = side[:, :LANES] + bg_ref[...]
    lane = lax.broadcasted_iota(jnp.int32, g.shape, 1)
    gates_o[...] = jnp.where(lane < M_HEADS, g, _log_sigmoid(g))

    cos, sin = cos_ref[...], sin_ref[...]
    cq = side[:, o_cq:o_ckv]
    cqn = _rms(cq, qn_ref[...]).astype(BF16)
    qf = jnp.dot(cqn, wq_ref[...], preferred_element_type=F32)
    q_scale = MLA_SCALE * LOG2E
    first_half = lax.broadcasted_iota(jnp.int32, (1, LANES), 1) < A_ROPE // 2
    for hh in range(A_HEADS):
        base = hh * 2 * LANES
        q_o[:, base:base + LANES] = (qf[:, base:base + LANES] * q_scale).astype(BF16)
        ra = qf[:, base + LANES:base + 2 * LANES]
        rb = jnp.where(first_half, pltpu.roll(ra, LANES - A_ROPE // 2, 1), pltpu.roll(ra, A_ROPE // 2, 1))
        rot = ra * cos + rb * sin
        q_o[:, base + LANES:base + 2 * LANES] = (rot * q_scale).astype(BF16)

    ckv = side[:, o_ckv:o_kra]
    lat = _rms(ckv, kvn_ref[...])
    lat_o[...] = lat
    latb = lat.astype(BF16)
    kn_o[...] = jnp.dot(latb, wuk_ref[...], preferred_element_type=F32).astype(BF16)
    if v_transposed:
        v_o[0] = _bdot_nt(wuv_ref[...], latb).astype(BF16)
    else:
        v_o[...] = jnp.dot(latb, wuv_ref[...], preferred_element_type=F32).astype(BF16)

    kr = side[:, o_kra:o_krb] * cos + side[:, o_krb:] * sin
    kr128_o[...] = kr.astype(BF16)
    kr_o[...] = kr[:, :A_ROPE]


def _even_in(x, gain, w, cos, sin, *, tm, seq_len):
    n = x.shape[0]
    row = lambda i: (i, 0)
    n_pos_tiles = cos.shape[0] // tm
    pos = lambda i: (i % n_pos_tiles, 0)
    v_transposed = seq_len % tm == 0
    n_time_tiles = max(seq_len // tm, 1)
    weights = [w["wm"], w["wside"], w["bg"], w["qn"], w["kvn"],
               w["wq"], w["wuk"], w["wuv"].T if v_transposed else w["wuv"]]
    in_specs = ([pl.BlockSpec((tm, D_MODEL), row), _const_spec((1, D_MODEL))]
                + [_const_spec(a.shape) for a in weights]
                + [pl.BlockSpec((tm, LANES), pos), pl.BlockSpec((tm, LANES), pos)])
    outs = [(M_W, BF16), (M_W, BF16), (M_W, BF16), (M_W, F32), (LANES, F32),
            (A_HEADS * 2 * LANES, BF16), (A_HEADS * A_NOPE, BF16), (A_HEADS * A_VDIM, BF16),
            (LANES, BF16), (KV_LORA, F32), (A_ROPE, F32)]
    out_specs = [pl.BlockSpec((tm, c), row) for c, _ in outs]
    out_shape = [jax.ShapeDtypeStruct((n, c), dt) for c, dt in outs]
    if v_transposed:
        v_idx = 7
        out_specs[v_idx] = pl.BlockSpec((1, A_HEADS * A_VDIM, tm),
                                        lambda i: (i // n_time_tiles, 0, i % n_time_tiles))
        out_shape[v_idx] = jax.ShapeDtypeStruct((n // seq_len, A_HEADS * A_VDIM, seq_len), BF16)
    return pl.pallas_call(
        functools.partial(_even_in_kernel, v_transposed=v_transposed),
        grid=(n // tm,),
        in_specs=in_specs,
        out_specs=out_specs,
        out_shape=out_shape,
        compiler_params=_cparams("parallel"),
        name="even_in",
    )(x, gain.reshape(1, D_MODEL), *weights, cos, sin)


def _kv_expand_kernel(lat_ref, wuk_ref, wuv_ref, kn_o, v_o):
    latb = lat_ref[...].astype(BF16)
    kn_o[...] = jnp.dot(latb, wuk_ref[...], preferred_element_type=F32).astype(BF16)
    v_o[...] = jnp.dot(latb, wuv_ref[...], preferred_element_type=F32).astype(BF16)


def _kv_expand(lat, wuk, wuv):
    n = lat.shape[0]
    tm = min(4 * TOKEN_TILE, n)
    assert n % tm == 0
    row = lambda i: (i, 0)
    return pl.pallas_call(
        _kv_expand_kernel,
        grid=(n // tm,),
        in_specs=[pl.BlockSpec((tm, KV_LORA), row), _const_spec(wuk.shape), _const_spec(wuv.shape)],
        out_specs=[pl.BlockSpec((tm, wuk.shape[1]), row), pl.BlockSpec((tm, wuv.shape[1]), row)],
        out_shape=[jax.ShapeDtypeStruct((n, wuk.shape[1]), BF16),
                   jax.ShapeDtypeStruct((n, wuv.shape[1]), BF16)],
        compiler_params=_cparams("parallel"),
        name="kv_expand",
    )(lat, wuk, wuv)


def _mlstm_kernel(qm_ref, km_ref, vm_ref, om_ref, g_ref, onorm_ref, c0_ref, n0_ref, m0_ref,
                  h_ref, cout_ref, nout_ref, mout_ref, c_sc, n_sc, m_sc, *, n_chunks, n_streams):
    lt = pl.program_id(1)

    @pl.when(lt == 0)
    def _():
        c_sc[...] = c0_ref[...]
        n_sc[...] = n0_ref[...]
        m_sc[...] = m0_ref[...]

    row = lax.broadcasted_iota(jnp.int32, (CHUNK, CHUNK), 0)
    col = lax.broadcasted_iota(jnp.int32, (CHUNK, CHUNK), 1)
    causal = col <= row
    ltri = causal.astype(BF16)
    utri = (row <= col).astype(BF16)
    ones_kl = jnp.ones((CHUNK, LANES), BF16)

    def split3(a):
        hi = a.astype(BF16)
        r1 = a - hi.astype(F32)
        mid = r1.astype(BF16)
        return hi, mid, (r1 - mid.astype(F32)).astype(BF16)

    def chunk(c, carry):
        r0 = pl.multiple_of(c * CHUNK, CHUNK)
        rows = pl.ds(r0, CHUNK)
        chains = [(bb, hh) for bb in range(n_streams) for hh in range(M_HEADS)]
        hsl = lambda hh: slice(hh * M_DH, (hh + 1) * M_DH)
        gate = {}
        for bb in range(n_streams):
            gates = g_ref[bb, rows, :]
            gates_t = gates.T
            cc = jnp.dot(ltri, jnp.concatenate(split3(gates), axis=1), preferred_element_type=F32)
            cum_col = cc[:, :LANES] + cc[:, LANES:2 * LANES] + cc[:, 2 * LANES:]
            cr = jnp.dot(jnp.concatenate(split3(gates_t), axis=0), utri, preferred_element_type=F32)
            cum_row = cr[:LANES] + cr[LANES:2 * LANES] + cr[2 * LANES:]
            gate[bb] = (gates, gates_t, cum_col, cum_row)
        qk, qc, qn = {}, {}, {}
        for ch in chains:
            bb, hh = ch
            rhs = jnp.concatenate([c_sc[bb, hh].astype(BF16),
                                   jnp.broadcast_to(n_sc[bb, hh], (LANES, M_DH)).astype(BF16),
                                   km_ref[bb, rows, hsl(hh)]], axis=0)
            r = _bdot_nt(qm_ref[bb, rows, hsl(hh)], rhs)
            qc[ch], qn[ch], qk[ch] = r[:, :M_DH], r[:, M_DH:M_DH + LANES], r[:, M_DH + LANES:]
        st = {}
        for ch in chains:
            bb, hh = ch
            gates, gates_t, cum_col, cum_row = gate[bb]
            ig_row = gates_t[hh:hh + 1, :]
            ig_col = gates[:, hh:hh + 1]
            b_row = cum_row[M_HEADS + hh:M_HEADS + hh + 1, :]
            b_col = cum_col[:, M_HEADS + hh:M_HEADS + hh + 1]
            m_prev = m_sc[bb, hh][:, :1]
            b_lanes = jnp.broadcast_to(b_col, (CHUNK, LANES))
            log_d = jnp.where(causal, b_lanes[:, :CHUNK] - b_row + ig_row, -jnp.inf)
            m_t = jnp.maximum(b_col + m_prev, jnp.max(log_d, axis=-1, keepdims=True))
            m_lanes = jnp.broadcast_to(m_t, (CHUNK, LANES))
            s = (qk[ch] * jnp.exp(log_d - m_lanes[:, :CHUNK])).astype(BF16)
            w_inter = jnp.exp(b_lanes + m_prev - m_lanes)
            m_new = m_t[CHUNK - 1:CHUNK, :]
            b_last = b_col[CHUNK - 1:CHUNK, :]
            w_s = jnp.exp(b_last - b_col + ig_col - m_new)
            decay = jnp.exp(b_last + m_prev - m_new)
            r = jnp.dot(s, jnp.concatenate([vm_ref[bb, rows, hsl(hh)], ones_kl], axis=1),
                        preferred_element_type=F32)
            st[ch] = (r[:, :M_DH], r[:, M_DH:], w_inter, m_lanes, m_new, w_s, decay)
        for ch in chains:
            bb, hh = ch
            sv, s_sum, w_inter, m_lanes, m_new, w_s, decay = st[ch]
            k = km_ref[bb, rows, hsl(hh)]
            v = vm_ref[bb, rows, hsl(hh)]
            c_h = c_sc[bb, hh]
            n_h = n_sc[bb, hh]
            num = w_inter * qc[ch] + sv
            den = w_inter * qn[ch] + s_sum
            hv = num / jnp.maximum(jnp.abs(den), jnp.exp(-m_lanes))
            wv_t = (jnp.broadcast_to(w_s, (CHUNK, M_DH)) * v).T
            c_sc[bb, hh] = decay * c_h + _bdot(wv_t, k)
            n_sc[bb, hh] = decay * n_h + jnp.sum(w_s * k.astype(F32), axis=0, keepdims=True)
            m_sc[bb, hh] = jnp.broadcast_to(m_new, (1, LANES))
            hn = _rms(hv, onorm_ref[:, hsl(hh)])
            h_ref[bb, rows, hsl(hh)] = (_sigmoid(om_ref[bb, rows, hsl(hh)]) * hn).astype(BF16)
        return carry

    lax.fori_loop(0, n_chunks, chunk, 0)

    @pl.when(lt == pl.num_programs(1) - 1)
    def _():
        cout_ref[...] = c_sc[...]
        nout_ref[...] = n_sc[...]
        mout_ref[...] = m_sc[...]


def _mlstm(qm, km, vm, om, gates, onorm, c0, n0, m0, *, tl):
    b, l, _ = qm.shape
    ns = MLSTM_STREAMS
    tile = lambda i, j: (i, j, 0)
    st4 = lambda i, j: (i, 0, 0, 0)
    state = [((ns, M_HEADS, M_DH, M_DH), (b, M_HEADS, M_DH, M_DH)),
             ((ns, M_HEADS, 1, M_DH), (b, M_HEADS, 1, M_DH)),
             ((ns, M_HEADS, 1, LANES), (b, M_HEADS, 1, LANES))]
    return pl.pallas_call(
        functools.partial(_mlstm_kernel, n_chunks=tl // CHUNK, n_streams=ns),
        grid=(b // ns, l // tl),
        in_specs=[pl.BlockSpec((ns, tl, M_W), tile)] * 4
        + [pl.BlockSpec((ns, tl, LANES), tile), _const_spec((1, M_W))]
        + [pl.BlockSpec(blk, st4) for blk, _ in state],
        out_specs=[pl.BlockSpec((ns, tl, M_W), tile)] + [pl.BlockSpec(blk, st4) for blk, _ in state],
        out_shape=[jax.ShapeDtypeStruct((b, l, M_W), BF16)]
        + [jax.ShapeDtypeStruct(full, F32) for _, full in state],
        scratch_shapes=[pltpu.VMEM(blk, F32) for blk, _ in state],
        compiler_params=_cparams("parallel", "arbitrary"),
        name="mlstm",
    )(qm, km, vm, om, gates, onorm.reshape(1, M_W), c0, n0, m0)


MLA_SUM_ROWS = 16


MLA_EXP_RANGE = 60.0


def _mla_kernel(qi_ref, kj_ref, flag_ref, q_ref, kn_ref, kr_ref, vt_ref, o_ref, m_sc, acc_sc, bad_sc,
                redo_sm, *, tq, tkb, offset, variants):
    p = pl.program_id(1)
    i = qi_ref[p]
    j = kj_ref[p]
    flags = flag_ref[p]

    @pl.when(j == 0)
    def _():
        m_sc[...] = jnp.zeros_like(m_sc)
        acc_sc[...] = jnp.zeros_like(acc_sc)

    def score_tiles(nk, masked):
        kr = kr_ref[0, :nk, :]
        scores = []
        for hh in range(A_HEADS):
            q = q_ref[0, :, hh * 2 * LANES:(hh + 1) * 2 * LANES]
            k = jnp.concatenate([kn_ref[0, :nk, hh * A_NOPE:(hh + 1) * A_NOPE], kr], axis=-1)
            scores.append(_bdot_nt(k, q))
        if masked:
            k_chunk = (j * tkb + lax.broadcasted_iota(jnp.int32, (nk, tq), 0)) // CHUNK
            q_chunk = (offset + i * tq + lax.broadcasted_iota(jnp.int32, (nk, tq), 1)) // CHUNK
            visible = k_chunk <= q_chunk
            scores = [jnp.where(visible, s, -jnp.inf) for s in scores]
        return scores

    def values_t(hh, nk):
        return jnp.concatenate([vt_ref[0, hh * A_VDIM:(hh + 1) * A_VDIM, :nk],
                                jnp.ones((MLA_SUM_ROWS, nk), BF16)], axis=0)

    def add_block(nk, masked):
        scores = score_tiles(nk, masked)
        any_bad = jnp.zeros((1, tq), F32)
        for hh in range(A_HEADS):
            s = scores[hh]
            m_ref = m_sc[hh]
            d = jnp.max(s, axis=0, keepdims=True) - m_ref
            ok = jnp.logical_and(d <= MLA_EXP_RANGE, jnp.logical_or(d >= -MLA_EXP_RANGE, j > 0))
            pr = jnp.exp2(s - m_ref).astype(BF16)
            acc = acc_sc[hh]
            acc_sc[hh] = jnp.where(ok, acc + jnp.dot(values_t(hh, nk), pr, preferred_element_type=F32),
                                   acc)
            bad = jnp.where(ok, 0.0, 1.0)
            bad_sc[hh] = bad
            any_bad = jnp.maximum(any_bad, bad)
        redo_sm[0] = (jnp.max(any_bad) > 0.0).astype(jnp.int32)

    def redo_block(nk, masked):
        scores = score_tiles(nk, masked)
        for hh in range(A_HEADS):
            s = scores[hh]
            bad = bad_sc[hh] > 0.0
            m_prev = jnp.where(j == 0, -jnp.inf, m_sc[hh])
            m_new = jnp.maximum(m_prev, jnp.max(s, axis=0, keepdims=True))
            alpha = jnp.exp2(m_prev - m_new)
            pr = jnp.exp2(s - m_new).astype(BF16)
            acc = acc_sc[hh]
            acc_sc[hh] = jnp.where(
                bad, alpha * acc + jnp.dot(values_t(hh, nk), pr, preferred_element_type=F32), acc)
            m_sc[hh] = jnp.where(bad, m_new, m_sc[hh])

    for vi, (nk, masked) in enumerate(variants):
        pl.when((flags >> 1) == vi)(functools.partial(add_block, nk, masked))
    for vi, (nk, masked) in enumerate(variants):
        pl.when(jnp.logical_and((flags >> 1) == vi, redo_sm[0] != 0))(
            functools.partial(redo_block, nk, masked))

    @pl.when((flags & 1) != 0)
    def _():
        for hh in range(A_HEADS):
            acc = acc_sc[hh]
            o_ref[0, :, hh * A_VDIM:(hh + 1) * A_VDIM] = (
                acc[:A_VDIM] / acc[A_VDIM:A_VDIM + 1]).T.astype(BF16)


def _transpose_kernel(x_ref, o_ref):
    o_ref[0] = x_ref[0].astype(F32).T.astype(o_ref.dtype)


def _transpose_tiles(x, t):
    b, l, c = x.shape
    return pl.pallas_call(
        _transpose_kernel,
        grid=(b, l // t),
        in_specs=[pl.BlockSpec((1, t, c), lambda bb, i: (bb, i, 0))],
        out_specs=pl.BlockSpec((1, c, t), lambda bb, i: (bb, 0, i)),
        out_shape=jax.ShapeDtypeStruct((b, c, l), x.dtype),
        compiler_params=_cparams("parallel", "parallel"),
        name="transpose_tiles",
    )(x)


def _mla_pairs(lq, lk, tq, tkb, sub, offset):
    qi, kj, flags, variants = [], [], [], []
    for i in range(lq // tq):
        first_row, last_row = offset + i * tq, offset + (i + 1) * tq - 1
        k_end = min((last_row // CHUNK + 1) * CHUNK, lk)
        k_all = (first_row // CHUNK + 1) * CHUNK
        nj = -(-k_end // tkb)
        for j in range(nj):
            nk = -(-(min(k_end, (j + 1) * tkb) - j * tkb) // sub) * sub
            variant = (nk, j * tkb + nk > k_all)
            if variant not in variants:
                variants.append(variant)
            qi.append(i)
            kj.append(j)
            flags.append(int(j == nj - 1) + 2 * variants.index(variant))
    return (np.asarray(qi, np.int32), np.asarray(kj, np.int32), np.asarray(flags, np.int32),
            tuple(variants))


def _mla_attention(q, kn, kr, vt, *, tq, tkb, sub, offset):
    b, lq, _ = q.shape
    lk = kn.shape[1]
    assert lk % tkb == 0 and tkb % sub == 0 and vt.shape[2] == lk
    qi, kj, flags, variants = _mla_pairs(lq, lk, tq, tkb, sub, offset)
    qmap = lambda bb, p, qi_r, kj_r, flag_r: (bb, qi_r[p], 0)
    kmap = lambda bb, p, qi_r, kj_r, flag_r: (bb, kj_r[p], 0)
    grid_spec = pltpu.PrefetchScalarGridSpec(
        num_scalar_prefetch=3,
        grid=(b, len(qi)),
        in_specs=[pl.BlockSpec((1, tq, q.shape[-1]), qmap),
                  pl.BlockSpec((1, tkb, kn.shape[-1]), kmap),
                  pl.BlockSpec((1, tkb, kr.shape[-1]), kmap),
                  pl.BlockSpec((1, vt.shape[1], tkb), lambda bb, p, qi_r, kj_r, flag_r: (bb, 0, kj_r[p]))],
        out_specs=pl.BlockSpec((1, tq, A_HEADS * A_VDIM), qmap),
        scratch_shapes=[pltpu.VMEM((A_HEADS, 1, tq), F32),
                        pltpu.VMEM((A_HEADS, A_VDIM + MLA_SUM_ROWS, tq), F32),
                        pltpu.VMEM((A_HEADS, 1, tq), F32),
                        pltpu.SMEM((1,), jnp.int32)],
    )
    return pl.pallas_call(
        functools.partial(_mla_kernel, tq=tq, tkb=tkb, offset=offset, variants=variants),
        grid_spec=grid_spec,
        out_shape=jax.ShapeDtypeStruct((b, lq, A_HEADS * A_VDIM), BF16),
        compiler_params=_cparams("parallel", "arbitrary"),
        name="mla_attention",
    )(jnp.asarray(qi), jnp.asarray(kj), jnp.asarray(flags), q, kn, kr, vt)


def _odd_in_kernel(x_ref, gain_ref, wu_ref, wq_ref, wk_ref, wv_ref, u_o, q_o, k_o, v_o, kb_o, vb_o):
    h = _rms(x_ref[...], gain_ref[...]).astype(BF16)
    u_o[...] = jnp.dot(h, wu_ref[...], preferred_element_type=F32)
    q_o[...] = (jnp.dot(h, wq_ref[...], preferred_element_type=F32) * (SB_DH ** -0.5 * LOG2E)).astype(BF16)
    k = jnp.dot(h, wk_ref[...], preferred_element_type=F32)
    v = jnp.dot(h, wv_ref[...], preferred_element_type=F32)
    k_o[...] = pltpu.einshape("t(hd)->thd", k, h=SB_HEADS)
    v_o[...] = pltpu.einshape("t(hd)->thd", v, h=SB_HEADS)
    kb_o[...] = k.astype(BF16)
    vb_o[...] = v.astype(BF16)


def _odd_in(x, gain, w, *, tm):
    n = x.shape[0]
    row = lambda i: (i, 0)
    weights = [w["wu"], w["wq"], w["wk"], w["wv"]]
    return pl.pallas_call(
        _odd_in_kernel,
        grid=(n // tm,),
        in_specs=[pl.BlockSpec((tm, D_MODEL), row), _const_spec((1, D_MODEL))]
        + [_const_spec(a.shape) for a in weights],
        out_specs=[pl.BlockSpec((tm, S5_W), row), pl.BlockSpec((tm, SB_W), row)]
        + [pl.BlockSpec((tm, SB_HEADS, SB_DH), lambda i: (i, 0, 0))] * 2
        + [pl.BlockSpec((tm, SB_W), row)] * 2,
        out_shape=[jax.ShapeDtypeStruct((n, S5_W), F32),
                   jax.ShapeDtypeStruct((n, SB_W), BF16),
                   jax.ShapeDtypeStruct((n, SB_HEADS, SB_DH), F32),
                   jax.ShapeDtypeStruct((n, SB_HEADS, SB_DH), F32),
                   jax.ShapeDtypeStruct((n, SB_W), BF16),
                   jax.ShapeDtypeStruct((n, SB_W), BF16)],
        compiler_params=_cparams("parallel"),
        name="odd_in",
    )(x, gain.reshape(1, D_MODEL), *weights)


S5_BLOCKS = 4
S5_BLK_IN = S5_W // S5_BLOCKS
S5_BLK_ST = S5_STATE // S5_BLOCKS


def _s5_kernel(u_ref, bbr_ref, bbi_ref, ar_ref, ai_ref, ccr_ref, cci_ref, d_ref, wglu_ref,
               x0r_ref, x0i_ref, hs_o, xr_o, xi_o, u_tm, hs_tm, xr_buf, xi_buf, xr_st, xi_st,
               *, tt, nb):
    t = pl.program_id(0)

    @pl.when(t == 0)
    def _():
        xr_st[...] = x0r_ref[...]
        xi_st[...] = x0i_ref[...]

    for bb in range(nb):
        for k in range(S5_BLOCKS):
            u_tm[k, pl.ds(bb, tt, stride=nb), :] = u_ref[bb, :, k * S5_BLK_IN:(k + 1) * S5_BLK_IN]

    for k in range(S5_BLOCKS):
        cs = slice(k * S5_BLK_ST, (k + 1) * S5_BLK_ST)
        ub = u_tm[k].astype(BF16)
        xr_buf[:, cs] = jnp.dot(ub, bbr_ref[k], preferred_element_type=F32)
        xi_buf[:, cs] = jnp.dot(ub, bbi_ref[k], preferred_element_type=F32)

    for k in range(0, S5_BLOCKS, 2):
        cols = [slice((k + d) * S5_BLK_ST, (k + d + 1) * S5_BLK_ST) for d in range(2)]
        decay = [(jnp.broadcast_to(ar_ref[:, cs], (nb, S5_BLK_ST)),
                  jnp.broadcast_to(ai_ref[:, cs], (nb, S5_BLK_ST))) for cs in cols]

        def step(s, carry, cols=cols, decay=decay):
            rows = pl.ds(pl.multiple_of(s * nb, nb), nb)
            out = []
            for (xr, xi), (ar, ai), cs in zip(carry, decay, cols):
                nxr = ar * xr - ai * xi + xr_buf[rows, cs]
                nxi = ar * xi + ai * xr + xi_buf[rows, cs]
                xr_buf[rows, cs] = nxr
                xi_buf[rows, cs] = nxi
                out.append((nxr, nxi))
            return tuple(out)

        final = lax.fori_loop(0, tt, step, tuple((xr_st[:, cs], xi_st[:, cs]) for cs in cols), unroll=8)
        for (xr, xi), cs in zip(final, cols):
            xr_st[:, cs] = xr
            xi_st[:, cs] = xi

    gs = []
    for k in range(S5_BLOCKS):
        cs = slice(k * S5_BLK_ST, (k + 1) * S5_BLK_ST)
        y = (_bdot(xr_buf[:, cs], ccr_ref[k]) + _bdot(xi_buf[:, cs], cci_ref[k])
             + d_ref[:, k * S5_BLK_IN:(k + 1) * S5_BLK_IN] * u_tm[k])
        gs.append(jax.nn.gelu(y, approximate=True))
    g = jnp.concatenate(gs, axis=-1)
    hs = g * _sigmoid(_bdot(g, wglu_ref[...]))
    for k in range(S5_BLOCKS):
        hs_tm[k] = hs[:, k * S5_BLK_IN:(k + 1) * S5_BLK_IN]
    for bb in range(nb):
        for k in range(S5_BLOCKS):
            hs_o[bb, :, k * S5_BLK_IN:(k + 1) * S5_BLK_IN] = (
                hs_tm[k, pl.ds(bb, tt, stride=nb), :].astype(BF16))

    @pl.when(t == pl.num_programs(0) - 1)
    def _():
        xr_o[...] = xr_st[...]
        xi_o[...] = xi_st[...]


def _s5(u, w, x0r, x0i, *, tt):
    nb, l, _ = u.shape
    blk = lambda t: (0, t, 0)
    consts = [w["bbr"], w["bbi"], w["ar"], w["ai"], w["ccr"], w["cci"], w["d"], w["wglu"], x0r, x0i]
    return pl.pallas_call(
        functools.partial(_s5_kernel, tt=tt, nb=nb),
        grid=(l // tt,),
        in_specs=[pl.BlockSpec((nb, tt, S5_W), blk)] + [_const_spec(a.shape) for a in consts],
        out_specs=[pl.BlockSpec((nb, tt, S5_W), blk), _const_spec((nb, S5_STATE)),
                   _const_spec((nb, S5_STATE))],
        out_shape=[jax.ShapeDtypeStruct((nb, l, S5_W), BF16),
                   jax.ShapeDtypeStruct((nb, S5_STATE), F32),
                   jax.ShapeDtypeStruct((nb, S5_STATE), F32)],
        scratch_shapes=[pltpu.VMEM((S5_BLOCKS, tt * nb, S5_BLK_IN), F32),
                        pltpu.VMEM((S5_BLOCKS, tt * nb, S5_BLK_IN), F32),
                        pltpu.VMEM((tt * nb, S5_STATE), F32), pltpu.VMEM((tt * nb, S5_STATE), F32),
                        pltpu.VMEM((nb, S5_STATE), F32), pltpu.VMEM((nb, S5_STATE), F32)],
        compiler_params=_cparams("arbitrary"),
        name="s5",
    )(u, *consts)


def _sb_kernel(q_ref, k_ref, v_ref, o_ref, r_sc, acc_sc, qh_sc, *, tq, tk, offset, n_sub):
    for su in range(n_sub):
        _sb_query_tile(pl.program_id(1) * n_sub + su, slice(su * tq, (su + 1) * tq),
                       q_ref, k_ref, v_ref, o_ref, r_sc, acc_sc, qh_sc, tq=tq, tk=tk, offset=offset)


def _sb_query_tile(i, rs, q_ref, k_ref, v_ref, o_ref, r_sc, acc_sc, qh_sc, *, tq, tk, offset):
    n_pairs = SB_HEADS // 2
    half = lax.broadcasted_iota(jnp.int32, (1, LANES), 1) // SB_DH
    for pr in range(n_pairs):
        qq = q_ref[0, rs, pr * LANES:(pr + 1) * LANES]
        qh_sc[pr, :tq] = jnp.where(half == 0, qq, jnp.zeros_like(qq))
        qh_sc[pr, tq:] = jnp.where(half == 1, qq, jnp.zeros_like(qq))
    r_sc[...] = jnp.zeros_like(r_sc)
    acc_sc[...] = jnp.zeros_like(acc_sc)

    hk = tk // 2
    jrow = lax.broadcasted_iota(jnp.int32, (tk, tk), 0)
    jcol = lax.broadcasted_iota(jnp.int32, (tk, tk), 1)
    later = (jrow > jcol).astype(BF16)
    later_pk = jnp.logical_and(jrow > jcol, (jrow // hk) == (jcol // hk)).astype(BF16)
    first_half = lax.broadcasted_iota(jnp.int32, (1, tk), 1) < hk
    j_diag = (offset + i * tq) // tk

    def softplus2(z):
        return jnp.maximum(z, 0.0) + jnp.log2(1.0 + jnp.exp2(-jnp.abs(z)))

    def split2(a):
        hi = a.astype(BF16)
        return [hi, (a - hi.astype(F32)).astype(BF16)]

    def by_head(x):
        zero = jnp.zeros_like(x)
        return [jnp.where(half == 0, x, zero), jnp.where(half == 1, x, zero)]

    def sweep(tiles):
        zs, vv, keep_key = {}, {}, {}
        for t, (j, kind, live, limit) in enumerate(tiles):
            k0 = pl.multiple_of(j * tk, tk)
            if kind == "masked":
                q_pos = offset + i * tq + lax.broadcasted_iota(jnp.int32, (tq, tk), 0)
                keep_key[t] = (k0 + lax.broadcasted_iota(jnp.int32, (tq, tk), 1)) < q_pos
            elif limit is not None:
                keep_key[t] = lax.broadcasted_iota(jnp.int32, (tq, tk), 1) < limit
            for pr in range(n_pairs):
                ps = slice(pr * LANES, (pr + 1) * LANES)
                if kind == "half":
                    kh = k_ref[0, pl.ds(k0 + hk, hk), ps]
                    zero = jnp.zeros_like(kh)
                    k2 = jnp.concatenate([jnp.concatenate([kh, zero], axis=1),
                                          jnp.concatenate([zero, kh], axis=1)], axis=0)
                    q2 = jnp.concatenate([qh_sc[pr, :tq], qh_sc[pr, tq:]], axis=1)
                    zs[t, pr] = _bdot_nt(q2, k2)
                    vv[t, pr] = jnp.concatenate(by_head(v_ref[0, pl.ds(k0 + hk, hk), ps]), axis=0)
                else:
                    zs[t, pr] = _bdot_nt(qh_sc[pr], k_ref[0, pl.ds(k0, tk), ps])
                    vv[t, pr] = jnp.concatenate(by_head(v_ref[0, pl.ds(k0, tk), ps]), axis=0)
        l1s, lbs, cums = {}, {}, {}
        for t, (j, kind, live, limit) in enumerate(tiles):
            parts, slots = [], []
            for pr in range(n_pairs):
                for sub in range(1 if kind == "half" else 2):
                    slot = pr if kind == "half" else 2 * pr + sub
                    z = zs[t, pr] if kind == "half" else zs[t, pr][sub * tq:(sub + 1) * tq]
                    sp = softplus2(z)
                    l1 = -sp
                    if t in keep_key:
                        l1 = jnp.where(keep_key[t], l1, 0.0)
                    parts += split2(l1)
                    slots.append(slot)
                    l1s[t, slot], lbs[t, slot] = l1, z - sp
            c = jnp.dot(jnp.concatenate(parts, axis=0), later_pk if kind == "half" else later,
                        preferred_element_type=F32)
            for n, slot in enumerate(slots):
                cums[t, slot] = c[2 * n * tq:(2 * n + 1) * tq] + c[(2 * n + 1) * tq:(2 * n + 2) * tq]
        for t, (j, kind, live, limit) in enumerate(tiles):
            r_max = jnp.full((1, 1), -jnp.inf, F32)
            dead_shift = 0.0 if live is None else jnp.where(live, 0.0, -1e30)
            keep = 1.0 if live is None else jnp.where(live, 1.0, 0.0)
            for pr in range(n_pairs):
                if kind == "half":
                    r0, r1 = r_sc[2 * pr], r_sc[2 * pr + 1]
                    log_w = lbs[t, pr] + cums[t, pr] + (jnp.where(first_half, r0, r1) + dead_shift)
                    w = jnp.exp2(log_w).astype(BF16)
                    s_all = jnp.sum(l1s[t, pr], axis=-1, keepdims=True)
                    s0 = jnp.sum(jnp.where(first_half, l1s[t, pr], 0.0), axis=-1, keepdims=True)
                    r_new = [r0 + keep * s0, r1 + keep * (s_all - s0)]
                else:
                    ws, r_new = [], []
                    for sub in range(2):
                        hh = 2 * pr + sub
                        r_old = r_sc[hh]
                        log_w = lbs[t, hh] + cums[t, hh] + (r_old + dead_shift)
                        if t in keep_key:
                            log_w = jnp.where(keep_key[t], log_w, -jnp.inf)
                        ws.append(jnp.exp2(log_w).astype(BF16))
                        r_new.append(r_old + keep * jnp.sum(l1s[t, hh], axis=-1, keepdims=True))
                    w = jnp.concatenate(ws, axis=1)
                for sub in range(2):
                    r_sc[2 * pr + sub] = r_new[sub]
                    r_max = jnp.maximum(r_max, jnp.max(r_new[sub], axis=0, keepdims=True))
                acc_sc[:, pr * LANES:(pr + 1) * LANES] += jnp.dot(w, vv[t, pr], preferred_element_type=F32)
        return r_max[0, 0]

    has_prev = j_diag >= 1
    r_first = sweep([(j_diag, "masked", None, None), (jnp.maximum(j_diag - 1, 0), "half", has_prev, None)])

    def cond(carry):
        j, r_max, _ = carry
        return jnp.logical_and(j >= 0, r_max > SB_LOG_FLOOR * LOG2E)

    def body(carry):
        j, _, limit = carry
        return j - 1, sweep([(j, "full", None, limit)]), jnp.int32(tk)

    lax.while_loop(cond, body, (j_diag - 1, r_first, jnp.int32(hk)))
    o_ref[0, rs] = acc_sc[...].astype(BF16)


def _sb_attention(q, k, v, *, tq, tk, offset):
    b, lq, _ = q.shape
    lk = k.shape[1]
    assert tk % tq == 0 and offset % tk == 0 and lk % tk == 0
    n_sub = math.gcd(lq // tq, SB_TILES_PER_STEP)
    return pl.pallas_call(
        functools.partial(_sb_kernel, tq=tq, tk=tk, offset=offset, n_sub=n_sub),
        grid=(b, lq // (tq * n_sub)),
        in_specs=[pl.BlockSpec((1, tq * n_sub, SB_W), lambda bb, i: (bb, i, 0)),
                  pl.BlockSpec((1, lk, SB_W), lambda bb, i: (bb, 0, 0)),
                  pl.BlockSpec((1, lk, SB_W), lambda bb, i: (bb, 0, 0))],
        out_specs=pl.BlockSpec((1, tq * n_sub, SB_W), lambda bb, i: (bb, i, 0)),
        out_shape=jax.ShapeDtypeStruct((b, lq, SB_W), BF16),
        scratch_shapes=[pltpu.VMEM((SB_HEADS, tq, 1), F32),
                        pltpu.VMEM((tq, SB_W), F32),
                        pltpu.VMEM((SB_HEADS // 2, 2 * tq, LANES), BF16)],
        compiler_params=_cparams("parallel", "arbitrary"),
        name="sb_attention",
    )(q, k, v)


def _pad_cols(w, width):
    return jnp.pad(w, ((0, 0), (0, width - w.shape[1])))


def _even_weights(w_in, b_ig, b_fg, q_norm, kv_norm, w_uq, w_ukv):
    o = 4 * M_W
    w_gate = w_in[:, o:o + 2 * M_HEADS]
    o += 2 * M_HEADS
    w_cq = w_in[:, o:o + Q_LORA]
    o += Q_LORA
    w_ckv = w_in[:, o:o + KV_LORA]
    o += KV_LORA
    w_kr = w_in[:, o:o + A_ROPE]
    hr = A_ROPE // 2
    wkra = _pad_cols(w_kr, LANES)
    wkrb = _pad_cols(jnp.concatenate([w_kr[:, hr:], w_kr[:, :hr]], axis=1), LANES)
    uq = w_uq.reshape(Q_LORA, A_HEADS, A_NOPE + A_ROPE)
    zeros = jnp.zeros((Q_LORA, A_HEADS, LANES - A_ROPE), w_uq.dtype)
    wq = jnp.concatenate([uq, zeros], axis=-1).reshape(Q_LORA, A_HEADS * 2 * LANES)
    ukv = w_ukv.reshape(KV_LORA, A_HEADS, A_NOPE + A_VDIM)
    return dict(
        wm=w_in[:, :4 * M_W].astype(BF16),
        wside=jnp.concatenate([_pad_cols(w_gate, LANES), w_cq, w_ckv, wkra, wkrb], axis=1).astype(BF16),
        bg=_pad_cols(jnp.concatenate([b_ig, b_fg]).reshape(1, 2 * M_HEADS).astype(F32), LANES),
        qn=q_norm.reshape(1, Q_LORA).astype(F32), kvn=kv_norm.reshape(1, KV_LORA).astype(F32),
        wq=wq.astype(BF16),
        wuk=ukv[..., :A_NOPE].reshape(KV_LORA, A_HEADS * A_NOPE).astype(BF16),
        wuv=ukv[..., A_NOPE:].reshape(KV_LORA, A_HEADS * A_VDIM).astype(BF16),
    )


def _rope_tables(pos):
    half = A_ROPE // 2
    inv_freq = ROPE_THETA ** (-jnp.arange(half, dtype=F32) / half)
    ang = pos.astype(F32)[:, None] * inv_freq[None, :]
    cos, sin = jnp.cos(ang), jnp.sin(ang)
    pad = jnp.zeros((pos.shape[0], LANES - A_ROPE), F32)
    return (jnp.concatenate([cos, cos, pad], axis=1), jnp.concatenate([-sin, sin, pad], axis=1))


def _s5_weights(a_re, a_im, b_re, b_im, c_re, c_im, d, log_dt, w_glu):
    dt = jnp.exp(log_dt.astype(F32))[:, None]
    ar, ai = a_re.astype(F32), a_im.astype(F32)
    mag = jnp.exp(dt * ar)
    ab_r, ab_i = mag * jnp.cos(dt * ai), mag * jnp.sin(dt * ai)
    den = ar * ar + ai * ai
    c_r = ((ab_r - 1.0) * ar + ab_i * ai) / den
    c_i = (ab_i * ar - (ab_r - 1.0) * ai) / den
    br, bi = b_re.astype(F32), b_im.astype(F32)
    bb_r = c_r[..., None] * br - c_i[..., None] * bi
    bb_i = c_r[..., None] * bi + c_i[..., None] * br
    gb = S5_GROUPS // S5_BLOCKS
    eye = jnp.eye(gb, dtype=F32)
    bd_in = lambda m: jnp.einsum("kgpc,gh->kgchp", m.reshape(S5_BLOCKS, gb, S5_P, S5_GSIZE),
                                 eye).reshape(S5_BLOCKS, S5_BLK_IN, S5_BLK_ST)
    bd_out = lambda m: jnp.einsum("kgcp,gh->kgphc", m.reshape(S5_BLOCKS, gb, S5_GSIZE, S5_P),
                                  eye).reshape(S5_BLOCKS, S5_BLK_ST, S5_BLK_IN)
    return dict(
        bbr=bd_in(bb_r).astype(BF16), bbi=bd_in(bb_i).astype(BF16),
        ar=ab_r.reshape(1, S5_STATE), ai=ab_i.reshape(1, S5_STATE),
        ccr=bd_out(c_re.astype(F32)).astype(BF16), cci=bd_out(-c_im.astype(F32)).astype(BF16),
        d=d.reshape(1, S5_W).astype(F32), wglu=w_glu.astype(BF16),
    )


def _pad_keys(a, mult):
    pad = (-a.shape[1]) % mult
    return jnp.pad(a, ((0, 0), (0, pad), (0, 0))) if pad else a


def _trunk(x, pos0, cache, p):
    nb, l, _ = x.shape
    assert nb == SUBLANES, "the S5 layout puts the streams of one time step on the sublanes"
    assert l % CHUNK == 0
    n = nb * l
    tm = min(TOKEN_TILE, n)
    tl = min(tm, l)
    assert l % tl == 0 and tm % tl == 0
    past = 0 if cache is None else cache["sb_k"].shape[1]
    assert pos0 == past
    xf = x.reshape(n, D_MODEL)
    bf = lambda a: a.astype(BF16)
    ffn_w = (bf(p["ffn_w_gate"]), bf(p["ffn_w_up"]), bf(p["ffn_w_down"]))

    xf = _ffn(xf, p["norm_ffn"][0, 0], *ffn_w, (0, 0), tm=tm)
    ew = _even_weights(p["even_w_in"][0], p["mlstm_b_igate"][0], p["mlstm_b_fgate"][0],
                       p["mla_q_norm"][0], p["mla_kv_norm"][0], p["mla_w_uq"][0], p["mla_w_ukv"][0])
    cos, sin = _rope_tables(pos0 + jnp.arange(l))
    if tm > l:
        cos, sin = jnp.tile(cos, (tm // l, 1)), jnp.tile(sin, (tm // l, 1))
    qm, km, vm, om, gates, q, kn, v, kr128, lat, kr = _even_in(
        xf, p["norm_mix"][0], ew, cos, sin, tm=tm, seq_len=l)
    v_transposed = v.ndim == 3

    if cache is None:
        c0 = jnp.zeros((nb, M_HEADS, M_DH, M_DH), F32)
        n0 = jnp.zeros((nb, M_HEADS, 1, M_DH), F32)
        m0 = jnp.zeros((nb, M_HEADS, 1, LANES), F32)
    else:
        c0 = cache["mlstm_c"].astype(F32)
        n0 = cache["mlstm_n"].astype(F32).reshape(nb, M_HEADS, 1, M_DH)
        m0 = jnp.broadcast_to(cache["mlstm_m"].astype(F32)[..., None, None], (nb, M_HEADS, 1, LANES))
    r3 = lambda a: a.reshape(nb, l, a.shape[-1])
    h_m, c_new, n_new, m_new = _mlstm(r3(qm), r3(km), r3(vm), r3(om), r3(gates),
                                      p["mlstm_out_norm"][0], c0, n0, m0, tl=tl)

    mla_tq = min(MLA_TILE, -(-l // LANES) * LANES)
    if cache is None:
        mla_sub = mla_tq
        mla_tkb = min(2 * mla_tq, l)
    else:
        mla_sub = mla_tkb = -(-(past + max(l, mla_tq)) // LANES) * LANES
    kn3, kr3 = r3(kn), r3(kr128)
    v3, vt = (None, v) if v_transposed else (r3(v), None)
    if cache is not None:
        ckn, cv = _kv_expand(cache["mla_latent"].astype(F32).reshape(nb * past, KV_LORA),
                             ew["wuk"], ew["wuv"])
        ckr = jnp.pad(cache["mla_krope"], ((0, 0), (0, 0), (0, LANES - A_ROPE))).astype(BF16)
        kn3 = jnp.concatenate([ckn.reshape(nb, past, -1), kn3], axis=1)
        kr3 = jnp.concatenate([ckr, kr3], axis=1)
        cv3 = cv.reshape(nb, past, -1)
        if v_transposed:
            vt = jnp.concatenate([_transpose_tiles(cv3, math.gcd(past, TOKEN_TILE)), vt], axis=2)
        else:
            v3 = jnp.concatenate([cv3, v3], axis=1)
    if vt is None:
        vt = _transpose_tiles(_pad_keys(v3, mla_tkb), mla_sub)
    else:
        vt = jnp.pad(vt, ((0, 0), (0, 0), (0, (-vt.shape[2]) % mla_tkb)))
    h_a = _mla_attention(_pad_keys(r3(q), mla_tq), _pad_keys(kn3, mla_tkb), _pad_keys(kr3, mla_tkb),
                         vt, tq=mla_tq, tkb=mla_tkb, sub=mla_sub, offset=past)[:, :l]

    w_out = bf(p["even_w_out"][0])
    xf = _ffn(xf, p["norm_ffn"][0, 1], *ffn_w, (0, 1),
              mix=(h_m.reshape(n, M_W), h_a.reshape(n, -1), w_out[:M_W], w_out[M_W:]), tm=tm)

    xf = _ffn(xf, p["norm_ffn"][1, 0], *ffn_w, (1, 0), tm=tm)
    w_in = p["odd_w_in"][0]
    ow = dict(wu=bf(w_in[:, :S5_W]), wq=bf(w_in[:, S5_W:S5_W + SB_W]),
              wk=bf(w_in[:, S5_W + SB_W:S5_W + 2 * SB_W]), wv=bf(w_in[:, S5_W + 2 * SB_W:]))
    u, qs, ks, vs, ksb, vsb = _odd_in(xf, p["norm_mix"][1], ow, tm=tm)

    sw = _s5_weights(p["s5_A_re"][0], p["s5_A_im"][0], p["s5_B_re"][0], p["s5_B_im"][0],
                     p["s5_C_re"][0], p["s5_C_im"][0], p["s5_D"][0], p["s5_log_dt"][0], p["s5_w_glu"][0])
    if cache is None:
        x0r = jnp.zeros((nb, S5_STATE), F32)
        x0i = jnp.zeros((nb, S5_STATE), F32)
    else:
        x0r = cache["s5_re"].astype(F32).reshape(nb, S5_STATE)
        x0i = cache["s5_im"].astype(F32).reshape(nb, S5_STATE)
    h_s, xr_new, xi_new = _s5(r3(u), sw, x0r, x0i, tt=min(S5_TT, l))

    k3, v3 = r3(ksb), r3(vsb)
    if cache is not None:
        k3 = jnp.concatenate([bf(cache["sb_k"].reshape(nb, past, SB_W)), k3], axis=1)
        v3 = jnp.concatenate([bf(cache["sb_v"].reshape(nb, past, SB_W)), v3], axis=1)
    h_b = _sb_attention(r3(qs), _pad_keys(k3, SB_TILE), _pad_keys(v3, SB_TILE), tq=min(SB_TILE, l),
                        tk=SB_TILE, offset=past)

    w_out = bf(p["odd_w_out"][0])
    y = _ffn(xf, p["norm_ffn"][1, 1], *ffn_w, (1, 1),
             mix=(h_s.reshape(n, S5_W), h_b.reshape(n, SB_W), w_out[:S5_W], w_out[S5_W:]),
             final_gain=p["norm_final"], tm=tm)

    return (y.reshape(nb, l, D_MODEL),
            lat.reshape(1, nb, l, KV_LORA), kr.reshape(1, nb, l, A_ROPE),
            c_new[None], n_new.reshape(1, nb, M_HEADS, M_DH), m_new[:, :, 0, 0][None],
            xr_new.reshape(1, nb, S5_GROUPS, S5_P), xi_new.reshape(1, nb, S5_GROUPS, S5_P),
            ks.reshape(1, nb, l, SB_HEADS, SB_DH), vs.reshape(1, nb, l, SB_HEADS, SB_DH))


def kernel(x_prompt, x_sample, cache_mla_latent, cache_mla_krope, state_mlstm_C, state_mlstm_n, state_mlstm_m, state_s5_re, state_s5_im, cache_sb_k, cache_sb_v, norm_ffn, norm_mix, norm_final, ffn_w_gate, ffn_w_up, ffn_w_down, even_w_in, even_w_out, mlstm_b_igate, mlstm_b_fgate, mlstm_out_norm, mla_q_norm, mla_kv_norm, mla_w_uq, mla_w_ukv, odd_w_in, odd_w_out, s5_A_re, s5_A_im, s5_B_re, s5_B_im, s5_C_re, s5_C_im, s5_D, s5_log_dt, s5_w_glu):
    p = dict(norm_ffn=norm_ffn, norm_mix=norm_mix, norm_final=norm_final, ffn_w_gate=ffn_w_gate,
             ffn_w_up=ffn_w_up, ffn_w_down=ffn_w_down, even_w_in=even_w_in, even_w_out=even_w_out,
             mlstm_b_igate=mlstm_b_igate, mlstm_b_fgate=mlstm_b_fgate, mlstm_out_norm=mlstm_out_norm,
             mla_q_norm=mla_q_norm, mla_kv_norm=mla_kv_norm, mla_w_uq=mla_w_uq, mla_w_ukv=mla_w_ukv,
             odd_w_in=odd_w_in, odd_w_out=odd_w_out, s5_A_re=s5_A_re, s5_A_im=s5_A_im, s5_B_re=s5_B_re,
             s5_B_im=s5_B_im, s5_C_re=s5_C_re, s5_C_im=s5_C_im, s5_D=s5_D, s5_log_dt=s5_log_dt,
             s5_w_glu=s5_w_glu)
    past = cache_sb_k.shape[2]
    cache = dict(mla_latent=cache_mla_latent[0], mla_krope=cache_mla_krope[0],
                 mlstm_c=state_mlstm_C[0], mlstm_n=state_mlstm_n[0], mlstm_m=state_mlstm_m[0],
                 s5_re=state_s5_re[0], s5_im=state_s5_im[0],
                 sb_k=cache_sb_k[0], sb_v=cache_sb_v[0])
    out_p = _trunk(x_prompt, 0, None, p)
    out_s = _trunk(x_sample, past, cache, p)
    return (out_p[0], out_s[0]) + tuple(out_p[1:]) + tuple(out_s[1:])
```

```python
import functools
import math

import jax
import jax.numpy as jnp
import numpy as np
from jax import lax
from jax.experimental import pallas as pl
from jax.experimental.pallas import tpu as pltpu

F32 = jnp.float32
BF16 = jnp.bfloat16

D_MODEL = 1024
CHUNK = 64
D_FF = 2816
EPS = 1e-6
M_HEADS = 4
M_DH = 128
M_W = M_HEADS * M_DH
A_HEADS = 4
A_NOPE = 128
A_ROPE = 64
A_VDIM = 128
Q_LORA = 384
KV_LORA = 256
ROPE_THETA = 10000.0
MLA_SCALE = (A_NOPE + A_ROPE) ** -0.5
S5_W = 512
S5_GSIZE = 16
S5_GROUPS = S5_W // S5_GSIZE
S5_P = 64
S5_STATE = S5_GROUPS * S5_P
SB_HEADS = 8
SB_DH = 64
SB_W = SB_HEADS * SB_DH

LANES = 128
SUBLANES = 8
VMEM_LIMIT_BYTES = 56 * 1024 * 1024

TOKEN_TILE = 512
FFN_SUB_TILE = 512
FFN_TILES_PER_STEP = 2
FF_CHUNK = 256
MLA_TILE = 512
SB_TILE = 128
SB_TILES_PER_STEP = 4
S5_TT = 128
MLSTM_STREAMS = 8
MLSTM_TIME_TILE = 256
SB_LOG_FLOOR = -24.0
LOG2E = math.log2(math.e)

HIGHEST = lax.Precision.HIGHEST


def _cparams(*sem):
    return pltpu.CompilerParams(dimension_semantics=sem, vmem_limit_bytes=VMEM_LIMIT_BYTES)


def _bdot(a, b):
    return jnp.dot(a.astype(BF16), b.astype(BF16), preferred_element_type=F32)


def _bdot_nt(a, b):
    return lax.dot_general(a.astype(BF16), b.astype(BF16), (((1,), (1,)), ((), ())),
                           preferred_element_type=F32)


def _rms(x, g):
    return x * lax.rsqrt(jnp.mean(x * x, axis=-1, keepdims=True) + EPS) * g


def _sigmoid(x):
    return 1.0 / (1.0 + jnp.exp(-x))


def _log_sigmoid(x):
    return jnp.minimum(x, 0.0) - jnp.log(1.0 + jnp.exp(-jnp.abs(x)))


def _const_spec(shape):
    nd = len(shape)
    return pl.BlockSpec(shape, lambda *_: (0,) * nd)


def _ffn_kernel(*refs, with_mix, with_final):
    it = iter(refs)
    x_ref = next(it)
    if with_mix:
        a_ref, b_ref, wa_ref, wb_ref = next(it), next(it), next(it), next(it)
    g_ref, wg_ref, wu_ref, wd_ref = next(it), next(it), next(it), next(it)
    gf_ref = next(it) if with_final else None
    o_ref = next(it)
    acc_ref = next(it)

    for r0 in range(0, x_ref.shape[0], FFN_SUB_TILE):
        rs = slice(r0, min(r0 + FFN_SUB_TILE, x_ref.shape[0]))
        x = x_ref[rs, :]
        if with_mix:
            x = x + _bdot(a_ref[rs, :], wa_ref[...]) + _bdot(b_ref[rs, :], wb_ref[...])
        h = _rms(x, g_ref[...]).astype(BF16)
        acc_ref[rs, :] = jnp.zeros((rs.stop - rs.start, D_MODEL), F32)
        for c in range(D_FF // FF_CHUNK):
            sl = slice(c * FF_CHUNK, (c + 1) * FF_CHUNK)
            g = jnp.dot(h, wg_ref[:, sl], preferred_element_type=F32)
            u = jnp.dot(h, wu_ref[:, sl], preferred_element_type=F32)
            a = (g * _sigmoid(g) * u).astype(BF16)
            acc_ref[rs, :] += jnp.dot(a, wd_ref[sl, :], preferred_element_type=F32)
        y = x + 0.5 * acc_ref[rs, :]
        if with_final:
            y = _rms(y, gf_ref[...])
        o_ref[rs, :] = y


def _ffn(x, gain, wg, wu, wd, which, *, mix=None, final_gain=None, tm):
    n = x.shape[0]
    if n % (FFN_TILES_PER_STEP * tm) == 0:
        tm = FFN_TILES_PER_STEP * tm
    row = lambda i: (i, 0)
    pick = lambda i: which + (0, 0)
    once = pl.Buffered(1)
    args, specs = [x], [pl.BlockSpec((tm, D_MODEL), row)]
    if mix is not None:
        a, b, wa, wb = mix
        args += [a, b, wa, wb]
        specs += [pl.BlockSpec((tm, a.shape[-1]), row), pl.BlockSpec((tm, b.shape[-1]), row),
                  pl.BlockSpec(wa.shape, lambda i: (0, 0), pipeline_mode=once),
                  pl.BlockSpec(wb.shape, lambda i: (0, 0), pipeline_mode=once)]
    args += [gain.reshape(1, D_MODEL), wg, wu, wd]
    specs += [_const_spec((1, D_MODEL))] + [
        pl.BlockSpec((None, None) + w.shape[2:], pick, pipeline_mode=once) for w in (wg, wu, wd)]
    if final_gain is not None:
        args.append(final_gain.reshape(1, D_MODEL))
        specs.append(_const_spec((1, D_MODEL)))
    return pl.pallas_call(
        functools.partial(_ffn_kernel, with_mix=mix is not None, with_final=final_gain is not None),
        grid=(n // tm,),
        in_specs=specs,
        out_specs=pl.BlockSpec((tm, D_MODEL), row),
        out_shape=jax.ShapeDtypeStruct((n, D_MODEL), F32),
        scratch_shapes=[pltpu.VMEM((tm, D_MODEL), F32)],
        compiler_params=_cparams("parallel"),
        name="ffn",
    )(*args)


def _even_in_kernel(x_ref, gain_ref, wm_ref, wside_ref, bg_ref, qn_ref, kvn_ref, wq_ref,
                    wuk_ref, wuv_ref, cos_ref, sin_ref,
                    qm_o, km_o, vm_o, om_o, gates_o, q_o, kn_o, v_o, kr128_o, lat_o, kr_o,
                    *, v_transposed):
    h = _rms(x_ref[...], gain_ref[...]).astype(BF16)
    m = jnp.dot(h, wm_ref[...], preferred_element_type=F32)
    qm_o[...] = m[:, :M_W].astype(BF16)
    km_o[...] = (m[:, M_W:2 * M_W] * M_DH ** -0.5).astype(BF16)
    vm_o[...] = m[:, 2 * M_W:3 * M_W].astype(BF16)
    om_o[...] = m[:, 3 * M_W:]

    side = jnp.dot(h, wside_ref[...], preferred_element_type=F32)
    o_cq, o_ckv, o_kra, o_krb = LANES, LANES + Q_LORA, LANES + Q_LORA + KV_LORA, 2 * LANES + Q_LORA + KV_LORA
    g = side[:, :LANES] + bg_ref[...]
    lane = lax.broadcasted_iota(jnp.int32, g.shape, 1)
    gates_o[...] = jnp.where(lane < M_HEADS, g, _log_sigmoid(g))

    cos, sin = cos_ref[...], sin_ref[...]
    cq = side[:, o_cq:o_ckv]
    cqn = _rms(cq, qn_ref[...]).astype(BF16)
    qf = jnp.dot(cqn, wq_ref[...], preferred_element_type=F32)
    q_scale = MLA_SCALE * LOG2E
    first_half = lax.broadcasted_iota(jnp.int32, (1, LANES), 1) < A_ROPE // 2
    for hh in range(A_HEADS):
        base = hh * 2 * LANES
        q_o[:, base:base + LANES] = (qf[:, base:base + LANES] * q_scale).astype(BF16)
        ra = qf[:, base + LANES:base + 2 * LANES]
        rb = jnp.where(first_half, pltpu.roll(ra, LANES - A_ROPE // 2, 1), pltpu.roll(ra, A_ROPE // 2, 1))
        rot = ra * cos + rb * sin
        q_o[:, base + LANES:base + 2 * LANES] = (rot * q_scale).astype(BF16)

    ckv = side[:, o_ckv:o_kra]
    lat = _rms(ckv, kvn_ref[...])
    lat_o[...] = lat
    latb = lat.astype(BF16)
    kn_o[...] = jnp.dot(latb, wuk_ref[...], preferred_element_type=F32).astype(BF16)
    if v_transposed:
        v_o[0] = _bdot_nt(wuv_ref[...], latb).astype(BF16)
    else:
        v_o[...] = jnp.dot(latb, wuv_ref[...], preferred_element_type=F32).astype(BF16)

    kr = side[:, o_kra:o_krb] * cos + side[:, o_krb:] * sin
    kr128_o[...] = kr.astype(BF16)
    kr_o[...] = kr[:, :A_ROPE]


def _even_in(x, gain, w, cos, sin, *, tm, seq_len):
    n = x.shape[0]
    row = lambda i: (i, 0)
    n_pos_tiles = cos.shape[0] // tm
    pos = lambda i: (i % n_pos_tiles, 0)
    v_transposed = seq_len % tm == 0
    n_time_tiles = max(seq_len // tm, 1)
    weights = [w["wm"], w["wside"], w["bg"], w["qn"], w["kvn"],
               w["wq"], w["wuk"], w["wuv"].T if v_transposed else w["wuv"]]
    in_specs = ([pl.BlockSpec((tm, D_MODEL), row), _const_spec((1, D_MODEL))]
                + [_const_spec(a.shape) for a in weights]
                + [pl.BlockSpec((tm, LANES), pos), pl.BlockSpec((tm, LANES), pos)])
    outs = [(M_W, BF16), (M_W, BF16), (M_W, BF16), (M_W, F32), (LANES, F32),
            (A_HEADS * 2 * LANES, BF16), (A_HEADS * A_NOPE, BF16), (A_HEADS * A_VDIM, BF16),
            (LANES, BF16), (KV_LORA, F32), (A_ROPE, F32)]
    out_specs = [pl.BlockSpec((tm, c), row) for c, _ in outs]
    out_shape = [jax.ShapeDtypeStruct((n, c), dt) for c, dt in outs]
    if v_transposed:
        v_idx = 7
        out_specs[v_idx] = pl.BlockSpec((1, A_HEADS * A_VDIM, tm),
                                        lambda i: (i // n_time_tiles, 0, i % n_time_tiles))
        out_shape[v_idx] = jax.ShapeDtypeStruct((n // seq_len, A_HEADS * A_VDIM, seq_len), BF16)
    return pl.pallas_call(
        functools.partial(_even_in_kernel, v_transposed=v_transposed),
        grid=(n // tm,),
        in_specs=in_specs,
        out_specs=out_specs,
        out_shape=out_shape,
        compiler_params=_cparams("parallel"),
        name="even_in",
    )(x, gain.reshape(1, D_MODEL), *weights, cos, sin)


def _kv_expand_kernel(lat_ref, wuk_ref, wuv_ref, kn_o, v_o):
    latb = lat_ref[...].astype(BF16)
    kn_o[...] = jnp.dot(latb, wuk_ref[...], preferred_element_type=F32).astype(BF16)
    v_o[...] = jnp.dot(latb, wuv_ref[...], preferred_element_type=F32).astype(BF16)


def _kv_expand(lat, wuk, wuv):
    n = lat.shape[0]
    tm = min(4 * TOKEN_TILE, n)
    assert n % tm == 0
    row = lambda i: (i, 0)
    return pl.pallas_call(
        _kv_expand_kernel,
        grid=(n // tm,),
        in_specs=[pl.BlockSpec((tm, KV_LORA), row), _const_spec(wuk.shape), _const_spec(wuv.shape)],
        out_specs=[pl.BlockSpec((tm, wuk.shape[1]), row), pl.BlockSpec((tm, wuv.shape[1]), row)],
        out_shape=[jax.ShapeDtypeStruct((n, wuk.shape[1]), BF16),
                   jax.ShapeDtypeStruct((n, wuv.shape[1]), BF16)],
        compiler_params=_cparams("parallel"),
        name="kv_expand",
    )(lat, wuk, wuv)


def _mlstm_kernel(qm_ref, km_ref, vm_ref, om_ref, g_ref, onorm_ref, c0_ref, n0_ref, m0_ref,
                  h_ref, cout_ref, nout_ref, mout_ref, c_sc, n_sc, m_sc, *, n_chunks, n_streams):
    lt = pl.program_id(1)

    @pl.when(lt == 0)
    def _():
        c_sc[...] = c0_ref[...]
        n_sc[...] = n0_ref[...]
        m_sc[...] = m0_ref[...]

    row = lax.broadcasted_iota(jnp.int32, (CHUNK, CHUNK), 0)
    col = lax.broadcasted_iota(jnp.int32, (CHUNK, CHUNK), 1)
    causal = col <= row
    ltri = causal.astype(BF16)
    utri = (row <= col).astype(BF16)
    ones_kl = jnp.ones((CHUNK, LANES), BF16)

    def split3(a):
        hi = a.astype(BF16)
        r1 = a - hi.astype(F32)
        mid = r1.astype(BF16)
        return hi, mid, (r1 - mid.astype(F32)).astype(BF16)

    def chunk(c, carry):
        r0 = pl.multiple_of(c * CHUNK, CHUNK)
        rows = pl.ds(r0, CHUNK)
        chains = [(bb, hh) for bb in range(n_streams) for hh in range(M_HEADS)]
        hsl = lambda hh: slice(hh * M_DH, (hh + 1) * M_DH)
        gate = {}
        for bb in range(n_streams):
            gates = g_ref[bb, rows, :]
            gates_t = gates.T
            cc = jnp.dot(ltri, jnp.concatenate(split3(gates), axis=1), preferred_element_type=F32)
            cum_col = cc[:, :LANES] + cc[:, LANES:2 * LANES] + cc[:, 2 * LANES:]
            cr = jnp.dot(jnp.concatenate(split3(gates_t), axis=0), utri, preferred_element_type=F32)
            cum_row = cr[:LANES] + cr[LANES:2 * LANES] + cr[2 * LANES:]
            gate[bb] = (gates, gates_t, cum_col, cum_row)
        qk, qc, qn = {}, {}, {}
        for ch in chains:
            bb, hh = ch
            rhs = jnp.concatenate([c_sc[bb, hh].astype(BF16),
                                   jnp.broadcast_to(n_sc[bb, hh], (LANES, M_DH)).astype(BF16),
                                   km_ref[bb, rows, hsl(hh)]], axis=0)
            r = _bdot_nt(qm_ref[bb, rows, hsl(hh)], rhs)
            qc[ch], qn[ch], qk[ch] = r[:, :M_DH], r[:, M_DH:M_DH + LANES], r[:, M_DH + LANES:]
        st = {}
        for ch in chains:
            bb, hh = ch
            gates, gates_t, cum_col, cum_row = gate[bb]
            ig_row = gates_t[hh:hh + 1, :]
            ig_col = gates[:, hh:hh + 1]
            b_row = cum_row[M_HEADS + hh:M_HEADS + hh + 1, :]
            b_col = cum_col[:, M_HEADS + hh:M_HEADS + hh + 1]
            m_prev = m_sc[bb, hh][:, :1]
            b_lanes = jnp.broadcast_to(b_col, (CHUNK, LANES))
            log_d = jnp.where(causal, b_lanes[:, :CHUNK] - b_row + ig_row, -jnp.inf)
            m_t = jnp.maximum(b_col + m_prev, jnp.max(log_d, axis=-1, keepdims=True))
            m_lanes = jnp.broadcast_to(m_t, (CHUNK, LANES))
            s = (qk[ch] * jnp.exp(log_d - m_lanes[:, :CHUNK])).astype(BF16)
            w_inter = jnp.exp(b_lanes + m_prev - m_lanes)
            m_new = m_t[CHUNK - 1:CHUNK, :]
            b_last = b_col[CHUNK - 1:CHUNK, :]
            w_s = jnp.exp(b_last - b_col + ig_col - m_new)
            decay = jnp.exp(b_last + m_prev - m_new)
            r = jnp.dot(s, jnp.concatenate([vm_ref[bb, rows, hsl(hh)], ones_kl], axis=1),
                        preferred_element_type=F32)
            st[ch] = (r[:, :M_DH], r[:, M_DH:], w_inter, m_lanes, m_new, w_s, decay)
        for ch in chains:
            bb, hh = ch
            sv, s_sum, w_inter, m_lanes, m_new, w_s, decay = st[ch]
            k = km_ref[bb, rows, hsl(hh)]
            v = vm_ref[bb, rows, hsl(hh)]
            c_h = c_sc[bb, hh]
            n_h = n_sc[bb, hh]
            num = w_inter * qc[ch] + sv
            den = w_inter * qn[ch] + s_sum
            hv = num / jnp.maximum(jnp.abs(den), jnp.exp(-m_lanes))
            wv_t = (jnp.broadcast_to(w_s, (CHUNK, M_DH)) * v).T
            c_sc[bb, hh] = decay * c_h + _bdot(wv_t, k)
            n_sc[bb, hh] = decay * n_h + jnp.sum(w_s * k.astype(F32), axis=0, keepdims=True)
            m_sc[bb, hh] = jnp.broadcast_to(m_new, (1, LANES))
            hn = _rms(hv, onorm_ref[:, hsl(hh)])
            h_ref[bb, rows, hsl(hh)] = (_sigmoid(om_ref[bb, rows, hsl(hh)]) * hn).astype(BF16)
        return carry

    lax.fori_loop(0, n_chunks, chunk, 0)

    @pl.when(lt == pl.num_programs(1) - 1)
    def _():
        cout_ref[...] = c_sc[...]
        nout_ref[...] = n_sc[...]
        mout_ref[...] = m_sc[...]


def _mlstm(qm, km, vm, om, gates, onorm, c0, n0, m0, *, tl):
    b, l, _ = qm.shape
    ns = MLSTM_STREAMS
    tile = lambda i, j: (i, j, 0)
    st4 = lambda i, j: (i, 0, 0, 0)
    state = [((ns, M_HEADS, M_DH, M_DH), (b, M_HEADS, M_DH, M_DH)),
             ((ns, M_HEADS, 1, M_DH), (b, M_HEADS, 1, M_DH)),
             ((ns, M_HEADS, 1, LANES), (b, M_HEADS, 1, LANES))]
    return pl.pallas_call(
        functools.partial(_mlstm_kernel, n_chunks=tl // CHUNK, n_streams=ns),
        grid=(b // ns, l // tl),
        in_specs=[pl.BlockSpec((ns, tl, M_W), tile)] * 4
        + [pl.BlockSpec((ns, tl, LANES), tile), _const_spec((1, M_W))]
        + [pl.BlockSpec(blk, st4) for blk, _ in state],
        out_specs=[pl.BlockSpec((ns, tl, M_W), tile)] + [pl.BlockSpec(blk, st4) for blk, _ in state],
        out_shape=[jax.ShapeDtypeStruct((b, l, M_W), BF16)]
        + [jax.ShapeDtypeStruct(full, F32) for _, full in state],
        scratch_shapes=[pltpu.VMEM(blk, F32) for blk, _ in state],
        compiler_params=_cparams("parallel", "arbitrary"),
        name="mlstm",
    )(qm, km, vm, om, gates, onorm.reshape(1, M_W), c0, n0, m0)


MLA_SUM_ROWS = 16


MLA_EXP_RANGE = 60.0


def _mla_kernel(qi_ref, kj_ref, flag_ref, q_ref, kn_ref, kr_ref, vt_ref, o_ref, m_sc, acc_sc, bad_sc,
                redo_sm, *, tq, tkb, offset, variants):
    p = pl.program_id(1)
    i = qi_ref[p]
    j = kj_ref[p]
    flags = flag_ref[p]

    @pl.when(j == 0)
    def _():
        m_sc[...] = jnp.zeros_like(m_sc)
        acc_sc[...] = jnp.zeros_like(acc_sc)

    def score_tiles(nk, masked):
        kr = kr_ref[0, :nk, :]
        scores = []
        for hh in range(A_HEADS):
            q = q_ref[0, :, hh * 2 * LANES:(hh + 1) * 2 * LANES]
            k = jnp.concatenate([kn_ref[0, :nk, hh * A_NOPE:(hh + 1) * A_NOPE], kr], axis=-1)
            scores.append(_bdot_nt(k, q))
        if masked:
            k_chunk = (j * tkb + lax.broadcasted_iota(jnp.int32, (nk, tq), 0)) // CHUNK
            q_chunk = (offset + i * tq + lax.broadcasted_iota(jnp.int32, (nk, tq), 1)) // CHUNK
            visible = k_chunk <= q_chunk
            scores = [jnp.where(visible, s, -jnp.inf) for s in scores]
        return scores

    def values_t(hh, nk):
        return jnp.concatenate([vt_ref[0, hh * A_VDIM:(hh + 1) * A_VDIM, :nk],
                                jnp.ones((MLA_SUM_ROWS, nk), BF16)], axis=0)

    def add_block(nk, masked):
        scores = score_tiles(nk, masked)
        any_bad = jnp.zeros((1, tq), F32)
        for hh in range(A_HEADS):
            s = scores[hh]
            m_ref = m_sc[hh]
            d = jnp.max(s, axis=0, keepdims=True) - m_ref
            ok = jnp.logical_and(d <= MLA_EXP_RANGE, jnp.logical_or(d >= -MLA_EXP_RANGE, j > 0))
            pr = jnp.exp2(s - m_ref).astype(BF16)
            acc = acc_sc[hh]
            acc_sc[hh] = jnp.where(ok, acc + jnp.dot(values_t(hh, nk), pr, preferred_element_type=F32),
                                   acc)
            bad = jnp.where(ok, 0.0, 1.0)
            bad_sc[hh] = bad
            any_bad = jnp.maximum(any_bad, bad)
        redo_sm[0] = (jnp.max(any_bad) > 0.0).astype(jnp.int32)

    def redo_block(nk, masked):
        scores = score_tiles(nk, masked)
        for hh in range(A_HEADS):
            s = scores[hh]
            bad = bad_sc[hh] > 0.0
            m_prev = jnp.where(j == 0, -jnp.inf, m_sc[hh])
            m_new = jnp.maximum(m_prev, jnp.max(s, axis=0, keepdims=True))
            alpha = jnp.exp2(m_prev - m_new)
            pr = jnp.exp2(s - m_new).astype(BF16)
            acc = acc_sc[hh]
            acc_sc[hh] = jnp.where(
                bad, alpha * acc + jnp.dot(values_t(hh, nk), pr, preferred_element_type=F32), acc)
            m_sc[hh] = jnp.where(bad, m_new, m_sc[hh])

    for vi, (nk, masked) in enumerate(variants):
        pl.when((flags >> 1) == vi)(functools.partial(add_block, nk, masked))
    for vi, (nk, masked) in enumerate(variants):
        pl.when(jnp.logical_and((flags >> 1) == vi, redo_sm[0] != 0))(
            functools.partial(redo_block, nk, masked))

    @pl.when((flags & 1) != 0)
    def _():
        for hh in range(A_HEADS):
            acc = acc_sc[hh]
            o_ref[0, :, hh * A_VDIM:(hh + 1) * A_VDIM] = (
                acc[:A_VDIM] / acc[A_VDIM:A_VDIM + 1]).T.astype(BF16)


def _transpose_kernel(x_ref, o_ref):
    o_ref[0] = x_ref[0].astype(F32).T.astype(o_ref.dtype)


def _transpose_tiles(x, t):
    b, l, c = x.shape
    return pl.pallas_call(
        _transpose_kernel,
        grid=(b, l // t),
        in_specs=[pl.BlockSpec((1, t, c), lambda bb, i: (bb, i, 0))],
        out_specs=pl.BlockSpec((1, c, t), lambda bb, i: (bb, 0, i)),
        out_shape=jax.ShapeDtypeStruct((b, c, l), x.dtype),
        compiler_params=_cparams("parallel", "parallel"),
        name="transpose_tiles",
    )(x)


def _mla_pairs(lq, lk, tq, tkb, sub, offset):
    qi, kj, flags, variants = [], [], [], []
    for i in range(lq // tq):
        first_row, last_row = offset + i * tq, offset + (i + 1) * tq - 1
        k_end = min((last_row // CHUNK + 1) * CHUNK, lk)
        k_all = (first_row // CHUNK + 1) * CHUNK
        nj = -(-k_end // tkb)
        for j in range(nj):
            nk = -(-(min(k_end, (j + 1) * tkb) - j * tkb) // sub) * sub
            variant = (nk, j * tkb + nk > k_all)
            if variant not in variants:
                variants.append(variant)
            qi.append(i)
            kj.append(j)
            flags.append(int(j == nj - 1) + 2 * variants.index(variant))
    return (np.asarray(qi, np.int32), np.asarray(kj, np.int32), np.asarray(flags, np.int32),
            tuple(variants))


def _mla_attention(q, kn, kr, vt, *, tq, tkb, sub, offset):
    b, lq, _ = q.shape
    lk = kn.shape[1]
    assert lk % tkb == 0 and tkb % sub == 0 and vt.shape[2] == lk
    qi, kj, flags, variants = _mla_pairs(lq, lk, tq, tkb, sub, offset)
    qmap = lambda bb, p, qi_r, kj_r, flag_r: (bb, qi_r[p], 0)
    kmap = lambda bb, p, qi_r, kj_r, flag_r: (bb, kj_r[p], 0)
    grid_spec = pltpu.PrefetchScalarGridSpec(
        num_scalar_prefetch=3,
        grid=(b, len(qi)),
        in_specs=[pl.BlockSpec((1, tq, q.shape[-1]), qmap),
                  pl.BlockSpec((1, tkb, kn.shape[-1]), kmap),
                  pl.BlockSpec((1, tkb, kr.shape[-1]), kmap),
                  pl.BlockSpec((1, vt.shape[1], tkb), lambda bb, p, qi_r, kj_r, flag_r: (bb, 0, kj_r[p]))],
        out_specs=pl.BlockSpec((1, tq, A_HEADS * A_VDIM), qmap),
        scratch_shapes=[pltpu.VMEM((A_HEADS, 1, tq), F32),
                        pltpu.VMEM((A_HEADS, A_VDIM + MLA_SUM_ROWS, tq), F32),
                        pltpu.VMEM((A_HEADS, 1, tq), F32),
                        pltpu.SMEM((1,), jnp.int32)],
    )
    return pl.pallas_call(
        functools.partial(_mla_kernel, tq=tq, tkb=tkb, offset=offset, variants=variants),
        grid_spec=grid_spec,
        out_shape=jax.ShapeDtypeStruct((b, lq, A_HEADS * A_VDIM), BF16),
        compiler_params=_cparams("parallel", "arbitrary"),
        name="mla_attention",
    )(jnp.asarray(qi), jnp.asarray(kj), jnp.asarray(flags), q, kn, kr, vt)


def _odd_in_kernel(x_ref, gain_ref, wu_ref, wq_ref, wk_ref, wv_ref, u_o, q_o, k_o, v_o, kb_o, vb_o):
    h = _rms(x_ref[...], gain_ref[...]).astype(BF16)
    u_o[...] = jnp.dot(h, wu_ref[...], preferred_element_type=F32)
    q_o[...] = (jnp.dot(h, wq_ref[...], preferred_element_type=F32) * (SB_DH ** -0.5 * LOG2E)).astype(BF16)
    k = jnp.dot(h, wk_ref[...], preferred_element_type=F32)
    v = jnp.dot(h, wv_ref[...], preferred_element_type=F32)
    k_o[...] = pltpu.einshape("t(hd)->thd", k, h=SB_HEADS)
    v_o[...] = pltpu.einshape("t(hd)->thd", v, h=SB_HEADS)
    kb_o[...] = k.astype(BF16)
    vb_o[...] = v.astype(BF16)


def _odd_in(x, gain, w, *, tm):
    n = x.shape[0]
    row = lambda i: (i, 0)
    weights = [w["wu"], w["wq"], w["wk"], w["wv"]]
    return pl.pallas_call(
        _odd_in_kernel,
        grid=(n // tm,),
        in_specs=[pl.BlockSpec((tm, D_MODEL), row), _const_spec((1, D_MODEL))]
        + [_const_spec(a.shape) for a in weights],
        out_specs=[pl.BlockSpec((tm, S5_W), row), pl.BlockSpec((tm, SB_W), row)]
        + [pl.BlockSpec((tm, SB_HEADS, SB_DH), lambda i: (i, 0, 0))] * 2
        + [pl.BlockSpec((tm, SB_W), row)] * 2,
        out_shape=[jax.ShapeDtypeStruct((n, S5_W), F32),
                   jax.ShapeDtypeStruct((n, SB_W), BF16),
                   jax.ShapeDtypeStruct((n, SB_HEADS, SB_DH), F32),
                   jax.ShapeDtypeStruct((n, SB_HEADS, SB_DH), F32),
                   jax.ShapeDtypeStruct((n, SB_W), BF16),
                   jax.ShapeDtypeStruct((n, SB_W), BF16)],
        compiler_params=_cparams("parallel"),
        name="odd_in",
    )(x, gain.reshape(1, D_MODEL), *weights)


S5_BLOCKS = 4
S5_BLK_IN = S5_W // S5_BLOCKS
S5_BLK_ST = S5_STATE // S5_BLOCKS


def _s5_kernel(u_ref, bbr_ref, bbi_ref, ar_ref, ai_ref, ccr_ref, cci_ref, d_ref, wglu_ref,
               x0r_ref, x0i_ref, hs_o, xr_o, xi_o, u_tm, hs_tm, xr_buf, xi_buf, xr_st, xi_st,
               *, tt, nb):
    t = pl.program_id(0)

    @pl.when(t == 0)
    def _():
        xr_st[...] = x0r_ref[...]
        xi_st[...] = x0i_ref[...]

    for bb in range(nb):
        for k in range(S5_BLOCKS):
            u_tm[k, pl.ds(bb, tt, stride=nb), :] = u_ref[bb, :, k * S5_BLK_IN:(k + 1) * S5_BLK_IN]

    for k in range(S5_BLOCKS):
        cs = slice(k * S5_BLK_ST, (k + 1) * S5_BLK_ST)
        ub = u_tm[k].astype(BF16)
        xr_buf[:, cs] = jnp.dot(ub, bbr_ref[k], preferred_element_type=F32)
        xi_buf[:, cs] = jnp.dot(ub, bbi_ref[k], preferred_element_type=F32)

    for k in range(0, S5_BLOCKS, 2):
        cols = [slice((k + d) * S5_BLK_ST, (k + d + 1) * S5_BLK_ST) for d in range(2)]
        decay = [(jnp.broadcast_to(ar_ref[:, cs], (nb, S5_BLK_ST)),
                  jnp.broadcast_to(ai_ref[:, cs], (nb, S5_BLK_ST))) for cs in cols]

        def step(s, carry, cols=cols, decay=decay):
            rows = pl.ds(pl.multiple_of(s * nb, nb), nb)
            out = []
            for (xr, xi), (ar, ai), cs in zip(carry, decay, cols):
                nxr = ar * xr - ai * xi + xr_buf[rows, cs]
                nxi = ar * xi + ai * xr + xi_buf[rows, cs]
                xr_buf[rows, cs] = nxr
                xi_buf[rows, cs] = nxi
                out.append((nxr, nxi))
            return tuple(out)

        final = lax.fori_loop(0, tt, step, tuple((xr_st[:, cs], xi_st[:, cs]) for cs in cols), unroll=8)
        for (xr, xi), cs in zip(final, cols):
            xr_st[:, cs] = xr
            xi_st[:, cs] = xi

    gs = []
    for k in range(S5_BLOCKS):
        cs = slice(k * S5_BLK_ST, (k + 1) * S5_BLK_ST)
        y = (_bdot(xr_buf[:, cs], ccr_ref[k]) + _bdot(xi_buf[:, cs], cci_ref[k])
             + d_ref[:, k * S5_BLK_IN:(k + 1) * S5_BLK_IN] * u_tm[k])
        gs.append(jax.nn.gelu(y, approximate=True))
    g = jnp.concatenate(gs, axis=-1)
    hs = g * _sigmoid(_bdot(g, wglu_ref[...]))
    for k in range(S5_BLOCKS):
        hs_tm[k] = hs[:, k * S5_BLK_IN:(k + 1) * S5_BLK_IN]
    for bb in range(nb):
        for k in range(S5_BLOCKS):
            hs_o[bb, :, k * S5_BLK_IN:(k + 1) * S5_BLK_IN] = (
                hs_tm[k, pl.ds(bb, tt, stride=nb), :].astype(BF16))

    @pl.when(t == pl.num_programs(0) - 1)
    def _():
        xr_o[...] = xr_st[...]
        xi_o[...] = xi_st[...]


def _s5(u, w, x0r, x0i, *, tt):
    nb, l, _ = u.shape
    blk = lambda t: (0, t, 0)
    consts = [w["bbr"], w["bbi"], w["ar"], w["ai"], w["ccr"], w["cci"], w["d"], w["wglu"], x0r, x0i]
    return pl.pallas_call(
        functools.partial(_s5_kernel, tt=tt, nb=nb),
        grid=(l // tt,),
        in_specs=[pl.BlockSpec((nb, tt, S5_W), blk)] + [_const_spec(a.shape) for a in consts],
        out_specs=[pl.BlockSpec((nb, tt, S5_W), blk), _const_spec((nb, S5_STATE)),
                   _const_spec((nb, S5_STATE))],
        out_shape=[jax.ShapeDtypeStruct((nb, l, S5_W), BF16),
                   jax.ShapeDtypeStruct((nb, S5_STATE), F32),
                   jax.ShapeDtypeStruct((nb, S5_STATE), F32)],
        scratch_shapes=[pltpu.VMEM((S5_BLOCKS, tt * nb, S5_BLK_IN), F32),
                        pltpu.VMEM((S5_BLOCKS, tt * nb, S5_BLK_IN), F32),
                        pltpu.VMEM((tt * nb, S5_STATE), F32), pltpu.VMEM((tt * nb, S5_STATE), F32),
                        pltpu.VMEM((nb, S5_STATE), F32), pltpu.VMEM((nb, S5_STATE), F32)],
        compiler_params=_cparams("arbitrary"),
        name="s5",
    )(u, *consts)


def _sb_kernel(q_ref, k_ref, v_ref, o_ref, r_sc, acc_sc, qh_sc, *, tq, tk, offset, n_sub):
    for su in range(n_sub):
        _sb_query_tile(pl.program_id(1) * n_sub + su, slice(su * tq, (su + 1) * tq),
                       q_ref, k_ref, v_ref, o_ref, r_sc, acc_sc, qh_sc, tq=tq, tk=tk, offset=offset)


def _sb_query_tile(i, rs, q_ref, k_ref, v_ref, o_ref, r_sc, acc_sc, qh_sc, *, tq, tk, offset):
    n_pairs = SB_HEADS // 2
    half = lax.broadcasted_iota(jnp.int32, (1, LANES), 1) // SB_DH
    for pr in range(n_pairs):
        qq = q_ref[0, rs, pr * LANES:(pr + 1) * LANES]
        qh_sc[pr, :tq] = jnp.where(half == 0, qq, jnp.zeros_like(qq))
        qh_sc[pr, tq:] = jnp.where(half == 1, qq, jnp.zeros_like(qq))
    r_sc[...] = jnp.zeros_like(r_sc)
    acc_sc[...] = jnp.zeros_like(acc_sc)

    hk = tk // 2
    jrow = lax.broadcasted_iota(jnp.int32, (tk, tk), 0)
    jcol = lax.broadcasted_iota(jnp.int32, (tk, tk), 1)
    later = (jrow > jcol).astype(BF16)
    later_pk = jnp.logical_and(jrow > jcol, (jrow // hk) == (jcol // hk)).astype(BF16)
    first_half = lax.broadcasted_iota(jnp.int32, (1, tk), 1) < hk
    j_diag = (offset + i * tq) // tk

    def softplus2(z):
        return jnp.maximum(z, 0.0) + jnp.log2(1.0 + jnp.exp2(-jnp.abs(z)))

    def split2(a):
        hi = a.astype(BF16)
        return [hi, (a - hi.astype(F32)).astype(BF16)]

    def by_head(x):
        zero = jnp.zeros_like(x)
        return [jnp.where(half == 0, x, zero), jnp.where(half == 1, x, zero)]

    def sweep(tiles):
        zs, vv, keep_key = {}, {}, {}
        for t, (j, kind, live, limit) in enumerate(tiles):
            k0 = pl.multiple_of(j * tk, tk)
            if kind == "masked":
                q_pos = offset + i * tq + lax.broadcasted_iota(jnp.int32, (tq, tk), 0)
                keep_key[t] = (k0 + lax.broadcasted_iota(jnp.int32, (tq, tk), 1)) < q_pos
            elif limit is not None:
                keep_key[t] = lax.broadcasted_iota(jnp.int32, (tq, tk), 1) < limit
            for pr in range(n_pairs):
                ps = slice(pr * LANES, (pr + 1) * LANES)
                if kind == "half":
                    kh = k_ref[0, pl.ds(k0 + hk, hk), ps]
                    zero = jnp.zeros_like(kh)
                    k2 = jnp.concatenate([jnp.concatenate([kh, zero], axis=1),
                                          jnp.concatenate([zero, kh], axis=1)], axis=0)
                    q2 = jnp.concatenate([qh_sc[pr, :tq], qh_sc[pr, tq:]], axis=1)
                    zs[t, pr] = _bdot_nt(q2, k2)
                    vv[t, pr] = jnp.concatenate(by_head(v_ref[0, pl.ds(k0 + hk, hk), ps]), axis=0)
                else:
                    zs[t, pr] = _bdot_nt(qh_sc[pr], k_ref[0, pl.ds(k0, tk), ps])
                    vv[t, pr] = jnp.concatenate(by_head(v_ref[0, pl.ds(k0, tk), ps]), axis=0)
        l1s, lbs, cums = {}, {}, {}
        for t, (j, kind, live, limit) in enumerate(tiles):
            parts, slots = [], []
            for pr in range(n_pairs):
                for sub in range(1 if kind == "half" else 2):
                    slot = pr if kind == "half" else 2 * pr + sub
                    z = zs[t, pr] if kind == "half" else zs[t, pr][sub * tq:(sub + 1) * tq]
                    sp = softplus2(z)
                    l1 = -sp
                    if t in keep_key:
                        l1 = jnp.where(keep_key[t], l1, 0.0)
                    parts += split2(l1)
                    slots.append(slot)
                    l1s[t, slot], lbs[t, slot] = l1, z - sp
            c = jnp.dot(jnp.concatenate(parts, axis=0), later_pk if kind == "half" else later,
                        preferred_element_type=F32)
            for n, slot in enumerate(slots):
                cums[t, slot] = c[2 * n * tq:(2 * n + 1) * tq] + c[(2 * n + 1) * tq:(2 * n + 2) * tq]
        for t, (j, kind, live, limit) in enumerate(tiles):
            r_max = jnp.full((1, 1), -jnp.inf, F32)
            dead_shift = 0.0 if live is None else jnp.where(live, 0.0, -1e30)
            keep = 1.0 if live is None else jnp.where(live, 1.0, 0.0)
            for pr in range(n_pairs):
                if kind == "half":
                    r0, r1 = r_sc[2 * pr], r_sc[2 * pr + 1]
                    log_w = lbs[t, pr] + cums[t, pr] + (jnp.where(first_half, r0, r1) + dead_shift)
                    w = jnp.exp2(log_w).astype(BF16)
                    s_all = jnp.sum(l1s[t, pr], axis=-1, keepdims=True)
                    s0 = jnp.sum(jnp.where(first_half, l1s[t, pr], 0.0), axis=-1, keepdims=True)
                    r_new = [r0 + keep * s0, r1 + keep * (s_all - s0)]
                else:
                    ws, r_new = [], []
                    for sub in range(2):
                        hh = 2 * pr + sub
                        r_old = r_sc[hh]
                        log_w = lbs[t, hh] + cums[t, hh] + (r_old + dead_shift)
                        if t in keep_key:
                            log_w = jnp.where(keep_key[t], log_w, -jnp.inf)
                        ws.append(jnp.exp2(log_w).astype(BF16))
                        r_new.append(r_old + keep * jnp.sum(l1s[t, hh], axis=-1, keepdims=True))
                    w = jnp.concatenate(ws, axis=1)
                for sub in range(2):
                    r_sc[2 * pr + sub] = r_new[sub]
                    r_max = jnp.maximum(r_max, jnp.max(r_new[sub], axis=0, keepdims=True))
                acc_sc[:, pr * LANES:(pr + 1) * LANES] += jnp.dot(w, vv[t, pr], preferred_element_type=F32)
        return r_max[0, 0]

    has_prev = j_diag >= 1
    r_first = sweep([(j_diag, "masked", None, None), (jnp.maximum(j_diag - 1, 0), "half", has_prev, None)])

    def cond(carry):
        j, r_max, _ = carry
        return jnp.logical_and(j >= 0, r_max > SB_LOG_FLOOR * LOG2E)

    def body(carry):
        j, _, limit = carry
        return j - 1, sweep([(j, "full", None, limit)]), jnp.int32(tk)

    lax.while_loop(cond, body, (j_diag - 1, r_first, jnp.int32(hk)))
    o_ref[0, rs] = acc_sc[...].astype(BF16)


def _sb_attention(q, k, v, *, tq, tk, offset):
    b, lq, _ = q.shape
    lk = k.shape[1]
    assert tk % tq == 0 and offset % tk == 0 and lk % tk == 0
    n_sub = math.gcd(lq // tq, SB_TILES_PER_STEP)
    return pl.pallas_call(
        functools.partial(_sb_kernel, tq=tq, tk=tk, offset=offset, n_sub=n_sub),
        grid=(b, lq // (tq * n_sub)),
        in_specs=[pl.BlockSpec((1, tq * n_sub, SB_W), lambda bb, i: (bb, i, 0)),
                  pl.BlockSpec((1, lk, SB_W), lambda bb, i: (bb, 0, 0)),
                  pl.BlockSpec((1, lk, SB_W), lambda bb, i: (bb, 0, 0))],
        out_specs=pl.BlockSpec((1, tq * n_sub, SB_W), lambda bb, i: (bb, i, 0)),
        out_shape=jax.ShapeDtypeStruct((b, lq, SB_W), BF16),
        scratch_shapes=[pltpu.VMEM((SB_HEADS, tq, 1), F32),
                        pltpu.VMEM((tq, SB_W), F32),
                        pltpu.VMEM((SB_HEADS // 2, 2 * tq, LANES), BF16)],
        compiler_params=_cparams("parallel", "arbitrary"),
        name="sb_attention",
    )(q, k, v)


def _pad_cols(w, width):
    return jnp.pad(w, ((0, 0), (0, width - w.shape[1])))


def _even_weights(w_in, b_ig, b_fg, q_norm, kv_norm, w_uq, w_ukv):
    o = 4 * M_W
    w_gate = w_in[:, o:o + 2 * M_HEADS]
    o += 2 * M_HEADS
    w_cq = w_in[:, o:o + Q_LORA]
    o += Q_LORA
    w_ckv = w_in[:, o:o + KV_LORA]
    o += KV_LORA
    w_kr = w_in[:, o:o + A_ROPE]
    hr = A_ROPE // 2
    wkra = _pad_cols(w_kr, LANES)
    wkrb = _pad_cols(jnp.concatenate([w_kr[:, hr:], w_kr[:, :hr]], axis=1), LANES)
    uq = w_uq.reshape(Q_LORA, A_HEADS, A_NOPE + A_ROPE)
    zeros = jnp.zeros((Q_LORA, A_HEADS, LANES - A_ROPE), w_uq.dtype)
    wq = jnp.concatenate([uq, zeros], axis=-1).reshape(Q_LORA, A_HEADS * 2 * LANES)
    ukv = w_ukv.reshape(KV_LORA, A_HEADS, A_NOPE + A_VDIM)
    return dict(
        wm=w_in[:, :4 * M_W].astype(BF16),
        wside=jnp.concatenate([_pad_cols(w_gate, LANES), w_cq, w_ckv, wkra, wkrb], axis=1).astype(BF16),
        bg=_pad_cols(jnp.concatenate([b_ig, b_fg]).reshape(1, 2 * M_HEADS).astype(F32), LANES),
        qn=q_norm.reshape(1, Q_LORA).astype(F32), kvn=kv_norm.reshape(1, KV_LORA).astype(F32),
        wq=wq.astype(BF16),
        wuk=ukv[..., :A_NOPE].reshape(KV_LORA, A_HEADS * A_NOPE).astype(BF16),
        wuv=ukv[..., A_NOPE:].reshape(KV_LORA, A_HEADS * A_VDIM).astype(BF16),
    )


def _rope_tables(pos):
    half = A_ROPE // 2
    inv_freq = ROPE_THETA ** (-jnp.arange(half, dtype=F32) / half)
    ang = pos.astype(F32)[:, None] * inv_freq[None, :]
    cos, sin = jnp.cos(ang), jnp.sin(ang)
    pad = jnp.zeros((pos.shape[0], LANES - A_ROPE), F32)
    return (jnp.concatenate([cos, cos, pad], axis=1), jnp.concatenate([-sin, sin, pad], axis=1))


def _s5_weights(a_re, a_im, b_re, b_im, c_re, c_im, d, log_dt, w_glu):
    dt = jnp.exp(log_dt.astype(F32))[:, None]
    ar, ai = a_re.astype(F32), a_im.astype(F32)
    mag = jnp.exp(dt * ar)
    ab_r, ab_i = mag * jnp.cos(dt * ai), mag * jnp.sin(dt * ai)
    den = ar * ar + ai * ai
    c_r = ((ab_r - 1.0) * ar + ab_i * ai) / den
    c_i = (ab_i * ar - (ab_r - 1.0) * ai) / den
    br, bi = b_re.astype(F32), b_im.astype(F32)
    bb_r = c_r[..., None] * br - c_i[..., None] * bi
    bb_i = c_r[..., None] * bi + c_i[..., None] * br
    gb = S5_GROUPS // S5_BLOCKS
    eye = jnp.eye(gb, dtype=F32)
    bd_in = lambda m: jnp.einsum("kgpc,gh->kgchp", m.reshape(S5_BLOCKS, gb, S5_P, S5_GSIZE),
                                 eye).reshape(S5_BLOCKS, S5_BLK_IN, S5_BLK_ST)
    bd_out = lambda m: jnp.einsum("kgcp,gh->kgphc", m.reshape(S5_BLOCKS, gb, S5_GSIZE, S5_P),
                                  eye).reshape(S5_BLOCKS, S5_BLK_ST, S5_BLK_IN)
    return dict(
        bbr=bd_in(bb_r).astype(BF16), bbi=bd_in(bb_i).astype(BF16),
        ar=ab_r.reshape(1, S5_STATE), ai=ab_i.reshape(1, S5_STATE),
        ccr=bd_out(c_re.astype(F32)).astype(BF16), cci=bd_out(-c_im.astype(F32)).astype(BF16),
        d=d.reshape(1, S5_W).astype(F32), wglu=w_glu.astype(BF16),
    )


def _pad_keys(a, mult):
    pad = (-a.shape[1]) % mult
    return jnp.pad(a, ((0, 0), (0, pad), (0, 0))) if pad else a


def _trunk(x, pos0, cache, p):
    nb, l, _ = x.shape
    assert nb == SUBLANES, "the S5 layout puts the streams of one time step on the sublanes"
    assert l % CHUNK == 0
    n = nb * l
    tm = min(TOKEN_TILE, n)
    tl = min(tm, l)
    assert l % tl == 0 and tm % tl == 0
    past = 0 if cache is None else cache["sb_k"].shape[1]
    assert pos0 == past
    xf = x.reshape(n, D_MODEL)
    bf = lambda a: a.astype(BF16)
    ffn_w = (bf(p["ffn_w_gate"]), bf(p["ffn_w_up"]), bf(p["ffn_w_down"]))

    xf = _ffn(xf, p["norm_ffn"][0, 0], *ffn_w, (0, 0), tm=tm)
    ew = _even_weights(p["even_w_in"][0], p["mlstm_b_igate"][0], p["mlstm_b_fgate"][0],
                       p["mla_q_norm"][0], p["mla_kv_norm"][0], p["mla_w_uq"][0], p["mla_w_ukv"][0])
    cos, sin = _rope_tables(pos0 + jnp.arange(l))
    if tm > l:
        cos, sin = jnp.tile(cos, (tm // l, 1)), jnp.tile(sin, (tm // l, 1))
    qm, km, vm, om, gates, q, kn, v, kr128, lat, kr = _even_in(
        xf, p["norm_mix"][0], ew, cos, sin, tm=tm, seq_len=l)
    v_transposed = v.ndim == 3

    if cache is None:
        c0 = jnp.zeros((nb, M_HEADS, M_DH, M_DH), F32)
        n0 = jnp.zeros((nb, M_HEADS, 1, M_DH), F32)
        m0 = jnp.zeros((nb, M_HEADS, 1, LANES), F32)
    else:
        c0 = cache["mlstm_c"].astype(F32)
        n0 = cache["mlstm_n"].astype(F32).reshape(nb, M_HEADS, 1, M_DH)
        m0 = jnp.broadcast_to(cache["mlstm_m"].astype(F32)[..., None, None], (nb, M_HEADS, 1, LANES))
    r3 = lambda a: a.reshape(nb, l, a.shape[-1])
    h_m, c_new, n_new, m_new = _mlstm(r3(qm), r3(km), r3(vm), r3(om), r3(gates),
                                      p["mlstm_out_norm"][0], c0, n0, m0, tl=min(tl, MLSTM_TIME_TILE))

    mla_tq = min(MLA_TILE, -(-l // LANES) * LANES)
    if cache is None:
        mla_sub = mla_tq
        mla_tkb = min(2 * mla_tq, l)
    else:
        mla_sub = mla_tkb = -(-(past + max(l, mla_tq)) // LANES) * LANES
    kn3, kr3 = r3(kn), r3(kr128)
    v3, vt = (None, v) if v_transposed else (r3(v), None)
    if cache is not None:
        ckn, cv = _kv_expand(cache["mla_latent"].astype(F32).reshape(nb * past, KV_LORA),
                             ew["wuk"], ew["wuv"])
        ckr = jnp.pad(cache["mla_krope"], ((0, 0), (0, 0), (0, LANES - A_ROPE))).astype(BF16)
        kn3 = jnp.concatenate([ckn.reshape(nb, past, -1), kn3], axis=1)
        kr3 = jnp.concatenate([ckr, kr3], axis=1)
        cv3 = cv.reshape(nb, past, -1)
        if v_transposed:
            vt = jnp.concatenate([_transpose_tiles(cv3, math.gcd(past, TOKEN_TILE)), vt], axis=2)
        else:
            v3 = jnp.concatenate([cv3, v3], axis=1)
    if vt is None:
        vt = _transpose_tiles(_pad_keys(v3, mla_tkb), mla_sub)
    else:
        vt = jnp.pad(vt, ((0, 0), (0, 0), (0, (-vt.shape[2]) % mla_tkb)))
    h_a = _mla_attention(_pad_keys(r3(q), mla_tq), _pad_keys(kn3, mla_tkb), _pad_keys(kr3, mla_tkb),
                         vt, tq=mla_tq, tkb=mla_tkb, sub=mla_sub, offset=past)[:, :l]

    w_out = bf(p["even_w_out"][0])
    xf = _ffn(xf, p["norm_ffn"][0, 1], *ffn_w, (0, 1),
              mix=(h_m.reshape(n, M_W), h_a.reshape(n, -1), w_out[:M_W], w_out[M_W:]), tm=tm)

    xf = _ffn(xf, p["norm_ffn"][1, 0], *ffn_w, (1, 0), tm=tm)
    w_in = p["odd_w_in"][0]
    ow = dict(wu=bf(w_in[:, :S5_W]), wq=bf(w_in[:, S5_W:S5_W + SB_W]),
              wk=bf(w_in[:, S5_W + SB_W:S5_W + 2 * SB_W]), wv=bf(w_in[:, S5_W + 2 * SB_W:]))
    u, qs, ks, vs, ksb, vsb = _odd_in(xf, p["norm_mix"][1], ow, tm=tm)

    sw = _s5_weights(p["s5_A_re"][0], p["s5_A_im"][0], p["s5_B_re"][0], p["s5_B_im"][0],
                     p["s5_C_re"][0], p["s5_C_im"][0], p["s5_D"][0], p["s5_log_dt"][0], p["s5_w_glu"][0])
    if cache is None:
        x0r = jnp.zeros((nb, S5_STATE), F32)
        x0i = jnp.zeros((nb, S5_STATE), F32)
    else:
        x0r = cache["s5_re"].astype(F32).reshape(nb, S5_STATE)
        x0i = cache["s5_im"].astype(F32).reshape(nb, S5_STATE)
    h_s, xr_new, xi_new = _s5(r3(u), sw, x0r, x0i, tt=min(S5_TT, l))

    k3, v3 = r3(ksb), r3(vsb)
    if cache is not None:
        k3 = jnp.concatenate([bf(cache["sb_k"].reshape(nb, past, SB_W)), k3], axis=1)
        v3 = jnp.concatenate([bf(cache["sb_v"].reshape(nb, past, SB_W)), v3], axis=1)
    h_b = _sb_attention(r3(qs), _pad_keys(k3, SB_TILE), _pad_keys(v3, SB_TILE), tq=min(SB_TILE, l),
                        tk=SB_TILE, offset=past)

    w_out = bf(p["odd_w_out"][0])
    y = _ffn(xf, p["norm_ffn"][1, 1], *ffn_w, (1, 1),
             mix=(h_s.reshape(n, S5_W), h_b.reshape(n, SB_W), w_out[:S5_W], w_out[S5_W:]),
             final_gain=p["norm_final"], tm=tm)

    return (y.reshape(nb, l, D_MODEL),
            lat.reshape(1, nb, l, KV_LORA), kr.reshape(1, nb, l, A_ROPE),
            c_new[None], n_new.reshape(1, nb, M_HEADS, M_DH), m_new[:, :, 0, 0][None],
            xr_new.reshape(1, nb, S5_GROUPS, S5_P), xi_new.reshape(1, nb, S5_GROUPS, S5_P),
            ks.reshape(1, nb, l, SB_HEADS, SB_DH), vs.reshape(1, nb, l, SB_HEADS, SB_DH))


def kernel(x_prompt, x_sample, cache_mla_latent, cache_mla_krope, state_mlstm_C, state_mlstm_n, state_mlstm_m, state_s5_re, state_s5_im, cache_sb_k, cache_sb_v, norm_ffn, norm_mix, norm_final, ffn_w_gate, ffn_w_up, ffn_w_down, even_w_in, even_w_out, mlstm_b_igate, mlstm_b_fgate, mlstm_out_norm, mla_q_norm, mla_kv_norm, mla_w_uq, mla_w_ukv, odd_w_in, odd_w_out, s5_A_re, s5_A_im, s5_B_re, s5_B_im, s5_C_re, s5_C_im, s5_D, s5_log_dt, s5_w_glu):
    p = dict(norm_ffn=norm_ffn, norm_mix=norm_mix, norm_final=norm_final, ffn_w_gate=ffn_w_gate,
             ffn_w_up=ffn_w_up, ffn_w_down=ffn_w_down, even_w_in=even_w_in, even_w_out=even_w_out,
             mlstm_b_igate=mlstm_b_igate, mlstm_b_fgate=mlstm_b_fgate, mlstm_out_norm=mlstm_out_norm,
             mla_q_norm=mla_q_norm, mla_kv_norm=mla_kv_norm, mla_w_uq=mla_w_uq, mla_w_ukv=mla_w_ukv,
             odd_w_in=odd_w_in, odd_w_out=odd_w_out, s5_A_re=s5_A_re, s5_A_im=s5_A_im, s5_B_re=s5_B_re,
             s5_B_im=s5_B_im, s5_C_re=s5_C_re, s5_C_im=s5_C_im, s5_D=s5_D, s5_log_dt=s5_log_dt,
             s5_w_glu=s5_w_glu)
    past = cache_sb_k.shape[2]
    cache = dict(mla_latent=cache_mla_latent[0], mla_krope=cache_mla_krope[0],
                 mlstm_c=state_mlstm_C[0], mlstm_n=state_mlstm_n[0], mlstm_m=state_mlstm_m[0],
                 s5_re=state_s5_re[0], s5_im=state_s5_im[0],
                 sb_k=cache_sb_k[0], sb_v=cache_sb_v[0])
    out_p = _trunk(x_prompt, 0, None, p)
    out_s = _trunk(x_sample, past, cache, p)
    return (out_p[0], out_s[0]) + tuple(out_p[1:]) + tuple(out_s[1:])
```

```python
import functools
import math

import jax
import jax.numpy as jnp
import numpy as np
from jax import lax
from jax.experimental import pallas as pl
from jax.experimental.pallas import tpu as pltpu

F32 = jnp.float32
BF16 = jnp.bfloat16

D_MODEL = 1024
CHUNK = 64
D_FF = 2816
EPS = 1e-6
M_HEADS = 4
M_DH = 128
M_W = M_HEADS * M_DH
A_HEADS = 4
A_NOPE = 128
A_ROPE = 64
A_VDIM = 128
Q_LORA = 384
KV_LORA = 256
ROPE_THETA = 10000.0
MLA_SCALE = (A_NOPE + A_ROPE) ** -0.5
S5_W = 512
S5_GSIZE = 16
S5_GROUPS = S5_W // S5_GSIZE
S5_P = 64
S5_STATE = S5_GROUPS * S5_P
SB_HEADS = 8
SB_DH = 64
SB_W = SB_HEADS * SB_DH

LANES = 128
SUBLANES = 8
VMEM_LIMIT_BYTES = 56 * 1024 * 1024

TOKEN_TILE = 512
FFN_SUB_TILE = 512
FFN_TILES_PER_STEP = 2
FF_CHUNK = 256
MLA_TILE = 512
SB_TILE = 128
SB_TILES_PER_STEP = 8
S5_TT = 128
MLSTM_STREAMS = 8
MLSTM_TIME_TILE = 256
SB_LOG_FLOOR = -24.0
LOG2E = math.log2(math.e)

HIGHEST = lax.Precision.HIGHEST


def _cparams(*sem):
    return pltpu.CompilerParams(dimension_semantics=sem, vmem_limit_bytes=VMEM_LIMIT_BYTES)


def _bdot(a, b):
    return jnp.dot(a.astype(BF16), b.astype(BF16), preferred_element_type=F32)


def _bdot_nt(a, b):
    return lax.dot_general(a.astype(BF16), b.astype(BF16), (((1,), (1,)), ((), ())),
                           preferred_element_type=F32)


def _rms(x, g):
    return x * lax.rsqrt(jnp.mean(x * x, axis=-1, keepdims=True) + EPS) * g


def _sigmoid(x):
    return 1.0 / (1.0 + jnp.exp(-x))


def _log_sigmoid(x):
    return jnp.minimum(x, 0.0) - jnp.log(1.0 + jnp.exp(-jnp.abs(x)))


def _const_spec(shape):
    nd = len(shape)
    return pl.BlockSpec(shape, lambda *_: (0,) * nd)


def _ffn_kernel(*refs, with_mix, with_final):
    it = iter(refs)
    x_ref = next(it)
    if with_mix:
        a_ref, b_ref, wa_ref, wb_ref = next(it), next(it), next(it), next(it)
    g_ref, wg_ref, wu_ref, wd_ref = next(it), next(it), next(it), next(it)
    gf_ref = next(it) if with_final else None
    o_ref = next(it)
    acc_ref = next(it)

    for r0 in range(0, x_ref.shape[0], FFN_SUB_TILE):
        rs = slice(r0, min(r0 + FFN_SUB_TILE, x_ref.shape[0]))
        x = x_ref[rs, :]
        if with_mix:
            x = x + _bdot(a_ref[rs, :], wa_ref[...]) + _bdot(b_ref[rs, :], wb_ref[...])
        h = _rms(x, g_ref[...]).astype(BF16)
        acc_ref[rs, :] = jnp.zeros((rs.stop - rs.start, D_MODEL), F32)
        for c in range(D_FF // FF_CHUNK):
            sl = slice(c * FF_CHUNK, (c + 1) * FF_CHUNK)
            g = jnp.dot(h, wg_ref[:, sl], preferred_element_type=F32)
            u = jnp.dot(h, wu_ref[:, sl], preferred_element_type=F32)
            a = (g * _sigmoid(g) * u).astype(BF16)
            acc_ref[rs, :] += jnp.dot(a, wd_ref[sl, :], preferred_element_type=F32)
        y = x + 0.5 * acc_ref[rs, :]
        if with_final:
            y = _rms(y, gf_ref[...])
        o_ref[rs, :] = y


def _ffn(x, gain, wg, wu, wd, which, *, mix=None, final_gain=None, tm):
    n = x.shape[0]
    if n % (FFN_TILES_PER_STEP * tm) == 0:
        tm = FFN_TILES_PER_STEP * tm
    row = lambda i: (i, 0)
    pick = lambda i: which + (0, 0)
    once = pl.Buffered(1)
    args, specs = [x], [pl.BlockSpec((tm, D_MODEL), row)]
    if mix is not None:
        a, b, wa, wb = mix
        args += [a, b, wa, wb]
        specs += [pl.BlockSpec((tm, a.shape[-1]), row), pl.BlockSpec((tm, b.shape[-1]), row),
                  pl.BlockSpec(wa.shape, lambda i: (0, 0), pipeline_mode=once),
                  pl.BlockSpec(wb.shape, lambda i: (0, 0), pipeline_mode=once)]
    args += [gain.reshape(1, D_MODEL), wg, wu, wd]
    specs += [_const_spec((1, D_MODEL))] + [
        pl.BlockSpec((None, None) + w.shape[2:], pick, pipeline_mode=once) for w in (wg, wu, wd)]
    if final_gain is not None:
        args.append(final_gain.reshape(1, D_MODEL))
        specs.append(_const_spec((1, D_MODEL)))
    return pl.pallas_call(
        functools.partial(_ffn_kernel, with_mix=mix is not None, with_final=final_gain is not None),
        grid=(n // tm,),
        in_specs=specs,
        out_specs=pl.BlockSpec((tm, D_MODEL), row),
        out_shape=jax.ShapeDtypeStruct((n, D_MODEL), F32),
        scratch_shapes=[pltpu.VMEM((tm, D_MODEL), F32)],
        compiler_params=_cparams("parallel"),
        name="ffn",
    )(*args)


def _even_in_kernel(x_ref, gain_ref, wm_ref, wside_ref, bg_ref, qn_ref, kvn_ref, wq_ref,
                    wuk_ref, wuv_ref, cos_ref, sin_ref,
                    qm_o, km_o, vm_o, om_o, gates_o, q_o, kn_o, v_o, kr128_o, lat_o, kr_o,
                    *, v_transposed):
    h = _rms(x_ref[...], gain_ref[...]).astype(BF16)
    m = jnp.dot(h, wm_ref[...], preferred_element_type=F32)
    qm_o[...] = m[:, :M_W].astype(BF16)
    km_o[...] = (m[:, M_W:2 * M_W] * M_DH ** -0.5).astype(BF16)
    vm_o[...] = m[:, 2 * M_W:3 * M_W].astype(BF16)
    om_o[...] = m[:, 3 * M_W:]

    side = jnp.dot(h, wside_ref[...], preferred_element_type=F32)
    o_cq, o_ckv, o_kra, o_krb = LANES, LANES + Q_LORA, LANES + Q_LORA + KV_LORA, 2 * LANES + Q_LORA + KV_LORA
    g = side[:, :LANES] + bg_ref[...]
    lane = lax.broadcasted_iota(jnp.int32, g.shape, 1)
    gates_o[...] = jnp.where(lane < M_HEADS, g, _log_sigmoid(g))

    cos, sin = cos_ref[...], sin_ref[...]
    cq = side[:, o_cq:o_ckv]
    cqn = _rms(cq, qn_ref[...]).astype(BF16)
    qf = jnp.dot(cqn, wq_ref[...], preferred_element_type=F32)
    q_scale = MLA_SCALE * LOG2E
    first_half = lax.broadcasted_iota(jnp.int32, (1, LANES), 1) < A_ROPE // 2
    for hh in range(A_HEADS):
        base = hh * 2 * LANES
        q_o[:, base:base + LANES] = (qf[:, base:base + LANES] * q_scale).astype(BF16)
        ra = qf[:, base + LANES:base + 2 * LANES]
        rb = jnp.where(first_half, pltpu.roll(ra, LANES - A_ROPE // 2, 1), pltpu.roll(ra, A_ROPE // 2, 1))
        rot = ra * cos + rb * sin
        q_o[:, base + LANES:base + 2 * LANES] = (rot * q_scale).astype(BF16)

    ckv = side[:, o_ckv:o_kra]
    lat = _rms(ckv, kvn_ref[...])
    lat_o[...] = lat
    latb = lat.astype(BF16)
    kn_o[...] = jnp.dot(latb, wuk_ref[...], preferred_element_type=F32).astype(BF16)
    if v_transposed:
        v_o[0] = _bdot_nt(wuv_ref[...], latb).astype(BF16)
    else:
        v_o[...] = jnp.dot(latb, wuv_ref[...], preferred_element_type=F32).astype(BF16)

    kr = side[:, o_kra:o_krb] * cos + side[:, o_krb:] * sin
    kr128_o[...] = kr.astype(BF16)
    kr_o[...] = kr[:, :A_ROPE]


def _even_in(x, gain, w, cos, sin, *, tm, seq_len):
    n = x.shape[0]
    row = lambda i: (i, 0)
    n_pos_tiles = cos.shape[0] // tm
    pos = lambda i: (i % n_pos_tiles, 0)
    v_transposed = seq_len % tm == 0
    n_time_tiles = max(seq_len // tm, 1)
    weights = [w["wm"], w["wside"], w["bg"], w["qn"], w["kvn"],
               w["wq"], w["wuk"], w["wuv"].T if v_transposed else w["wuv"]]
    in_specs = ([pl.BlockSpec((tm, D_MODEL), row), _const_spec((1, D_MODEL))]
                + [_const_spec(a.shape) for a in weights]
                + [pl.BlockSpec((tm, LANES), pos), pl.BlockSpec((tm, LANES), pos)])
    outs = [(M_W, BF16), (M_W, BF16), (M_W, BF16), (M_W, F32), (LANES, F32),
            (A_HEADS * 2 * LANES, BF16), (A_HEADS * A_NOPE, BF16), (A_HEADS * A_VDIM, BF16),
            (LANES, BF16), (KV_LORA, F32), (A_ROPE, F32)]
    out_specs = [pl.BlockSpec((tm, c), row) for c, _ in outs]
    out_shape = [jax.ShapeDtypeStruct((n, c), dt) for c, dt in outs]
    if v_transposed:
        v_idx = 7
        out_specs[v_idx] = pl.BlockSpec((1, A_HEADS * A_VDIM, tm),
                                        lambda i: (i // n_time_tiles, 0, i % n_time_tiles))
        out_shape[v_idx] = jax.ShapeDtypeStruct((n // seq_len, A_HEADS * A_VDIM, seq_len), BF16)
    return pl.pallas_call(
        functools.partial(_even_in_kernel, v_transposed=v_transposed),
        grid=(n // tm,),
        in_specs=in_specs,
        out_specs=out_specs,
        out_shape=out_shape,
        compiler_params=_cparams("parallel"),
        name="even_in",
    )(x, gain.reshape(1, D_MODEL), *weights, cos, sin)


def _kv_expand_kernel(lat_ref, wuk_ref, wuv_ref, kn_o, v_o):
    latb = lat_ref[...].astype(BF16)
    kn_o[...] = jnp.dot(latb, wuk_ref[...], preferred_element_type=F32).astype(BF16)
    v_o[...] = jnp.dot(latb, wuv_ref[...], preferred_element_type=F32).astype(BF16)


def _kv_expand(lat, wuk, wuv):
    n = lat.shape[0]
    tm = min(4 * TOKEN_TILE, n)
    assert n % tm == 0
    row = lambda i: (i, 0)
    return pl.pallas_call(
        _kv_expand_kernel,
        grid=(n // tm,),
        in_specs=[pl.BlockSpec((tm, KV_LORA), row), _const_spec(wuk.shape), _const_spec(wuv.shape)],
        out_specs=[pl.BlockSpec((tm, wuk.shape[1]), row), pl.BlockSpec((tm, wuv.shape[1]), row)],
        out_shape=[jax.ShapeDtypeStruct((n, wuk.shape[1]), BF16),
                   jax.ShapeDtypeStruct((n, wuv.shape[1]), BF16)],
        compiler_params=_cparams("parallel"),
        name="kv_expand",
    )(lat, wuk, wuv)


def _mlstm_kernel(qm_ref, km_ref, vm_ref, om_ref, g_ref, onorm_ref, c0_ref, n0_ref, m0_ref,
                  h_ref, cout_ref, nout_ref, mout_ref, c_sc, n_sc, m_sc, *, n_chunks, n_streams):
    lt = pl.program_id(1)

    @pl.when(lt == 0)
    def _():
        c_sc[...] = c0_ref[...]
        n_sc[...] = n0_ref[...]
        m_sc[...] = m0_ref[...]

    row = lax.broadcasted_iota(jnp.int32, (CHUNK, CHUNK), 0)
    col = lax.broadcasted_iota(jnp.int32, (CHUNK, CHUNK), 1)
    causal = col <= row
    ltri = causal.astype(BF16)
    utri = (row <= col).astype(BF16)
    ones_kl = jnp.ones((CHUNK, LANES), BF16)

    def split3(a):
        hi = a.astype(BF16)
        r1 = a - hi.astype(F32)
        mid = r1.astype(BF16)
        return hi, mid, (r1 - mid.astype(F32)).astype(BF16)

    def chunk(c, carry):
        r0 = pl.multiple_of(c * CHUNK, CHUNK)
        rows = pl.ds(r0, CHUNK)
        chains = [(bb, hh) for bb in range(n_streams) for hh in range(M_HEADS)]
        hsl = lambda hh: slice(hh * M_DH, (hh + 1) * M_DH)
        gate = {}
        for bb in range(n_streams):
            gates = g_ref[bb, rows, :]
            gates_t = gates.T
            cc = jnp.dot(ltri, jnp.concatenate(split3(gates), axis=1), preferred_element_type=F32)
            cum_col = cc[:, :LANES] + cc[:, LANES:2 * LANES] + cc[:, 2 * LANES:]
            cr = jnp.dot(jnp.concatenate(split3(gates_t), axis=0), utri, preferred_element_type=F32)
            cum_row = cr[:LANES] + cr[LANES:2 * LANES] + cr[2 * LANES:]
            gate[bb] = (gates, gates_t, cum_col, cum_row)
        qk, qc, qn = {}, {}, {}
        for ch in chains:
            bb, hh = ch
            rhs = jnp.concatenate([c_sc[bb, hh].astype(BF16),
                                   jnp.broadcast_to(n_sc[bb, hh], (LANES, M_DH)).astype(BF16),
                                   km_ref[bb, rows, hsl(hh)]], axis=0)
            r = _bdot_nt(qm_ref[bb, rows, hsl(hh)], rhs)
            qc[ch], qn[ch], qk[ch] = r[:, :M_DH], r[:, M_DH:M_DH + LANES], r[:, M_DH + LANES:]
        st = {}
        for ch in chains:
            bb, hh = ch
            gates, gates_t, cum_col, cum_row = gate[bb]
            ig_row = gates_t[hh:hh + 1, :]
            ig_col = gates[:, hh:hh + 1]
            b_row = cum_row[M_HEADS + hh:M_HEADS + hh + 1, :]
            b_col = cum_col[:, M_HEADS + hh:M_HEADS + hh + 1]
            m_prev = m_sc[bb, hh][:, :1]
            b_lanes = jnp.broadcast_to(b_col, (CHUNK, LANES))
            log_d = jnp.where(causal, b_lanes[:, :CHUNK] - b_row + ig_row, -jnp.inf)
            m_t = jnp.maximum(b_col + m_prev, jnp.max(log_d, axis=-1, keepdims=True))
            m_lanes = jnp.broadcast_to(m_t, (CHUNK, LANES))
            s = (qk[ch] * jnp.exp(log_d - m_lanes[:, :CHUNK])).astype(BF16)
            w_inter = jnp.exp(b_lanes + m_prev - m_lanes)
            m_new = m_t[CHUNK - 1:CHUNK, :]
            b_last = b_col[CHUNK - 1:CHUNK, :]
            w_s = jnp.exp(b_last - b_col + ig_col - m_new)
            decay = jnp.exp(b_last + m_prev - m_new)
            r = jnp.dot(s, jnp.concatenate([vm_ref[bb, rows, hsl(hh)], ones_kl], axis=1),
                        preferred_element_type=F32)
            st[ch] = (r[:, :M_DH], r[:, M_DH:], w_inter, m_lanes, m_new, w_s, decay)
        for ch in chains:
            bb, hh = ch
            sv, s_sum, w_inter, m_lanes, m_new, w_s, decay = st[ch]
            k = km_ref[bb, rows, hsl(hh)]
            v = vm_ref[bb, rows, hsl(hh)]
            c_h = c_sc[bb, hh]
            n_h = n_sc[bb, hh]
            num = w_inter * qc[ch] + sv
            den = w_inter * qn[ch] + s_sum
            hv = num / jnp.maximum(jnp.abs(den), jnp.exp(-m_lanes))
            wv_t = (jnp.broadcast_to(w_s, (CHUNK, M_DH)) * v).T
            c_sc[bb, hh] = decay * c_h + _bdot(wv_t, k)
            n_sc[bb, hh] = decay * n_h + jnp.sum(w_s * k.astype(F32), axis=0, keepdims=True)
            m_sc[bb, hh] = jnp.broadcast_to(m_new, (1, LANES))
            hn = _rms(hv, onorm_ref[:, hsl(hh)])
            h_ref[bb, rows, hsl(hh)] = (_sigmoid(om_ref[bb, rows, hsl(hh)]) * hn).astype(BF16)
        return carry

    lax.fori_loop(0, n_chunks, chunk, 0)

    @pl.when(lt == pl.num_programs(1) - 1)
    def _():
        cout_ref[...] = c_sc[...]
        nout_ref[...] = n_sc[...]
        mout_ref[...] = m_sc[...]


def _mlstm(qm, km, vm, om, gates, onorm, c0, n0, m0, *, tl):
    b, l, _ = qm.shape
    ns = MLSTM_STREAMS
    tile = lambda i, j: (i, j, 0)
    st4 = lambda i, j: (i, 0, 0, 0)
    state = [((ns, M_HEADS, M_DH, M_DH), (b, M_HEADS, M_DH, M_DH)),
             ((ns, M_HEADS, 1, M_DH), (b, M_HEADS, 1, M_DH)),
             ((ns, M_HEADS, 1, LANES), (b, M_HEADS, 1, LANES))]
    return pl.pallas_call(
        functools.partial(_mlstm_kernel, n_chunks=tl // CHUNK, n_streams=ns),
        grid=(b // ns, l // tl),
        in_specs=[pl.BlockSpec((ns, tl, M_W), tile)] * 4
        + [pl.BlockSpec((ns, tl, LANES), tile), _const_spec((1, M_W))]
        + [pl.BlockSpec(blk, st4) for blk, _ in state],
        out_specs=[pl.BlockSpec((ns, tl, M_W), tile)] + [pl.BlockSpec(blk, st4) for blk, _ in state],
        out_shape=[jax.ShapeDtypeStruct((b, l, M_W), BF16)]
        + [jax.ShapeDtypeStruct(full, F32) for _, full in state],
        scratch_shapes=[pltpu.VMEM(blk, F32) for blk, _ in state],
        compiler_params=_cparams("parallel", "arbitrary"),
        name="mlstm",
    )(qm, km, vm, om, gates, onorm.reshape(1, M_W), c0, n0, m0)


MLA_SUM_ROWS = 16


MLA_EXP_RANGE = 60.0


def _mla_kernel(qi_ref, kj_ref, flag_ref, q_ref, kn_ref, kr_ref, vt_ref, o_ref, m_sc, acc_sc, bad_sc,
                redo_sm, *, tq, tkb, offset, variants):
    p = pl.program_id(1)
    i = qi_ref[p]
    j = kj_ref[p]
    flags = flag_ref[p]

    @pl.when(j == 0)
    def _():
        m_sc[...] = jnp.zeros_like(m_sc)
        acc_sc[...] = jnp.zeros_like(acc_sc)

    def score_tiles(nk, masked):
        kr = kr_ref[0, :nk, :]
        scores = []
        for hh in range(A_HEADS):
            q = q_ref[0, :, hh * 2 * LANES:(hh + 1) * 2 * LANES]
            k = jnp.concatenate([kn_ref[0, :nk, hh * A_NOPE:(hh + 1) * A_NOPE], kr], axis=-1)
            scores.append(_bdot_nt(k, q))
        if masked:
            k_chunk = (j * tkb + lax.broadcasted_iota(jnp.int32, (nk, tq), 0)) // CHUNK
            q_chunk = (offset + i * tq + lax.broadcasted_iota(jnp.int32, (nk, tq), 1)) // CHUNK
            visible = k_chunk <= q_chunk
            scores = [jnp.where(visible, s, -jnp.inf) for s in scores]
        return scores

    def values_t(hh, nk):
        return jnp.concatenate([vt_ref[0, hh * A_VDIM:(hh + 1) * A_VDIM, :nk],
                                jnp.ones((MLA_SUM_ROWS, nk), BF16)], axis=0)

    def add_block(nk, masked):
        scores = score_tiles(nk, masked)
        any_bad = jnp.zeros((1, tq), F32)
        for hh in range(A_HEADS):
            s = scores[hh]
            m_ref = m_sc[hh]
            d = jnp.max(s, axis=0, keepdims=True) - m_ref
            ok = jnp.logical_and(d <= MLA_EXP_RANGE, jnp.logical_or(d >= -MLA_EXP_RANGE, j > 0))
            pr = jnp.exp2(s - m_ref).astype(BF16)
            acc = acc_sc[hh]
            acc_sc[hh] = jnp.where(ok, acc + jnp.dot(values_t(hh, nk), pr, preferred_element_type=F32),
                                   acc)
            bad = jnp.where(ok, 0.0, 1.0)
            bad_sc[hh] = bad
            any_bad = jnp.maximum(any_bad, bad)
        redo_sm[0] = (jnp.max(any_bad) > 0.0).astype(jnp.int32)

    def redo_block(nk, masked):
        scores = score_tiles(nk, masked)
        for hh in range(A_HEADS):
            s = scores[hh]
            bad = bad_sc[hh] > 0.0
            m_prev = jnp.where(j == 0, -jnp.inf, m_sc[hh])
            m_new = jnp.maximum(m_prev, jnp.max(s, axis=0, keepdims=True))
            alpha = jnp.exp2(m_prev - m_new)
            pr = jnp.exp2(s - m_new).astype(BF16)
            acc = acc_sc[hh]
            acc_sc[hh] = jnp.where(
                bad, alpha * acc + jnp.dot(values_t(hh, nk), pr, preferred_element_type=F32), acc)
            m_sc[hh] = jnp.where(bad, m_new, m_sc[hh])

    for vi, (nk, masked) in enumerate(variants):
        pl.when((flags >> 1) == vi)(functools.partial(add_block, nk, masked))
    for vi, (nk, masked) in enumerate(variants):
        pl.when(jnp.logical_and((flags >> 1) == vi, redo_sm[0] != 0))(
            functools.partial(redo_block, nk, masked))

    @pl.when((flags & 1) != 0)
    def _():
        for hh in range(A_HEADS):
            acc = acc_sc[hh]
            o_ref[0, :, hh * A_VDIM:(hh + 1) * A_VDIM] = (
                acc[:A_VDIM] / acc[A_VDIM:A_VDIM + 1]).T.astype(BF16)


def _transpose_kernel(x_ref, o_ref):
    o_ref[0] = x_ref[0].astype(F32).T.astype(o_ref.dtype)


def _transpose_tiles(x, t):
    b, l, c = x.shape
    return pl.pallas_call(
        _transpose_kernel,
        grid=(b, l // t),
        in_specs=[pl.BlockSpec((1, t, c), lambda bb, i: (bb, i, 0))],
        out_specs=pl.BlockSpec((1, c, t), lambda bb, i: (bb, 0, i)),
        out_shape=jax.ShapeDtypeStruct((b, c, l), x.dtype),
        compiler_params=_cparams("parallel", "parallel"),
        name="transpose_tiles",
    )(x)


def _mla_pairs(lq, lk, tq, tkb, sub, offset):
    qi, kj, flags, variants = [], [], [], []
    for i in range(lq // tq):
        first_row, last_row = offset + i * tq, offset + (i + 1) * tq - 1
        k_end = min((last_row // CHUNK + 1) * CHUNK, lk)
        k_all = (first_row // CHUNK + 1) * CHUNK
        nj = -(-k_end // tkb)
        for j in range(nj):
            nk = -(-(min(k_end, (j + 1) * tkb) - j * tkb) // sub) * sub
            variant = (nk, j * tkb + nk > k_all)
            if variant not in variants:
                variants.append(variant)
            qi.append(i)
            kj.append(j)
            flags.append(int(j == nj - 1) + 2 * variants.index(variant))
    return (np.asarray(qi, np.int32), np.asarray(kj, np.int32), np.asarray(flags, np.int32),
            tuple(variants))


def _mla_attention(q, kn, kr, vt, *, tq, tkb, sub, offset):
    b, lq, _ = q.shape
    lk = kn.shape[1]
    assert lk % tkb == 0 and tkb % sub == 0 and vt.shape[2] == lk
    qi, kj, flags, variants = _mla_pairs(lq, lk, tq, tkb, sub, offset)
    qmap = lambda bb, p, qi_r, kj_r, flag_r: (bb, qi_r[p], 0)
    kmap = lambda bb, p, qi_r, kj_r, flag_r: (bb, kj_r[p], 0)
    grid_spec = pltpu.PrefetchScalarGridSpec(
        num_scalar_prefetch=3,
        grid=(b, len(qi)),
        in_specs=[pl.BlockSpec((1, tq, q.shape[-1]), qmap),
                  pl.BlockSpec((1, tkb, kn.shape[-1]), kmap),
                  pl.BlockSpec((1, tkb, kr.shape[-1]), kmap),
                  pl.BlockSpec((1, vt.shape[1], tkb), lambda bb, p, qi_r, kj_r, flag_r: (bb, 0, kj_r[p]))],
        out_specs=pl.BlockSpec((1, tq, A_HEADS * A_VDIM), qmap),
        scratch_shapes=[pltpu.VMEM((A_HEADS, 1, tq), F32),
                        pltpu.VMEM((A_HEADS, A_VDIM + MLA_SUM_ROWS, tq), F32),
                        pltpu.VMEM((A_HEADS, 1, tq), F32),
                        pltpu.SMEM((1,), jnp.int32)],
    )
    return pl.pallas_call(
        functools.partial(_mla_kernel, tq=tq, tkb=tkb, offset=offset, variants=variants),
        grid_spec=grid_spec,
        out_shape=jax.ShapeDtypeStruct((b, lq, A_HEADS * A_VDIM), BF16),
        compiler_params=_cparams("parallel", "arbitrary"),
        name="mla_attention",
    )(jnp.asarray(qi), jnp.asarray(kj), jnp.asarray(flags), q, kn, kr, vt)


def _odd_in_kernel(x_ref, gain_ref, wu_ref, wq_ref, wk_ref, wv_ref, u_o, q_o, k_o, v_o, kb_o, vb_o):
    h = _rms(x_ref[...], gain_ref[...]).astype(BF16)
    u_o[...] = jnp.dot(h, wu_ref[...], preferred_element_type=F32)
    q_o[...] = (jnp.dot(h, wq_ref[...], preferred_element_type=F32) * (SB_DH ** -0.5 * LOG2E)).astype(BF16)
    k = jnp.dot(h, wk_ref[...], preferred_element_type=F32)
    v = jnp.dot(h, wv_ref[...], preferred_element_type=F32)
    k_o[...] = pltpu.einshape("t(hd)->thd", k, h=SB_HEADS)
    v_o[...] = pltpu.einshape("t(hd)->thd", v, h=SB_HEADS)
    kb_o[...] = k.astype(BF16)
    vb_o[...] = v.astype(BF16)


def _odd_in(x, gain, w, *, tm):
    n = x.shape[0]
    row = lambda i: (i, 0)
    weights = [w["wu"], w["wq"], w["wk"], w["wv"]]
    return pl.pallas_call(
        _odd_in_kernel,
        grid=(n // tm,),
        in_specs=[pl.BlockSpec((tm, D_MODEL), row), _const_spec((1, D_MODEL))]
        + [_const_spec(a.shape) for a in weights],
        out_specs=[pl.BlockSpec((tm, S5_W), row), pl.BlockSpec((tm, SB_W), row)]
        + [pl.BlockSpec((tm, SB_HEADS, SB_DH), lambda i: (i, 0, 0))] * 2
        + [pl.BlockSpec((tm, SB_W), row)] * 2,
        out_shape=[jax.ShapeDtypeStruct((n, S5_W), F32),
                   jax.ShapeDtypeStruct((n, SB_W), BF16),
                   jax.ShapeDtypeStruct((n, SB_HEADS, SB_DH), F32),
                   jax.ShapeDtypeStruct((n, SB_HEADS, SB_DH), F32),
                   jax.ShapeDtypeStruct((n, SB_W), BF16),
                   jax.ShapeDtypeStruct((n, SB_W), BF16)],
        compiler_params=_cparams("parallel"),
        name="odd_in",
    )(x, gain.reshape(1, D_MODEL), *weights)


S5_BLOCKS = 4
S5_BLK_IN = S5_W // S5_BLOCKS
S5_BLK_ST = S5_STATE // S5_BLOCKS


def _s5_kernel(u_ref, bbr_ref, bbi_ref, ar_ref, ai_ref, ccr_ref, cci_ref, d_ref, wglu_ref,
               x0r_ref, x0i_ref, hs_o, xr_o, xi_o, u_tm, hs_tm, xr_buf, xi_buf, xr_st, xi_st,
               *, tt, nb):
    t = pl.program_id(0)

    @pl.when(t == 0)
    def _():
        xr_st[...] = x0r_ref[...]
        xi_st[...] = x0i_ref[...]

    for bb in range(nb):
        for k in range(S5_BLOCKS):
            u_tm[k, pl.ds(bb, tt, stride=nb), :] = u_ref[bb, :, k * S5_BLK_IN:(k + 1) * S5_BLK_IN]

    for k in range(S5_BLOCKS):
        cs = slice(k * S5_BLK_ST, (k + 1) * S5_BLK_ST)
        ub = u_tm[k].astype(BF16)
        xr_buf[:, cs] = jnp.dot(ub, bbr_ref[k], preferred_element_type=F32)
        xi_buf[:, cs] = jnp.dot(ub, bbi_ref[k], preferred_element_type=F32)

    for k in range(0, S5_BLOCKS, 2):
        cols = [slice((k + d) * S5_BLK_ST, (k + d + 1) * S5_BLK_ST) for d in range(2)]
        decay = [(jnp.broadcast_to(ar_ref[:, cs], (nb, S5_BLK_ST)),
                  jnp.broadcast_to(ai_ref[:, cs], (nb, S5_BLK_ST))) for cs in cols]

        def step(s, carry, cols=cols, decay=decay):
            rows = pl.ds(pl.multiple_of(s * nb, nb), nb)
            out = []
            for (xr, xi), (ar, ai), cs in zip(carry, decay, cols):
                nxr = ar * xr - ai * xi + xr_buf[rows, cs]
                nxi = ar * xi + ai * xr + xi_buf[rows, cs]
                xr_buf[rows, cs] = nxr
                xi_buf[rows, cs] = nxi
                out.append((nxr, nxi))
            return tuple(out)

        final = lax.fori_loop(0, tt, step, tuple((xr_st[:, cs], xi_st[:, cs]) for cs in cols), unroll=8)
        for (xr, xi), cs in zip(final, cols):
            xr_st[:, cs] = xr
            xi_st[:, cs] = xi

    gs = []
    for k in range(S5_BLOCKS):
        cs = slice(k * S5_BLK_ST, (k + 1) * S5_BLK_ST)
        y = (_bdot(xr_buf[:, cs], ccr_ref[k]) + _bdot(xi_buf[:, cs], cci_ref[k])
             + d_ref[:, k * S5_BLK_IN:(k + 1) * S5_BLK_IN] * u_tm[k])
        gs.append(jax.nn.gelu(y, approximate=True))
    g = jnp.concatenate(gs, axis=-1)
    hs = g * _sigmoid(_bdot(g, wglu_ref[...]))
    for k in range(S5_BLOCKS):
        hs_tm[k] = hs[:, k * S5_BLK_IN:(k + 1) * S5_BLK_IN]
    for bb in range(nb):
        for k in range(S5_BLOCKS):
            hs_o[bb, :, k * S5_BLK_IN:(k + 1) * S5_BLK_IN] = (
                hs_tm[k, pl.ds(bb, tt, stride=nb), :].astype(BF16))

    @pl.when(t == pl.num_programs(0) - 1)
    def _():
        xr_o[...] = xr_st[...]
        xi_o[...] = xi_st[...]


def _s5(u, w, x0r, x0i, *, tt):
    nb, l, _ = u.shape
    blk = lambda t: (0, t, 0)
    consts = [w["bbr"], w["bbi"], w["ar"], w["ai"], w["ccr"], w["cci"], w["d"], w["wglu"], x0r, x0i]
    return pl.pallas_call(
        functools.partial(_s5_kernel, tt=tt, nb=nb),
        grid=(l // tt,),
        in_specs=[pl.BlockSpec((nb, tt, S5_W), blk)] + [_const_spec(a.shape) for a in consts],
        out_specs=[pl.BlockSpec((nb, tt, S5_W), blk), _const_spec((nb, S5_STATE)),
                   _const_spec((nb, S5_STATE))],
        out_shape=[jax.ShapeDtypeStruct((nb, l, S5_W), BF16),
                   jax.ShapeDtypeStruct((nb, S5_STATE), F32),
                   jax.ShapeDtypeStruct((nb, S5_STATE), F32)],
        scratch_shapes=[pltpu.VMEM((S5_BLOCKS, tt * nb, S5_BLK_IN), F32),
                        pltpu.VMEM((S5_BLOCKS, tt * nb, S5_BLK_IN), F32),
                        pltpu.VMEM((tt * nb, S5_STATE), F32), pltpu.VMEM((tt * nb, S5_STATE), F32),
                        pltpu.VMEM((nb, S5_STATE), F32), pltpu.VMEM((nb, S5_STATE), F32)],
        compiler_params=_cparams("arbitrary"),
        name="s5",
    )(u, *consts)


def _sb_kernel(q_ref, k_ref, v_ref, o_ref, r_sc, acc_sc, qh_sc, *, tq, tk, offset, n_sub):
    for su in range(n_sub):
        _sb_query_tile(pl.program_id(1) * n_sub + su, slice(su * tq, (su + 1) * tq),
                       q_ref, k_ref, v_ref, o_ref, r_sc, acc_sc, qh_sc, tq=tq, tk=tk, offset=offset)


def _sb_query_tile(i, rs, q_ref, k_ref, v_ref, o_ref, r_sc, acc_sc, qh_sc, *, tq, tk, offset):
    n_pairs = SB_HEADS // 2
    half = lax.broadcasted_iota(jnp.int32, (1, LANES), 1) // SB_DH
    for pr in range(n_pairs):
        qq = q_ref[0, rs, pr * LANES:(pr + 1) * LANES]
        qh_sc[pr, :tq] = jnp.where(half == 0, qq, jnp.zeros_like(qq))
        qh_sc[pr, tq:] = jnp.where(half == 1, qq, jnp.zeros_like(qq))
    r_sc[...] = jnp.zeros_like(r_sc)
    acc_sc[...] = jnp.zeros_like(acc_sc)

    hk = tk // 2
    jrow = lax.broadcasted_iota(jnp.int32, (tk, tk), 0)
    jcol = lax.broadcasted_iota(jnp.int32, (tk, tk), 1)
    later = (jrow > jcol).astype(BF16)
    later_pk = jnp.logical_and(jrow > jcol, (jrow // hk) == (jcol // hk)).astype(BF16)
    first_half = lax.broadcasted_iota(jnp.int32, (1, tk), 1) < hk
    j_diag = (offset + i * tq) // tk

    def softplus2(z):
        return jnp.maximum(z, 0.0) + jnp.log2(1.0 + jnp.exp2(-jnp.abs(z)))

    def split2(a):
        hi = a.astype(BF16)
        return [hi, (a - hi.astype(F32)).astype(BF16)]

    def by_head(x):
        zero = jnp.zeros_like(x)
        return [jnp.where(half == 0, x, zero), jnp.where(half == 1, x, zero)]

    def sweep(tiles):
        zs, vv, keep_key = {}, {}, {}
        for t, (j, kind, live, limit) in enumerate(tiles):
            k0 = pl.multiple_of(j * tk, tk)
            if kind == "masked":
                q_pos = offset + i * tq + lax.broadcasted_iota(jnp.int32, (tq, tk), 0)
                keep_key[t] = (k0 + lax.broadcasted_iota(jnp.int32, (tq, tk), 1)) < q_pos
            elif limit is not None:
                keep_key[t] = lax.broadcasted_iota(jnp.int32, (tq, tk), 1) < limit
            for pr in range(n_pairs):
                ps = slice(pr * LANES, (pr + 1) * LANES)
                if kind == "half":
                    kh = k_ref[0, pl.ds(k0 + hk, hk), ps]
                    zero = jnp.zeros_like(kh)
                    k2 = jnp.concatenate([jnp.concatenate([kh, zero], axis=1),
                                          jnp.concatenate([zero, kh], axis=1)], axis=0)
                    q2 = jnp.concatenate([qh_sc[pr, :tq], qh_sc[pr, tq:]], axis=1)
                    zs[t, pr] = _bdot_nt(q2, k2)
                    vv[t, pr] = jnp.concatenate(by_head(v_ref[0, pl.ds(k0 + hk, hk), ps]), axis=0)
                else:
                    zs[t, pr] = _bdot_nt(qh_sc[pr], k_ref[0, pl.ds(k0, tk), ps])
                    vv[t, pr] = jnp.concatenate(by_head(v_ref[0, pl.ds(k0, tk), ps]), axis=0)
        l1s, lbs, cums = {}, {}, {}
        for t, (j, kind, live, limit) in enumerate(tiles):
            parts, slots = [], []
            for pr in range(n_pairs):
                for sub in range(1 if kind == "half" else 2):
                    slot = pr if kind == "half" else 2 * pr + sub
                    z = zs[t, pr] if kind == "half" else zs[t, pr][sub * tq:(sub + 1) * tq]
                    sp = softplus2(z)
                    l1 = -sp
                    if t in keep_key:
                        l1 = jnp.where(keep_key[t], l1, 0.0)
                    parts += split2(l1)
                    slots.append(slot)
                    l1s[t, slot], lbs[t, slot] = l1, z - sp
            c = jnp.dot(jnp.concatenate(parts, axis=0), later_pk if kind == "half" else later,
                        preferred_element_type=F32)
            for n, slot in enumerate(slots):
                cums[t, slot] = c[2 * n * tq:(2 * n + 1) * tq] + c[(2 * n + 1) * tq:(2 * n + 2) * tq]
        for t, (j, kind, live, limit) in enumerate(tiles):
            r_max = jnp.full((1, 1), -jnp.inf, F32)
            dead_shift = 0.0 if live is None else jnp.where(live, 0.0, -1e30)
            keep = 1.0 if live is None else jnp.where(live, 1.0, 0.0)
            for pr in range(n_pairs):
                if kind == "half":
                    r0, r1 = r_sc[2 * pr], r_sc[2 * pr + 1]
                    log_w = lbs[t, pr] + cums[t, pr] + (jnp.where(first_half, r0, r1) + dead_shift)
                    w = jnp.exp2(log_w).astype(BF16)
                    s_all = jnp.sum(l1s[t, pr], axis=-1, keepdims=True)
                    s0 = jnp.sum(jnp.where(first_half, l1s[t, pr], 0.0), axis=-1, keepdims=True)
                    r_new = [r0 + keep * s0, r1 + keep * (s_all - s0)]
                else:
                    ws, r_new = [], []
                    for sub in range(2):
                        hh = 2 * pr + sub
                        r_old = r_sc[hh]
                        log_w = lbs[t, hh] + cums[t, hh] + (r_old + dead_shift)
                        if t in keep_key:
                            log_w = jnp.where(keep_key[t], log_w, -jnp.inf)
                        ws.append(jnp.exp2(log_w).astype(BF16))
                        r_new.append(r_old + keep * jnp.sum(l1s[t, hh], axis=-1, keepdims=True))
                    w = jnp.concatenate(ws, axis=1)
                for sub in range(2):
                    r_sc[2 * pr + sub] = r_new[sub]
                    r_max = jnp.maximum(r_max, jnp.max(r_new[sub], axis=0, keepdims=True))
                acc_sc[:, pr * LANES:(pr + 1) * LANES] += jnp.dot(w, vv[t, pr], preferred_element_type=F32)
        return r_max[0, 0]

    has_prev = j_diag >= 1
    r_first = sweep([(j_diag, "masked", None, None), (jnp.maximum(j_diag - 1, 0), "half", has_prev, None)])

    def cond(carry):
        j, r_max, _ = carry
        return jnp.logical_and(j >= 0, r_max > SB_LOG_FLOOR * LOG2E)

    def body(carry):
        j, _, limit = carry
        return j - 1, sweep([(j, "full", None, limit)]), jnp.int32(tk)

    lax.while_loop(cond, body, (j_diag - 1, r_first, jnp.int32(hk)))
    o_ref[0, rs] = acc_sc[...].astype(BF16)


def _sb_attention(q, k, v, *, tq, tk, offset):
    b, lq, _ = q.shape
    lk = k.shape[1]
    assert tk % tq == 0 and offset % tk == 0 and lk % tk == 0
    n_sub = math.gcd(lq // tq, SB_TILES_PER_STEP)
    return pl.pallas_call(
        functools.partial(_sb_kernel, tq=tq, tk=tk, offset=offset, n_sub=n_sub),
        grid=(b, lq // (tq * n_sub)),
        in_specs=[pl.BlockSpec((1, tq * n_sub, SB_W), lambda bb, i: (bb, i, 0)),
                  pl.BlockSpec((1, lk, SB_W), lambda bb, i: (bb, 0, 0)),
                  pl.BlockSpec((1, lk, SB_W), lambda bb, i: (bb, 0, 0))],
        out_specs=pl.BlockSpec((1, tq * n_sub, SB_W), lambda bb, i: (bb, i, 0)),
        out_shape=jax.ShapeDtypeStruct((b, lq, SB_W), BF16),
        scratch_shapes=[pltpu.VMEM((SB_HEADS, tq, 1), F32),
                        pltpu.VMEM((tq, SB_W), F32),
                        pltpu.VMEM((SB_HEADS // 2, 2 * tq, LANES), BF16)],
        compiler_params=_cparams("parallel", "arbitrary"),
        name="sb_attention",
    )(q, k, v)


def _pad_cols(w, width):
    return jnp.pad(w, ((0, 0), (0, width - w.shape[1])))


def _even_weights(w_in, b_ig, b_fg, q_norm, kv_norm, w_uq, w_ukv):
    o = 4 * M_W
    w_gate = w_in[:, o:o + 2 * M_HEADS]
    o += 2 * M_HEADS
    w_cq = w_in[:, o:o + Q_LORA]
    o += Q_LORA
    w_ckv = w_in[:, o:o + KV_LORA]
    o += KV_LORA
    w_kr = w_in[:, o:o + A_ROPE]
    hr = A_ROPE // 2
    wkra = _pad_cols(w_kr, LANES)
    wkrb = _pad_cols(jnp.concatenate([w_kr[:, hr:], w_kr[:, :hr]], axis=1), LANES)
    uq = w_uq.reshape(Q_LORA, A_HEADS, A_NOPE + A_ROPE)
    zeros = jnp.zeros((Q_LORA, A_HEADS, LANES - A_ROPE), w_uq.dtype)
    wq = jnp.concatenate([uq, zeros], axis=-1).reshape(Q_LORA, A_HEADS * 2 * LANES)
    ukv = w_ukv.reshape(KV_LORA, A_HEADS, A_NOPE + A_VDIM)
    return dict(
        wm=w_in[:, :4 * M_W].astype(BF16),
        wside=jnp.concatenate([_pad_cols(w_gate, LANES), w_cq, w_ckv, wkra, wkrb], axis=1).astype(BF16),
        bg=_pad_cols(jnp.concatenate([b_ig, b_fg]).reshape(1, 2 * M_HEADS).astype(F32), LANES),
        qn=q_norm.reshape(1, Q_LORA).astype(F32), kvn=kv_norm.reshape(1, KV_LORA).astype(F32),
        wq=wq.astype(BF16),
        wuk=ukv[..., :A_NOPE].reshape(KV_LORA, A_HEADS * A_NOPE).astype(BF16),
        wuv=ukv[..., A_NOPE:].reshape(KV_LORA, A_HEADS * A_VDIM).astype(BF16),
    )


def _rope_tables(pos):
    half = A_ROPE // 2
    inv_freq = ROPE_THETA ** (-jnp.arange(half, dtype=F32) / half)
    ang = pos.astype(F32)[:, None] * inv_freq[None, :]
    cos, sin = jnp.cos(ang), jnp.sin(ang)
    pad = jnp.zeros((pos.shape[0], LANES - A_ROPE), F32)
    return (jnp.concatenate([cos, cos, pad], axis=1), jnp.concatenate([-sin, sin, pad], axis=1))


def _s5_weights(a_re, a_im, b_re, b_im, c_re, c_im, d, log_dt, w_glu):
    dt = jnp.exp(log_dt.astype(F32))[:, None]
    ar, ai = a_re.astype(F32), a_im.astype(F32)
    mag = jnp.exp(dt * ar)
    ab_r, ab_i = mag * jnp.cos(dt * ai), mag * jnp.sin(dt * ai)
    den = ar * ar + ai * ai
    c_r = ((ab_r - 1.0) * ar + ab_i * ai) / den
    c_i = (ab_i * ar - (ab_r - 1.0) * ai) / den
    br, bi = b_re.astype(F32), b_im.astype(F32)
    bb_r = c_r[..., None] * br - c_i[..., None] * bi
    bb_i = c_r[..., None] * bi + c_i[..., None] * br
    gb = S5_GROUPS // S5_BLOCKS
    eye = jnp.eye(gb, dtype=F32)
    bd_in = lambda m: jnp.einsum("kgpc,gh->kgchp", m.reshape(S5_BLOCKS, gb, S5_P, S5_GSIZE),
                                 eye).reshape(S5_BLOCKS, S5_BLK_IN, S5_BLK_ST)
    bd_out = lambda m: jnp.einsum("kgcp,gh->kgphc", m.reshape(S5_BLOCKS, gb, S5_GSIZE, S5_P),
                                  eye).reshape(S5_BLOCKS, S5_BLK_ST, S5_BLK_IN)
    return dict(
        bbr=bd_in(bb_r).astype(BF16), bbi=bd_in(bb_i).astype(BF16),
        ar=ab_r.reshape(1, S5_STATE), ai=ab_i.reshape(1, S5_STATE),
        ccr=bd_out(c_re.astype(F32)).astype(BF16), cci=bd_out(-c_im.astype(F32)).astype(BF16),
        d=d.reshape(1, S5_W).astype(F32), wglu=w_glu.astype(BF16),
    )


def _pad_keys(a, mult):
    pad = (-a.shape[1]) % mult
    return jnp.pad(a, ((0, 0), (0, pad), (0, 0))) if pad else a


def _trunk(x, pos0, cache, p):
    nb, l, _ = x.shape
    assert nb == SUBLANES, "the S5 layout puts the streams of one time step on the sublanes"
    assert l % CHUNK == 0
    n = nb * l
    tm = min(TOKEN_TILE, n)
    tl = min(tm, l)
    assert l % tl == 0 and tm % tl == 0
    past = 0 if cache is None else cache["sb_k"].shape[1]
    assert pos0 == past
    xf = x.reshape(n, D_MODEL)
    bf = lambda a: a.astype(BF16)
    ffn_w = (bf(p["ffn_w_gate"]), bf(p["ffn_w_up"]), bf(p["ffn_w_down"]))

    xf = _ffn(xf, p["norm_ffn"][0, 0], *ffn_w, (0, 0), tm=tm)
    ew = _even_weights(p["even_w_in"][0], p["mlstm_b_igate"][0], p["mlstm_b_fgate"][0],
                       p["mla_q_norm"][0], p["mla_kv_norm"][0], p["mla_w_uq"][0], p["mla_w_ukv"][0])
    cos, sin = _rope_tables(pos0 + jnp.arange(l))
    if tm > l:
        cos, sin = jnp.tile(cos, (tm // l, 1)), jnp.tile(sin, (tm // l, 1))
    qm, km, vm, om, gates, q, kn, v, kr128, lat, kr = _even_in(
        xf, p["norm_mix"][0], ew, cos, sin, tm=tm, seq_len=l)
    v_transposed = v.ndim == 3

    if cache is None:
        c0 = jnp.zeros((nb, M_HEADS, M_DH, M_DH), F32)
        n0 = jnp.zeros((nb, M_HEADS, 1, M_DH), F32)
        m0 = jnp.zeros((nb, M_HEADS, 1, LANES), F32)
    else:
        c0 = cache["mlstm_c"].astype(F32)
        n0 = cache["mlstm_n"].astype(F32).reshape(nb, M_HEADS, 1, M_DH)
        m0 = jnp.broadcast_to(cache["mlstm_m"].astype(F32)[..., None, None], (nb, M_HEADS, 1, LANES))
    r3 = lambda a: a.reshape(nb, l, a.shape[-1])
    h_m, c_new, n_new, m_new = _mlstm(r3(qm), r3(km), r3(vm), r3(om), r3(gates),
                                      p["mlstm_out_norm"][0], c0, n0, m0, tl=min(tl, MLSTM_TIME_TILE))

    mla_tq = min(MLA_TILE, -(-l // LANES) * LANES)
    if cache is None:
        mla_sub = mla_tq
        mla_tkb = min(2 * mla_tq, l)
    else:
        mla_sub = mla_tkb = -(-(past + max(l, mla_tq)) // LANES) * LANES
    kn3, kr3 = r3(kn), r3(kr128)
    v3, vt = (None, v) if v_transposed else (r3(v), None)
    if cache is not None:
        ckn, cv = _kv_expand(cache["mla_latent"].astype(F32).reshape(nb * past, KV_LORA),
                             ew["wuk"], ew["wuv"])
        ckr = jnp.pad(cache["mla_krope"], ((0, 0), (0, 0), (0, LANES - A_ROPE))).astype(BF16)
        kn3 = jnp.concatenate([ckn.reshape(nb, past, -1), kn3], axis=1)
        kr3 = jnp.concatenate([ckr, kr3], axis=1)
        cv3 = cv.reshape(nb, past, -1)
        if v_transposed:
            vt = jnp.concatenate([_transpose_tiles(cv3, math.gcd(past, TOKEN_TILE)), vt], axis=2)
        else:
            v3 = jnp.concatenate([cv3, v3], axis=1)
    if vt is None:
        vt = _transpose_tiles(_pad_keys(v3, mla_tkb), mla_sub)
    else:
        vt = jnp.pad(vt, ((0, 0), (0, 0), (0, (-vt.shape[2]) % mla_tkb)))
    h_a = _mla_attention(_pad_keys(r3(q), mla_tq), _pad_keys(kn3, mla_tkb), _pad_keys(kr3, mla_tkb),
                         vt, tq=mla_tq, tkb=mla_tkb, sub=mla_sub, offset=past)[:, :l]

    w_out = bf(p["even_w_out"][0])
    xf = _ffn(xf, p["norm_ffn"][0, 1], *ffn_w, (0, 1),
              mix=(h_m.reshape(n, M_W), h_a.reshape(n, -1), w_out[:M_W], w_out[M_W:]), tm=tm)

    xf = _ffn(xf, p["norm_ffn"][1, 0], *ffn_w, (1, 0), tm=tm)
    w_in = p["odd_w_in"][0]
    ow = dict(wu=bf(w_in[:, :S5_W]), wq=bf(w_in[:, S5_W:S5_W + SB_W]),
              wk=bf(w_in[:, S5_W + SB_W:S5_W + 2 * SB_W]), wv=bf(w_in[:, S5_W + 2 * SB_W:]))
    u, qs, ks, vs, ksb, vsb = _odd_in(xf, p["norm_mix"][1], ow, tm=tm)

    sw = _s5_weights(p["s5_A_re"][0], p["s5_A_im"][0], p["s5_B_re"][0], p["s5_B_im"][0],
                     p["s5_C_re"][0], p["s5_C_im"][0], p["s5_D"][0], p["s5_log_dt"][0], p["s5_w_glu"][0])
    if cache is None:
        x0r = jnp.zeros((nb, S5_STATE), F32)
        x0i = jnp.zeros((nb, S5_STATE), F32)
    else:
        x0r = cache["s5_re"].astype(F32).reshape(nb, S5_STATE)
        x0i = cache["s5_im"].astype(F32).reshape(nb, S5_STATE)
    h_s, xr_new, xi_new = _s5(r3(u), sw, x0r, x0i, tt=min(S5_TT, l))

    k3, v3 = r3(ksb), r3(vsb)
    if cache is not None:
        k3 = jnp.concatenate([bf(cache["sb_k"].reshape(nb, past, SB_W)), k3], axis=1)
        v3 = jnp.concatenate([bf(cache["sb_v"].reshape(nb, past, SB_W)), v3], axis=1)
    h_b = _sb_attention(r3(qs), _pad_keys(k3, SB_TILE), _pad_keys(v3, SB_TILE), tq=min(SB_TILE, l),
                        tk=SB_TILE, offset=past)

    w_out = bf(p["odd_w_out"][0])
    y = _ffn(xf, p["norm_ffn"][1, 1], *ffn_w, (1, 1),
             mix=(h_s.reshape(n, S5_W), h_b.reshape(n, SB_W), w_out[:S5_W], w_out[S5_W:]),
             final_gain=p["norm_final"], tm=tm)

    return (y.reshape(nb, l, D_MODEL),
            lat.reshape(1, nb, l, KV_LORA), kr.reshape(1, nb, l, A_ROPE),
            c_new[None], n_new.reshape(1, nb, M_HEADS, M_DH), m_new[:, :, 0, 0][None],
            xr_new.reshape(1, nb, S5_GROUPS, S5_P), xi_new.reshape(1, nb, S5_GROUPS, S5_P),
            ks.reshape(1, nb, l, SB_HEADS, SB_DH), vs.reshape(1, nb, l, SB_HEADS, SB_DH))


def kernel(x_prompt, x_sample, cache_mla_latent, cache_mla_krope, state_mlstm_C, state_mlstm_n, state_mlstm_m, state_s5_re, state_s5_im, cache_sb_k, cache_sb_v, norm_ffn, norm_mix, norm_final, ffn_w_gate, ffn_w_up, ffn_w_down, even_w_in, even_w_out, mlstm_b_igate, mlstm_b_fgate, mlstm_out_norm, mla_q_norm, mla_kv_norm, mla_w_uq, mla_w_ukv, odd_w_in, odd_w_out, s5_A_re, s5_A_im, s5_B_re, s5_B_im, s5_C_re, s5_C_im, s5_D, s5_log_dt, s5_w_glu):
    p = dict(norm_ffn=norm_ffn, norm_mix=norm_mix, norm_final=norm_final, ffn_w_gate=ffn_w_gate,
             ffn_w_up=ffn_w_up, ffn_w_down=ffn_w_down, even_w_in=even_w_in, even_w_out=even_w_out,
             mlstm_b_igate=mlstm_b_igate, mlstm_b_fgate=mlstm_b_fgate, mlstm_out_norm=mlstm_out_norm,
             mla_q_norm=mla_q_norm, mla_kv_norm=mla_kv_norm, mla_w_uq=mla_w_uq, mla_w_ukv=mla_w_ukv,
             odd_w_in=odd_w_in, odd_w_out=odd_w_out, s5_A_re=s5_A_re, s5_A_im=s5_A_im, s5_B_re=s5_B_re,
             s5_B_im=s5_B_im, s5_C_re=s5_C_re, s5_C_im=s5_C_im, s5_D=s5_D, s5_log_dt=s5_log_dt,
             s5_w_glu=s5_w_glu)
    past = cache_sb_k.shape[2]
    cache = dict(mla_latent=cache_mla_latent[0], mla_krope=cache_mla_krope[0],
                 mlstm_c=state_mlstm_C[0], mlstm_n=state_mlstm_n[0], mlstm_m=state_mlstm_m[0],
                 s5_re=state_s5_re[0], s5_im=state_s5_im[0],
                 sb_k=cache_sb_k[0], sb_v=cache_sb_v[0])
    out_p = _trunk(x_prompt, 0, None, p)
    out_s = _trunk(x_sample, past, cache, p)
    return (out_p[0], out_s[0]) + tuple(out_p[1:]) + tuple(out_s[1:])
```
